```python
import jax, jax.numpy as jnp
from jax import lax
import numpy as np

D_MODEL = 2048
BATCH = 4
SEQ = 2048
DEPTH = 4
DEC_BATCH = 128
DEC_SEQ = 4
PAST_LEN = 16384
PAGE_SIZE = 128

N_META = 16
POOL_WINDOWS = (2, 4, 8, 16)
N_POOL_GROUPS = len(POOL_WINDOWS)
POOL_GROUP = D_MODEL // N_POOL_GROUPS
POOL_STATE = max(POOL_WINDOWS) - 1
HEAD_SIZE = 64
N_HEADS = D_MODEL // HEAD_SIZE
DECAY_LORA = max(32, int(round(1.8 * D_MODEL ** 0.5 / 32)) * 32)
AAA_LORA = max(32, int(round(1.8 * D_MODEL ** 0.5 / 32)) * 32)
MV_LORA = max(32, int(round(1.3 * D_MODEL ** 0.5 / 32)) * 32)
GATE_LORA = max(32, int(round(0.6 * D_MODEL ** 0.8 / 32)) * 32)
D_FF = ((8 * D_MODEL // 3 + 127) // 128) * 128
CONV_WIDTH = 3
N_POOL_LAYERS = (DEPTH + 1) // 2
N_RWKV_LAYERS = DEPTH // 2
NORM_EPS = 1e-6
GN_EPS = 64e-5

kernel_name = "poolformer_rwkv7_convffn_hybrid_step"


def rmsnorm(x, g):
    xf = x.astype(jnp.float32)
    y = xf * lax.rsqrt(jnp.mean(xf * xf, axis=-1, keepdims=True) + NORM_EPS)
    return (y * g.astype(jnp.float32)).astype(x.dtype)


def pool_mixer(h, prefix, start, w_pool, scale):
    B, T, D = h.shape
    ext = jnp.concatenate([prefix.astype(h.dtype), h], axis=1)
    ef = ext.astype(jnp.float32)
    csum = jnp.concatenate([jnp.zeros((B, 1, D), jnp.float32), jnp.cumsum(ef, axis=1)], axis=1)
    end = csum[:, POOL_STATE + 1:]
    pos = start + jnp.arange(T)
    diffs = []
    for g, w in enumerate(POOL_WINDOWS):
        sl = slice(g * POOL_GROUP, (g + 1) * POOL_GROUP)
        begin = csum[:, POOL_STATE + 1 - w: POOL_STATE + 1 - w + T, sl]
        cnt = jnp.minimum(w, pos + 1).astype(jnp.float32)[None, :, None]
        diffs.append((end[..., sl] - begin) / cnt - ef[:, POOL_STATE:, sl])
    d = jnp.stack(diffs, axis=2).astype(h.dtype)
    y = jnp.einsum('btgc,gcd->btgd', d, w_pool).reshape(B, T, D)
    return y * scale, ext[:, -POOL_STATE:]


def wkv7_scan(S0, r, w, k, v, a_in, b_in):
    def step(S, inp):
        r_t, w_t, k_t, v_t, a_t, b_t = inp
        sa = jnp.einsum('bhvk,bhk->bhv', S, a_t)
        S = S * w_t[:, :, None, :] + sa[..., None] * b_t[:, :, None, :] + v_t[..., None] * k_t[:, :, None, :]
        y = jnp.einsum('bhvk,bhk->bhv', S, r_t)
        return S, y
    xs = tuple(jnp.moveaxis(z, 1, 0) for z in (r, w, k, v, a_in, b_in))
    S, ys = lax.scan(step, S0, xs)
    return S, jnp.moveaxis(ys, 0, 1)


def rwkv_mixer(h, shift_prev, S0, v_first, p, j):
    B, T, D = h.shape
    f32 = jnp.float32
    heads = lambda z: z.astype(f32).reshape(B, T, N_HEADS, HEAD_SIZE)
    prev = jnp.concatenate([shift_prev[:, None].astype(h.dtype), h[:, :-1]], axis=1)
    xx = prev - h
    mu = p['rwkv_mu'][j]
    xr, xw, xk, xv, xa, xg = [h + xx * mu[i] for i in range(6)]
    r = xr @ p['rwkv_wr'][j]
    k = xk @ p['rwkv_wk'][j]
    v = xv @ p['rwkv_wv'][j]
    wlog = -jax.nn.softplus(-(p['rwkv_w0'][j] + jnp.tanh(xw @ p['rwkv_w1'][j]) @ p['rwkv_w2'][j]).astype(f32)) - 0.5
    decay = jnp.exp(-jnp.exp(wlog))
    if v_first is None:
        v_first = v
    else:
        jv = j - 1
        v = v + (v_first - v) * jax.nn.sigmoid(p['rwkv_v0'][jv] + (xv @ p['rwkv_v1'][jv]) @ p['rwkv_v2'][jv])
    a = jax.nn.sigmoid(p['rwkv_a0'][j] + (xa @ p['rwkv_a1'][j]) @ p['rwkv_a2'][j])
    g = jax.nn.sigmoid(xg @ p['rwkv_g1'][j]) @ p['rwkv_g2'][j]
    kk = heads(k * p['rwkv_kk'][j])
    kk = kk / jnp.maximum(jnp.sqrt(jnp.sum(kk * kk, axis=-1, keepdims=True)), 1e-12)
    k = k * (1.0 + (a - 1.0) * p['rwkv_ka'][j])
    rh, kh, vh, ah = heads(r), heads(k), heads(v), heads(a)
    S, y = wkv7_scan(S0.astype(f32), rh, heads(decay), kh, vh, -kk, kk * ah)
    mean = jnp.mean(y, axis=-1, keepdims=True)
    var = jnp.mean(jnp.square(y - mean), axis=-1, keepdims=True)
    yn = ((y - mean) * lax.rsqrt(var + GN_EPS)).reshape(B, T, D)
    yn = yn * p['rwkv_lnw'][j].astype(f32) + p['rwkv_lnb'][j].astype(f32)
    bonus = jnp.sum(rh * kh * p['rwkv_rk'][j].astype(f32), axis=-1, keepdims=True) * vh
    out = ((yn + bonus.reshape(B, T, D)).astype(h.dtype) * g) @ p['rwkv_wo'][j]
    return out, h[:, -1], S, v_first


def conv_ffn(h, prefix, w_in, conv_w, conv_b, w_out):
    T = h.shape[1]
    u = h @ w_in
    ext = jnp.concatenate([prefix.astype(u.dtype), u], axis=1)
    c = conv_b + ext[:, 0:T] * conv_w[0]
    for i in range(1, CONV_WIDTH):
        c = c + ext[:, i:i + T] * conv_w[i]
    val, gate = jnp.split(c, 2, axis=-1)
    return (jax.nn.silu(gate) * val) @ w_out, ext[:, -(CONV_WIDTH - 1):]


def trunk(x, start, pool_st, shift_st, wkv_st, conv_st, p):
    new_pool, new_shift, new_wkv, new_conv = [], [], [], []
    v_first = None
    for i in range(DEPTH):
        j = i // 2
        h = rmsnorm(x, p['norm_mix'][i])
        if i % 2 == 0:
            y, st = pool_mixer(h, pool_st[j], start, p['pool_w'][j], p['pool_scale'][j])
            new_pool.append(st)
        else:
            y, sh, S, v_first = rwkv_mixer(h, shift_st[j], wkv_st[j], v_first, p, j)
            new_shift.append(sh)
            new_wkv.append(S)
        x = x + y
        h = rmsnorm(x, p['norm_ffn'][i])
        y, cs = conv_ffn(h, conv_st[i], p['ffn_w_in'][i], p['ffn_conv_w'][i], p['ffn_conv_b'][i], p['ffn_w_out'][i])
        new_conv.append(cs)
        x = x + y
    return (rmsnorm(x, p['norm_out']), jnp.stack(new_pool), jnp.stack(new_shift),
            jnp.stack(new_wkv), jnp.stack(new_conv))


def setup_inputs(seed: int = 0) -> dict:
    key = jax.random.key(seed)
    ks = iter(jax.random.split(key, 48))
    f32 = jnp.float32
    nrm = lambda shape, s: jax.random.normal(next(ks), shape, f32) * s
    D, NP, NR = D_MODEL, N_POOL_LAYERS, N_RWKV_LAYERS
    F2 = 2 * D_FF
    return {
        'x_prompt': nrm((BATCH, SEQ, D), 1.0),
        'x_sample': nrm((DEC_BATCH, DEC_SEQ, D), 1.0),
        'state_pool': nrm((NP, DEC_BATCH, POOL_STATE, D), 1.0),
        'state_rwkv_shift': nrm((NR, DEC_BATCH, D), 1.0),
        'state_rwkv_wkv': nrm((NR, DEC_BATCH, N_HEADS, HEAD_SIZE, HEAD_SIZE), 0.3),
        'state_ffn_conv': nrm((DEPTH, DEC_BATCH, CONV_WIDTH - 1, F2), 1.0),
        'meta_tokens': nrm((N_META, D), 1.0),
        'norm_mix': 1.0 + nrm((DEPTH, D), 0.05),
        'norm_ffn': 1.0 + nrm((DEPTH, D), 0.05),
        'norm_out': 1.0 + nrm((D,), 0.05),
        'pool_w': nrm((NP, N_POOL_GROUPS, POOL_GROUP, POOL_GROUP), POOL_GROUP ** -0.5),
        'pool_scale': 0.5 + nrm((NP, D), 0.05),
        'rwkv_mu': jax.random.uniform(next(ks), (NR, 6, D), f32),
        'rwkv_wr': nrm((NR, D, D), D ** -0.5),
        'rwkv_wk': nrm((NR, D, D), D ** -0.5),
        'rwkv_wv': nrm((NR, D, D), D ** -0.5),
        'rwkv_wo': nrm((NR, D, D), D ** -0.5),
        'rwkv_w0': jax.random.uniform(next(ks), (NR, D), f32, -6.0, 1.0),
        'rwkv_w1': nrm((NR, D, DECAY_LORA), D ** -0.5),
        'rwkv_w2': nrm((NR, DECAY_LORA, D), 0.5 * DECAY_LORA ** -0.5),
        'rwkv_a0': nrm((NR, D), 0.5),
        'rwkv_a1': nrm((NR, D, AAA_LORA), D ** -0.5),
        'rwkv_a2': nrm((NR, AAA_LORA, D), 0.5 * AAA_LORA ** -0.5),
        'rwkv_v0': nrm((NR - 1, D), 0.5),
        'rwkv_v1': nrm((NR - 1, D, MV_LORA), D ** -0.5),
        'rwkv_v2': nrm((NR - 1, MV_LORA, D), 0.5 * MV_LORA ** -0.5),
        'rwkv_g1': nrm((NR, D, GATE_LORA), D ** -0.5),
        'rwkv_g2': nrm((NR, GATE_LORA, D), GATE_LORA ** -0.5),
        'rwkv_kk': 0.85 + nrm((NR, D), 0.05),
        'rwkv_ka': 1.0 + nrm((NR, D), 0.05),
        'rwkv_rk': nrm((NR, N_HEADS, HEAD_SIZE), 0.1),
        'rwkv_lnw': 1.0 + nrm((NR, D), 0.05),
        'rwkv_lnb': nrm((NR, D), 0.01),
        'ffn_w_in': nrm((DEPTH, D, F2), D ** -0.5),
        'ffn_conv_w': nrm((DEPTH, CONV_WIDTH, F2), CONV_WIDTH ** -0.5),
        'ffn_conv_b': nrm((DEPTH, F2), 0.01),
        'ffn_w_out': nrm((DEPTH, D_FF, D), D_FF ** -0.5),
    }


def reference(x_prompt, x_sample, state_pool, state_rwkv_shift, state_rwkv_wkv, state_ffn_conv,
              meta_tokens, norm_mix, norm_ffn, norm_out, pool_w, pool_scale,
              rwkv_mu, rwkv_wr, rwkv_wk, rwkv_wv, rwkv_wo, rwkv_w0, rwkv_w1, rwkv_w2,
              rwkv_a0, rwkv_a1, rwkv_a2, rwkv_v0, rwkv_v1, rwkv_v2, rwkv_g1, rwkv_g2,
              rwkv_kk, rwkv_ka, rwkv_rk, rwkv_lnw, rwkv_lnb,
              ffn_w_in, ffn_conv_w, ffn_conv_b, ffn_w_out):
    p = dict(norm_mix=norm_mix, norm_ffn=norm_ffn, norm_out=norm_out, pool_w=pool_w, pool_scale=pool_scale,
             rwkv_mu=rwkv_mu, rwkv_wr=rwkv_wr, rwkv_wk=rwkv_wk, rwkv_wv=rwkv_wv, rwkv_wo=rwkv_wo,
             rwkv_w0=rwkv_w0, rwkv_w1=rwkv_w1, rwkv_w2=rwkv_w2, rwkv_a0=rwkv_a0, rwkv_a1=rwkv_a1,
             rwkv_a2=rwkv_a2, rwkv_v0=rwkv_v0, rwkv_v1=rwkv_v1, rwkv_v2=rwkv_v2, rwkv_g1=rwkv_g1,
             rwkv_g2=rwkv_g2, rwkv_kk=rwkv_kk, rwkv_ka=rwkv_ka, rwkv_rk=rwkv_rk, rwkv_lnw=rwkv_lnw,
             rwkv_lnb=rwkv_lnb, ffn_w_in=ffn_w_in, ffn_conv_w=ffn_conv_w, ffn_conv_b=ffn_conv_b,
             ffn_w_out=ffn_w_out)
    B, dt = x_prompt.shape[0], x_prompt.dtype
    xp = jnp.concatenate([jnp.broadcast_to(meta_tokens.astype(dt)[None], (B, N_META, D_MODEL)), x_prompt], axis=1)
    zp_pool = jnp.zeros((N_POOL_LAYERS, B, POOL_STATE, D_MODEL), dt)
    zp_shift = jnp.zeros((N_RWKV_LAYERS, B, D_MODEL), dt)
    zp_wkv = jnp.zeros((N_RWKV_LAYERS, B, N_HEADS, HEAD_SIZE, HEAD_SIZE), jnp.float32)
    zp_conv = jnp.zeros((DEPTH, B, CONV_WIDTH - 1, 2 * D_FF), dt)
    yp, pool_p, shift_p, wkv_p, conv_p = trunk(xp, 0, zp_pool, zp_shift, zp_wkv, zp_conv, p)
    y_prompt = yp[:, N_META:]
    y_sample, pool_s, shift_s, wkv_s, conv_s = trunk(x_sample, PAST_LEN, state_pool, state_rwkv_shift,
                                                     state_rwkv_wkv, state_ffn_conv, p)
    return (y_prompt, y_sample, pool_p, pool_s, shift_p, shift_s, wkv_p, wkv_s, conv_p, conv_s)
```

```python
import functools
import math

import jax
import jax.numpy as jnp
from jax import lax
from jax.experimental import pallas as pl
from jax.experimental.pallas import tpu as pltpu

F32 = jnp.float32
BF16 = jnp.bfloat16

HEAD_SIZE = 64
LANES = 128
N_META = 16
PAST_LEN = 16384
POOL_WINDOWS = (2, 4, 8, 16)
POOL_STATE = max(POOL_WINDOWS) - 1
POOL_HALO = 16
CONV_WIDTH = 3
CONV_HALO = 8
NORM_EPS = 1e-6
GN_EPS = 64e-5
CHUNK = 64
SAMPLE_CHUNK = 16
FRONT_PAD = CHUNK - N_META

FFN_TM_CAP = 704
PROJ_TM_CAP = 352
POOL_TM_CAP = 704
WO_TM_CAP = 704
FFN_TF = 512
PROJ_TN = 512
VMEM_LIMIT = 56 * 1024 * 1024


def _row_tile(n, cap, mult=16):
    best = None
    for t in range(mult, min(n, cap) + 1, mult):
        if n % t == 0:
            best = t
    assert best is not None, (n, cap)
    return best


def _round_up(n, m):
    return (n + m - 1) // m * m


def _params(sem):
    return pltpu.CompilerParams(dimension_semantics=sem, vmem_limit_bytes=VMEM_LIMIT)


def _rms(x, g):
    return x * lax.rsqrt(jnp.mean(x * x, axis=-1, keepdims=True) + NORM_EPS) * g


def _dot(a, b):
    return jnp.dot(a, b, preferred_element_type=F32)


def _split(x):
    hi = x.astype(BF16)
    lo = (x - hi.astype(F32)).astype(BF16)
    return hi, lo


def _pool_prompt_kernel(x_ref, g_ref, w_ref, sc_ref, o_ref, st_ref, carry_ref, *, tm, padf, cg):
    i = pl.program_id(1)

    @pl.when(i == 0)
    def _():
        carry_ref[...] = jnp.zeros_like(carry_ref)

    x = x_ref[0]
    row = i * tm + lax.broadcasted_iota(jnp.int32, (tm, 1), 0)
    h = jnp.where(row >= padf, _rms(x, g_ref[...]), 0.0)
    ext = jnp.concatenate([carry_ref[...], h], axis=0)
    pos = row - padf
    for g, w in enumerate(POOL_WINDOWS):
        sl = slice(g * cg, (g + 1) * cg)
        a = ext[:, sl]
        k = 1
        while k < w:
            n = a.shape[0]
            a = a[:n - k] + a[k:]
            k *= 2
        win = a[POOL_HALO + 1 - w: POOL_HALO + 1 - w + tm]
        cnt = jnp.clip(pos + 1, 1, w).astype(F32)
        d = win / cnt - h[:, sl]
        y = _dot(d.astype(BF16), w_ref[g])
        o_ref[0, :, sl] = x[:, sl] + y * sc_ref[:, sl]
    carry_ref[...] = h[tm - POOL_HALO:]
    st_ref[0] = h[tm - POOL_HALO:]


def _pool_prompt(x, g, w, sc, *, tp):
    b, _, d = x.shape
    tm = _row_tile(tp, POOL_TM_CAP)
    cg = d // len(POOL_WINDOWS)
    kern = functools.partial(_pool_prompt_kernel, tm=tm, padf=FRONT_PAD, cg=cg)
    return pl.pallas_call(
        kern,
        grid=(b, tp // tm),
        in_specs=[
            pl.BlockSpec((1, tm, d), lambda bi, i: (bi, i, 0)),
            pl.BlockSpec((1, d), lambda bi, i: (0, 0)),
            pl.BlockSpec((len(POOL_WINDOWS), cg, cg), lambda bi, i: (0, 0, 0)),
            pl.BlockSpec((1, d), lambda bi, i: (0, 0)),
        ],
        out_specs=[
            pl.BlockSpec((1, tm, d), lambda bi, i: (bi, i, 0)),
            pl.BlockSpec((1, POOL_HALO, d), lambda bi, i: (bi, 0, 0)),
        ],
        out_shape=[
            jax.ShapeDtypeStruct((b, tp, d), F32),
            jax.ShapeDtypeStruct((b, POOL_HALO, d), F32),
        ],
        scratch_shapes=[pltpu.VMEM((POOL_HALO, d), F32)],
        compiler_params=_params(("arbitrary", "arbitrary")),
        name="pool_prompt",
    )(x, g, w, sc)


def _pool_sample_kernel(x_ref, pre_ref, g_ref, w_ref, sc_ref, o_ref, st_ref, *, tb, d, cg, nt, start):
    hs = [_rms(x_ref[t], g_ref[...]) for t in range(nt)]
    ext = [pre_ref[:, j * d:(j + 1) * d] for j in range(POOL_STATE)] + hs
    for g, w in enumerate(POOL_WINDOWS):
        sl = slice(g * cg, (g + 1) * cg)
        ds = []
        for t in range(nt):
            e = POOL_STATE + t
            acc = ext[e][:, sl]
            for q in range(1, w):
                acc = acc + ext[e - q][:, sl]
            cnt = float(min(w, start + t + 1))
            ds.append(acc / cnt - hs[t][:, sl])
        y = _dot(jnp.concatenate(ds, axis=0).astype(BF16), w_ref[g])
        for t in range(nt):
            o_ref[t, :, sl] = x_ref[t][:, sl] + y[t * tb:(t + 1) * tb] * sc_ref[:, sl]
    for j in range(POOL_STATE):
        st_ref[:, j * d:(j + 1) * d] = ext[nt + j]


def _pool_sample(x, pre, g, w, sc, *, nt, bs):
    d = x.shape[-1]
    tb = min(bs, 32)
    cg = d // len(POOL_WINDOWS)
    kern = functools.partial(_pool_sample_kernel, tb=tb, d=d, cg=cg, nt=nt, start=PAST_LEN)
    return pl.pallas_call(
        kern,
        grid=(bs // tb,),
        in_specs=[
            pl.BlockSpec((nt, tb, d), lambda i: (0, i, 0)),
            pl.BlockSpec((tb, POOL_STATE * d), lambda i: (i, 0)),
            pl.BlockSpec((1, d), lambda i: (0, 0)),
            pl.BlockSpec((len(POOL_WINDOWS), cg, cg), lambda i: (0, 0, 0)),
            pl.BlockSpec((1, d), lambda i: (0, 0)),
        ],
        out_specs=[
            pl.BlockSpec((nt, tb, d), lambda i: (0, i, 0)),
            pl.BlockSpec((tb, POOL_STATE * d), lambda i: (i, 0)),
        ],
        out_shape=[
            jax.ShapeDtypeStruct((nt, bs, d), F32),
            jax.ShapeDtypeStruct((bs, POOL_STATE * d), F32),
        ],
        compiler_params=_params(("arbitrary",)),
        name="pool_sample",
    )(x, pre, g, w, sc)


def _ffn_prompt_kernel(x_ref, g_ref, wv_ref, wg_ref, cwv_ref, cwg_ref, cbv_ref, cbg_ref, wo_ref, go_ref,
                       o_ref, stv_ref, stg_ref, hb_ref, carv_ref, carg_ref, *, tm, tps, padf, nj, final_norm):
    i = pl.program_id(0)
    j = pl.program_id(1)
    ti = i % tps

    @pl.when(j == 0)
    def _():
        x = x_ref[...]
        row = ti * tm + lax.broadcasted_iota(jnp.int32, (tm, 1), 0)
        h = jnp.where(row >= padf, _rms(x, g_ref[...]), 0.0)
        hb_ref[...] = h.astype(BF16)
        o_ref[...] = x

    hb = hb_ref[...]
    keep = ti != 0

    def branch(w_ref, cw_ref, cb_ref, car_ref, st_ref):
        u = _dot(hb, w_ref[...])
        prev = jnp.where(keep, car_ref[j], 0.0)
        ext = jnp.concatenate([prev, u], axis=0)
        cw = cw_ref[...]
        c = (cb_ref[...] + ext[CONV_HALO - 2:CONV_HALO - 2 + tm] * cw[0:1]
             + ext[CONV_HALO - 1:CONV_HALO - 1 + tm] * cw[1:2] + u * cw[2:3])
        car_ref[j] = u[tm - CONV_HALO:]
        st_ref[0] = u[tm - CONV_HALO:]
        return c

    cv = branch(wv_ref, cwv_ref, cbv_ref, carv_ref, stv_ref)
    cgate = branch(wg_ref, cwg_ref, cbg_ref, carg_ref, stg_ref)
    act = cgate * jax.nn.sigmoid(cgate) * cv
    o_ref[...] += _dot(act.astype(BF16), wo_ref[...])

    if final_norm:
        @pl.when(j == nj - 1)
        def _():
            o_ref[...] = _rms(o_ref[...], go_ref[...])


def _ffn_prompt(x, g, w_in, cw, cb, w_out, g_out, *, tp, final_norm):
    rows, d = x.shape
    fp = w_out.shape[0]
    tf = FFN_TF
    nj = fp // tf
    tm = _row_tile(tp, FFN_TM_CAP)
    tps = tp // tm
    kern = functools.partial(_ffn_prompt_kernel, tm=tm, tps=tps, padf=FRONT_PAD, nj=nj, final_norm=final_norm)
    return pl.pallas_call(
        kern,
        grid=(rows // tm, nj),
        in_specs=[
            pl.BlockSpec((tm, d), lambda i, j: (i, 0)),
            pl.BlockSpec((1, d), lambda i, j: (0, 0)),
            pl.BlockSpec((d, tf), lambda i, j: (0, j)),
            pl.BlockSpec((d, tf), lambda i, j: (0, nj + j)),
            pl.BlockSpec((CONV_WIDTH, tf), lambda i, j: (0, j)),
            pl.BlockSpec((CONV_WIDTH, tf), lambda i, j: (0, nj + j)),
            pl.BlockSpec((1, tf), lambda i, j: (0, j)),
            pl.BlockSpec((1, tf), lambda i, j: (0, nj + j)),
            pl.BlockSpec((tf, d), lambda i, j: (j, 0)),
            pl.BlockSpec((1, d), lambda i, j: (0, 0)),
        ],
        out_specs=[
            pl.BlockSpec((tm, d), lambda i, j: (i, 0)),
            pl.BlockSpec((1, CONV_HALO, tf), lambda i, j: (i, 0, j)),
            pl.BlockSpec((1, CONV_HALO, tf), lambda i, j: (i, 0, j)),
        ],
        out_shape=[
            jax.ShapeDtypeStruct((rows, d), F32),
            jax.ShapeDtypeStruct((rows // tm, CONV_HALO, fp), F32),
            jax.ShapeDtypeStruct((rows // tm, CONV_HALO, fp), F32),
        ],
        scratch_shapes=[
            pltpu.VMEM((tm, d), BF16),
            pltpu.VMEM((nj, CONV_HALO, tf), F32),
            pltpu.VMEM((nj, CONV_HALO, tf), F32),
        ],
        compiler_params=_params(("arbitrary", "arbitrary")),
        name="ffn_prompt",
    )(x, g, w_in, w_in, cw, cw, cb, cb, w_out, g_out)


def _ffn_sample_kernel(x_ref, g_ref, wv_ref, wg_ref, cwv_ref, cwg_ref, cbv_ref, cbg_ref, wo_ref, go_ref,
                       p0v_ref, p0g_ref, p1v_ref, p1g_ref,
                       o_ref, s0v_ref, s0g_ref, s1v_ref, s1g_ref, hb_ref, *, bs, nt, nj, final_norm):
    j = pl.program_id(0)

    @pl.when(j == 0)
    def _():
        x = x_ref[...]
        hb_ref[...] = _rms(x, g_ref[...]).astype(BF16)
        o_ref[...] = x

    hb = hb_ref[...]

    def branch(w_ref, cw_ref, cb_ref, p0_ref, p1_ref, s0_ref, s1_ref):
        u = _dot(hb, w_ref[...])
        ext = [p0_ref[...], p1_ref[...]] + [u[t * bs:(t + 1) * bs] for t in range(nt)]
        cw = cw_ref[...]
        cs = [cb_ref[...] + ext[t] * cw[0:1] + ext[t + 1] * cw[1:2] + ext[t + 2] * cw[2:3] for t in range(nt)]
        s0_ref[...] = ext[nt]
        s1_ref[...] = ext[nt + 1]
        return jnp.concatenate(cs, axis=0)

    cv = branch(wv_ref, cwv_ref, cbv_ref, p0v_ref, p1v_ref, s0v_ref, s1v_ref)
    cgate = branch(wg_ref, cwg_ref, cbg_ref, p0g_ref, p1g_ref, s0g_ref, s1g_ref)
    act = cgate * jax.nn.sigmoid(cgate) * cv
    o_ref[...] += _dot(act.astype(BF16), wo_ref[...])

    if final_norm:
        @pl.when(j == nj - 1)
        def _():
            o_ref[...] = _rms(o_ref[...], go_ref[...])


def _ffn_sample(x, g, w_in, cw, cb, w_out, g_out, st, *, bs, nt, final_norm):
    rows, d = x.shape
    fp = w_out.shape[0]
    tf = FFN_TF
    nj = fp // tf
    kern = functools.partial(_ffn_sample_kernel, bs=bs, nt=nt, nj=nj, final_norm=final_norm)
    st_spec = lambda q: pl.BlockSpec((bs, tf), lambda j: (0, q * nj + j))
    return pl.pallas_call(
        kern,
        grid=(nj,),
        in_specs=[
            pl.BlockSpec((rows, d), lambda j: (0, 0)),
            pl.BlockSpec((1, d), lambda j: (0, 0)),
            pl.BlockSpec((d, tf), lambda j: (0, j)),
            pl.BlockSpec((d, tf), lambda j: (0, nj + j)),
            pl.BlockSpec((CONV_WIDTH, tf), lambda j: (0, j)),
            pl.BlockSpec((CONV_WIDTH, tf), lambda j: (0, nj + j)),
            pl.BlockSpec((1, tf), lambda j: (0, j)),
            pl.BlockSpec((1, tf), lambda j: (0, nj + j)),
            pl.BlockSpec((tf, d), lambda j: (j, 0)),
            pl.BlockSpec((1, d), lambda j: (0, 0)),
            st_spec(0), st_spec(1), st_spec(2), st_spec(3),
        ],
        out_specs=[pl.BlockSpec((rows, d), lambda j: (0, 0))] + [pl.BlockSpec((bs, tf), lambda j: (0, j))] * 4,
        out_shape=[jax.ShapeDtypeStruct((rows, d), F32)] + [jax.ShapeDtypeStruct((bs, fp), F32)] * 4,
        scratch_shapes=[pltpu.VMEM((rows, d), BF16)],
        compiler_params=_params(("arbitrary",)),
        name="ffn_sample",
    )(x, g, w_in, w_in, cw, cw, cb, cb, w_out, g_out, st, st, st, st)


def _proj_kernel(*refs, tm, tps, padf, sample, bs, has_vlora):
    it = iter(refs)
    x_ref = next(it)
    sh_ref = next(it) if sample else None
    g_ref, mu_ref, wr_ref, wk_ref, wv_ref, w1_ref, a1_ref, g1_ref = (next(it) for _ in range(8))
    w2_ref, a2_ref, g2_ref, w0_ref, a0_ref, kk_ref, ka_ref, bd_ref = (next(it) for _ in range(8))
    if has_vlora:
        v1_ref, v2_ref, v0_ref, vf_ref = (next(it) for _ in range(4))
    r_o, ld_o, k_o, v_o, kn_o, a_o, g_o, hl_o = (next(it) for _ in range(8))
    xr_s, xk_s, xv_s, lw_s, la_s, lg_s = (next(it) for _ in range(6))
    lv_s = next(it) if has_vlora else None
    car_s = None if sample else next(it)

    i = pl.program_id(0)
    j = pl.program_id(1)

    @pl.when(j == 0)
    def _():
        x = x_ref[...]
        h = _rms(x, g_ref[...])
        if sample:
            prev = jnp.concatenate([sh_ref[...], h[:tm - bs]], axis=0)
            hl_o[...] = h[tm - bs:]
        else:
            ti = i % tps
            rloc = lax.broadcasted_iota(jnp.int32, (tm, 1), 0)
            h = jnp.where(ti * tm + rloc >= padf, h, 0.0)
            last = jnp.where(ti != 0, car_s[CONV_HALO - 1:CONV_HALO, :], 0.0)
            prev = jnp.where(rloc == 0, last, pltpu.roll(h, 1, 0))
            car_s[...] = h[tm - CONV_HALO:]
            hl_o[0] = h[tm - CONV_HALO:]
        xx = prev - h
        mu = mu_ref[...]
        mix = lambda q: (h + xx * mu[q:q + 1]).astype(BF16)
        xr_s[...] = mix(0)
        xk_s[...] = mix(2)
        xv = mix(3)
        xv_s[...] = xv
        lw_s[...] = jnp.tanh(_dot(mix(1), w1_ref[...])).astype(BF16)
        la_s[...] = _dot(mix(4), a1_ref[...]).astype(BF16)
        lg_s[...] = jax.nn.sigmoid(_dot(mix(5), g1_ref[...])).astype(BF16)
        if has_vlora:
            lv_s[...] = _dot(xv, v1_ref[...]).astype(BF16)

    r = _dot(xr_s[...], wr_ref[...])
    k = _dot(xk_s[...], wk_ref[...])
    v = _dot(xv_s[...], wv_ref[...])
    z = w0_ref[...] + _dot(lw_s[...], w2_ref[...])
    ld_o[...] = -math.exp(-0.5) * jax.nn.sigmoid(z)
    a = jax.nn.sigmoid(a0_ref[...] + _dot(la_s[...], a2_ref[...]))
    if has_vlora:
        v = v + (vf_ref[...] - v) * jax.nn.sigmoid(v0_ref[...] + _dot(lv_s[...], v2_ref[...]))
    g_o[...] = _dot(lg_s[...], g2_ref[...])
    kk = k * kk_ref[...]
    sq_hi, sq_lo = _split(kk * kk)
    ss = _dot(sq_hi, bd_ref[...]) + _dot(sq_lo, bd_ref[...])
    kn_o[...] = kk / jnp.maximum(jnp.sqrt(ss), 1e-12)
    r_o[...] = r
    k_o[...] = k * (1.0 + (a - 1.0) * ka_ref[...])
    v_o[...] = v
    a_o[...] = a


def _proj(x, sh, p, vl, vfirst, *, tp, sample, bs):
    rows, d = x.shape
    tn = min(PROJ_TN, d)
    nj = d // tn
    if sample:
        tm, tps = rows, 1
    else:
        tm = _row_tile(tp, PROJ_TM_CAP)
        tps = tp // tm
    has_vlora = vl is not None
    kern = functools.partial(_proj_kernel, tm=tm, tps=tps, padf=FRONT_PAD, sample=sample, bs=bs,
                             has_vlora=has_vlora)
    full = lambda a: pl.BlockSpec(a.shape, lambda i, j: (0,) * a.ndim)
    colb = lambda a: pl.BlockSpec((a.shape[0], tn), lambda i, j: (0, j))
    rowb = pl.BlockSpec((tm, tn), lambda i, j: (i, j))
    args, specs = [x], [pl.BlockSpec((tm, d), lambda i, j: (i, 0))]
    if sample:
        args.append(sh)
        specs.append(full(sh))
    for name in ("g", "mu"):
        args.append(p[name]); specs.append(full(p[name]))
    for name in ("wr", "wk", "wv"):
        args.append(p[name]); specs.append(colb(p[name]))
    for name in ("w1", "a1", "g1"):
        args.append(p[name]); specs.append(full(p[name]))
    for name in ("w2", "a2", "g2", "w0", "a0", "kk", "ka"):
        args.append(p[name]); specs.append(colb(p[name]))
    args.append(p["bd"]); specs.append(full(p["bd"]))
    if has_vlora:
        args += [vl["v1"], vl["v2"], vl["v0"], vfirst]
        specs += [full(vl["v1"]), colb(vl["v2"]), colb(vl["v0"]), rowb]
    out_shape = [jax.ShapeDtypeStruct((rows, d), F32)] * 7
    out_specs = [rowb] * 7
    if sample:
        out_shape.append(jax.ShapeDtypeStruct((bs, d), F32))
        out_specs.append(pl.BlockSpec((bs, d), lambda i, j: (0, 0)))
    else:
        nb = rows // tp
        out_shape.append(jax.ShapeDtypeStruct((nb, CONV_HALO, d), F32))
        out_specs.append(pl.BlockSpec((1, CONV_HALO, d), lambda i, j: (i // tps, 0, 0)))
    lw, la, lg = p["w1"].shape[1], p["a1"].shape[1], p["g1"].shape[1]
    scratch = [pltpu.VMEM((tm, d), BF16)] * 3 + [pltpu.VMEM((tm, lw), BF16), pltpu.VMEM((tm, la), BF16),
                                                 pltpu.VMEM((tm, lg), BF16)]
    if has_vlora:
        scratch.append(pltpu.VMEM((tm, vl["v1"].shape[1]), BF16))
    if not sample:
        scratch.append(pltpu.VMEM((CONV_HALO, d), F32))
    return pl.pallas_call(
        kern,
        grid=(rows // tm, nj),
        in_specs=specs,
        out_specs=out_specs,
        out_shape=out_shape,
        scratch_shapes=scratch,
        compiler_params=_params(("arbitrary", "arbitrary")),
        name="rwkv_proj_sample" if sample else "rwkv_proj_prompt",
    )(*args)


def _mm(a, b):
    return _dot(a.astype(BF16), b.astype(BF16))


def _mm_nt(a, b):
    return lax.dot_general(a.astype(BF16), b.astype(BF16), (((1,), (1,)), ((), ())),
                           preferred_element_type=F32)


def _mm3(a, b):
    ah, al = _split(a)
    bh, bl = _split(b)
    return _dot(ah, bh) + _dot(ah, bl) + _dot(al, bh)


def _wkv_kernel(r_ref, ld_ref, k_ref, v_ref, kn_ref, a_ref, g_ref, s0_ref, lnw_ref, lnb_ref, rk_ref,
                z_ref, so_ref, *, bb, L, nchunks):
    hs = HEAD_SIZE
    lane = lax.broadcasted_iota(jnp.int32, (1, LANES), 1)
    head0 = lane < hs
    m0 = head0.astype(F32)
    m1 = 1.0 - m0
    ri = lax.broadcasted_iota(jnp.int32, (LANES, LANES), 0)
    ci = lax.broadcasted_iota(jnp.int32, (LANES, LANES), 1)
    bdmask = ((ri < hs) == (ci < hs)).astype(F32)
    rl = lax.broadcasted_iota(jnp.int32, (L, L), 0)
    cl = lax.broadcasted_iota(jnp.int32, (L, L), 1)
    tril_incl = (cl <= rl).astype(BF16)
    eye = (cl == rl).astype(F32)
    prow = lax.broadcasted_iota(jnp.int32, (4 * L, 2 * L), 0)
    pcol = lax.broadcasted_iota(jnp.int32, (4 * L, 2 * L), 1) % L
    pmask = jnp.where(prow >= 2 * L, (pcol <= prow % L).astype(F32), (pcol < prow % L).astype(F32))
    kcols = (lax.broadcasted_iota(jnp.int32, (1, 2 * L), 1) >= L).astype(F32)
    lnw, lnb, rk = lnw_ref[...], lnb_ref[...], rk_ref[...]
    nsteps = int(math.log2(L)) - 1

    def sel(zz):
        return jnp.where(head0, zz[:L], zz[L:])

    def head_sum(y):
        s0 = jnp.sum(y * m0, axis=-1, keepdims=True)
        s1 = jnp.sum(y * m1, axis=-1, keepdims=True)
        return jnp.where(head0, s0, s1)

    def inverse(a):
        t = eye + a
        pw = a
        for _ in range(nsteps):
            pw = _mm3(pw, pw)
            t = t + _mm3(t, pw)
        return t

    def seq_body(s, carry):
        s_init = s0_ref[s]
        zed = jnp.zeros((hs, hs), F32)
        state0 = jnp.concatenate([jnp.concatenate([s_init[0], zed], axis=1),
                                  jnp.concatenate([zed, s_init[1]], axis=1)], axis=0)

        def chunk_body(c, state):
            rows = pl.ds(pl.multiple_of(c * L, L), L)
            r, ld, k, v = r_ref[s, rows, :], ld_ref[s, rows, :], k_ref[s, rows, :], v_ref[s, rows, :]
            kn, a, g = kn_ref[s, rows, :], a_ref[s, rows, :], g_ref[s, rows, :]
            ld_hi = ld.astype(BF16)
            ld_r = ld - ld_hi.astype(F32)
            ld_mid = ld_r.astype(BF16)
            ld_lo = (ld_r - ld_mid.astype(F32)).astype(BF16)
            cs = _dot(tril_incl, ld_hi) + _dot(tril_incl, ld_mid) + _dot(tril_incl, ld_lo)
            c_end = cs[L - 1:L, :]
            w_in = jnp.exp(cs)
            w_inv = jnp.exp(-cs)
            w_prev = jnp.exp(cs - ld)
            w_end = jnp.exp(c_end - cs)
            b = kn * a
            rt, kt, bt, at = r * w_in, k * w_inv, b * w_inv, -kn * w_prev
            kh, bh = k * w_end, b * w_end
            lhs = jnp.concatenate([at * m0, at * m1, rt * m0, rt * m1], axis=0)
            pm = _mm_nt(lhs, jnp.concatenate([bt, kt], axis=0)) * pmask
            a_bk = pm[:2 * L]
            r_bk = pm[2 * L:]
            gs = _mm_nt(jnp.concatenate([at, rt], axis=0), state)
            vv = jnp.concatenate([v, v], axis=0)
            x = gs[:L] + sel(_mm(a_bk * kcols, vv))
            a_b = a_bk[:, :L]
            u = jnp.where(head0, _mm3(inverse(a_b[:L]), x), _mm3(inverse(a_b[L:]), x))
            uv = jnp.concatenate([u, v], axis=0)
            y = gs[L:] + sel(_mm(r_bk, uv))
            upd = _mm(uv.T, jnp.concatenate([bh, kh], axis=0))
            state = state * jnp.exp(c_end) + upd * bdmask
            mean = head_sum(y) * (1.0 / hs)
            yc = y - mean
            var = head_sum(yc * yc) * (1.0 / hs)
            yn = yc * lax.rsqrt(var + GN_EPS) * lnw + lnb
            bonus = head_sum(r * k * rk) * v
            z_ref[s, rows, :] = ((yn + bonus) * g).astype(z_ref.dtype)
            return state

        state = lax.fori_loop(0, nchunks, chunk_body, state0)
        so_ref[s, 0] = state[:hs, :hs]
        so_ref[s, 1] = state[hs:, hs:]
        return carry

    lax.fori_loop(0, bb, seq_body, 0)


def _wkv(r, ld, k, v, kn, a, g, s0, lnw, lnb, rk, *, L, bb):
    nb, tt, d = r.shape
    nh = d // HEAD_SIZE
    kern = functools.partial(_wkv_kernel, bb=bb, L=L, nchunks=tt // L)
    seqb = pl.BlockSpec((bb, tt, LANES), lambda i, p: (i, 0, p))
    vecb = pl.BlockSpec((1, LANES), lambda i, p: (0, p))
    stb = pl.BlockSpec((bb, 2, HEAD_SIZE, HEAD_SIZE), lambda i, p: (i, p, 0, 0))
    return pl.pallas_call(
        kern,
        grid=(nb // bb, nh // 2),
        in_specs=[seqb] * 7 + [stb, vecb, vecb, vecb],
        out_specs=[seqb, stb],
        out_shape=[jax.ShapeDtypeStruct((nb, tt, d), BF16),
                   jax.ShapeDtypeStruct((nb, nh, HEAD_SIZE, HEAD_SIZE), F32)],
        compiler_params=_params(("arbitrary", "arbitrary")),
        name=f"wkv_chunk{L}",
    )(r, ld, k, v, kn, a, g, s0, lnw, lnb, rk)


def _wo_kernel(x_ref, z_ref, w_ref, o_ref):
    o_ref[...] = x_ref[...] + _dot(z_ref[...].astype(BF16), w_ref[...])


def _wo(x, z, w):
    rows, d = x.shape
    tm = _row_tile(rows, WO_TM_CAP)
    return pl.pallas_call(
        _wo_kernel,
        grid=(rows // tm,),
        in_specs=[pl.BlockSpec((tm, d), lambda i: (i, 0)), pl.BlockSpec((tm, d), lambda i: (i, 0)),
                  pl.BlockSpec((d, d), lambda i: (0, 0))],
        out_specs=pl.BlockSpec((tm, d), lambda i: (i, 0)),
        out_shape=jax.ShapeDtypeStruct((rows, d), F32),
        compiler_params=_params(("arbitrary",)),
        name="rwkv_wo",
    )(x, z, w)


def _pad_cols(a, n):
    return jnp.pad(a, [(0, 0)] * (a.ndim - 1) + [(0, n - a.shape[-1])])


def _pad_rows(a, n):
    return jnp.pad(a, [(0, n - a.shape[0])] + [(0, 0)] * (a.ndim - 1))


def kernel(x_prompt, x_sample, state_pool, state_rwkv_shift, state_rwkv_wkv, state_ffn_conv, meta_tokens,
           norm_mix, norm_ffn, norm_out, pool_w, pool_scale, rwkv_mu, rwkv_wr, rwkv_wk, rwkv_wv, rwkv_wo,
           rwkv_w0, rwkv_w1, rwkv_w2, rwkv_a0, rwkv_a1, rwkv_a2, rwkv_v0, rwkv_v1, rwkv_v2, rwkv_g1, rwkv_g2,
           rwkv_kk, rwkv_ka, rwkv_rk, rwkv_lnw, rwkv_lnb, ffn_w_in, ffn_conv_w, ffn_conv_b, ffn_w_out):
    b, seq, d = x_prompt.shape
    bs, nt, _ = x_sample.shape
    depth = norm_mix.shape[0]
    f = ffn_w_out.shape[1]
    fp = _round_up(f, FFN_TF)
    nh = d // HEAD_SIZE
    tp = FRONT_PAD + N_META + seq
    assert tp % CHUNK == 0 and d % (2 * LANES) == 0 and nt <= SAMPLE_CHUNK and nt >= CONV_WIDTH - 1

    row = lambda a: a.reshape(1, -1)
    lora_in = lambda a: _pad_cols(a, _round_up(a.shape[-1], LANES)).astype(BF16)
    lora_out = lambda a: _pad_rows(a, _round_up(a.shape[0], LANES)).astype(BF16)
    ffn = []
    for i in range(depth):
        w_in = ffn_w_in[i].reshape(d, 2, f)
        ffn.append(dict(
            g=row(norm_ffn[i]),
            w_in=_pad_cols(w_in, fp).reshape(d, 2 * fp).astype(BF16),
            cw=_pad_cols(ffn_conv_w[i].reshape(CONV_WIDTH, 2, f), fp).reshape(CONV_WIDTH, 2 * fp),
            cb=_pad_cols(ffn_conv_b[i].reshape(1, 2, f), fp).reshape(1, 2 * fp),
            w_out=_pad_rows(ffn_w_out[i], fp).astype(BF16)))
    tn = min(PROJ_TN, d)
    bd = jnp.kron(jnp.eye(tn // HEAD_SIZE, dtype=F32), jnp.ones((HEAD_SIZE, HEAD_SIZE), F32)).astype(BF16)
    rw = []
    for j in range(depth // 2):
        rw.append(dict(
            g=row(norm_mix[2 * j + 1]), mu=rwkv_mu[j],
            wr=rwkv_wr[j].astype(BF16), wk=rwkv_wk[j].astype(BF16), wv=rwkv_wv[j].astype(BF16),
            w1=lora_in(rwkv_w1[j]), a1=lora_in(rwkv_a1[j]), g1=lora_in(rwkv_g1[j]),
            w2=lora_out(rwkv_w2[j]), a2=lora_out(rwkv_a2[j]), g2=lora_out(rwkv_g2[j]),
            w0=row(rwkv_w0[j]), a0=row(rwkv_a0[j]), kk=row(rwkv_kk[j]), ka=row(rwkv_ka[j]), bd=bd,
            wo=rwkv_wo[j].astype(BF16), lnw=row(rwkv_lnw[j]), lnb=row(rwkv_lnb[j]), rk=row(rwkv_rk[j])))
    vls = [None] + [dict(v1=lora_in(rwkv_v1[j]), v2=lora_out(rwkv_v2[j]), v0=row(rwkv_v0[j]))
                    for j in range(depth // 2 - 1)]
    pw = [pool_w[j].astype(BF16) for j in range((depth + 1) // 2)]
    g_out = row(norm_out)

    xp = jnp.concatenate([jnp.zeros((b, FRONT_PAD, d), F32),
                          jnp.broadcast_to(meta_tokens[None], (b, N_META, d)), x_prompt], axis=1)
    xs = x_sample.transpose(1, 0, 2)
    zero_wkv = jnp.zeros((b, nh, HEAD_SIZE, HEAD_SIZE), F32)

    pool_p, pool_s, shift_p, shift_s, wkv_p, wkv_s, conv_p, conv_s = [], [], [], [], [], [], [], []
    vfirst_p = vfirst_s = None
    for i in range(depth):
        j = i // 2
        if i % 2 == 0:
            xp, st = _pool_prompt(xp, row(norm_mix[i]), pw[j], row(pool_scale[j]), tp=tp)
            pool_p.append(st[:, POOL_HALO - POOL_STATE:])
            xs, st = _pool_sample(xs, state_pool[j].reshape(bs, POOL_STATE * d), row(norm_mix[i]), pw[j],
                                  row(pool_scale[j]), nt=nt, bs=bs)
            pool_s.append(st.reshape(bs, POOL_STATE, d))
        else:
            p = rw[j]
            r, ld, k, v, kn, a, g, hl = _proj(xp.reshape(b * tp, d), None, p, vls[j], vfirst_p,
                                              tp=tp, sample=False, bs=0)
            if vfirst_p is None:
                vfirst_p = v
            shift_p.append(hl[:, CONV_HALO - 1])
            sq = lambda t: t.reshape(b, tp, d)
            z, s_new = _wkv(sq(r), sq(ld), sq(k), sq(v), sq(kn), sq(a), sq(g), zero_wkv,
                            p["lnw"], p["lnb"], p["rk"], L=CHUNK, bb=1)
            wkv_p.append(s_new)
            xp = _wo(xp.reshape(b * tp, d), z.reshape(b * tp, d), p["wo"]).reshape(b, tp, d)
            r, ld, k, v, kn, a, g, hl = _proj(xs.reshape(nt * bs, d), state_rwkv_shift[j], p, vls[j], vfirst_s,
                                              tp=0, sample=True, bs=bs)
            if vfirst_s is None:
                vfirst_s = v
            shift_s.append(hl)
            sq = lambda t: jnp.pad(t.reshape(nt, bs, d).transpose(1, 0, 2),
                                   ((0, 0), (0, SAMPLE_CHUNK - nt), (0, 0)))
            z, s_new = _wkv(sq(r), sq(ld), sq(k), sq(v), sq(kn), sq(a), sq(g), state_rwkv_wkv[j],
                            p["lnw"], p["lnb"], p["rk"], L=SAMPLE_CHUNK, bb=min(bs, 32))
            wkv_s.append(s_new)
            z = z[:, :nt].transpose(1, 0, 2).reshape(nt * bs, d)
            xs = _wo(xs.reshape(nt * bs, d), z, p["wo"]).reshape(nt, bs, d)
        fi = ffn[i]
        last = i == depth - 1
        xo, stv, stg = _ffn_prompt(xp.reshape(b * tp, d), fi["g"], fi["w_in"], fi["cw"], fi["cb"], fi["w_out"],
                                   g_out, tp=tp, final_norm=last)
        xp = xo.reshape(b, tp, d)
        stv, stg = (t.reshape(b, -1, CONV_HALO, fp)[:, -1] for t in (stv, stg))
        conv_p.append(jnp.concatenate([stv[:, CONV_HALO - 2:, :f], stg[:, CONV_HALO - 2:, :f]], axis=-1))
        st_in = _pad_cols(state_ffn_conv[i].reshape(bs, CONV_WIDTH - 1, 2, f), fp).reshape(bs, -1)
        xo, s0v, s0g, s1v, s1g = _ffn_sample(xs.reshape(nt * bs, d), fi["g"], fi["w_in"], fi["cw"], fi["cb"],
                                             fi["w_out"], g_out, st_in, bs=bs, nt=nt, final_norm=last)
        xs = xo.reshape(nt, bs, d)
        conv_s.append(jnp.stack([jnp.concatenate([s0v[:, :f], s0g[:, :f]], axis=-1),
                                 jnp.concatenate([s1v[:, :f], s1g[:, :f]], axis=-1)], axis=1))

    y_prompt = xp[:, FRONT_PAD + N_META:]
    y_sample = xs.transpose(1, 0, 2)
    return (y_prompt, y_sample, jnp.stack(pool_p), jnp.stack(pool_s), jnp.stack(shift_p), jnp.stack(shift_s),
            jnp.stack(wkv_p), jnp.stack(wkv_s), jnp.stack(conv_p), jnp.stack(conv_s))
```

```python
import functools
import math

import jax
import jax.numpy as jnp
from jax import lax
from jax.experimental import pallas as pl
from jax.experimental.pallas import tpu as pltpu

F32 = jnp.float32
BF16 = jnp.bfloat16

HEAD_SIZE = 64
LANES = 128
N_META = 16
PAST_LEN = 16384
POOL_WINDOWS = (2, 4, 8, 16)
POOL_STATE = max(POOL_WINDOWS) - 1
POOL_HALO = 16
CONV_WIDTH = 3
CONV_HALO = 8
NORM_EPS = 1e-6
GN_EPS = 64e-5
CHUNK = 64
SAMPLE_CHUNK = 16
SAMPLE_TOK = 8
FRONT_PAD = CHUNK - N_META

FFN_TM_CAP = 704
PROJ_TM_CAP = 352
POOL_TM_CAP = 704
WO_TM_CAP = 704
WKV_ROWS_CAP = 704
WKV_PAIRS = 4
FFN_TF = 512
PROJ_TN = 512
VMEM_LIMIT = 56 * 1024 * 1024


def _row_tile(n, cap, mult=16):
    best = None
    for t in range(mult, min(n, cap) + 1, mult):
        if n % t == 0:
            best = t
    assert best is not None, (n, cap)
    return best


def _round_up(n, m):
    return (n + m - 1) // m * m


def _params(sem):
    return pltpu.CompilerParams(dimension_semantics=sem, vmem_limit_bytes=VMEM_LIMIT)


def _rms(x, g):
    return x * lax.rsqrt(jnp.mean(x * x, axis=-1, keepdims=True) + NORM_EPS) * g


def _dot(a, b):
    return jnp.dot(a, b, preferred_element_type=F32)


def _split(x):
    hi = x.astype(BF16)
    lo = (x - hi.astype(F32)).astype(BF16)
    return hi, lo


def _pool_prompt_kernel(x_ref, g_ref, w_ref, sc_ref, o_ref, st_ref, carry_ref, *, tm, padf, cg):
    i = pl.program_id(1)

    @pl.when(i == 0)
    def _():
        carry_ref[...] = jnp.zeros_like(carry_ref)

    x = x_ref[0]
    row = i * tm + lax.broadcasted_iota(jnp.int32, (tm, 1), 0)
    h = jnp.where(row >= padf, _rms(x, g_ref[...]), 0.0)
    ext = jnp.concatenate([carry_ref[...], h], axis=0)
    pos = row - padf
    for g, w in enumerate(POOL_WINDOWS):
        sl = slice(g * cg, (g + 1) * cg)
        a = ext[:, sl]
        k = 1
        while k < w:
            n = a.shape[0]
            a = a[:n - k] + a[k:]
            k *= 2
        win = a[POOL_HALO + 1 - w: POOL_HALO + 1 - w + tm]
        cnt = jnp.clip(pos + 1, 1, w).astype(F32)
        d = win / cnt - h[:, sl]
        y = _dot(d.astype(BF16), w_ref[g])
        o_ref[0, :, sl] = x[:, sl] + y * sc_ref[:, sl]
    carry_ref[...] = h[tm - POOL_HALO:]
    st_ref[0] = h[tm - POOL_HALO:]


def _pool_prompt(x, g, w, sc, *, tp):
    b, _, d = x.shape
    tm = _row_tile(tp, POOL_TM_CAP)
    cg = d // len(POOL_WINDOWS)
    kern = functools.partial(_pool_prompt_kernel, tm=tm, padf=FRONT_PAD, cg=cg)
    return pl.pallas_call(
        kern,
        grid=(b, tp // tm),
        in_specs=[
            pl.BlockSpec((1, tm, d), lambda bi, i: (bi, i, 0)),
            pl.BlockSpec((1, d), lambda bi, i: (0, 0)),
            pl.BlockSpec((len(POOL_WINDOWS), cg, cg), lambda bi, i: (0, 0, 0)),
            pl.BlockSpec((1, d), lambda bi, i: (0, 0)),
        ],
        out_specs=[
            pl.BlockSpec((1, tm, d), lambda bi, i: (bi, i, 0)),
            pl.BlockSpec((1, POOL_HALO, d), lambda bi, i: (bi, 0, 0)),
        ],
        out_shape=[
            jax.ShapeDtypeStruct((b, tp, d), F32),
            jax.ShapeDtypeStruct((b, POOL_HALO, d), F32),
        ],
        scratch_shapes=[pltpu.VMEM((POOL_HALO, d), F32)],
        compiler_params=_params(("arbitrary", "arbitrary")),
        name="pool_prompt",
    )(x, g, w, sc)


def _pool_sample_kernel(x_ref, pre_ref, g_ref, w_ref, sc_ref, o_ref, st_ref, *, tb, d, cg, nt, start):
    hs = [_rms(x_ref[t], g_ref[...]) for t in range(nt)]
    ext = [pre_ref[:, j * d:(j + 1) * d] for j in range(POOL_STATE)] + hs
    for g, w in enumerate(POOL_WINDOWS):
        sl = slice(g * cg, (g + 1) * cg)
        ds = []
        for t in range(nt):
            e = POOL_STATE + t
            acc = ext[e][:, sl]
            for q in range(1, w):
                acc = acc + ext[e - q][:, sl]
            cnt = float(min(w, start + t + 1))
            ds.append(acc / cnt - hs[t][:, sl])
        y = _dot(jnp.concatenate(ds, axis=0).astype(BF16), w_ref[g])
        for t in range(nt):
            o_ref[t, :, sl] = x_ref[t][:, sl] + y[t * tb:(t + 1) * tb] * sc_ref[:, sl]
    for j in range(POOL_STATE):
        st_ref[:, j * d:(j + 1) * d] = ext[nt + j]


def _pool_sample(x, pre, g, w, sc, *, nt, bs):
    d = x.shape[-1]
    tb = min(bs, 32)
    cg = d // len(POOL_WINDOWS)
    kern = functools.partial(_pool_sample_kernel, tb=tb, d=d, cg=cg, nt=nt, start=PAST_LEN)
    return pl.pallas_call(
        kern,
        grid=(bs // tb,),
        in_specs=[
            pl.BlockSpec((nt, tb, d), lambda i: (0, i, 0)),
            pl.BlockSpec((tb, POOL_STATE * d), lambda i: (i, 0)),
            pl.BlockSpec((1, d), lambda i: (0, 0)),
            pl.BlockSpec((len(POOL_WINDOWS), cg, cg), lambda i: (0, 0, 0)),
            pl.BlockSpec((1, d), lambda i: (0, 0)),
        ],
        out_specs=[
            pl.BlockSpec((nt, tb, d), lambda i: (0, i, 0)),
            pl.BlockSpec((tb, POOL_STATE * d), lambda i: (i, 0)),
        ],
        out_shape=[
            jax.ShapeDtypeStruct((nt, bs, d), F32),
            jax.ShapeDtypeStruct((bs, POOL_STATE * d), F32),
        ],
        compiler_params=_params(("arbitrary",)),
        name="pool_sample",
    )(x, pre, g, w, sc)


def _ffn_prompt_kernel(x_ref, g_ref, wv_ref, wg_ref, cwv_ref, cwg_ref, cbv_ref, cbg_ref, wo_ref, go_ref,
                       o_ref, stv_ref, stg_ref, hb_ref, carv_ref, carg_ref, *, tm, tps, padf, nj, final_norm):
    i = pl.program_id(0)
    j = pl.program_id(1)
    ti = i % tps

    @pl.when(j == 0)
    def _():
        x = x_ref[...]
        row = ti * tm + lax.broadcasted_iota(jnp.int32, (tm, 1), 0)
        h = jnp.where(row >= padf, _rms(x, g_ref[...]), 0.0)
        hb_ref[...] = h.astype(BF16)
        o_ref[...] = x

    hb = hb_ref[...]
    keep = ti != 0

    def branch(w_ref, cw_ref, cb_ref, car_ref, st_ref):
        u = _dot(hb, w_ref[...])
        prev = jnp.where(keep, car_ref[j], 0.0)
        ext = jnp.concatenate([prev, u], axis=0)
        cw = cw_ref[...]
        c = (cb_ref[...] + ext[CONV_HALO - 2:CONV_HALO - 2 + tm] * cw[0:1]
             + ext[CONV_HALO - 1:CONV_HALO - 1 + tm] * cw[1:2] + u * cw[2:3])
        car_ref[j] = u[tm - CONV_HALO:]
        st_ref[0] = u[tm - CONV_HALO:]
        return c

    cv = branch(wv_ref, cwv_ref, cbv_ref, carv_ref, stv_ref)
    cgate = branch(wg_ref, cwg_ref, cbg_ref, carg_ref, stg_ref)
    act = cgate * jax.nn.sigmoid(cgate) * cv
    o_ref[...] += _dot(act.astype(BF16), wo_ref[...])

    if final_norm:
        @pl.when(j == nj - 1)
        def _():
            o_ref[...] = _rms(o_ref[...], go_ref[...])


def _ffn_prompt(x, g, w_in, cw, cb, w_out, g_out, *, tp, final_norm):
    rows, d = x.shape
    fp = w_out.shape[0]
    tf = FFN_TF
    nj = fp // tf
    tm = _row_tile(tp, FFN_TM_CAP)
    tps = tp // tm
    kern = functools.partial(_ffn_prompt_kernel, tm=tm, tps=tps, padf=FRONT_PAD, nj=nj, final_norm=final_norm)
    return pl.pallas_call(
        kern,
        grid=(rows // tm, nj),
        in_specs=[
            pl.BlockSpec((tm, d), lambda i, j: (i, 0)),
            pl.BlockSpec((1, d), lambda i, j: (0, 0)),
            pl.BlockSpec((d, tf), lambda i, j: (0, j)),
            pl.BlockSpec((d, tf), lambda i, j: (0, nj + j)),
            pl.BlockSpec((CONV_WIDTH, tf), lambda i, j: (0, j)),
            pl.BlockSpec((CONV_WIDTH, tf), lambda i, j: (0, nj + j)),
            pl.BlockSpec((1, tf), lambda i, j: (0, j)),
            pl.BlockSpec((1, tf), lambda i, j: (0, nj + j)),
            pl.BlockSpec((tf, d), lambda i, j: (j, 0)),
            pl.BlockSpec((1, d), lambda i, j: (0, 0)),
        ],
        out_specs=[
            pl.BlockSpec((tm, d), lambda i, j: (i, 0)),
            pl.BlockSpec((1, CONV_HALO, tf), lambda i, j: (i, 0, j)),
            pl.BlockSpec((1, CONV_HALO, tf), lambda i, j: (i, 0, j)),
        ],
        out_shape=[
            jax.ShapeDtypeStruct((rows, d), F32),
            jax.ShapeDtypeStruct((rows // tm, CONV_HALO, fp), F32),
            jax.ShapeDtypeStruct((rows // tm, CONV_HALO, fp), F32),
        ],
        scratch_shapes=[
            pltpu.VMEM((tm, d), BF16),
            pltpu.VMEM((nj, CONV_HALO, tf), F32),
            pltpu.VMEM((nj, CONV_HALO, tf), F32),
        ],
        compiler_params=_params(("arbitrary", "arbitrary")),
        name="ffn_prompt",
    )(x, g, w_in, w_in, cw, cw, cb, cb, w_out, g_out)


def _ffn_sample_kernel(x_ref, g_ref, wv_ref, wg_ref, cwv_ref, cwg_ref, cbv_ref, cbg_ref, wo_ref, go_ref,
                       p0v_ref, p0g_ref, p1v_ref, p1g_ref,
                       o_ref, s0v_ref, s0g_ref, s1v_ref, s1g_ref, hb_ref, *, bs, nt, nj, final_norm):
    j = pl.program_id(0)

    @pl.when(j == 0)
    def _():
        x = x_ref[...]
        hb_ref[...] = _rms(x, g_ref[...]).astype(BF16)
        o_ref[...] = x

    hb = hb_ref[...]

    def branch(w_ref, cw_ref, cb_ref, p0_ref, p1_ref, s0_ref, s1_ref):
        u = _dot(hb, w_ref[...])
        ext = [p0_ref[...], p1_ref[...]] + [u[t * bs:(t + 1) * bs] for t in range(nt)]
        cw = cw_ref[...]
        cs = [cb_ref[...] + ext[t] * cw[0:1] + ext[t + 1] * cw[1:2] + ext[t + 2] * cw[2:3] for t in range(nt)]
        s0_ref[...] = ext[nt]
        s1_ref[...] = ext[nt + 1]
        return jnp.concatenate(cs, axis=0)

    cv = branch(wv_ref, cwv_ref, cbv_ref, p0v_ref, p1v_ref, s0v_ref, s1v_ref)
    cgate = branch(wg_ref, cwg_ref, cbg_ref, p0g_ref, p1g_ref, s0g_ref, s1g_ref)
    act = cgate * jax.nn.sigmoid(cgate) * cv
    o_ref[...] += _dot(act.astype(BF16), wo_ref[...])

    if final_norm:
        @pl.when(j == nj - 1)
        def _():
            o_ref[...] = _rms(o_ref[...], go_ref[...])


def _ffn_sample(x, g, w_in, cw, cb, w_out, g_out, st, *, bs, nt, final_norm):
    rows, d = x.shape
    fp = w_out.shape[0]
    tf = FFN_TF
    nj = fp // tf
    kern = functools.partial(_ffn_sample_kernel, bs=bs, nt=nt, nj=nj, final_norm=final_norm)
    st_spec = lambda q: pl.BlockSpec((bs, tf), lambda j: (0, q * nj + j))
    return pl.pallas_call(
        kern,
        grid=(nj,),
        in_specs=[
            pl.BlockSpec((rows, d), lambda j: (0, 0)),
            pl.BlockSpec((1, d), lambda j: (0, 0)),
            pl.BlockSpec((d, tf), lambda j: (0, j)),
            pl.BlockSpec((d, tf), lambda j: (0, nj + j)),
            pl.BlockSpec((CONV_WIDTH, tf), lambda j: (0, j)),
            pl.BlockSpec((CONV_WIDTH, tf), lambda j: (0, nj + j)),
            pl.BlockSpec((1, tf), lambda j: (0, j)),
            pl.BlockSpec((1, tf), lambda j: (0, nj + j)),
            pl.BlockSpec((tf, d), lambda j: (j, 0)),
            pl.BlockSpec((1, d), lambda j: (0, 0)),
            st_spec(0), st_spec(1), st_spec(2), st_spec(3),
        ],
        out_specs=[pl.BlockSpec((rows, d), lambda j: (0, 0))] + [pl.BlockSpec((bs, tf), lambda j: (0, j))] * 4,
        out_shape=[jax.ShapeDtypeStruct((rows, d), F32)] + [jax.ShapeDtypeStruct((bs, fp), F32)] * 4,
        scratch_shapes=[pltpu.VMEM((rows, d), BF16)],
        compiler_params=_params(("arbitrary",)),
        name="ffn_sample",
    )(x, g, w_in, w_in, cw, cw, cb, cb, w_out, g_out, st, st, st, st)


def _proj_kernel(*refs, tm, tps, padf, sample, bs, has_vlora):
    it = iter(refs)
    x_ref = next(it)
    sh_ref = next(it) if sample else None
    g_ref, mu_ref, wr_ref, wk_ref, wv_ref, w1_ref, a1_ref, g1_ref = (next(it) for _ in range(8))
    w2_ref, a2_ref, g2_ref, w0_ref, a0_ref, kk_ref, ka_ref, bd_ref = (next(it) for _ in range(8))
    if has_vlora:
        v1_ref, v2_ref, v0_ref, vf_ref = (next(it) for _ in range(4))
    r_o, ld_o, k_o, v_o, kn_o, a_o, g_o, hl_o = (next(it) for _ in range(8))
    xr_s, xk_s, xv_s, lw_s, la_s, lg_s = (next(it) for _ in range(6))
    lv_s = next(it) if has_vlora else None
    car_s = None if sample else next(it)

    i = pl.program_id(0)
    j = pl.program_id(1)

    @pl.when(j == 0)
    def _():
        x = x_ref[...]
        h = _rms(x, g_ref[...])
        if sample:
            prev = jnp.concatenate([sh_ref[...], h[:tm - bs]], axis=0)
            hl_o[...] = h[tm - bs:]
        else:
            ti = i % tps
            rloc = lax.broadcasted_iota(jnp.int32, (tm, 1), 0)
            h = jnp.where(ti * tm + rloc >= padf, h, 0.0)
            last = jnp.where(ti != 0, car_s[CONV_HALO - 1:CONV_HALO, :], 0.0)
            prev = jnp.where(rloc == 0, last, pltpu.roll(h, 1, 0))
            car_s[...] = h[tm - CONV_HALO:]
            hl_o[0] = h[tm - CONV_HALO:]
        xx = prev - h
        mu = mu_ref[...]
        mix = lambda q: (h + xx * mu[q:q + 1]).astype(BF16)
        xr_s[...] = mix(0)
        xk_s[...] = mix(2)
        xv = mix(3)
        xv_s[...] = xv
        lw_s[...] = jnp.tanh(_dot(mix(1), w1_ref[...])).astype(BF16)
        la_s[...] = _dot(mix(4), a1_ref[...]).astype(BF16)
        lg_s[...] = jax.nn.sigmoid(_dot(mix(5), g1_ref[...])).astype(BF16)
        if has_vlora:
            lv_s[...] = _dot(xv, v1_ref[...]).astype(BF16)

    r = _dot(xr_s[...], wr_ref[...])
    k = _dot(xk_s[...], wk_ref[...])
    v = _dot(xv_s[...], wv_ref[...])
    z = w0_ref[...] + _dot(lw_s[...], w2_ref[...])
    ld_o[...] = -math.exp(-0.5) * jax.nn.sigmoid(z)
    a = jax.nn.sigmoid(a0_ref[...] + _dot(la_s[...], a2_ref[...]))
    if has_vlora:
        v = v + (vf_ref[...] - v) * jax.nn.sigmoid(v0_ref[...] + _dot(lv_s[...], v2_ref[...]))
    g_o[...] = _dot(lg_s[...], g2_ref[...])
    kk = k * kk_ref[...]
    sq_hi, sq_lo = _split(kk * kk)
    ss = _dot(sq_hi, bd_ref[...]) + _dot(sq_lo, bd_ref[...])
    kn_o[...] = kk / jnp.maximum(jnp.sqrt(ss), 1e-12)
    r_o[...] = r
    k_o[...] = k * (1.0 + (a - 1.0) * ka_ref[...])
    v_o[...] = v
    a_o[...] = a


def _proj(x, sh, p, vl, vfirst, *, tp, sample, bs):
    rows, d = x.shape
    tn = min(PROJ_TN, d)
    nj = d // tn
    if sample:
        tm, tps = rows, 1
    else:
        tm = _row_tile(tp, PROJ_TM_CAP)
        tps = tp // tm
    has_vlora = vl is not None
    kern = functools.partial(_proj_kernel, tm=tm, tps=tps, padf=FRONT_PAD, sample=sample, bs=bs,
                             has_vlora=has_vlora)
    full = lambda a: pl.BlockSpec(a.shape, lambda i, j: (0,) * a.ndim)
    colb = lambda a: pl.BlockSpec((a.shape[0], tn), lambda i, j: (0, j))
    rowb = pl.BlockSpec((tm, tn), lambda i, j: (i, j))
    args, specs = [x], [pl.BlockSpec((tm, d), lambda i, j: (i, 0))]
    if sample:
        args.append(sh)
        specs.append(full(sh))
    for name in ("g", "mu"):
        args.append(p[name]); specs.append(full(p[name]))
    for name in ("wr", "wk", "wv"):
        args.append(p[name]); specs.append(colb(p[name]))
    for name in ("w1", "a1", "g1"):
        args.append(p[name]); specs.append(full(p[name]))
    for name in ("w2", "a2", "g2", "w0", "a0", "kk", "ka"):
        args.append(p[name]); specs.append(colb(p[name]))
    args.append(p["bd"]); specs.append(full(p["bd"]))
    if has_vlora:
        args += [vl["v1"], vl["v2"], vl["v0"], vfirst]
        specs += [full(vl["v1"]), colb(vl["v2"]), colb(vl["v0"]), rowb]
    out_shape = [jax.ShapeDtypeStruct((rows, d), F32)] * 7
    out_specs = [rowb] * 7
    if sample:
        out_shape.append(jax.ShapeDtypeStruct((bs, d), F32))
        out_specs.append(pl.BlockSpec((bs, d), lambda i, j: (0, 0)))
    else:
        nb = rows // tp
        out_shape.append(jax.ShapeDtypeStruct((nb, CONV_HALO, d), F32))
        out_specs.append(pl.BlockSpec((1, CONV_HALO, d), lambda i, j: (i // tps, 0, 0)))
    lw, la, lg = p["w1"].shape[1], p["a1"].shape[1], p["g1"].shape[1]
    scratch = [pltpu.VMEM((tm, d), BF16)] * 3 + [pltpu.VMEM((tm, lw), BF16), pltpu.VMEM((tm, la), BF16),
                                                 pltpu.VMEM((tm, lg), BF16)]
    if has_vlora:
        scratch.append(pltpu.VMEM((tm, vl["v1"].shape[1]), BF16))
    if not sample:
        scratch.append(pltpu.VMEM((CONV_HALO, d), F32))
    return pl.pallas_call(
        kern,
        grid=(rows // tm, nj),
        in_specs=specs,
        out_specs=out_specs,
        out_shape=out_shape,
        scratch_shapes=scratch,
        compiler_params=_params(("arbitrary", "arbitrary")),
        name="rwkv_proj_sample" if sample else "rwkv_proj_prompt",
    )(*args)


def _mm(a, b):
    return _dot(a.astype(BF16), b.astype(BF16))


def _mm_nt(a, b):
    return lax.dot_general(a.astype(BF16), b.astype(BF16), (((1,), (1,)), ((), ())),
                           preferred_element_type=F32)


def _mm3(a, b):
    ah, al = _split(a)
    bh, bl = _split(b)
    return _dot(ah, bh) + _dot(ah, bl) + _dot(al, bh)


def _wkv_kernel(r_ref, ld_ref, k_ref, v_ref, kn_ref, a_ref, g_ref, s0_ref, lnw_ref, lnb_ref, rk_ref,
                z_ref, so_ref, *, bb, L, nchunks):
    hs = HEAD_SIZE
    lane = lax.broadcasted_iota(jnp.int32, (1, LANES), 1)
    head0 = lane < hs
    m0 = head0.astype(F32)
    m1 = 1.0 - m0
    ri = lax.broadcasted_iota(jnp.int32, (LANES, LANES), 0)
    ci = lax.broadcasted_iota(jnp.int32, (LANES, LANES), 1)
    bdmask = ((ri < hs) == (ci < hs)).astype(F32)
    rl = lax.broadcasted_iota(jnp.int32, (L, L), 0)
    cl = lax.broadcasted_iota(jnp.int32, (L, L), 1)
    tril_incl = (cl <= rl).astype(BF16)
    eye = (cl == rl).astype(F32)
    prow = lax.broadcasted_iota(jnp.int32, (4 * L, 2 * L), 0)
    pcol = lax.broadcasted_iota(jnp.int32, (4 * L, 2 * L), 1) % L
    pmask = jnp.where(prow >= 2 * L, (pcol <= prow % L).astype(F32), (pcol < prow % L).astype(F32))
    kcols = (lax.broadcasted_iota(jnp.int32, (1, 2 * L), 1) >= L).astype(F32)
    lnw, lnb, rk = lnw_ref[...], lnb_ref[...], rk_ref[...]
    nsteps = int(math.log2(L)) - 1

    def sel(zz):
        return jnp.where(head0, zz[:L], zz[L:])

    def head_sum(y):
        s0 = jnp.sum(y * m0, axis=-1, keepdims=True)
        s1 = jnp.sum(y * m1, axis=-1, keepdims=True)
        return jnp.where(head0, s0, s1)

    def inverse(a):
        t = eye + a
        pw = a
        for _ in range(nsteps):
            pw = _mm3(pw, pw)
            t = t + _mm3(t, pw)
        return t

    def seq_body(s, carry):
        s_init = s0_ref[s]
        zed = jnp.zeros((hs, hs), F32)
        state0 = jnp.concatenate([jnp.concatenate([s_init[0], zed], axis=1),
                                  jnp.concatenate([zed, s_init[1]], axis=1)], axis=0)

        def chunk_body(c, state):
            rows = pl.ds(pl.multiple_of(c * L, L), L)
            r, ld, k, v = r_ref[s, rows, :], ld_ref[s, rows, :], k_ref[s, rows, :], v_ref[s, rows, :]
            kn, a, g = kn_ref[s, rows, :], a_ref[s, rows, :], g_ref[s, rows, :]
            ld_hi = ld.astype(BF16)
            ld_r = ld - ld_hi.astype(F32)
            ld_mid = ld_r.astype(BF16)
            ld_lo = (ld_r - ld_mid.astype(F32)).astype(BF16)
            cs = _dot(tril_incl, ld_hi) + _dot(tril_incl, ld_mid) + _dot(tril_incl, ld_lo)
            c_end = cs[L - 1:L, :]
            w_in = jnp.exp(cs)
            w_inv = jnp.exp(-cs)
            w_prev = jnp.exp(cs - ld)
            w_end = jnp.exp(c_end - cs)
            b = kn * a
            rt, kt, bt, at = r * w_in, k * w_inv, b * w_inv, -kn * w_prev
            kh, bh = k * w_end, b * w_end
            lhs = jnp.concatenate([at * m0, at * m1, rt * m0, rt * m1], axis=0)
            pm = _mm_nt(lhs, jnp.concatenate([bt, kt], axis=0)) * pmask
            a_bk = pm[:2 * L]
            r_bk = pm[2 * L:]
            gs = _mm_nt(jnp.concatenate([at, rt], axis=0), state)
            vv = jnp.concatenate([v, v], axis=0)
            x = gs[:L] + sel(_mm(a_bk * kcols, vv))
            a_b = a_bk[:, :L]
            u = jnp.where(head0, _mm3(inverse(a_b[:L]), x), _mm3(inverse(a_b[L:]), x))
            uv = jnp.concatenate([u, v], axis=0)
            y = gs[L:] + sel(_mm(r_bk, uv))
            upd = _mm(uv.T, jnp.concatenate([bh, kh], axis=0))
            state = state * jnp.exp(c_end) + upd * bdmask
            mean = head_sum(y) * (1.0 / hs)
            yc = y - mean
            var = head_sum(yc * yc) * (1.0 / hs)
            yn = yc * lax.rsqrt(var + GN_EPS) * lnw + lnb
            bonus = head_sum(r * k * rk) * v
            z_ref[s, rows, :] = ((yn + bonus) * g).astype(z_ref.dtype)
            return state

        state = lax.fori_loop(0, nchunks, chunk_body, state0)
        so_ref[s, 0] = state[:hs, :hs]
        so_ref[s, 1] = state[hs:, hs:]
        return carry

    lax.fori_loop(0, bb, seq_body, 0)


def _wkv(r, ld, k, v, kn, a, g, s0, lnw, lnb, rk, *, L, bb):
    nb, tt, d = r.shape
    nh = d // HEAD_SIZE
    kern = functools.partial(_wkv_kernel, bb=bb, L=L, nchunks=tt // L)
    seqb = pl.BlockSpec((bb, tt, LANES), lambda i, p: (i, 0, p))
    vecb = pl.BlockSpec((1, LANES), lambda i, p: (0, p))
    stb = pl.BlockSpec((bb, 2, HEAD_SIZE, HEAD_SIZE), lambda i, p: (i, p, 0, 0))
    return pl.pallas_call(
        kern,
        grid=(nb // bb, nh // 2),
        in_specs=[seqb] * 7 + [stb, vecb, vecb, vecb],
        out_specs=[seqb, stb],
        out_shape=[jax.ShapeDtypeStruct((nb, tt, d), BF16),
                   jax.ShapeDtypeStruct((nb, nh, HEAD_SIZE, HEAD_SIZE), F32)],
        compiler_params=_params(("arbitrary", "arbitrary")),
        name=f"wkv_chunk{L}",
    )(r, ld, k, v, kn, a, g, s0, lnw, lnb, rk)


def _wkv_prompt_kernel(r_ref, ld_ref, k_ref, v_ref, kn_ref, a_ref, g_ref, s0_ref, lnw_ref, lnb_ref, rk_ref,
                       z_ref, so_ref, st_ref, *, L, nchunks, npairs):
    hs = HEAD_SIZE
    ib = pl.program_id(2)
    lane = lax.broadcasted_iota(jnp.int32, (1, LANES), 1)
    head0 = lane < hs
    head0w = (lax.broadcasted_iota(jnp.int32, (1, 2 * LANES), 1) % LANES) < hs
    m0 = head0.astype(F32)
    m1 = 1.0 - m0
    ri = lax.broadcasted_iota(jnp.int32, (LANES, LANES), 0)
    ci = lax.broadcasted_iota(jnp.int32, (LANES, LANES), 1)
    bdmask = ((ri < hs) == (ci < hs)).astype(F32)
    rl = lax.broadcasted_iota(jnp.int32, (L, L), 0)
    cl = lax.broadcasted_iota(jnp.int32, (L, L), 1)
    tril_incl = (cl <= rl).astype(BF16)
    eye = (cl == rl).astype(F32)
    prow = lax.broadcasted_iota(jnp.int32, (4 * L, 2 * L), 0)
    pcol = lax.broadcasted_iota(jnp.int32, (4 * L, 2 * L), 1) % L
    pmask = jnp.where(prow >= 2 * L, (pcol <= prow % L).astype(F32), (pcol < prow % L).astype(F32))
    kcols = (lax.broadcasted_iota(jnp.int32, (1, 2 * L), 1) >= L).astype(F32)
    nsteps = int(math.log2(L)) - 1
    zeros_l = jnp.zeros((L, LANES), F32)

    @pl.when(ib == 0)
    def _():
        zed = jnp.zeros((hs, hs), F32)
        for p in range(npairs):
            st_ref[p] = jnp.concatenate([jnp.concatenate([s0_ref[0, 2 * p], zed], axis=1),
                                         jnp.concatenate([zed, s0_ref[0, 2 * p + 1]], axis=1)], axis=0)

    def head_sum(y):
        s0 = jnp.sum(y * m0, axis=-1, keepdims=True)
        s1 = jnp.sum(y * m1, axis=-1, keepdims=True)
        return jnp.where(head0, s0, s1)

    def inverse(a):
        t = eye + a
        pw = a
        for _ in range(nsteps):
            pw = _mm(pw, pw)
            t = t + _mm(t, pw)
        return t

    def chunk_body(c, carry):
        rows = pl.ds(pl.multiple_of(c * L, L), L)
        pr = range(npairs)
        lss = [slice(p * LANES, (p + 1) * LANES) for p in pr]
        ld = [ld_ref[0, rows, ls] for ls in lss]
        ld_hi = [x.astype(BF16) for x in ld]
        ld_r = [x - h.astype(F32) for x, h in zip(ld, ld_hi)]
        ld_mid = [x.astype(BF16) for x in ld_r]
        ld_lo = [(x - m.astype(F32)).astype(BF16) for x, m in zip(ld_r, ld_mid)]
        cs = [_dot(tril_incl, ld_hi[p]) + _dot(tril_incl, ld_mid[p]) + _dot(tril_incl, ld_lo[p]) for p in pr]
        c_end = [x[L - 1:L, :] for x in cs]
        r = [r_ref[0, rows, ls] for ls in lss]
        k = [k_ref[0, rows, ls] for ls in lss]
        v = [v_ref[0, rows, ls] for ls in lss]
        kn = [kn_ref[0, rows, ls] for ls in lss]
        b = [kn[p] * a_ref[0, rows, lss[p]] for p in pr]
        w_inv = [jnp.exp(-x) for x in cs]
        rt = [r[p] * jnp.exp(cs[p]) for p in pr]
        kt = [k[p] * w_inv[p] for p in pr]
        bt = [b[p] * w_inv[p] for p in pr]
        at = [-kn[p] * jnp.exp(cs[p] - ld[p]) for p in pr]
        w_end = [jnp.exp(c_end[p] - cs[p]) for p in pr]
        kh = [k[p] * w_end[p] for p in pr]
        bh = [b[p] * w_end[p] for p in pr]
        pm = [_mm_nt(jnp.concatenate([at[p] * m0, at[p] * m1, rt[p] * m0, rt[p] * m1], axis=0),
                     jnp.concatenate([bt[p], kt[p]], axis=0)) * pmask for p in pr]
        a_bk = [x[:2 * L] for x in pm]
        r_bk = [x[2 * L:] for x in pm]
        akv = [_mm(a_bk[p] * kcols, jnp.concatenate([v[p], v[p]], axis=0)) for p in pr]
        akv = [jnp.where(head0, x[:L], x[L:]) for x in akv]
        pw = [a_bk[p][h * L:(h + 1) * L, :L] for p in pr for h in range(2)]
        tinv = [eye + x for x in pw]
        for _ in range(nsteps):
            pw = [_mm(x, x) for x in pw]
            tinv = [t + _mm(t, x) for t, x in zip(tinv, pw)]
        tz_rhs = [jnp.concatenate([at[p], akv[p]], axis=1) for p in pr]
        tz = [jnp.where(head0w, _mm(tinv[2 * p], tz_rhs[p]), _mm(tinv[2 * p + 1], tz_rhs[p])) for p in pr]
        ry = [_mm(r_bk[p], jnp.concatenate([tz[p], jnp.concatenate([zeros_l, v[p]], axis=1)], axis=0))
              for p in pr]
        ry = [jnp.where(head0w, x[:L], x[L:]) for x in ry]
        r2 = [rt[p] + ry[p][:, :LANES] for p in pr]
        n_mat = [_mm(tz[p][:, :LANES].T, bh[p]) * bdmask for p in pr]
        c_mat = [_mm(jnp.concatenate([tz[p][:, LANES:], v[p]], axis=0).T,
                     jnp.concatenate([bh[p], kh[p]], axis=0)) * bdmask for p in pr]
        state = [st_ref[p] for p in pr]
        y = [_mm_nt(r2[p], state[p]) + ry[p][:, LANES:] for p in pr]
        for p in pr:
            st_ref[p] = state[p] * jnp.exp(c_end[p]) + _mm(state[p], n_mat[p]) + c_mat[p]
        mean = [head_sum(x) * (1.0 / hs) for x in y]
        yc = [y[p] - mean[p] for p in pr]
        var = [head_sum(x * x) * (1.0 / hs) for x in yc]
        bonus = [head_sum(r[p] * k[p] * rk_ref[:, lss[p]]) * v[p] for p in pr]
        for p in pr:
            yn = yc[p] * lax.rsqrt(var[p] + GN_EPS) * lnw_ref[:, lss[p]] + lnb_ref[:, lss[p]]
            z_ref[0, rows, lss[p]] = ((yn + bonus[p]) * g_ref[0, rows, lss[p]]).astype(z_ref.dtype)
        return carry

    lax.fori_loop(0, nchunks, chunk_body, 0)

    @pl.when(ib == pl.num_programs(2) - 1)
    def _():
        for p in range(npairs):
            state = st_ref[p]
            so_ref[0, 2 * p] = state[:hs, :hs]
            so_ref[0, 2 * p + 1] = state[hs:, hs:]


def _wkv_prompt(r, ld, k, v, kn, a, g, s0, lnw, lnb, rk):
    nb, tt, d = r.shape
    nh = d // HEAD_SIZE
    npairs = WKV_PAIRS
    tr = _row_tile(tt, WKV_ROWS_CAP, CHUNK)
    lw = npairs * LANES
    kern = functools.partial(_wkv_prompt_kernel, L=CHUNK, nchunks=tr // CHUNK, npairs=npairs)
    seqb = pl.BlockSpec((1, tr, lw), lambda bi, p, i: (bi, i, p))
    vecb = pl.BlockSpec((1, lw), lambda bi, p, i: (0, p))
    stb = pl.BlockSpec((1, 2 * npairs, HEAD_SIZE, HEAD_SIZE), lambda bi, p, i: (bi, p, 0, 0))
    return pl.pallas_call(
        kern,
        grid=(nb, nh // (2 * npairs), tt // tr),
        in_specs=[seqb] * 7 + [stb, vecb, vecb, vecb],
        out_specs=[seqb, stb],
        out_shape=[jax.ShapeDtypeStruct((nb, tt, d), BF16),
                   jax.ShapeDtypeStruct((nb, nh, HEAD_SIZE, HEAD_SIZE), F32)],
        scratch_shapes=[pltpu.VMEM((npairs, LANES, LANES), F32)],
        compiler_params=_params(("arbitrary", "arbitrary", "arbitrary")),
        name="wkv_prompt",
    )(r, ld, k, v, kn, a, g, s0, lnw, lnb, rk)


def _wkv_sample_kernel(r_ref, ld_ref, k_ref, v_ref, kn_ref, a_ref, g_ref, s0_ref, lnw_ref, lnb_ref, rk_ref,
                       z_ref, so_ref, *, ngroups, npairs):
    hs, L, T = HEAD_SIZE, CHUNK, SAMPLE_TOK
    ns = L // T
    lane = lax.broadcasted_iota(jnp.int32, (1, LANES), 1)
    head0 = lane < hs
    head0w = (lax.broadcasted_iota(jnp.int32, (1, 2 * LANES), 1) % LANES) < hs
    m0 = head0.astype(F32)
    m1 = 1.0 - m0
    ri = lax.broadcasted_iota(jnp.int32, (LANES, LANES), 0)
    ci = lax.broadcasted_iota(jnp.int32, (LANES, LANES), 1)
    bdmask = ((ri < hs) == (ci < hs)).astype(F32)
    eye_w = (ri == ci).astype(BF16)
    rl = lax.broadcasted_iota(jnp.int32, (L, L), 0)
    cl = lax.broadcasted_iota(jnp.int32, (L, L), 1)
    same = (rl // T) == (cl // T)
    cum_lhs = jnp.concatenate([jnp.where(same, (cl % T <= rl % T).astype(F32), 0.0),
                               same.astype(F32)], axis=0).astype(BF16)
    eye = (cl == rl).astype(F32)
    prow = lax.broadcasted_iota(jnp.int32, (4 * L, 2 * L), 0)
    pcol = lax.broadcasted_iota(jnp.int32, (4 * L, 2 * L), 1) % L
    psame = ((prow % L) // T) == (pcol // T)
    pcaus = jnp.where(prow >= 2 * L, (pcol % T <= prow % T).astype(F32), (pcol % T < prow % T).astype(F32))
    pmask = jnp.where(psame, pcaus, 0.0)
    kcols = (lax.broadcasted_iota(jnp.int32, (1, 2 * L), 1) >= L).astype(F32)
    nsteps = int(math.log2(T)) - 1
    zeros_l = jnp.zeros((L, LANES), F32)
    zeros_t = jnp.zeros((T, LANES), F32)
    zed = jnp.zeros((hs, hs), F32)

    def head_sum(y):
        s0 = jnp.sum(y * m0, axis=-1, keepdims=True)
        s1 = jnp.sum(y * m1, axis=-1, keepdims=True)
        return jnp.where(head0, s0, s1)

    def group_body(gi, carry):
        seqs = pl.ds(pl.multiple_of(gi * ns, ns), ns)
        pr = range(npairs)
        lss = [slice(p * LANES, (p + 1) * LANES) for p in pr]
        tile = lambda ref, ls: ref[seqs, :, ls].reshape(L, LANES)
        ld = [tile(ld_ref, ls) for ls in lss]
        ld_hi = [x.astype(BF16) for x in ld]
        ld_r = [x - h.astype(F32) for x, h in zip(ld, ld_hi)]
        ld_mid = [x.astype(BF16) for x in ld_r]
        ld_lo = [(x - m.astype(F32)).astype(BF16) for x, m in zip(ld_r, ld_mid)]
        cc = [_dot(cum_lhs, ld_hi[p]) + _dot(cum_lhs, ld_mid[p]) + _dot(cum_lhs, ld_lo[p]) for p in pr]
        cs = [x[:L] for x in cc]
        c_end = [x[L:] for x in cc]
        r = [tile(r_ref, ls) for ls in lss]
        k = [tile(k_ref, ls) for ls in lss]
        v = [tile(v_ref, ls) for ls in lss]
        kn = [tile(kn_ref, ls) for ls in lss]
        b = [kn[p] * tile(a_ref, lss[p]) for p in pr]
        w_inv = [jnp.exp(-x) for x in cs]
        rt = [r[p] * jnp.exp(cs[p]) for p in pr]
        kt = [k[p] * w_inv[p] for p in pr]
        bt = [b[p] * w_inv[p] for p in pr]
        at = [-kn[p] * jnp.exp(cs[p] - ld[p]) for p in pr]
        w_end = [jnp.exp(c_end[p] - cs[p]) for p in pr]
        w_tot = [jnp.exp(x) for x in c_end]
        kh = [k[p] * w_end[p] for p in pr]
        bh = [b[p] * w_end[p] for p in pr]
        pm = [_mm_nt(jnp.concatenate([at[p] * m0, at[p] * m1, rt[p] * m0, rt[p] * m1], axis=0),
                     jnp.concatenate([bt[p], kt[p]], axis=0)) * pmask for p in pr]
        a_bk = [x[:2 * L] for x in pm]
        r_bk = [x[2 * L:] for x in pm]
        akv = [_mm(a_bk[p] * kcols, jnp.concatenate([v[p], v[p]], axis=0)) for p in pr]
        akv = [jnp.where(head0, x[:L], x[L:]) for x in akv]
        pw = [a_bk[p][h * L:(h + 1) * L, :L] for p in pr for h in range(2)]
        tinv = [eye + x for x in pw]
        for _ in range(nsteps):
            pw = [_mm(x, x) for x in pw]
            tinv = [t + _mm(t, x) for t, x in zip(tinv, pw)]
        tz_rhs = [jnp.concatenate([at[p], akv[p]], axis=1) for p in pr]
        tz = [jnp.where(head0w, _mm(tinv[2 * p], tz_rhs[p]), _mm(tinv[2 * p + 1], tz_rhs[p])) for p in pr]
        ry = [_mm(r_bk[p], jnp.concatenate([tz[p], jnp.concatenate([zeros_l, v[p]], axis=1)], axis=0))
              for p in pr]
        ry = [jnp.where(head0w, x[:L], x[L:]) for x in ry]
        r2 = [rt[p] + ry[p][:, :LANES] for p in pr]
        ps = [(p, i) for p in pr for i in range(ns)]
        rs = [slice(i * T, (i + 1) * T) for i in range(ns)]
        state = [jnp.concatenate([jnp.concatenate([s0_ref[gi * ns + i, 2 * p], zed], axis=1),
                                  jnp.concatenate([zed, s0_ref[gi * ns + i, 2 * p + 1]], axis=1)], axis=0)
                 for p, i in ps]
        sb = [x.astype(BF16) for x in state]
        e_rhs = [jnp.concatenate([tz[p][rs[i]], jnp.concatenate([zeros_t, v[p][rs[i]]], axis=1)], axis=0)
                 for p, i in ps]
        e = [_mm_nt(jnp.concatenate([sb[q], eye_w], axis=1), e_rhs[q]) for q in range(len(ps))]
        upd = [_mm(e[q], jnp.concatenate([bh[p][rs[i]], kh[p][rs[i]]], axis=0)) * bdmask
               for q, (p, i) in enumerate(ps)]
        ys = [_mm_nt(r2[p][rs[i]], sb[q]) for q, (p, i) in enumerate(ps)]
        for q, (p, i) in enumerate(ps):
            new = state[q] * w_tot[p][i * T:i * T + 1] + upd[q]
            so_ref[gi * ns + i, 2 * p] = new[:hs, :hs]
            so_ref[gi * ns + i, 2 * p + 1] = new[hs:, hs:]
        y = [jnp.concatenate(ys[p * ns:(p + 1) * ns], axis=0) + ry[p][:, LANES:] for p in pr]
        mean = [head_sum(x) * (1.0 / hs) for x in y]
        yc = [y[p] - mean[p] for p in pr]
        var = [head_sum(x * x) * (1.0 / hs) for x in yc]
        bonus = [head_sum(r[p] * k[p] * rk_ref[:, lss[p]]) * v[p] for p in pr]
        for p in pr:
            yn = yc[p] * lax.rsqrt(var[p] + GN_EPS) * lnw_ref[:, lss[p]] + lnb_ref[:, lss[p]]
            z_ref[seqs, :, lss[p]] = ((yn + bonus[p]) * tile(g_ref, lss[p])).reshape(ns, T, LANES)
        return carry

    lax.fori_loop(0, ngroups, group_body, 0)


def _wkv_sample(r, ld, k, v, kn, a, g, s0, lnw, lnb, rk):
    nb, tt, d = r.shape
    nh = d // HEAD_SIZE
    npairs = WKV_PAIRS
    ns = CHUNK // SAMPLE_TOK
    bb = min(nb, 2 * ns)
    lw = npairs * LANES
    kern = functools.partial(_wkv_sample_kernel, ngroups=bb // ns, npairs=npairs)
    seqb = pl.BlockSpec((bb, tt, lw), lambda i, p: (i, 0, p))
    vecb = pl.BlockSpec((1, lw), lambda i, p: (0, p))
    stb = pl.BlockSpec((bb, 2 * npairs, HEAD_SIZE, HEAD_SIZE), lambda i, p: (i, p, 0, 0))
    return pl.pallas_call(
        kern,
        grid=(nb // bb, nh // (2 * npairs)),
        in_specs=[seqb] * 7 + [stb, vecb, vecb, vecb],
        out_specs=[seqb, stb],
        out_shape=[jax.ShapeDtypeStruct((nb, tt, d), F32),
                   jax.ShapeDtypeStruct((nb, nh, HEAD_SIZE, HEAD_SIZE), F32)],
        compiler_params=_params(("arbitrary", "arbitrary")),
        name="wkv_sample",
    )(r, ld, k, v, kn, a, g, s0, lnw, lnb, rk)


def _wo_kernel(x_ref, z_ref, w_ref, o_ref):
    o_ref[...] = x_ref[...] + _dot(z_ref[...].astype(BF16), w_ref[...])


def _wo(x, z, w):
    rows, d = x.shape
    tm = _row_tile(rows, WO_TM_CAP)
    return pl.pallas_call(
        _wo_kernel,
        grid=(rows // tm,),
        in_specs=[pl.BlockSpec((tm, d), lambda i: (i, 0)), pl.BlockSpec((tm, d), lambda i: (i, 0)),
                  pl.BlockSpec((d, d), lambda i: (0, 0))],
        out_specs=pl.BlockSpec((tm, d), lambda i: (i, 0)),
        out_shape=jax.ShapeDtypeStruct((rows, d), F32),
        compiler_params=_params(("arbitrary",)),
        name="rwkv_wo",
    )(x, z, w)


def _pad_cols(a, n):
    return jnp.pad(a, [(0, 0)] * (a.ndim - 1) + [(0, n - a.shape[-1])])


def _pad_rows(a, n):
    return jnp.pad(a, [(0, n - a.shape[0])] + [(0, 0)] * (a.ndim - 1))


def kernel(x_prompt, x_sample, state_pool, state_rwkv_shift, state_rwkv_wkv, state_ffn_conv, meta_tokens,
           norm_mix, norm_ffn, norm_out, pool_w, pool_scale, rwkv_mu, rwkv_wr, rwkv_wk, rwkv_wv, rwkv_wo,
           rwkv_w0, rwkv_w1, rwkv_w2, rwkv_a0, rwkv_a1, rwkv_a2, rwkv_v0, rwkv_v1, rwkv_v2, rwkv_g1, rwkv_g2,
           rwkv_kk, rwkv_ka, rwkv_rk, rwkv_lnw, rwkv_lnb, ffn_w_in, ffn_conv_w, ffn_conv_b, ffn_w_out):
    b, seq, d = x_prompt.shape
    bs, nt, _ = x_sample.shape
    depth = norm_mix.shape[0]
    f = ffn_w_out.shape[1]
    fp = _round_up(f, FFN_TF)
    nh = d // HEAD_SIZE
    tp = FRONT_PAD + N_META + seq
    assert tp % CHUNK == 0 and d % (2 * LANES) == 0 and CONV_WIDTH - 1 <= nt <= SAMPLE_TOK

    row = lambda a: a.reshape(1, -1)
    lora_in = lambda a: _pad_cols(a, _round_up(a.shape[-1], LANES)).astype(BF16)
    lora_out = lambda a: _pad_rows(a, _round_up(a.shape[0], LANES)).astype(BF16)
    ffn = []
    for i in range(depth):
        w_in = ffn_w_in[i].reshape(d, 2, f)
        ffn.append(dict(
            g=row(norm_ffn[i]),
            w_in=_pad_cols(w_in, fp).reshape(d, 2 * fp).astype(BF16),
            cw=_pad_cols(ffn_conv_w[i].reshape(CONV_WIDTH, 2, f), fp).reshape(CONV_WIDTH, 2 * fp),
            cb=_pad_cols(ffn_conv_b[i].reshape(1, 2, f), fp).reshape(1, 2 * fp),
            w_out=_pad_rows(ffn_w_out[i], fp).astype(BF16)))
    tn = min(PROJ_TN, d)
    bd = jnp.kron(jnp.eye(tn // HEAD_SIZE, dtype=F32), jnp.ones((HEAD_SIZE, HEAD_SIZE), F32)).astype(BF16)
    rw = []
    for j in range(depth // 2):
        rw.append(dict(
            g=row(norm_mix[2 * j + 1]), mu=rwkv_mu[j],
            wr=rwkv_wr[j].astype(BF16), wk=rwkv_wk[j].astype(BF16), wv=rwkv_wv[j].astype(BF16),
            w1=lora_in(rwkv_w1[j]), a1=lora_in(rwkv_a1[j]), g1=lora_in(rwkv_g1[j]),
            w2=lora_out(rwkv_w2[j]), a2=lora_out(rwkv_a2[j]), g2=lora_out(rwkv_g2[j]),
            w0=row(rwkv_w0[j]), a0=row(rwkv_a0[j]), kk=row(rwkv_kk[j]), ka=row(rwkv_ka[j]), bd=bd,
            wo=rwkv_wo[j].astype(BF16), lnw=row(rwkv_lnw[j]), lnb=row(rwkv_lnb[j]), rk=row(rwkv_rk[j])))
    vls = [None] + [dict(v1=lora_in(rwkv_v1[j]), v2=lora_out(rwkv_v2[j]), v0=row(rwkv_v0[j]))
                    for j in range(depth // 2 - 1)]
    pw = [pool_w[j].astype(BF16) for j in range((depth + 1) // 2)]
    g_out = row(norm_out)

    xp = jnp.concatenate([jnp.zeros((b, FRONT_PAD, d), F32),
                          jnp.broadcast_to(meta_tokens[None], (b, N_META, d)), x_prompt], axis=1)
    xs = x_sample.transpose(1, 0, 2)
    zero_wkv = jnp.zeros((b, nh, HEAD_SIZE, HEAD_SIZE), F32)

    pool_p, pool_s, shift_p, shift_s, wkv_p, wkv_s, conv_p, conv_s = [], [], [], [], [], [], [], []
    vfirst_p = vfirst_s = None
    for i in range(depth):
        j = i // 2
        if i % 2 == 0:
            xp, st = _pool_prompt(xp, row(norm_mix[i]), pw[j], row(pool_scale[j]), tp=tp)
            pool_p.append(st[:, POOL_HALO - POOL_STATE:])
            xs, st = _pool_sample(xs, state_pool[j].reshape(bs, POOL_STATE * d), row(norm_mix[i]), pw[j],
                                  row(pool_scale[j]), nt=nt, bs=bs)
            pool_s.append(st.reshape(bs, POOL_STATE, d))
        else:
            p = rw[j]
            r, ld, k, v, kn, a, g, hl = _proj(xp.reshape(b * tp, d), None, p, vls[j], vfirst_p,
                                              tp=tp, sample=False, bs=0)
            if vfirst_p is None:
                vfirst_p = v
            shift_p.append(hl[:, CONV_HALO - 1])
            sq = lambda t: t.reshape(b, tp, d)
            z, s_new = _wkv_prompt(sq(r), sq(ld), sq(k), sq(v), sq(kn), sq(a), sq(g), zero_wkv,
                                   p["lnw"], p["lnb"], p["rk"])
            wkv_p.append(s_new)
            xp = _wo(xp.reshape(b * tp, d), z.reshape(b * tp, d), p["wo"]).reshape(b, tp, d)
            r, ld, k, v, kn, a, g, hl = _proj(xs.reshape(nt * bs, d), state_rwkv_shift[j], p, vls[j], vfirst_s,
                                              tp=0, sample=True, bs=bs)
            if vfirst_s is None:
                vfirst_s = v
            shift_s.append(hl)
            sq = lambda t: jnp.pad(t.reshape(nt, bs, d).transpose(1, 0, 2),
                                   ((0, 0), (0, SAMPLE_TOK - nt), (0, 0)))
            z, s_new = _wkv_sample(sq(r), sq(ld), sq(k), sq(v), sq(kn), sq(a), sq(g), state_rwkv_wkv[j],
                                   p["lnw"], p["lnb"], p["rk"])
            wkv_s.append(s_new)
            z = z[:, :nt].transpose(1, 0, 2).reshape(nt * bs, d)
            xs = _wo(xs.reshape(nt * bs, d), z, p["wo"]).reshape(nt, bs, d)
        fi = ffn[i]
        last = i == depth - 1
        xo, stv, stg = _ffn_prompt(xp.reshape(b * tp, d), fi["g"], fi["w_in"], fi["cw"], fi["cb"], fi["w_out"],
                                   g_out, tp=tp, final_norm=last)
        xp = xo.reshape(b, tp, d)
        stv, stg = (t.reshape(b, -1, CONV_HALO, fp)[:, -1] for t in (stv, stg))
        conv_p.append(jnp.concatenate([stv[:, CONV_HALO - 2:, :f], stg[:, CONV_HALO - 2:, :f]], axis=-1))
        st_in = _pad_cols(state_ffn_conv[i].reshape(bs, CONV_WIDTH - 1, 2, f), fp).reshape(bs, -1)
        xo, s0v, s0g, s1v, s1g = _ffn_sample(xs.reshape(nt * bs, d), fi["g"], fi["w_in"], fi["cw"], fi["cb"],
                                             fi["w_out"], g_out, st_in, bs=bs, nt=nt, final_norm=last)
        xs = xo.reshape(nt, bs, d)
        conv_s.append(jnp.stack([jnp.concatenate([s0v[:, :f], s0g[:, :f]], axis=-1),
                                 jnp.concatenate([s1v[:, :f], s1g[:, :f]], axis=-1)], axis=1))

    y_prompt = xp[:, FRONT_PAD + N_META:]
    y_sample = xs.transpose(1, 0, 2)
    return (y_prompt, y_sample, jnp.stack(pool_p), jnp.stack(pool_s), jnp.stack(shift_p), jnp.stack(shift_s),
            jnp.stack(wkv_p), jnp.stack(wkv_s), jnp.stack(conv_p), jnp.stack(conv_s))
```

```python
import functools
import math

import jax
import jax.numpy as jnp
from jax import lax
from jax.experimental import pallas as pl
from jax.experimental.pallas import tpu as pltpu

F32 = jnp.float32
BF16 = jnp.bfloat16

HEAD_SIZE = 64
LANES = 128
N_META = 16
PAST_LEN = 16384
POOL_WINDOWS = (2, 4, 8, 16)
POOL_STATE = max(POOL_WINDOWS) - 1
POOL_HALO = 16
CONV_WIDTH = 3
CONV_HALO = 8
NORM_EPS = 1e-6
GN_EPS = 64e-5
CHUNK = 64
SAMPLE_CHUNK = 16
SAMPLE_TOK = 8
FRONT_PAD = CHUNK - N_META

FFN_TM_CAP = 704
PROJ_TM_CAP = 352
POOL_TM_CAP = 704
WO_TM_CAP = 704
WKV_ROWS_CAP = 192
WKV_PAIRS = 8
CAST_ROWS = 512
FFN_TF = 512
PROJ_TN = 512
PROJ_TN_SAMPLE = 256
VMEM_LIMIT = 56 * 1024 * 1024


def _row_tile(n, cap, mult=16):
    best = None
    for t in range(mult, min(n, cap) + 1, mult):
        if n % t == 0:
            best = t
    assert best is not None, (n, cap)
    return best


def _round_up(n, m):
    return (n + m - 1) // m * m


def _params(sem):
    return pltpu.CompilerParams(dimension_semantics=sem, vmem_limit_bytes=VMEM_LIMIT)


def _rms(x, g):
    return x * lax.rsqrt(jnp.mean(x * x, axis=-1, keepdims=True) + NORM_EPS) * g


def _dot(a, b):
    return jnp.dot(a, b, preferred_element_type=F32)


def _split(x):
    hi = x.astype(BF16)
    lo = (x - hi.astype(F32)).astype(BF16)
    return hi, lo


def _cast_kernel(x_ref, o_ref, *, valid_rows, tr):
    x = x_ref[...]
    if valid_rows is not None:
        row = pl.program_id(1) * tr + lax.broadcasted_iota(jnp.int32, (1, tr, 1), 1)
        x = jnp.where(row < valid_rows, x, 0.0)
    o_ref[...] = x.astype(o_ref.dtype)


def _cast_rows(w, out_rows):
    n, rows, cols = w.shape
    tr = CAST_ROWS
    kern = functools.partial(_cast_kernel, valid_rows=None if rows % tr == 0 else rows, tr=tr)
    return pl.pallas_call(
        kern,
        grid=(n, out_rows // tr),
        in_specs=[pl.BlockSpec((1, tr, cols), lambda l, i: (l, i, 0))],
        out_specs=pl.BlockSpec((1, tr, cols), lambda l, i: (l, i, 0)),
        out_shape=jax.ShapeDtypeStruct((n, out_rows, cols), BF16),
        compiler_params=_params(("arbitrary", "arbitrary")),
        name="cast_rows",
    )(w)


def _cast_halves_kernel(x_ref, o_ref, *, f, fp):
    x = x_ref[0]
    pad = jnp.zeros((x.shape[0], fp - f), o_ref.dtype)
    o_ref[0, :, :f] = x[:, :f].astype(o_ref.dtype)
    o_ref[0, :, fp:fp + f] = x[:, f:].astype(o_ref.dtype)
    if fp > f:
        o_ref[0, :, f:fp] = pad
        o_ref[0, :, fp + f:] = pad


def _cast_halves(w, fp):
    n, rows, f2 = w.shape
    f = f2 // 2
    tr = min(rows, CAST_ROWS // 2)
    return pl.pallas_call(
        functools.partial(_cast_halves_kernel, f=f, fp=fp),
        grid=(n, rows // tr),
        in_specs=[pl.BlockSpec((1, tr, f2), lambda l, i: (l, i, 0))],
        out_specs=pl.BlockSpec((1, tr, 2 * fp), lambda l, i: (l, i, 0)),
        out_shape=jax.ShapeDtypeStruct((n, rows, 2 * fp), BF16),
        compiler_params=_params(("arbitrary", "arbitrary")),
        name="cast_halves",
    )(w)


def _pool_prompt_kernel(x_ref, g_ref, w_ref, sc_ref, o_ref, st_ref, carry_ref, *, tm, padf, cg):
    i = pl.program_id(1)

    @pl.when(i == 0)
    def _():
        carry_ref[...] = jnp.zeros_like(carry_ref)

    x = x_ref[0]
    row = i * tm + lax.broadcasted_iota(jnp.int32, (tm, 1), 0)
    h = jnp.where(row >= padf, _rms(x, g_ref[...]), 0.0)
    ext = jnp.concatenate([carry_ref[...], h], axis=0)
    pos = row - padf
    for g, w in enumerate(POOL_WINDOWS):
        sl = slice(g * cg, (g + 1) * cg)
        a = ext[:, sl]
        k = 1
        while k < w:
            n = a.shape[0]
            a = a[:n - k] + a[k:]
            k *= 2
        win = a[POOL_HALO + 1 - w: POOL_HALO + 1 - w + tm]
        cnt = jnp.clip(pos + 1, 1, w).astype(F32)
        d = win / cnt - h[:, sl]
        y = _dot(d.astype(BF16), w_ref[g])
        o_ref[0, :, sl] = x[:, sl] + y * sc_ref[:, sl]
    carry_ref[...] = h[tm - POOL_HALO:]
    st_ref[0] = h[tm - POOL_HALO:]


def _pool_prompt(x, g, w, sc, *, tp):
    b, _, d = x.shape
    tm = _row_tile(tp, POOL_TM_CAP)
    cg = d // len(POOL_WINDOWS)
    kern = functools.partial(_pool_prompt_kernel, tm=tm, padf=FRONT_PAD, cg=cg)
    return pl.pallas_call(
        kern,
        grid=(b, tp // tm),
        in_specs=[
            pl.BlockSpec((1, tm, d), lambda bi, i: (bi, i, 0)),
            pl.BlockSpec((1, d), lambda bi, i: (0, 0)),
            pl.BlockSpec((len(POOL_WINDOWS), cg, cg), lambda bi, i: (0, 0, 0)),
            pl.BlockSpec((1, d), lambda bi, i: (0, 0)),
        ],
        out_specs=[
            pl.BlockSpec((1, tm, d), lambda bi, i: (bi, i, 0)),
            pl.BlockSpec((1, POOL_HALO, d), lambda bi, i: (bi, 0, 0)),
        ],
        out_shape=[
            jax.ShapeDtypeStruct((b, tp, d), F32),
            jax.ShapeDtypeStruct((b, POOL_HALO, d), F32),
        ],
        scratch_shapes=[pltpu.VMEM((POOL_HALO, d), F32)],
        compiler_params=_params(("arbitrary", "arbitrary")),
        name="pool_prompt",
    )(x, g, w, sc)


def _pool_sample_kernel(x_ref, pre_ref, g_ref, w_ref, sc_ref, o_ref, st_ref, *, tb, d, cg, nt, start):
    hs = [_rms(x_ref[t], g_ref[...]) for t in range(nt)]
    ext = [pre_ref[:, j, :] for j in range(POOL_STATE)] + hs
    for g, w in enumerate(POOL_WINDOWS):
        sl = slice(g * cg, (g + 1) * cg)
        ds = []
        for t in range(nt):
            e = POOL_STATE + t
            acc = ext[e][:, sl]
            for q in range(1, w):
                acc = acc + ext[e - q][:, sl]
            cnt = float(min(w, start + t + 1))
            ds.append(acc / cnt - hs[t][:, sl])
        y = _dot(jnp.concatenate(ds, axis=0).astype(BF16), w_ref[g])
        for t in range(nt):
            o_ref[t, :, sl] = x_ref[t][:, sl] + y[t * tb:(t + 1) * tb] * sc_ref[:, sl]
    for j in range(POOL_STATE):
        st_ref[:, j, :] = ext[nt + j]


def _pool_sample(x, pre, g, w, sc, *, nt, bs):
    d = x.shape[-1]
    tb = min(bs, 32)
    cg = d // len(POOL_WINDOWS)
    kern = functools.partial(_pool_sample_kernel, tb=tb, d=d, cg=cg, nt=nt, start=PAST_LEN)
    return pl.pallas_call(
        kern,
        grid=(bs // tb,),
        in_specs=[
            pl.BlockSpec((nt, tb, d), lambda i: (0, i, 0)),
            pl.BlockSpec((tb, POOL_STATE, d), lambda i: (i, 0, 0)),
            pl.BlockSpec((1, d), lambda i: (0, 0)),
            pl.BlockSpec((len(POOL_WINDOWS), cg, cg), lambda i: (0, 0, 0)),
            pl.BlockSpec((1, d), lambda i: (0, 0)),
        ],
        out_specs=[
            pl.BlockSpec((nt, tb, d), lambda i: (0, i, 0)),
            pl.BlockSpec((tb, POOL_STATE, d), lambda i: (i, 0, 0)),
        ],
        out_shape=[
            jax.ShapeDtypeStruct((nt, bs, d), F32),
            jax.ShapeDtypeStruct((bs, POOL_STATE, d), F32),
        ],
        compiler_params=_params(("arbitrary",)),
        name="pool_sample",
    )(x, pre, g, w, sc)


def _ffn_prompt_kernel(x_ref, g_ref, wv_ref, wg_ref, cwv_ref, cwg_ref, cbv_ref, cbg_ref, wo_ref, go_ref,
                       o_ref, stv_ref, stg_ref, hb_ref, carv_ref, carg_ref, *, tm, tps, padf, nj, final_norm):
    i = pl.program_id(0)
    j = pl.program_id(1)
    ti = i % tps

    @pl.when(j == 0)
    def _():
        x = x_ref[...]
        row = ti * tm + lax.broadcasted_iota(jnp.int32, (tm, 1), 0)
        h = jnp.where(row >= padf, _rms(x, g_ref[...]), 0.0)
        hb_ref[...] = h.astype(BF16)
        o_ref[...] = x

    hb = hb_ref[...]
    keep = ti != 0

    def branch(w_ref, cw_ref, cb_ref, car_ref, st_ref):
        u = _dot(hb, w_ref[...])
        prev = jnp.where(keep, car_ref[j], 0.0)
        ext = jnp.concatenate([prev, u], axis=0)
        cw = cw_ref[...]
        c = (cb_ref[...] + ext[CONV_HALO - 2:CONV_HALO - 2 + tm] * cw[0:1]
             + ext[CONV_HALO - 1:CONV_HALO - 1 + tm] * cw[1:2] + u * cw[2:3])
        car_ref[j] = u[tm - CONV_HALO:]
        st_ref[0] = u[tm - CONV_HALO:]
        return c

    cv = branch(wv_ref, cwv_ref, cbv_ref, carv_ref, stv_ref)
    cgate = branch(wg_ref, cwg_ref, cbg_ref, carg_ref, stg_ref)
    act = cgate * jax.nn.sigmoid(cgate) * cv
    o_ref[...] += _dot(act.astype(BF16), wo_ref[...])

    if final_norm:
        @pl.when(j == nj - 1)
        def _():
            o_ref[...] = _rms(o_ref[...], go_ref[...])


def _ffn_prompt(x, g, w_in, cw, cb, w_out, g_out, *, layer, tp, final_norm):
    rows, d = x.shape
    fp = w_out.shape[1]
    tf = FFN_TF
    nj = fp // tf
    tm = _row_tile(tp, FFN_TM_CAP)
    tps = tp // tm
    kern = functools.partial(_ffn_prompt_kernel, tm=tm, tps=tps, padf=FRONT_PAD, nj=nj, final_norm=final_norm)
    return pl.pallas_call(
        kern,
        grid=(rows // tm, nj),
        in_specs=[
            pl.BlockSpec((tm, d), lambda i, j: (i, 0)),
            pl.BlockSpec((1, d), lambda i, j: (0, 0)),
            pl.BlockSpec((None, d, tf), lambda i, j: (layer, 0, j)),
            pl.BlockSpec((None, d, tf), lambda i, j: (layer, 0, nj + j)),
            pl.BlockSpec((CONV_WIDTH, tf), lambda i, j: (0, j)),
            pl.BlockSpec((CONV_WIDTH, tf), lambda i, j: (0, nj + j)),
            pl.BlockSpec((1, tf), lambda i, j: (0, j)),
            pl.BlockSpec((1, tf), lambda i, j: (0, nj + j)),
            pl.BlockSpec((None, tf, d), lambda i, j: (layer, j, 0)),
            pl.BlockSpec((1, d), lambda i, j: (0, 0)),
        ],
        out_specs=[
            pl.BlockSpec((tm, d), lambda i, j: (i, 0)),
            pl.BlockSpec((1, CONV_HALO, tf), lambda i, j: (i, 0, j)),
            pl.BlockSpec((1, CONV_HALO, tf), lambda i, j: (i, 0, j)),
        ],
        out_shape=[
            jax.ShapeDtypeStruct((rows, d), F32),
            jax.ShapeDtypeStruct((rows // tm, CONV_HALO, fp), F32),
            jax.ShapeDtypeStruct((rows // tm, CONV_HALO, fp), F32),
        ],
        scratch_shapes=[
            pltpu.VMEM((tm, d), BF16),
            pltpu.VMEM((nj, CONV_HALO, tf), F32),
            pltpu.VMEM((nj, CONV_HALO, tf), F32),
        ],
        compiler_params=_params(("arbitrary", "arbitrary")),
        name="ffn_prompt",
    )(x, g, w_in, w_in, cw, cw, cb, cb, w_out, g_out)


def _ffn_sample_kernel(x_ref, g_ref, wv_ref, wg_ref, cwv_ref, cwg_ref, cbv_ref, cbg_ref, wo_ref, go_ref,
                       pv_ref, pg_ref, o_ref, sv_ref, sg_ref, hb_ref, *, bs, nt, nj, tf, f, final_norm):
    j = pl.program_id(0)

    @pl.when(j == 0)
    def _():
        x = x_ref[...]
        hb_ref[...] = _rms(x, g_ref[...]).astype(BF16)
        o_ref[...] = x

    hb = hb_ref[...]
    valid = j * tf + lax.broadcasted_iota(jnp.int32, (1, tf), 1) < f

    def branch(w_ref, cw_ref, cb_ref, p_ref, s_ref):
        u = _dot(hb, w_ref[...])
        prev = [jnp.where(valid, p_ref[:, q, :], 0.0) for q in range(CONV_WIDTH - 1)]
        ext = prev + [u[t * bs:(t + 1) * bs] for t in range(nt)]
        cw = cw_ref[...]
        cs = [cb_ref[...] + ext[t] * cw[0:1] + ext[t + 1] * cw[1:2] + ext[t + 2] * cw[2:3] for t in range(nt)]
        for q in range(CONV_WIDTH - 1):
            s_ref[:, q, :] = ext[nt + q]
        return jnp.concatenate(cs, axis=0)

    cv = branch(wv_ref, cwv_ref, cbv_ref, pv_ref, sv_ref)
    cgate = branch(wg_ref, cwg_ref, cbg_ref, pg_ref, sg_ref)
    act = cgate * jax.nn.sigmoid(cgate) * cv
    o_ref[...] += _dot(act.astype(BF16), wo_ref[...])

    if final_norm:
        @pl.when(j == nj - 1)
        def _():
            o_ref[...] = _rms(o_ref[...], go_ref[...])


def _ffn_sample(x, g, w_in, cw, cb, w_out, g_out, st_val, st_gate, *, layer, bs, nt, final_norm):
    rows, d = x.shape
    fp = w_out.shape[1]
    f = st_val.shape[-1]
    tf = FFN_TF
    nj = fp // tf
    kern = functools.partial(_ffn_sample_kernel, bs=bs, nt=nt, nj=nj, tf=tf, f=f, final_norm=final_norm)
    st_spec = pl.BlockSpec((bs, CONV_WIDTH - 1, tf), lambda j: (0, 0, j))
    return pl.pallas_call(
        kern,
        grid=(nj,),
        in_specs=[
            pl.BlockSpec((rows, d), lambda j: (0, 0)),
            pl.BlockSpec((1, d), lambda j: (0, 0)),
            pl.BlockSpec((None, d, tf), lambda j: (layer, 0, j)),
            pl.BlockSpec((None, d, tf), lambda j: (layer, 0, nj + j)),
            pl.BlockSpec((CONV_WIDTH, tf), lambda j: (0, j)),
            pl.BlockSpec((CONV_WIDTH, tf), lambda j: (0, nj + j)),
            pl.BlockSpec((1, tf), lambda j: (0, j)),
            pl.BlockSpec((1, tf), lambda j: (0, nj + j)),
            pl.BlockSpec((None, tf, d), lambda j: (layer, j, 0)),
            pl.BlockSpec((1, d), lambda j: (0, 0)),
            st_spec, st_spec,
        ],
        out_specs=[pl.BlockSpec((rows, d), lambda j: (0, 0)), st_spec, st_spec],
        out_shape=[jax.ShapeDtypeStruct((rows, d), F32)]
        + [jax.ShapeDtypeStruct((bs, CONV_WIDTH - 1, fp), F32)] * 2,
        scratch_shapes=[pltpu.VMEM((rows, d), BF16)],
        compiler_params=_params(("arbitrary",)),
        name="ffn_sample",
    )(x, g, w_in, w_in, cw, cw, cb, cb, w_out, g_out, st_val, st_gate)


def _proj_kernel(*refs, tm, tps, padf, sample, bs, has_vlora):
    it = iter(refs)
    x_ref = next(it)
    sh_ref = next(it) if sample else None
    g_ref, mu_ref, wr_ref, wk_ref, wv_ref, w1_ref, a1_ref, g1_ref = (next(it) for _ in range(8))
    w2_ref, a2_ref, g2_ref, w0_ref, a0_ref, kk_ref, ka_ref, bd_ref = (next(it) for _ in range(8))
    if has_vlora:
        v1_ref, v2_ref, v0_ref, vf_ref = (next(it) for _ in range(4))
    r_o, ld_o, k_o, v_o, kn_o, a_o, g_o, hl_o = (next(it) for _ in range(8))
    xr_s, xk_s, xv_s, lw_s, la_s, lg_s = (next(it) for _ in range(6))
    lv_s = next(it) if has_vlora else None
    car_s = None if sample else next(it)

    i = pl.program_id(0)
    j = pl.program_id(1)

    @pl.when(j == 0)
    def _():
        x = x_ref[...]
        h = _rms(x, g_ref[...])
        if sample:
            prev = jnp.concatenate([sh_ref[...], h[:tm - bs]], axis=0)
            hl_o[...] = h[tm - bs:]
        else:
            ti = i % tps
            rloc = lax.broadcasted_iota(jnp.int32, (tm, 1), 0)
            h = jnp.where(ti * tm + rloc >= padf, h, 0.0)
            last = jnp.where(ti != 0, car_s[CONV_HALO - 1:CONV_HALO, :], 0.0)
            prev = jnp.where(rloc == 0, last, pltpu.roll(h, 1, 0))
            car_s[...] = h[tm - CONV_HALO:]
            hl_o[0] = h[tm - CONV_HALO:]
        xx = prev - h
        mu = mu_ref[...]
        mix = lambda q: (h + xx * mu[q:q + 1]).astype(BF16)
        xr_s[...] = mix(0)
        xk_s[...] = mix(2)
        xv = mix(3)
        xv_s[...] = xv
        lw_s[...] = jnp.tanh(_dot(mix(1), w1_ref[...])).astype(BF16)
        la_s[...] = _dot(mix(4), a1_ref[...]).astype(BF16)
        lg_s[...] = jax.nn.sigmoid(_dot(mix(5), g1_ref[...])).astype(BF16)
        if has_vlora:
            lv_s[...] = _dot(xv, v1_ref[...]).astype(BF16)

    r = _dot(xr_s[...], wr_ref[...])
    k = _dot(xk_s[...], wk_ref[...])
    v = _dot(xv_s[...], wv_ref[...])
    z = w0_ref[...] + _dot(lw_s[...], w2_ref[...])
    a = jax.nn.sigmoid(a0_ref[...] + _dot(la_s[...], a2_ref[...]))
    nt = tm // bs if sample else 0

    def put(o_ref, val):
        if sample:
            for t in range(SAMPLE_TOK):
                o_ref[:, t, :] = val[t * bs:(t + 1) * bs] if t < nt else jnp.zeros((bs, val.shape[1]), F32)
        else:
            o_ref[...] = val

    put(ld_o, -math.exp(-0.5) * jax.nn.sigmoid(z))
    if has_vlora:
        vf = jnp.concatenate([vf_ref[:, t, :] for t in range(nt)], axis=0) if sample else vf_ref[...]
        v = v + (vf - v) * jax.nn.sigmoid(v0_ref[...] + _dot(lv_s[...], v2_ref[...]))
    put(g_o, _dot(lg_s[...], g2_ref[...]))
    kk = k * kk_ref[...]
    sq_hi, sq_lo = _split(kk * kk)
    ss = _dot(sq_hi, bd_ref[...]) + _dot(sq_lo, bd_ref[...])
    put(kn_o, kk / jnp.maximum(jnp.sqrt(ss), 1e-12))
    put(r_o, r)
    put(k_o, k * (1.0 + (a - 1.0) * ka_ref[...]))
    put(v_o, v)
    put(a_o, a)


def _proj(x, sh, p, vl, vfirst, *, layer, tp, sample, bs):
    rows, d = x.shape
    tn = min(PROJ_TN_SAMPLE if sample else PROJ_TN, d)
    nj = d // tn
    p = dict(p, bd=p["bd"][:tn, :tn])
    if sample:
        tm, tps = rows, 1
    else:
        tm = _row_tile(tp, PROJ_TM_CAP)
        tps = tp // tm
    has_vlora = vl is not None
    kern = functools.partial(_proj_kernel, tm=tm, tps=tps, padf=FRONT_PAD, sample=sample, bs=bs,
                             has_vlora=has_vlora)
    full = lambda a: pl.BlockSpec(a.shape, lambda i, j: (0,) * a.ndim)
    colb = lambda a: pl.BlockSpec((a.shape[0], tn), lambda i, j: (0, j))
    if sample:
        rowb = pl.BlockSpec((bs, SAMPLE_TOK, tn), lambda i, j: (0, 0, j))
        row_shape = jax.ShapeDtypeStruct((bs, SAMPLE_TOK, d), F32)
    else:
        rowb = pl.BlockSpec((tm, tn), lambda i, j: (i, j))
        row_shape = jax.ShapeDtypeStruct((rows, d), F32)
    args, specs = [x], [pl.BlockSpec((tm, d), lambda i, j: (i, 0))]
    if sample:
        args.append(sh)
        specs.append(full(sh))
    for name in ("g", "mu"):
        args.append(p[name]); specs.append(full(p[name]))
    for name in ("wr", "wk", "wv"):
        args.append(p[name]); specs.append(pl.BlockSpec((None, d, tn), lambda i, j: (layer, 0, j)))
    for name in ("w1", "a1", "g1"):
        args.append(p[name]); specs.append(full(p[name]))
    for name in ("w2", "a2", "g2", "w0", "a0", "kk", "ka"):
        args.append(p[name]); specs.append(colb(p[name]))
    args.append(p["bd"]); specs.append(full(p["bd"]))
    if has_vlora:
        args += [vl["v1"], vl["v2"], vl["v0"], vfirst]
        specs += [full(vl["v1"]), colb(vl["v2"]), colb(vl["v0"]), rowb]
    out_shape = [row_shape] * 7
    out_specs = [rowb] * 7
    if sample:
        out_shape.append(jax.ShapeDtypeStruct((bs, d), F32))
        out_specs.append(pl.BlockSpec((bs, d), lambda i, j: (0, 0)))
    else:
        nb = rows // tp
        out_shape.append(jax.ShapeDtypeStruct((nb, CONV_HALO, d), F32))
        out_specs.append(pl.BlockSpec((1, CONV_HALO, d), lambda i, j: (i // tps, 0, 0)))
    lw, la, lg = p["w1"].shape[1], p["a1"].shape[1], p["g1"].shape[1]
    scratch = [pltpu.VMEM((tm, d), BF16)] * 3 + [pltpu.VMEM((tm, lw), BF16), pltpu.VMEM((tm, la), BF16),
                                                 pltpu.VMEM((tm, lg), BF16)]
    if has_vlora:
        scratch.append(pltpu.VMEM((tm, vl["v1"].shape[1]), BF16))
    if not sample:
        scratch.append(pltpu.VMEM((CONV_HALO, d), F32))
    return pl.pallas_call(
        kern,
        grid=(rows // tm, nj),
        in_specs=specs,
        out_specs=out_specs,
        out_shape=out_shape,
        scratch_shapes=scratch,
        compiler_params=_params(("arbitrary", "arbitrary")),
        name="rwkv_proj_sample" if sample else "rwkv_proj_prompt",
    )(*args)


def _mm(a, b):
    return _dot(a.astype(BF16), b.astype(BF16))


def _mm_nt(a, b):
    return lax.dot_general(a.astype(BF16), b.astype(BF16), (((1,), (1,)), ((), ())),
                           preferred_element_type=F32)


def _mm3(a, b):
    ah, al = _split(a)
    bh, bl = _split(b)
    return _dot(ah, bh) + _dot(ah, bl) + _dot(al, bh)


def _wkv_kernel(r_ref, ld_ref, k_ref, v_ref, kn_ref, a_ref, g_ref, s0_ref, lnw_ref, lnb_ref, rk_ref,
                z_ref, so_ref, *, bb, L, nchunks):
    hs = HEAD_SIZE
    lane = lax.broadcasted_iota(jnp.int32, (1, LANES), 1)
    head0 = lane < hs
    m0 = head0.astype(F32)
    m1 = 1.0 - m0
    ri = lax.broadcasted_iota(jnp.int32, (LANES, LANES), 0)
    ci = lax.broadcasted_iota(jnp.int32, (LANES, LANES), 1)
    bdmask = ((ri < hs) == (ci < hs)).astype(F32)
    rl = lax.broadcasted_iota(jnp.int32, (L, L), 0)
    cl = lax.broadcasted_iota(jnp.int32, (L, L), 1)
    tril_incl = (cl <= rl).astype(BF16)
    eye = (cl == rl).astype(F32)
    prow = lax.broadcasted_iota(jnp.int32, (4 * L, 2 * L), 0)
    pcol = lax.broadcasted_iota(jnp.int32, (4 * L, 2 * L), 1) % L
    pmask = jnp.where(prow >= 2 * L, (pcol <= prow % L).astype(F32), (pcol < prow % L).astype(F32))
    kcols = (lax.broadcasted_iota(jnp.int32, (1, 2 * L), 1) >= L).astype(F32)
    lnw, lnb, rk = lnw_ref[...], lnb_ref[...], rk_ref[...]
    nsteps = int(math.log2(L)) - 1

    def sel(zz):
        return jnp.where(head0, zz[:L], zz[L:])

    def head_sum(y):
        s0 = jnp.sum(y * m0, axis=-1, keepdims=True)
        s1 = jnp.sum(y * m1, axis=-1, keepdims=True)
        return jnp.where(head0, s0, s1)

    def inverse(a):
        t = eye + a
        pw = a
        for _ in range(nsteps):
            pw = _mm3(pw, pw)
            t = t + _mm3(t, pw)
        return t

    def seq_body(s, carry):
        s_init = s0_ref[s]
        zed = jnp.zeros((hs, hs), F32)
        state0 = jnp.concatenate([jnp.concatenate([s_init[0], zed], axis=1),
                                  jnp.concatenate([zed, s_init[1]], axis=1)], axis=0)

        def chunk_body(c, state):
            rows = pl.ds(pl.multiple_of(c * L, L), L)
            r, ld, k, v = r_ref[s, rows, :], ld_ref[s, rows, :], k_ref[s, rows, :], v_ref[s, rows, :]
            kn, a, g = kn_ref[s, rows, :], a_ref[s, rows, :], g_ref[s, rows, :]
            ld_hi = ld.astype(BF16)
            ld_r = ld - ld_hi.astype(F32)
            ld_mid = ld_r.astype(BF16)
            ld_lo = (ld_r - ld_mid.astype(F32)).astype(BF16)
            cs = _dot(tril_incl, ld_hi) + _dot(tril_incl, ld_mid) + _dot(tril_incl, ld_lo)
            c_end = cs[L - 1:L, :]
            w_in = jnp.exp(cs)
            w_inv = jnp.exp(-cs)
            w_prev = jnp.exp(cs - ld)
            w_end = jnp.exp(c_end - cs)
            b = kn * a
            rt, kt, bt, at = r * w_in, k * w_inv, b * w_inv, -kn * w_prev
            kh, bh = k * w_end, b * w_end
            lhs = jnp.concatenate([at * m0, at * m1, rt * m0, rt * m1], axis=0)
            pm = _mm_nt(lhs, jnp.concatenate([bt, kt], axis=0)) * pmask
            a_bk = pm[:2 * L]
            r_bk = pm[2 * L:]
            gs = _mm_nt(jnp.concatenate([at, rt], axis=0), state)
            vv = jnp.concatenate([v, v], axis=0)
            x = gs[:L] + sel(_mm(a_bk * kcols, vv))
            a_b = a_bk[:, :L]
            u = jnp.where(head0, _mm3(inverse(a_b[:L]), x), _mm3(inverse(a_b[L:]), x))
            uv = jnp.concatenate([u, v], axis=0)
            y = gs[L:] + sel(_mm(r_bk, uv))
            upd = _mm(uv.T, jnp.concatenate([bh, kh], axis=0))
            state = state * jnp.exp(c_end) + upd * bdmask
            mean = head_sum(y) * (1.0 / hs)
            yc = y - mean
            var = head_sum(yc * yc) * (1.0 / hs)
            yn = yc * lax.rsqrt(var + GN_EPS) * lnw + lnb
            bonus = head_sum(r * k * rk) * v
            z_ref[s, rows, :] = ((yn + bonus) * g).astype(z_ref.dtype)
            return state

        state = lax.fori_loop(0, nchunks, chunk_body, state0)
        so_ref[s, 0] = state[:hs, :hs]
        so_ref[s, 1] = state[hs:, hs:]
        return carry

    lax.fori_loop(0, bb, seq_body, 0)


def _wkv(r, ld, k, v, kn, a, g, s0, lnw, lnb, rk, *, L, bb):
    nb, tt, d = r.shape
    nh = d // HEAD_SIZE
    kern = functools.partial(_wkv_kernel, bb=bb, L=L, nchunks=tt // L)
    seqb = pl.BlockSpec((bb, tt, LANES), lambda i, p: (i, 0, p))
    vecb = pl.BlockSpec((1, LANES), lambda i, p: (0, p))
    stb = pl.BlockSpec((bb, 2, HEAD_SIZE, HEAD_SIZE), lambda i, p: (i, p, 0, 0))
    return pl.pallas_call(
        kern,
        grid=(nb // bb, nh // 2),
        in_specs=[seqb] * 7 + [stb, vecb, vecb, vecb],
        out_specs=[seqb, stb],
        out_shape=[jax.ShapeDtypeStruct((nb, tt, d), BF16),
                   jax.ShapeDtypeStruct((nb, nh, HEAD_SIZE, HEAD_SIZE), F32)],
        compiler_params=_params(("arbitrary", "arbitrary")),
        name=f"wkv_chunk{L}",
    )(r, ld, k, v, kn, a, g, s0, lnw, lnb, rk)


def _wkv_prompt_kernel(r_ref, ld_ref, k_ref, v_ref, kn_ref, a_ref, g_ref, s0_ref, lnw_ref, lnb_ref, rk_ref,
                       z_ref, so_ref, st_ref, *, L, nchunks, npairs):
    hs = HEAD_SIZE
    ib = pl.program_id(2)
    lane = lax.broadcasted_iota(jnp.int32, (1, LANES), 1)
    head0 = lane < hs
    head0w = (lax.broadcasted_iota(jnp.int32, (1, 2 * LANES), 1) % LANES) < hs
    m0 = head0.astype(F32)
    m1 = 1.0 - m0
    ri = lax.broadcasted_iota(jnp.int32, (LANES, LANES), 0)
    ci = lax.broadcasted_iota(jnp.int32, (LANES, LANES), 1)
    bdmask = ((ri < hs) == (ci < hs)).astype(F32)
    rl = lax.broadcasted_iota(jnp.int32, (L, L), 0)
    cl = lax.broadcasted_iota(jnp.int32, (L, L), 1)
    tril_incl = (cl <= rl).astype(BF16)
    eye = (cl == rl).astype(F32)
    prow = lax.broadcasted_iota(jnp.int32, (4 * L, 2 * L), 0)
    pcol = lax.broadcasted_iota(jnp.int32, (4 * L, 2 * L), 1) % L
    pmask = jnp.where(prow >= 2 * L, (pcol <= prow % L).astype(F32), (pcol < prow % L).astype(F32))
    kcols = (lax.broadcasted_iota(jnp.int32, (1, 2 * L), 1) >= L).astype(F32)
    nsteps = int(math.log2(L)) - 1
    zeros_l = jnp.zeros((L, LANES), F32)

    @pl.when(ib == 0)
    def _():
        zed = jnp.zeros((hs, hs), F32)
        for p in range(npairs):
            st_ref[p] = jnp.concatenate([jnp.concatenate([s0_ref[0, 2 * p], zed], axis=1),
                                         jnp.concatenate([zed, s0_ref[0, 2 * p + 1]], axis=1)], axis=0)

    def head_sum(y):
        s0 = jnp.sum(y * m0, axis=-1, keepdims=True)
        s1 = jnp.sum(y * m1, axis=-1, keepdims=True)
        return jnp.where(head0, s0, s1)

    def inverse(a):
        t = eye + a
        pw = a
        for _ in range(nsteps):
            pw = _mm(pw, pw)
            t = t + _mm(t, pw)
        return t

    def chunk_body(c, carry):
        rows = pl.ds(pl.multiple_of(c * L, L), L)
        pr = range(npairs)
        lss = [slice(p * LANES, (p + 1) * LANES) for p in pr]
        ld = [ld_ref[0, rows, ls] for ls in lss]
        ld_hi = [x.astype(BF16) for x in ld]
        ld_r = [x - h.astype(F32) for x, h in zip(ld, ld_hi)]
        ld_mid = [x.astype(BF16) for x in ld_r]
        ld_lo = [(x - m.astype(F32)).astype(BF16) for x, m in zip(ld_r, ld_mid)]
        cs = [_dot(tril_incl, ld_hi[p]) + _dot(tril_incl, ld_mid[p]) + _dot(tril_incl, ld_lo[p]) for p in pr]
        c_end = [x[L - 1:L, :] for x in cs]
        r = [r_ref[0, rows, ls] for ls in lss]
        k = [k_ref[0, rows, ls] for ls in lss]
        v = [v_ref[0, rows, ls] for ls in lss]
        kn = [kn_ref[0, rows, ls] for ls in lss]
        b = [kn[p] * a_ref[0, rows, lss[p]] for p in pr]
        w_inv = [jnp.exp(-x) for x in cs]
        rt = [r[p] * jnp.exp(cs[p]) for p in pr]
        kt = [k[p] * w_inv[p] for p in pr]
        bt = [b[p] * w_inv[p] for p in pr]
        at = [-kn[p] * jnp.exp(cs[p] - ld[p]) for p in pr]
        w_end = [jnp.exp(c_end[p] - cs[p]) for p in pr]
        kh = [k[p] * w_end[p] for p in pr]
        bh = [b[p] * w_end[p] for p in pr]
        pm = [_mm_nt(jnp.concatenate([at[p] * m0, at[p] * m1, rt[p] * m0, rt[p] * m1], axis=0),
                     jnp.concatenate([bt[p], kt[p]], axis=0)) * pmask for p in pr]
        a_bk = [x[:2 * L] for x in pm]
        r_bk = [x[2 * L:] for x in pm]
        akv = [_mm(a_bk[p] * kcols, jnp.concatenate([v[p], v[p]], axis=0)) for p in pr]
        akv = [jnp.where(head0, x[:L], x[L:]) for x in akv]
        pw = [a_bk[p][h * L:(h + 1) * L, :L] for p in pr for h in range(2)]
        tinv = [eye + x for x in pw]
        for _ in range(nsteps):
            pw = [_mm(x, x) for x in pw]
            tinv = [t + _mm(t, x) for t, x in zip(tinv, pw)]
        tz_rhs = [jnp.concatenate([at[p], akv[p]], axis=1) for p in pr]
        tz = [jnp.where(head0w, _mm(tinv[2 * p], tz_rhs[p]), _mm(tinv[2 * p + 1], tz_rhs[p])) for p in pr]
        ry = [_mm(r_bk[p], jnp.concatenate([tz[p], jnp.concatenate([zeros_l, v[p]], axis=1)], axis=0))
              for p in pr]
        ry = [jnp.where(head0w, x[:L], x[L:]) for x in ry]
        r2 = [rt[p] + ry[p][:, :LANES] for p in pr]
        n_mat = [_mm(tz[p][:, :LANES].T, bh[p]) * bdmask for p in pr]
        c_mat = [_mm(jnp.concatenate([tz[p][:, LANES:], v[p]], axis=0).T,
                     jnp.concatenate([bh[p], kh[p]], axis=0)) * bdmask for p in pr]
        state = [st_ref[p] for p in pr]
        y = [_mm_nt(r2[p], state[p]) + ry[p][:, LANES:] for p in pr]
        for p in pr:
            st_ref[p] = state[p] * jnp.exp(c_end[p]) + _mm(state[p], n_mat[p]) + c_mat[p]
        mean = [head_sum(x) * (1.0 / hs) for x in y]
        yc = [y[p] - mean[p] for p in pr]
        var = [head_sum(x * x) * (1.0 / hs) for x in yc]
        bonus = [head_sum(r[p] * k[p] * rk_ref[:, lss[p]]) * v[p] for p in pr]
        for p in pr:
            yn = yc[p] * lax.rsqrt(var[p] + GN_EPS) * lnw_ref[:, lss[p]] + lnb_ref[:, lss[p]]
            z_ref[0, rows, lss[p]] = ((yn + bonus[p]) * g_ref[0, rows, lss[p]]).astype(z_ref.dtype)
        return carry

    lax.fori_loop(0, nchunks, chunk_body, 0)

    @pl.when(ib == pl.num_programs(2) - 1)
    def _():
        for p in range(npairs):
            state = st_ref[p]
            so_ref[0, 2 * p] = state[:hs, :hs]
            so_ref[0, 2 * p + 1] = state[hs:, hs:]


def _wkv_prompt(r, ld, k, v, kn, a, g, s0, lnw, lnb, rk):
    nb, tt, d = r.shape
    nh = d // HEAD_SIZE
    npairs = min(WKV_PAIRS, nh // 2)
    tr = _row_tile(tt, WKV_ROWS_CAP, CHUNK)
    lw = npairs * LANES
    kern = functools.partial(_wkv_prompt_kernel, L=CHUNK, nchunks=tr // CHUNK, npairs=npairs)
    seqb = pl.BlockSpec((1, tr, lw), lambda bi, p, i: (bi, i, p))
    vecb = pl.BlockSpec((1, lw), lambda bi, p, i: (0, p))
    stb = pl.BlockSpec((1, 2 * npairs, HEAD_SIZE, HEAD_SIZE), lambda bi, p, i: (bi, p, 0, 0))
    return pl.pallas_call(
        kern,
        grid=(nb, nh // (2 * npairs), tt // tr),
        in_specs=[seqb] * 7 + [stb, vecb, vecb, vecb],
        out_specs=[seqb, stb],
        out_shape=[jax.ShapeDtypeStruct((nb, tt, d), BF16),
                   jax.ShapeDtypeStruct((nb, nh, HEAD_SIZE, HEAD_SIZE), F32)],
        scratch_shapes=[pltpu.VMEM((npairs, LANES, LANES), F32)],
        compiler_params=_params(("arbitrary", "arbitrary", "arbitrary")),
        name="wkv_prompt",
    )(r, ld, k, v, kn, a, g, s0, lnw, lnb, rk)


def _wkv_sample_kernel(r_ref, ld_ref, k_ref, v_ref, kn_ref, a_ref, g_ref, s0_ref, lnw_ref, lnb_ref, rk_ref,
                       z_ref, so_ref, *, ngroups, npairs):
    hs, L, T = HEAD_SIZE, CHUNK, SAMPLE_TOK
    ns = L // T
    lane = lax.broadcasted_iota(jnp.int32, (1, LANES), 1)
    head0 = lane < hs
    head0w = (lax.broadcasted_iota(jnp.int32, (1, 2 * LANES), 1) % LANES) < hs
    m0 = head0.astype(F32)
    m1 = 1.0 - m0
    ri = lax.broadcasted_iota(jnp.int32, (LANES, LANES), 0)
    ci = lax.broadcasted_iota(jnp.int32, (LANES, LANES), 1)
    bdmask = ((ri < hs) == (ci < hs)).astype(F32)
    eye_w = (ri == ci).astype(BF16)
    rl = lax.broadcasted_iota(jnp.int32, (L, L), 0)
    cl = lax.broadcasted_iota(jnp.int32, (L, L), 1)
    same = (rl // T) == (cl // T)
    cum_lhs = jnp.concatenate([jnp.where(same, (cl % T <= rl % T).astype(F32), 0.0),
                               same.astype(F32)], axis=0).astype(BF16)
    eye = (cl == rl).astype(F32)
    prow = lax.broadcasted_iota(jnp.int32, (4 * L, 2 * L), 0)
    pcol = lax.broadcasted_iota(jnp.int32, (4 * L, 2 * L), 1) % L
    psame = ((prow % L) // T) == (pcol // T)
    pcaus = jnp.where(prow >= 2 * L, (pcol % T <= prow % T).astype(F32), (pcol % T < prow % T).astype(F32))
    pmask = jnp.where(psame, pcaus, 0.0)
    kcols = (lax.broadcasted_iota(jnp.int32, (1, 2 * L), 1) >= L).astype(F32)
    nsteps = int(math.log2(T)) - 1
    zeros_l = jnp.zeros((L, LANES), F32)
    zeros_t = jnp.zeros((T, LANES), F32)
    zed = jnp.zeros((hs, hs), F32)

    def head_sum(y):
        s0 = jnp.sum(y * m0, axis=-1, keepdims=True)
        s1 = jnp.sum(y * m1, axis=-1, keepdims=True)
        return jnp.where(head0, s0, s1)

    def group_body(gi, carry):
        seqs = pl.ds(pl.multiple_of(gi * ns, ns), ns)
        pr = range(npairs)
        lss = [slice(p * LANES, (p + 1) * LANES) for p in pr]
        tile = lambda ref, ls: ref[seqs, :, ls].reshape(L, LANES)
        ld = [tile(ld_ref, ls) for ls in lss]
        ld_hi = [x.astype(BF16) for x in ld]
        ld_r = [x - h.astype(F32) for x, h in zip(ld, ld_hi)]
        ld_mid = [x.astype(BF16) for x in ld_r]
        ld_lo = [(x - m.astype(F32)).astype(BF16) for x, m in zip(ld_r, ld_mid)]
        cc = [_dot(cum_lhs, ld_hi[p]) + _dot(cum_lhs, ld_mid[p]) + _dot(cum_lhs, ld_lo[p]) for p in pr]
        cs = [x[:L] for x in cc]
        c_end = [x[L:] for x in cc]
        r = [tile(r_ref, ls) for ls in lss]
        k = [tile(k_ref, ls) for ls in lss]
        v = [tile(v_ref, ls) for ls in lss]
        kn = [tile(kn_ref, ls) for ls in lss]
        b = [kn[p] * tile(a_ref, lss[p]) for p in pr]
        w_inv = [jnp.exp(-x) for x in cs]
        rt = [r[p] * jnp.exp(cs[p]) for p in pr]
        kt = [k[p] * w_inv[p] for p in pr]
        bt = [b[p] * w_inv[p] for p in pr]
        at = [-kn[p] * jnp.exp(cs[p] - ld[p]) for p in pr]
        w_end = [jnp.exp(c_end[p] - cs[p]) for p in pr]
        w_tot = [jnp.exp(x) for x in c_end]
        kh = [k[p] * w_end[p] for p in pr]
        bh = [b[p] * w_end[p] for p in pr]
        pm = [_mm_nt(jnp.concatenate([at[p] * m0, at[p] * m1, rt[p] * m0, rt[p] * m1], axis=0),
                     jnp.concatenate([bt[p], kt[p]], axis=0)) * pmask for p in pr]
        a_bk = [x[:2 * L] for x in pm]
        r_bk = [x[2 * L:] for x in pm]
        akv = [_mm(a_bk[p] * kcols, jnp.concatenate([v[p], v[p]], axis=0)) for p in pr]
        akv = [jnp.where(head0, x[:L], x[L:]) for x in akv]
        pw = [a_bk[p][h * L:(h + 1) * L, :L] for p in pr for h in range(2)]
        tinv = [eye + x for x in pw]
        for _ in range(nsteps):
            pw = [_mm(x, x) for x in pw]
            tinv = [t + _mm(t, x) for t, x in zip(tinv, pw)]
        tz_rhs = [jnp.concatenate([at[p], akv[p]], axis=1) for p in pr]
        tz = [jnp.where(head0w, _mm(tinv[2 * p], tz_rhs[p]), _mm(tinv[2 * p + 1], tz_rhs[p])) for p in pr]
        ry = [_mm(r_bk[p], jnp.concatenate([tz[p], jnp.concatenate([zeros_l, v[p]], axis=1)], axis=0))
              for p in pr]
        ry = [jnp.where(head0w, x[:L], x[L:]) for x in ry]
        r2 = [rt[p] + ry[p][:, :LANES] for p in pr]
        ps = [(p, i) for p in pr for i in range(ns)]
        rs = [slice(i * T, (i + 1) * T) for i in range(ns)]
        state = [jnp.concatenate([jnp.concatenate([s0_ref[gi * ns + i, 2 * p], zed], axis=1),
                                  jnp.concatenate([zed, s0_ref[gi * ns + i, 2 * p + 1]], axis=1)], axis=0)
                 for p, i in ps]
        sb = [x.astype(BF16) for x in state]
        e_rhs = [jnp.concatenate([tz[p][rs[i]], jnp.concatenate([zeros_t, v[p][rs[i]]], axis=1)], axis=0)
                 for p, i in ps]
        e = [_mm_nt(jnp.concatenate([sb[q], eye_w], axis=1), e_rhs[q]) for q in range(len(ps))]
        upd = [_mm(e[q], jnp.concatenate([bh[p][rs[i]], kh[p][rs[i]]], axis=0)) * bdmask
               for q, (p, i) in enumerate(ps)]
        ys = [_mm_nt(r2[p][rs[i]], sb[q]) for q, (p, i) in enumerate(ps)]
        for q, (p, i) in enumerate(ps):
            new = state[q] * w_tot[p][i * T:i * T + 1] + upd[q]
            so_ref[gi * ns + i, 2 * p] = new[:hs, :hs]
            so_ref[gi * ns + i, 2 * p + 1] = new[hs:, hs:]
        y = [jnp.concatenate(ys[p * ns:(p + 1) * ns], axis=0) + ry[p][:, LANES:] for p in pr]
        mean = [head_sum(x) * (1.0 / hs) for x in y]
        yc = [y[p] - mean[p] for p in pr]
        var = [head_sum(x * x) * (1.0 / hs) for x in yc]
        bonus = [head_sum(r[p] * k[p] * rk_ref[:, lss[p]]) * v[p] for p in pr]
        for p in pr:
            yn = yc[p] * lax.rsqrt(var[p] + GN_EPS) * lnw_ref[:, lss[p]] + lnb_ref[:, lss[p]]
            z_ref[seqs, :, lss[p]] = ((yn + bonus[p]) * tile(g_ref, lss[p])).reshape(ns, T, LANES)
        return carry

    lax.fori_loop(0, ngroups, group_body, 0)


def _wkv_sample(r, ld, k, v, kn, a, g, s0, lnw, lnb, rk):
    nb, tt, d = r.shape
    nh = d // HEAD_SIZE
    npairs = min(WKV_PAIRS, nh // 2)
    ns = CHUNK // SAMPLE_TOK
    bb = min(nb, 2 * ns)
    lw = npairs * LANES
    kern = functools.partial(_wkv_sample_kernel, ngroups=bb // ns, npairs=npairs)
    seqb = pl.BlockSpec((bb, tt, lw), lambda i, p: (i, 0, p))
    vecb = pl.BlockSpec((1, lw), lambda i, p: (0, p))
    stb = pl.BlockSpec((bb, 2 * npairs, HEAD_SIZE, HEAD_SIZE), lambda i, p: (i, p, 0, 0))
    return pl.pallas_call(
        kern,
        grid=(nb // bb, nh // (2 * npairs)),
        in_specs=[seqb] * 7 + [stb, vecb, vecb, vecb],
        out_specs=[seqb, stb],
        out_shape=[jax.ShapeDtypeStruct((nb, tt, d), F32),
                   jax.ShapeDtypeStruct((nb, nh, HEAD_SIZE, HEAD_SIZE), F32)],
        compiler_params=_params(("arbitrary", "arbitrary")),
        name="wkv_sample",
    )(r, ld, k, v, kn, a, g, s0, lnw, lnb, rk)


def _wo_kernel(x_ref, z_ref, w_ref, o_ref, *, nt):
    if nt:
        z = jnp.concatenate([z_ref[:, t, :] for t in range(nt)], axis=0)
    else:
        z = z_ref[...]
    o_ref[...] = x_ref[...] + _dot(z.astype(BF16), w_ref[...])


def _wo(x, z, w, *, layer, nt=0):
    rows, d = x.shape
    tm = rows if nt else _row_tile(rows, WO_TM_CAP)
    zspec = pl.BlockSpec(z.shape, lambda i: (0, 0, 0)) if nt else pl.BlockSpec((tm, d), lambda i: (i, 0))
    return pl.pallas_call(
        functools.partial(_wo_kernel, nt=nt),
        grid=(rows // tm,),
        in_specs=[pl.BlockSpec((tm, d), lambda i: (i, 0)), zspec,
                  pl.BlockSpec((None, d, d), lambda i: (layer, 0, 0))],
        out_specs=pl.BlockSpec((tm, d), lambda i: (i, 0)),
        out_shape=jax.ShapeDtypeStruct((rows, d), F32),
        compiler_params=_params(("arbitrary",)),
        name="rwkv_wo_sample" if nt else "rwkv_wo",
    )(x, z, w)


def _pad_cols(a, n):
    return jnp.pad(a, [(0, 0)] * (a.ndim - 1) + [(0, n - a.shape[-1])])


def _pad_rows(a, n):
    return jnp.pad(a, [(0, n - a.shape[0])] + [(0, 0)] * (a.ndim - 1))


def kernel(x_prompt, x_sample, state_pool, state_rwkv_shift, state_rwkv_wkv, state_ffn_conv, meta_tokens,
           norm_mix, norm_ffn, norm_out, pool_w, pool_scale, rwkv_mu, rwkv_wr, rwkv_wk, rwkv_wv, rwkv_wo,
           rwkv_w0, rwkv_w1, rwkv_w2, rwkv_a0, rwkv_a1, rwkv_a2, rwkv_v0, rwkv_v1, rwkv_v2, rwkv_g1, rwkv_g2,
           rwkv_kk, rwkv_ka, rwkv_rk, rwkv_lnw, rwkv_lnb, ffn_w_in, ffn_conv_w, ffn_conv_b, ffn_w_out):
    b, seq, d = x_prompt.shape
    bs, nt, _ = x_sample.shape
    depth = norm_mix.shape[0]
    f = ffn_w_out.shape[1]
    fp = _round_up(f, FFN_TF)
    nh = d // HEAD_SIZE
    tp = FRONT_PAD + N_META + seq
    assert tp % CHUNK == 0 and d % (2 * LANES) == 0 and CONV_WIDTH - 1 <= nt <= SAMPLE_TOK

    row = lambda a: a.reshape(1, -1)
    lora_in = lambda a: _pad_cols(a, _round_up(a.shape[-1], LANES)).astype(BF16)
    lora_out = lambda a: _pad_rows(a, _round_up(a.shape[0], LANES)).astype(BF16)
    halves = lambda a: jnp.concatenate([_pad_cols(a[..., :f], fp), _pad_cols(a[..., f:], fp)], axis=-1)
    w_in_all = _cast_halves(ffn_w_in, fp)
    w_out_all = _cast_rows(ffn_w_out, fp)
    ffn = [dict(g=row(norm_ffn[i]), cw=halves(ffn_conv_w[i]), cb=halves(ffn_conv_b[i][None]))
           for i in range(depth)]
    tn = min(PROJ_TN, d)
    bd = jnp.kron(jnp.eye(tn // HEAD_SIZE, dtype=F32), jnp.ones((HEAD_SIZE, HEAD_SIZE), F32)).astype(BF16)
    wr_all, wk_all, wv_all, wo_all = (_cast_rows(w, d) for w in (rwkv_wr, rwkv_wk, rwkv_wv, rwkv_wo))
    rw = []
    for j in range(depth // 2):
        rw.append(dict(
            g=row(norm_mix[2 * j + 1]), mu=rwkv_mu[j], wr=wr_all, wk=wk_all, wv=wv_all,
            w1=lora_in(rwkv_w1[j]), a1=lora_in(rwkv_a1[j]), g1=lora_in(rwkv_g1[j]),
            w2=lora_out(rwkv_w2[j]), a2=lora_out(rwkv_a2[j]), g2=lora_out(rwkv_g2[j]),
            w0=row(rwkv_w0[j]), a0=row(rwkv_a0[j]), kk=row(rwkv_kk[j]), ka=row(rwkv_ka[j]), bd=bd,
            lnw=row(rwkv_lnw[j]), lnb=row(rwkv_lnb[j]), rk=row(rwkv_rk[j])))
    vls = [None] + [dict(v1=lora_in(rwkv_v1[j]), v2=lora_out(rwkv_v2[j]), v0=row(rwkv_v0[j]))
                    for j in range(depth // 2 - 1)]
    pw = [pool_w[j].astype(BF16) for j in range((depth + 1) // 2)]
    g_out = row(norm_out)

    xp = jnp.concatenate([jnp.zeros((b, FRONT_PAD, d), F32),
                          jnp.broadcast_to(meta_tokens[None], (b, N_META, d)), x_prompt], axis=1)
    xs = x_sample.transpose(1, 0, 2)
    zero_wkv = jnp.zeros((b, nh, HEAD_SIZE, HEAD_SIZE), F32)

    pool_p, pool_s, shift_p, shift_s, wkv_p, wkv_s, conv_p, conv_s = [], [], [], [], [], [], [], []
    vfirst_p = vfirst_s = None
    for i in range(depth):
        j = i // 2
        if i % 2 == 0:
            xp, st = _pool_prompt(xp, row(norm_mix[i]), pw[j], row(pool_scale[j]), tp=tp)
            pool_p.append(st[:, POOL_HALO - POOL_STATE:])
            xs, st = _pool_sample(xs, state_pool[j], row(norm_mix[i]), pw[j], row(pool_scale[j]), nt=nt, bs=bs)
            pool_s.append(st)
        else:
            p = rw[j]
            r, ld, k, v, kn, a, g, hl = _proj(xp.reshape(b * tp, d), None, p, vls[j], vfirst_p,
                                              layer=j, tp=tp, sample=False, bs=0)
            if vfirst_p is None:
                vfirst_p = v
            shift_p.append(hl[:, CONV_HALO - 1])
            sq = lambda t: t.reshape(b, tp, d)
            z, s_new = _wkv_prompt(sq(r), sq(ld), sq(k), sq(v), sq(kn), sq(a), sq(g), zero_wkv,
                                   p["lnw"], p["lnb"], p["rk"])
            wkv_p.append(s_new)
            xp = _wo(xp.reshape(b * tp, d), z.reshape(b * tp, d), wo_all, layer=j).reshape(b, tp, d)
            r, ld, k, v, kn, a, g, hl = _proj(xs.reshape(nt * bs, d), state_rwkv_shift[j], p, vls[j], vfirst_s,
                                              layer=j, tp=0, sample=True, bs=bs)
            if vfirst_s is None:
                vfirst_s = v
            shift_s.append(hl)
            z, s_new = _wkv_sample(r, ld, k, v, kn, a, g, state_rwkv_wkv[j], p["lnw"], p["lnb"], p["rk"])
            wkv_s.append(s_new)
            xs = _wo(xs.reshape(nt * bs, d), z, wo_all, layer=j, nt=nt).reshape(nt, bs, d)
        fi = ffn[i]
        last = i == depth - 1
        xo, stv, stg = _ffn_prompt(xp.reshape(b * tp, d), fi["g"], w_in_all, fi["cw"], fi["cb"], w_out_all,
                                   g_out, layer=i, tp=tp, final_norm=last)
        xp = xo.reshape(b, tp, d)
        stv, stg = (t.reshape(b, -1, CONV_HALO, fp)[:, -1] for t in (stv, stg))
        conv_p.append(jnp.concatenate([stv[:, CONV_HALO - 2:, :f], stg[:, CONV_HALO - 2:, :f]], axis=-1))
        xo, sv, sg = _ffn_sample(xs.reshape(nt * bs, d), fi["g"], w_in_all, fi["cw"], fi["cb"], w_out_all, g_out,
                                 state_ffn_conv[i][:, :, :f], state_ffn_conv[i][:, :, f:],
                                 layer=i, bs=bs, nt=nt, final_norm=last)
        xs = xo.reshape(nt, bs, d)
        conv_s.append(jnp.concatenate([sv[:, :, :f], sg[:, :, :f]], axis=-1))

    y_prompt = xp[:, FRONT_PAD + N_META:]
    y_sample = xs.transpose(1, 0, 2)
    return (y_prompt, y_sample, jnp.stack(pool_p), jnp.stack(pool_s), jnp.stack(shift_p), jnp.stack(shift_s),
            jnp.stack(wkv_p), jnp.stack(wkv_s), jnp.stack(conv_p), jnp.stack(conv_s))
```

```python
import functools
import math

import jax
import jax.numpy as jnp
from jax import lax
from jax.experimental import pallas as pl
from jax.experimental.pallas import tpu as pltpu

F32 = jnp.float32
BF16 = jnp.bfloat16

HEAD_SIZE = 64
LANES = 128
N_META = 16
PAST_LEN = 16384
POOL_WINDOWS = (2, 4, 8, 16)
POOL_STATE = max(POOL_WINDOWS) - 1
POOL_HALO = 16
CONV_WIDTH = 3
CONV_HALO = 8
NORM_EPS = 1e-6
GN_EPS = 64e-5
CHUNK = 64
SAMPLE_CHUNK = 16
SAMPLE_TOK = 8
FRONT_PAD = CHUNK - N_META

FFN_TM_CAP = 704
PROJ_TM_CAP = 352
POOL_TM_CAP = 704
WO_TM_CAP = 704
WKV_ROWS_CAP = 192
WKV_PAIRS = 8
CAST_ROWS = 512
FFN_TF = 512
PROJ_TN = 512
PROJ_TN_SAMPLE = 256
VMEM_LIMIT = 56 * 1024 * 1024


def _row_tile(n, cap, mult=16):
    best = None
    for t in range(mult, min(n, cap) + 1, mult):
        if n % t == 0:
            best = t
    assert best is not None, (n, cap)
    return best


def _round_up(n, m):
    return (n + m - 1) // m * m


def _params(sem):
    return pltpu.CompilerParams(dimension_semantics=sem, vmem_limit_bytes=VMEM_LIMIT)


def _rms(x, g):
    return x * lax.rsqrt(jnp.mean(x * x, axis=-1, keepdims=True) + NORM_EPS) * g


def _dot(a, b):
    return jnp.dot(a, b, preferred_element_type=F32)


def _split(x):
    hi = x.astype(BF16)
    lo = (x - hi.astype(F32)).astype(BF16)
    return hi, lo


def _cast_kernel(x_ref, o_ref, *, valid_rows, tr):
    x = x_ref[...]
    if valid_rows is not None:
        row = pl.program_id(1) * tr + lax.broadcasted_iota(jnp.int32, (1, tr, 1), 1)
        x = jnp.where(row < valid_rows, x, 0.0)
    o_ref[...] = x.astype(o_ref.dtype)


def _cast_rows(w, out_rows):
    n, rows, cols = w.shape
    tr = CAST_ROWS
    kern = functools.partial(_cast_kernel, valid_rows=None if rows % tr == 0 else rows, tr=tr)
    return pl.pallas_call(
        kern,
        grid=(n, out_rows // tr),
        in_specs=[pl.BlockSpec((1, tr, cols), lambda l, i: (l, i, 0))],
        out_specs=pl.BlockSpec((1, tr, cols), lambda l, i: (l, i, 0)),
        out_shape=jax.ShapeDtypeStruct((n, out_rows, cols), BF16),
        compiler_params=_params(("arbitrary", "arbitrary")),
        name="cast_rows",
    )(w)


def _cast_halves_kernel(x_ref, o_ref, *, f, fp):
    x = x_ref[0]
    pad = jnp.zeros((x.shape[0], fp - f), o_ref.dtype)
    o_ref[0, :, :f] = x[:, :f].astype(o_ref.dtype)
    o_ref[0, :, fp:fp + f] = x[:, f:].astype(o_ref.dtype)
    if fp > f:
        o_ref[0, :, f:fp] = pad
        o_ref[0, :, fp + f:] = pad


def _cast_halves(w, fp):
    n, rows, f2 = w.shape
    f = f2 // 2
    tr = min(rows, CAST_ROWS // 2)
    return pl.pallas_call(
        functools.partial(_cast_halves_kernel, f=f, fp=fp),
        grid=(n, rows // tr),
        in_specs=[pl.BlockSpec((1, tr, f2), lambda l, i: (l, i, 0))],
        out_specs=pl.BlockSpec((1, tr, 2 * fp), lambda l, i: (l, i, 0)),
        out_shape=jax.ShapeDtypeStruct((n, rows, 2 * fp), BF16),
        compiler_params=_params(("arbitrary", "arbitrary")),
        name="cast_halves",
    )(w)


def _pool_prompt_kernel(x_ref, g_ref, w_ref, sc_ref, o_ref, st_ref, carry_ref, *, tm, padf, cg):
    i = pl.program_id(1)

    @pl.when(i == 0)
    def _():
        carry_ref[...] = jnp.zeros_like(carry_ref)

    x = x_ref[0]
    row = i * tm + lax.broadcasted_iota(jnp.int32, (tm, 1), 0)
    h = jnp.where(row >= padf, _rms(x, g_ref[...]), 0.0)
    ext = jnp.concatenate([carry_ref[...], h], axis=0)
    pos = row - padf
    for g, w in enumerate(POOL_WINDOWS):
        sl = slice(g * cg, (g + 1) * cg)
        a = ext[:, sl]
        k = 1
        while k < w:
            n = a.shape[0]
            a = a[:n - k] + a[k:]
            k *= 2
        win = a[POOL_HALO + 1 - w: POOL_HALO + 1 - w + tm]
        cnt = jnp.clip(pos + 1, 1, w).astype(F32)
        d = win / cnt - h[:, sl]
        y = _dot(d.astype(BF16), w_ref[g])
        o_ref[0, :, sl] = x[:, sl] + y * sc_ref[:, sl]
    carry_ref[...] = h[tm - POOL_HALO:]
    st_ref[0] = h[tm - POOL_HALO:]


def _pool_prompt(x, g, w, sc, *, tp):
    b, _, d = x.shape
    tm = _row_tile(tp, POOL_TM_CAP)
    cg = d // len(POOL_WINDOWS)
    kern = functools.partial(_pool_prompt_kernel, tm=tm, padf=FRONT_PAD, cg=cg)
    return pl.pallas_call(
        kern,
        grid=(b, tp // tm),
        in_specs=[
            pl.BlockSpec((1, tm, d), lambda bi, i: (bi, i, 0)),
            pl.BlockSpec((1, d), lambda bi, i: (0, 0)),
            pl.BlockSpec((len(POOL_WINDOWS), cg, cg), lambda bi, i: (0, 0, 0)),
            pl.BlockSpec((1, d), lambda bi, i: (0, 0)),
        ],
        out_specs=[
            pl.BlockSpec((1, tm, d), lambda bi, i: (bi, i, 0)),
            pl.BlockSpec((1, POOL_HALO, d), lambda bi, i: (bi, 0, 0)),
        ],
        out_shape=[
            jax.ShapeDtypeStruct((b, tp, d), F32),
            jax.ShapeDtypeStruct((b, POOL_HALO, d), F32),
        ],
        scratch_shapes=[pltpu.VMEM((POOL_HALO, d), F32)],
        compiler_params=_params(("arbitrary", "arbitrary")),
        name="pool_prompt",
    )(x, g, w, sc)


def _pool_sample_kernel(x_ref, pre_ref, g_ref, w_ref, sc_ref, o_ref, st_ref, *, tb, d, cg, nt, start):
    hs = [_rms(x_ref[t], g_ref[...]) for t in range(nt)]
    ext = [pre_ref[:, j, :] for j in range(POOL_STATE)] + hs
    for g, w in enumerate(POOL_WINDOWS):
        sl = slice(g * cg, (g + 1) * cg)
        ds = []
        for t in range(nt):
            e = POOL_STATE + t
            acc = ext[e][:, sl]
            for q in range(1, w):
                acc = acc + ext[e - q][:, sl]
            cnt = float(min(w, start + t + 1))
            ds.append(acc / cnt - hs[t][:, sl])
        y = _dot(jnp.concatenate(ds, axis=0).astype(BF16), w_ref[g])
        for t in range(nt):
            o_ref[t, :, sl] = x_ref[t][:, sl] + y[t * tb:(t + 1) * tb] * sc_ref[:, sl]
    for j in range(POOL_STATE):
        st_ref[:, j, :] = ext[nt + j]


def _pool_sample(x, pre, g, w, sc, *, nt, bs):
    d = x.shape[-1]
    tb = min(bs, 32)
    cg = d // len(POOL_WINDOWS)
    kern = functools.partial(_pool_sample_kernel, tb=tb, d=d, cg=cg, nt=nt, start=PAST_LEN)
    return pl.pallas_call(
        kern,
        grid=(bs // tb,),
        in_specs=[
            pl.BlockSpec((nt, tb, d), lambda i: (0, i, 0)),
            pl.BlockSpec((tb, POOL_STATE, d), lambda i: (i, 0, 0)),
            pl.BlockSpec((1, d), lambda i: (0, 0)),
            pl.BlockSpec((len(POOL_WINDOWS), cg, cg), lambda i: (0, 0, 0)),
            pl.BlockSpec((1, d), lambda i: (0, 0)),
        ],
        out_specs=[
            pl.BlockSpec((nt, tb, d), lambda i: (0, i, 0)),
            pl.BlockSpec((tb, POOL_STATE, d), lambda i: (i, 0, 0)),
        ],
        out_shape=[
            jax.ShapeDtypeStruct((nt, bs, d), F32),
            jax.ShapeDtypeStruct((bs, POOL_STATE, d), F32),
        ],
        compiler_params=_params(("arbitrary",)),
        name="pool_sample",
    )(x, pre, g, w, sc)


def _ffn_prompt_kernel(x_ref, g_ref, wv_ref, wg_ref, cwv_ref, cwg_ref, cbv_ref, cbg_ref, wo_ref, go_ref,
                       o_ref, stv_ref, stg_ref, hb_ref, carv_ref, carg_ref, act_ref, *, tm, tps, padf, nj,
                       final_norm):
    i = pl.program_id(0)
    j = pl.program_id(1)
    ti = i % tps
    keep = ti != 0

    @pl.when(j == 0)
    def _():
        x = x_ref[...]
        row = ti * tm + lax.broadcasted_iota(jnp.int32, (tm, 1), 0)
        h = jnp.where(row >= padf, _rms(x, g_ref[...]), 0.0)
        hb_ref[...] = h.astype(BF16)
        o_ref[...] = x

    def conv(u, cw_ref, cb_ref, car_ref, st_ref):
        prev = jnp.where(keep, car_ref[j], 0.0)
        ext = jnp.concatenate([prev, u], axis=0)
        cw = cw_ref[...]
        c = (cb_ref[...] + ext[CONV_HALO - 2:CONV_HALO - 2 + tm] * cw[0:1]
             + ext[CONV_HALO - 1:CONV_HALO - 1 + tm] * cw[1:2] + u * cw[2:3])
        car_ref[j] = u[tm - CONV_HALO:]
        st_ref[0] = u[tm - CONV_HALO:]
        return c

    def up(down_too):
        hb = hb_ref[...]
        ug = _dot(hb, wg_ref[...])
        cgate = conv(ug, cwg_ref, cbg_ref, carg_ref, stg_ref)
        gate = cgate * jax.nn.sigmoid(cgate)
        uv = _dot(hb, wv_ref[...])
        act = (gate * conv(uv, cwv_ref, cbv_ref, carv_ref, stv_ref)).astype(BF16)
        if down_too:
            o_ref[...] += _dot(act_ref[(j + 1) % 2], wo_ref[...])
        act_ref[j % 2] = act

    @pl.when(j == 0)
    def _():
        up(False)

    @pl.when(jnp.logical_and(j > 0, j < nj))
    def _():
        up(True)

    @pl.when(j == nj)
    def _():
        out = o_ref[...] + _dot(act_ref[(j + 1) % 2], wo_ref[...])
        o_ref[...] = _rms(out, go_ref[...]) if final_norm else out


def _ffn_prompt(x, g, w_in, cw, cb, w_out, g_out, *, layer, tp, final_norm):
    rows, d = x.shape
    fp = w_out.shape[1]
    tf = FFN_TF
    nj = fp // tf
    tm = _row_tile(tp, FFN_TM_CAP)
    tps = tp // tm
    kern = functools.partial(_ffn_prompt_kernel, tm=tm, tps=tps, padf=FRONT_PAD, nj=nj, final_norm=final_norm)
    up_j = lambda j: jnp.minimum(j, nj - 1)
    dn_j = lambda j: jnp.maximum(j - 1, 0)
    return pl.pallas_call(
        kern,
        grid=(rows // tm, nj + 1),
        in_specs=[
            pl.BlockSpec((tm, d), lambda i, j: (i, 0)),
            pl.BlockSpec((1, d), lambda i, j: (0, 0)),
            pl.BlockSpec((None, d, tf), lambda i, j: (layer, 0, up_j(j))),
            pl.BlockSpec((None, d, tf), lambda i, j: (layer, 0, nj + up_j(j))),
            pl.BlockSpec((CONV_WIDTH, tf), lambda i, j: (0, up_j(j))),
            pl.BlockSpec((CONV_WIDTH, tf), lambda i, j: (0, nj + up_j(j))),
            pl.BlockSpec((1, tf), lambda i, j: (0, up_j(j))),
            pl.BlockSpec((1, tf), lambda i, j: (0, nj + up_j(j))),
            pl.BlockSpec((None, tf, d), lambda i, j: (layer, dn_j(j), 0)),
            pl.BlockSpec((1, d), lambda i, j: (0, 0)),
        ],
        out_specs=[
            pl.BlockSpec((tm, d), lambda i, j: (i, 0)),
            pl.BlockSpec((1, CONV_HALO, tf), lambda i, j: (i, 0, up_j(j))),
            pl.BlockSpec((1, CONV_HALO, tf), lambda i, j: (i, 0, up_j(j))),
        ],
        out_shape=[
            jax.ShapeDtypeStruct((rows, d), F32),
            jax.ShapeDtypeStruct((rows // tm, CONV_HALO, fp), F32),
            jax.ShapeDtypeStruct((rows // tm, CONV_HALO, fp), F32),
        ],
        scratch_shapes=[
            pltpu.VMEM((tm, d), BF16),
            pltpu.VMEM((nj, CONV_HALO, tf), F32),
            pltpu.VMEM((nj, CONV_HALO, tf), F32),
            pltpu.VMEM((2, tm, tf), BF16),
        ],
        compiler_params=_params(("arbitrary", "arbitrary")),
        name="ffn_prompt",
    )(x, g, w_in, w_in, cw, cw, cb, cb, w_out, g_out)


def _ffn_sample_kernel(x_ref, g_ref, wv_ref, wg_ref, cwv_ref, cwg_ref, cbv_ref, cbg_ref, wo_ref, go_ref,
                       pv_ref, pg_ref, o_ref, sv_ref, sg_ref, hb_ref, *, bs, nt, nj, tf, f, final_norm):
    j = pl.program_id(0)

    @pl.when(j == 0)
    def _():
        x = x_ref[...]
        hb_ref[...] = _rms(x, g_ref[...]).astype(BF16)
        o_ref[...] = x

    hb = hb_ref[...]
    valid = j * tf + lax.broadcasted_iota(jnp.int32, (1, tf), 1) < f

    def branch(w_ref, cw_ref, cb_ref, p_ref, s_ref):
        u = _dot(hb, w_ref[...])
        prev = [jnp.where(valid, p_ref[:, q, :], 0.0) for q in range(CONV_WIDTH - 1)]
        ext = prev + [u[t * bs:(t + 1) * bs] for t in range(nt)]
        cw = cw_ref[...]
        cs = [cb_ref[...] + ext[t] * cw[0:1] + ext[t + 1] * cw[1:2] + ext[t + 2] * cw[2:3] for t in range(nt)]
        for q in range(CONV_WIDTH - 1):
            s_ref[:, q, :] = ext[nt + q]
        return jnp.concatenate(cs, axis=0)

    cv = branch(wv_ref, cwv_ref, cbv_ref, pv_ref, sv_ref)
    cgate = branch(wg_ref, cwg_ref, cbg_ref, pg_ref, sg_ref)
    act = cgate * jax.nn.sigmoid(cgate) * cv
    o_ref[...] += _dot(act.astype(BF16), wo_ref[...])

    if final_norm:
        @pl.when(j == nj - 1)
        def _():
            o_ref[...] = _rms(o_ref[...], go_ref[...])


def _ffn_sample(x, g, w_in, cw, cb, w_out, g_out, st_val, st_gate, *, layer, bs, nt, final_norm):
    rows, d = x.shape
    fp = w_out.shape[1]
    f = st_val.shape[-1]
    tf = FFN_TF
    nj = fp // tf
    kern = functools.partial(_ffn_sample_kernel, bs=bs, nt=nt, nj=nj, tf=tf, f=f, final_norm=final_norm)
    st_spec = pl.BlockSpec((bs, CONV_WIDTH - 1, tf), lambda j: (0, 0, j))
    return pl.pallas_call(
        kern,
        grid=(nj,),
        in_specs=[
            pl.BlockSpec((rows, d), lambda j: (0, 0)),
            pl.BlockSpec((1, d), lambda j: (0, 0)),
            pl.BlockSpec((None, d, tf), lambda j: (layer, 0, j)),
            pl.BlockSpec((None, d, tf), lambda j: (layer, 0, nj + j)),
            pl.BlockSpec((CONV_WIDTH, tf), lambda j: (0, j)),
            pl.BlockSpec((CONV_WIDTH, tf), lambda j: (0, nj + j)),
            pl.BlockSpec((1, tf), lambda j: (0, j)),
            pl.BlockSpec((1, tf), lambda j: (0, nj + j)),
            pl.BlockSpec((None, tf, d), lambda j: (layer, j, 0)),
            pl.BlockSpec((1, d), lambda j: (0, 0)),
            st_spec, st_spec,
        ],
        out_specs=[pl.BlockSpec((rows, d), lambda j: (0, 0)), st_spec, st_spec],
        out_shape=[jax.ShapeDtypeStruct((rows, d), F32)]
        + [jax.ShapeDtypeStruct((bs, CONV_WIDTH - 1, fp), F32)] * 2,
        scratch_shapes=[pltpu.VMEM((rows, d), BF16)],
        compiler_params=_params(("arbitrary",)),
        name="ffn_sample",
    )(x, g, w_in, w_in, cw, cw, cb, cb, w_out, g_out, st_val, st_gate)


def _proj_kernel(*refs, tm, tps, padf, sample, bs, has_vlora):
    it = iter(refs)
    x_ref = next(it)
    sh_ref = next(it) if sample else None
    g_ref, mu_ref, wr_ref, wk_ref, wv_ref, w1_ref, a1_ref, g1_ref = (next(it) for _ in range(8))
    w2_ref, a2_ref, g2_ref, w0_ref, a0_ref, kk_ref, ka_ref, bd_ref = (next(it) for _ in range(8))
    if has_vlora:
        v1_ref, v2_ref, v0_ref, vf_ref = (next(it) for _ in range(4))
    r_o, ld_o, k_o, v_o, kn_o, a_o, g_o, hl_o = (next(it) for _ in range(8))
    xr_s, xk_s, xv_s, lw_s, la_s, lg_s = (next(it) for _ in range(6))
    lv_s = next(it) if has_vlora else None
    car_s = None if sample else next(it)

    i = pl.program_id(0)
    j = pl.program_id(1)

    @pl.when(j == 0)
    def _():
        x = x_ref[...]
        h = _rms(x, g_ref[...])
        if sample:
            prev = jnp.concatenate([sh_ref[...], h[:tm - bs]], axis=0)
            hl_o[...] = h[tm - bs:]
        else:
            ti = i % tps
            rloc = lax.broadcasted_iota(jnp.int32, (tm, 1), 0)
            h = jnp.where(ti * tm + rloc >= padf, h, 0.0)
            last = jnp.where(ti != 0, car_s[CONV_HALO - 1:CONV_HALO, :], 0.0)
            prev = jnp.where(rloc == 0, last, pltpu.roll(h, 1, 0))
            car_s[...] = h[tm - CONV_HALO:]
            hl_o[0] = h[tm - CONV_HALO:]
        xx = prev - h
        mu = mu_ref[...]
        mix = lambda q: (h + xx * mu[q:q + 1]).astype(BF16)
        xr_s[...] = mix(0)
        xk_s[...] = mix(2)
        xv = mix(3)
        xv_s[...] = xv
        lw_s[...] = jnp.tanh(_dot(mix(1), w1_ref[...])).astype(BF16)
        la_s[...] = _dot(mix(4), a1_ref[...]).astype(BF16)
        lg_s[...] = jax.nn.sigmoid(_dot(mix(5), g1_ref[...])).astype(BF16)
        if has_vlora:
            lv_s[...] = _dot(xv, v1_ref[...]).astype(BF16)

    nt = tm // bs if sample else 0

    def put(o_ref, val):
        if sample:
            for t in range(SAMPLE_TOK):
                o_ref[:, t, :] = val[t * bs:(t + 1) * bs] if t < nt else jnp.zeros((bs, val.shape[1]), F32)
        else:
            o_ref[...] = val

    z = w0_ref[...] + _dot(lw_s[...], w2_ref[...])
    a_pre = a0_ref[...] + _dot(la_s[...], a2_ref[...])
    put(g_o, _dot(lg_s[...], g2_ref[...]))
    if has_vlora:
        v_pre = v0_ref[...] + _dot(lv_s[...], v2_ref[...])
    put(ld_o, -math.exp(-0.5) * jax.nn.sigmoid(z))
    a = jax.nn.sigmoid(a_pre)
    put(a_o, a)
    k = _dot(xk_s[...], wk_ref[...])
    kk = k * kk_ref[...]
    sq_hi, sq_lo = _split(kk * kk)
    ss = _dot(sq_hi, bd_ref[...]) + _dot(sq_lo, bd_ref[...])
    put(kn_o, kk / jnp.maximum(jnp.sqrt(ss), 1e-12))
    put(k_o, k * (1.0 + (a - 1.0) * ka_ref[...]))
    put(r_o, _dot(xr_s[...], wr_ref[...]))
    v = _dot(xv_s[...], wv_ref[...])
    if has_vlora:
        vf = jnp.concatenate([vf_ref[:, t, :] for t in range(nt)], axis=0) if sample else vf_ref[...]
        v = v + (vf - v) * jax.nn.sigmoid(v_pre)
    put(v_o, v)


def _proj(x, sh, p, vl, vfirst, *, layer, tp, sample, bs):
    rows, d = x.shape
    tn = min(PROJ_TN_SAMPLE if sample else PROJ_TN, d)
    nj = d // tn
    p = dict(p, bd=p["bd"][:tn, :tn])
    if sample:
        tm, tps = rows, 1
    else:
        tm = _row_tile(tp, PROJ_TM_CAP)
        tps = tp // tm
    has_vlora = vl is not None
    kern = functools.partial(_proj_kernel, tm=tm, tps=tps, padf=FRONT_PAD, sample=sample, bs=bs,
                             has_vlora=has_vlora)
    full = lambda a: pl.BlockSpec(a.shape, lambda i, j: (0,) * a.ndim)
    colb = lambda a: pl.BlockSpec((a.shape[0], tn), lambda i, j: (0, j))
    if sample:
        rowb = pl.BlockSpec((bs, SAMPLE_TOK, tn), lambda i, j: (0, 0, j))
        row_shape = jax.ShapeDtypeStruct((bs, SAMPLE_TOK, d), F32)
    else:
        rowb = pl.BlockSpec((tm, tn), lambda i, j: (i, j))
        row_shape = jax.ShapeDtypeStruct((rows, d), F32)
    args, specs = [x], [pl.BlockSpec((tm, d), lambda i, j: (i, 0))]
    if sample:
        args.append(sh)
        specs.append(full(sh))
    for name in ("g", "mu"):
        args.append(p[name]); specs.append(full(p[name]))
    for name in ("wr", "wk", "wv"):
        args.append(p[name]); specs.append(pl.BlockSpec((None, d, tn), lambda i, j: (layer, 0, j)))
    for name in ("w1", "a1", "g1"):
        args.append(p[name]); specs.append(full(p[name]))
    for name in ("w2", "a2", "g2", "w0", "a0", "kk", "ka"):
        args.append(p[name]); specs.append(colb(p[name]))
    args.append(p["bd"]); specs.append(full(p["bd"]))
    if has_vlora:
        args += [vl["v1"], vl["v2"], vl["v0"], vfirst]
        specs += [full(vl["v1"]), colb(vl["v2"]), colb(vl["v0"]), rowb]
    out_shape = [row_shape] * 7
    out_specs = [rowb] * 7
    if sample:
        out_shape.append(jax.ShapeDtypeStruct((bs, d), F32))
        out_specs.append(pl.BlockSpec((bs, d), lambda i, j: (0, 0)))
    else:
        nb = rows // tp
        out_shape.append(jax.ShapeDtypeStruct((nb, CONV_HALO, d), F32))
        out_specs.append(pl.BlockSpec((1, CONV_HALO, d), lambda i, j: (i // tps, 0, 0)))
    lw, la, lg = p["w1"].shape[1], p["a1"].shape[1], p["g1"].shape[1]
    scratch = [pltpu.VMEM((tm, d), BF16)] * 3 + [pltpu.VMEM((tm, lw), BF16), pltpu.VMEM((tm, la), BF16),
                                                 pltpu.VMEM((tm, lg), BF16)]
    if has_vlora:
        scratch.append(pltpu.VMEM((tm, vl["v1"].shape[1]), BF16))
    if not sample:
        scratch.append(pltpu.VMEM((CONV_HALO, d), F32))
    return pl.pallas_call(
        kern,
        grid=(rows // tm, nj),
        in_specs=specs,
        out_specs=out_specs,
        out_shape=out_shape,
        scratch_shapes=scratch,
        compiler_params=_params(("arbitrary", "arbitrary")),
        name="rwkv_proj_sample" if sample else "rwkv_proj_prompt",
    )(*args)


def _mm(a, b):
    return _dot(a.astype(BF16), b.astype(BF16))


def _mm_nt(a, b):
    return lax.dot_general(a.astype(BF16), b.astype(BF16), (((1,), (1,)), ((), ())),
                           preferred_element_type=F32)


def _mm3(a, b):
    ah, al = _split(a)
    bh, bl = _split(b)
    return _dot(ah, bh) + _dot(ah, bl) + _dot(al, bh)


def _wkv_kernel(r_ref, ld_ref, k_ref, v_ref, kn_ref, a_ref, g_ref, s0_ref, lnw_ref, lnb_ref, rk_ref,
                z_ref, so_ref, *, bb, L, nchunks):
    hs = HEAD_SIZE
    lane = lax.broadcasted_iota(jnp.int32, (1, LANES), 1)
    head0 = lane < hs
    m0 = head0.astype(F32)
    m1 = 1.0 - m0
    ri = lax.broadcasted_iota(jnp.int32, (LANES, LANES), 0)
    ci = lax.broadcasted_iota(jnp.int32, (LANES, LANES), 1)
    bdmask = ((ri < hs) == (ci < hs)).astype(F32)
    rl = lax.broadcasted_iota(jnp.int32, (L, L), 0)
    cl = lax.broadcasted_iota(jnp.int32, (L, L), 1)
    tril_incl = (cl <= rl).astype(BF16)
    eye = (cl == rl).astype(F32)
    prow = lax.broadcasted_iota(jnp.int32, (4 * L, 2 * L), 0)
    pcol = lax.broadcasted_iota(jnp.int32, (4 * L, 2 * L), 1) % L
    pmask = jnp.where(prow >= 2 * L, (pcol <= prow % L).astype(F32), (pcol < prow % L).astype(F32))
    kcols = (lax.broadcasted_iota(jnp.int32, (1, 2 * L), 1) >= L).astype(F32)
    lnw, lnb, rk = lnw_ref[...], lnb_ref[...], rk_ref[...]
    nsteps = int(math.log2(L)) - 1

    def sel(zz):
        return jnp.where(head0, zz[:L], zz[L:])

    def head_sum(y):
        s0 = jnp.sum(y * m0, axis=-1, keepdims=True)
        s1 = jnp.sum(y * m1, axis=-1, keepdims=True)
        return jnp.where(head0, s0, s1)

    def inverse(a):
        t = eye + a
        pw = a
        for _ in range(nsteps):
            pw = _mm3(pw, pw)
            t = t + _mm3(t, pw)
        return t

    def seq_body(s, carry):
        s_init = s0_ref[s]
        zed = jnp.zeros((hs, hs), F32)
        state0 = jnp.concatenate([jnp.concatenate([s_init[0], zed], axis=1),
                                  jnp.concatenate([zed, s_init[1]], axis=1)], axis=0)

        def chunk_body(c, state):
            rows = pl.ds(pl.multiple_of(c * L, L), L)
            r, ld, k, v = r_ref[s, rows, :], ld_ref[s, rows, :], k_ref[s, rows, :], v_ref[s, rows, :]
            kn, a, g = kn_ref[s, rows, :], a_ref[s, rows, :], g_ref[s, rows, :]
            ld_hi = ld.astype(BF16)
            ld_r = ld - ld_hi.astype(F32)
            ld_mid = ld_r.astype(BF16)
            ld_lo = (ld_r - ld_mid.astype(F32)).astype(BF16)
            cs = _dot(tril_incl, ld_hi) + _dot(tril_incl, ld_mid) + _dot(tril_incl, ld_lo)
            c_end = cs[L - 1:L, :]
            w_in = jnp.exp(cs)
            w_inv = jnp.exp(-cs)
            w_prev = jnp.exp(cs - ld)
            w_end = jnp.exp(c_end - cs)
            b = kn * a
            rt, kt, bt, at = r * w_in, k * w_inv, b * w_inv, -kn * w_prev
            kh, bh = k * w_end, b * w_end
            lhs = jnp.concatenate([at * m0, at * m1, rt * m0, rt * m1], axis=0)
            pm = _mm_nt(lhs, jnp.concatenate([bt, kt], axis=0)) * pmask
            a_bk = pm[:2 * L]
            r_bk = pm[2 * L:]
            gs = _mm_nt(jnp.concatenate([at, rt], axis=0), state)
            vv = jnp.concatenate([v, v], axis=0)
            x = gs[:L] + sel(_mm(a_bk * kcols, vv))
            a_b = a_bk[:, :L]
            u = jnp.where(head0, _mm3(inverse(a_b[:L]), x), _mm3(inverse(a_b[L:]), x))
            uv = jnp.concatenate([u, v], axis=0)
            y = gs[L:] + sel(_mm(r_bk, uv))
            upd = _mm(uv.T, jnp.concatenate([bh, kh], axis=0))
            state = state * jnp.exp(c_end) + upd * bdmask
            mean = head_sum(y) * (1.0 / hs)
            yc = y - mean
            var = head_sum(yc * yc) * (1.0 / hs)
            yn = yc * lax.rsqrt(var + GN_EPS) * lnw + lnb
            bonus = head_sum(r * k * rk) * v
            z_ref[s, rows, :] = ((yn + bonus) * g).astype(z_ref.dtype)
            return state

        state = lax.fori_loop(0, nchunks, chunk_body, state0)
        so_ref[s, 0] = state[:hs, :hs]
        so_ref[s, 1] = state[hs:, hs:]
        return carry

    lax.fori_loop(0, bb, seq_body, 0)


def _wkv(r, ld, k, v, kn, a, g, s0, lnw, lnb, rk, *, L, bb):
    nb, tt, d = r.shape
    nh = d // HEAD_SIZE
    kern = functools.partial(_wkv_kernel, bb=bb, L=L, nchunks=tt // L)
    seqb = pl.BlockSpec((bb, tt, LANES), lambda i, p: (i, 0, p))
    vecb = pl.BlockSpec((1, LANES), lambda i, p: (0, p))
    stb = pl.BlockSpec((bb, 2, HEAD_SIZE, HEAD_SIZE), lambda i, p: (i, p, 0, 0))
    return pl.pallas_call(
        kern,
        grid=(nb // bb, nh // 2),
        in_specs=[seqb] * 7 + [stb, vecb, vecb, vecb],
        out_specs=[seqb, stb],
        out_shape=[jax.ShapeDtypeStruct((nb, tt, d), BF16),
                   jax.ShapeDtypeStruct((nb, nh, HEAD_SIZE, HEAD_SIZE), F32)],
        compiler_params=_params(("arbitrary", "arbitrary")),
        name=f"wkv_chunk{L}",
    )(r, ld, k, v, kn, a, g, s0, lnw, lnb, rk)


def _wkv_prompt_kernel(r_ref, ld_ref, k_ref, v_ref, kn_ref, a_ref, g_ref, s0_ref, lnw_ref, lnb_ref, rk_ref,
                       z_ref, so_ref, st_ref, *, L, nchunks, npairs):
    hs = HEAD_SIZE
    ib = pl.program_id(2)
    lane = lax.broadcasted_iota(jnp.int32, (1, LANES), 1)
    head0 = lane < hs
    head0w = (lax.broadcasted_iota(jnp.int32, (1, 2 * LANES), 1) % LANES) < hs
    m0 = head0.astype(F32)
    m1 = 1.0 - m0
    ri = lax.broadcasted_iota(jnp.int32, (LANES, LANES), 0)
    ci = lax.broadcasted_iota(jnp.int32, (LANES, LANES), 1)
    bdmask = ((ri < hs) == (ci < hs)).astype(F32)
    rl = lax.broadcasted_iota(jnp.int32, (L, L), 0)
    cl = lax.broadcasted_iota(jnp.int32, (L, L), 1)
    tril_incl = (cl <= rl).astype(BF16)
    eye = (cl == rl).astype(F32)
    prow = lax.broadcasted_iota(jnp.int32, (4 * L, 2 * L), 0)
    pcol = lax.broadcasted_iota(jnp.int32, (4 * L, 2 * L), 1) % L
    pmask = jnp.where(prow >= 2 * L, (pcol <= prow % L).astype(F32), (pcol < prow % L).astype(F32))
    kcols = (lax.broadcasted_iota(jnp.int32, (1, 2 * L), 1) >= L).astype(F32)
    nsteps = int(math.log2(L)) - 1
    zeros_l = jnp.zeros((L, LANES), F32)

    @pl.when(ib == 0)
    def _():
        zed = jnp.zeros((hs, hs), F32)
        for p in range(npairs):
            st_ref[p] = jnp.concatenate([jnp.concatenate([s0_ref[0, 2 * p], zed], axis=1),
                                         jnp.concatenate([zed, s0_ref[0, 2 * p + 1]], axis=1)], axis=0)

    def head_sum(y):
        s0 = jnp.sum(y * m0, axis=-1, keepdims=True)
        s1 = jnp.sum(y * m1, axis=-1, keepdims=True)
        return jnp.where(head0, s0, s1)

    def inverse(a):
        t = eye + a
        pw = a
        for _ in range(nsteps):
            pw = _mm(pw, pw)
            t = t + _mm(t, pw)
        return t

    def chunk_body(c, carry):
        rows = pl.ds(pl.multiple_of(c * L, L), L)
        pr = range(npairs)
        lss = [slice(p * LANES, (p + 1) * LANES) for p in pr]
        ld = [ld_ref[0, rows, ls] for ls in lss]
        ld_hi = [x.astype(BF16) for x in ld]
        ld_r = [x - h.astype(F32) for x, h in zip(ld, ld_hi)]
        ld_mid = [x.astype(BF16) for x in ld_r]
        ld_lo = [(x - m.astype(F32)).astype(BF16) for x, m in zip(ld_r, ld_mid)]
        cs = [_dot(tril_incl, ld_hi[p]) + _dot(tril_incl, ld_mid[p]) + _dot(tril_incl, ld_lo[p]) for p in pr]
        c_end = [x[L - 1:L, :] for x in cs]
        r = [r_ref[0, rows, ls] for ls in lss]
        k = [k_ref[0, rows, ls] for ls in lss]
        v = [v_ref[0, rows, ls] for ls in lss]
        kn = [kn_ref[0, rows, ls] for ls in lss]
        b = [kn[p] * a_ref[0, rows, lss[p]] for p in pr]
        w_inv = [jnp.exp(-x) for x in cs]
        rt = [r[p] * jnp.exp(cs[p]) for p in pr]
        kt = [k[p] * w_inv[p] for p in pr]
        bt = [b[p] * w_inv[p] for p in pr]
        at = [-kn[p] * jnp.exp(cs[p] - ld[p]) for p in pr]
        w_end = [jnp.exp(c_end[p] - cs[p]) for p in pr]
        kh = [k[p] * w_end[p] for p in pr]
        bh = [b[p] * w_end[p] for p in pr]
        pm = [_mm_nt(jnp.concatenate([at[p] * m0, at[p] * m1, rt[p] * m0, rt[p] * m1], axis=0),
                     jnp.concatenate([bt[p], kt[p]], axis=0)) * pmask for p in pr]
        a_bk = [x[:2 * L] for x in pm]
        r_bk = [x[2 * L:] for x in pm]
        akv = [_mm(a_bk[p] * kcols, jnp.concatenate([v[p], v[p]], axis=0)) for p in pr]
        akv = [jnp.where(head0, x[:L], x[L:]) for x in akv]
        pw = [a_bk[p][h * L:(h + 1) * L, :L] for p in pr for h in range(2)]
        tinv = [eye + x for x in pw]
        for _ in range(nsteps):
            pw = [_mm(x, x) for x in pw]
            tinv = [t + _mm(t, x) for t, x in zip(tinv, pw)]
        tz_rhs = [jnp.concatenate([at[p], akv[p]], axis=1) for p in pr]
        tz = [jnp.where(head0w, _mm(tinv[2 * p], tz_rhs[p]), _mm(tinv[2 * p + 1], tz_rhs[p])) for p in pr]
        ry = [_mm(r_bk[p], jnp.concatenate([tz[p], jnp.concatenate([zeros_l, v[p]], axis=1)], axis=0))
              for p in pr]
        ry = [jnp.where(head0w, x[:L], x[L:]) for x in ry]
        r2 = [rt[p] + ry[p][:, :LANES] for p in pr]
        n_mat = [_mm(tz[p][:, :LANES].T, bh[p]) * bdmask for p in pr]
        c_mat = [_mm(jnp.concatenate([tz[p][:, LANES:], v[p]], axis=0).T,
                     jnp.concatenate([bh[p], kh[p]], axis=0)) * bdmask for p in pr]
        state = [st_ref[p] for p in pr]
        y = [_mm_nt(r2[p], state[p]) + ry[p][:, LANES:] for p in pr]
        for p in pr:
            st_ref[p] = state[p] * jnp.exp(c_end[p]) + _mm(state[p], n_mat[p]) + c_mat[p]
        mean = [head_sum(x) * (1.0 / hs) for x in y]
        yc = [y[p] - mean[p] for p in pr]
        var = [head_sum(x * x) * (1.0 / hs) for x in yc]
        bonus = [head_sum(r[p] * k[p] * rk_ref[:, lss[p]]) * v[p] for p in pr]
        for p in pr:
            yn = yc[p] * lax.rsqrt(var[p] + GN_EPS) * lnw_ref[:, lss[p]] + lnb_ref[:, lss[p]]
            z_ref[0, rows, lss[p]] = ((yn + bonus[p]) * g_ref[0, rows, lss[p]]).astype(z_ref.dtype)
        return carry

    lax.fori_loop(0, nchunks, chunk_body, 0)

    @pl.when(ib == pl.num_programs(2) - 1)
    def _():
        for p in range(npairs):
            state = st_ref[p]
            so_ref[0, 2 * p] = state[:hs, :hs]
            so_ref[0, 2 * p + 1] = state[hs:, hs:]


def _wkv_prompt(r, ld, k, v, kn, a, g, s0, lnw, lnb, rk):
    nb, tt, d = r.shape
    nh = d // HEAD_SIZE
    npairs = min(WKV_PAIRS, nh // 2)
    tr = _row_tile(tt, WKV_ROWS_CAP, CHUNK)
    lw = npairs * LANES
    kern = functools.partial(_wkv_prompt_kernel, L=CHUNK, nchunks=tr // CHUNK, npairs=npairs)
    seqb = pl.BlockSpec((1, tr, lw), lambda bi, p, i: (bi, i, p))
    vecb = pl.BlockSpec((1, lw), lambda bi, p, i: (0, p))
    stb = pl.BlockSpec((1, 2 * npairs, HEAD_SIZE, HEAD_SIZE), lambda bi, p, i: (bi, p, 0, 0))
    return pl.pallas_call(
        kern,
        grid=(nb, nh // (2 * npairs), tt // tr),
        in_specs=[seqb] * 7 + [stb, vecb, vecb, vecb],
        out_specs=[seqb, stb],
        out_shape=[jax.ShapeDtypeStruct((nb, tt, d), BF16),
                   jax.ShapeDtypeStruct((nb, nh, HEAD_SIZE, HEAD_SIZE), F32)],
        scratch_shapes=[pltpu.VMEM((npairs, LANES, LANES), F32)],
        compiler_params=_params(("arbitrary", "arbitrary", "arbitrary")),
        name="wkv_prompt",
    )(r, ld, k, v, kn, a, g, s0, lnw, lnb, rk)


def _wkv_sample_kernel(r_ref, ld_ref, k_ref, v_ref, kn_ref, a_ref, g_ref, s0_ref, lnw_ref, lnb_ref, rk_ref,
                       z_ref, so_ref, *, ngroups, npairs):
    hs, L, T = HEAD_SIZE, CHUNK, SAMPLE_TOK
    ns = L // T
    lane = lax.broadcasted_iota(jnp.int32, (1, LANES), 1)
    head0 = lane < hs
    head0w = (lax.broadcasted_iota(jnp.int32, (1, 2 * LANES), 1) % LANES) < hs
    m0 = head0.astype(F32)
    m1 = 1.0 - m0
    ri = lax.broadcasted_iota(jnp.int32, (LANES, LANES), 0)
    ci = lax.broadcasted_iota(jnp.int32, (LANES, LANES), 1)
    bdmask = ((ri < hs) == (ci < hs)).astype(F32)
    eye_w = (ri == ci).astype(BF16)
    rl = lax.broadcasted_iota(jnp.int32, (L, L), 0)
    cl = lax.broadcasted_iota(jnp.int32, (L, L), 1)
    same = (rl // T) == (cl // T)
    cum_lhs = jnp.concatenate([jnp.where(same, (cl % T <= rl % T).astype(F32), 0.0),
                               same.astype(F32)], axis=0).astype(BF16)
    eye = (cl == rl).astype(F32)
    prow = lax.broadcasted_iota(jnp.int32, (4 * L, 2 * L), 0)
    pcol = lax.broadcasted_iota(jnp.int32, (4 * L, 2 * L), 1) % L
    psame = ((prow % L) // T) == (pcol // T)
    pcaus = jnp.where(prow >= 2 * L, (pcol % T <= prow % T).astype(F32), (pcol % T < prow % T).astype(F32))
    pmask = jnp.where(psame, pcaus, 0.0)
    kcols = (lax.broadcasted_iota(jnp.int32, (1, 2 * L), 1) >= L).astype(F32)
    nsteps = int(math.log2(T)) - 1
    zeros_l = jnp.zeros((L, LANES), F32)
    zeros_t = jnp.zeros((T, LANES), F32)
    zed = jnp.zeros((hs, hs), F32)

    def head_sum(y):
        s0 = jnp.sum(y * m0, axis=-1, keepdims=True)
        s1 = jnp.sum(y * m1, axis=-1, keepdims=True)
        return jnp.where(head0, s0, s1)

    def group_body(gi, carry):
        seqs = pl.ds(pl.multiple_of(gi * ns, ns), ns)
        pr = range(npairs)
        lss = [slice(p * LANES, (p + 1) * LANES) for p in pr]
        tile = lambda ref, ls: ref[seqs, :, ls].reshape(L, LANES)
        ld = [tile(ld_ref, ls) for ls in lss]
        ld_hi = [x.astype(BF16) for x in ld]
        ld_r = [x - h.astype(F32) for x, h in zip(ld, ld_hi)]
        ld_mid = [x.astype(BF16) for x in ld_r]
        ld_lo = [(x - m.astype(F32)).astype(BF16) for x, m in zip(ld_r, ld_mid)]
        cc = [_dot(cum_lhs, ld_hi[p]) + _dot(cum_lhs, ld_mid[p]) + _dot(cum_lhs, ld_lo[p]) for p in pr]
        cs = [x[:L] for x in cc]
        c_end = [x[L:] for x in cc]
        r = [tile(r_ref, ls) for ls in lss]
        k = [tile(k_ref, ls) for ls in lss]
        v = [tile(v_ref, ls) for ls in lss]
        kn = [tile(kn_ref, ls) for ls in lss]
        b = [kn[p] * tile(a_ref, lss[p]) for p in pr]
        w_inv = [jnp.exp(-x) for x in cs]
        rt = [r[p] * jnp.exp(cs[p]) for p in pr]
        kt = [k[p] * w_inv[p] for p in pr]
        bt = [b[p] * w_inv[p] for p in pr]
        at = [-kn[p] * jnp.exp(cs[p] - ld[p]) for p in pr]
        w_end = [jnp.exp(c_end[p] - cs[p]) for p in pr]
        w_tot = [jnp.exp(x) for x in c_end]
        kh = [k[p] * w_end[p] for p in pr]
        bh = [b[p] * w_end[p] for p in pr]
        pm = [_mm_nt(jnp.concatenate([at[p] * m0, at[p] * m1, rt[p] * m0, rt[p] * m1], axis=0),
                     jnp.concatenate([bt[p], kt[p]], axis=0)) * pmask for p in pr]
        a_bk = [x[:2 * L] for x in pm]
        r_bk = [x[2 * L:] for x in pm]
        akv = [_mm(a_bk[p] * kcols, jnp.concatenate([v[p], v[p]], axis=0)) for p in pr]
        akv = [jnp.where(head0, x[:L], x[L:]) for x in akv]
        pw = [a_bk[p][h * L:(h + 1) * L, :L] for p in pr for h in range(2)]
        tinv = [eye + x for x in pw]
        for _ in range(nsteps):
            pw = [_mm(x, x) for x in pw]
            tinv = [t + _mm(t, x) for t, x in zip(tinv, pw)]
        tz_rhs = [jnp.concatenate([at[p], akv[p]], axis=1) for p in pr]
        tz = [jnp.where(head0w, _mm(tinv[2 * p], tz_rhs[p]), _mm(tinv[2 * p + 1], tz_rhs[p])) for p in pr]
        ry = [_mm(r_bk[p], jnp.concatenate([tz[p], jnp.concatenate([zeros_l, v[p]], axis=1)], axis=0))
              for p in pr]
        ry = [jnp.where(head0w, x[:L], x[L:]) for x in ry]
        r2 = [rt[p] + ry[p][:, :LANES] for p in pr]
        ps = [(p, i) for p in pr for i in range(ns)]
        rs = [slice(i * T, (i + 1) * T) for i in range(ns)]
        state = [jnp.concatenate([jnp.concatenate([s0_ref[gi * ns + i, 2 * p], zed], axis=1),
                                  jnp.concatenate([zed, s0_ref[gi * ns + i, 2 * p + 1]], axis=1)], axis=0)
                 for p, i in ps]
        sb = [x.astype(BF16) for x in state]
        e_rhs = [jnp.concatenate([tz[p][rs[i]], jnp.concatenate([zeros_t, v[p][rs[i]]], axis=1)], axis=0)
                 for p, i in ps]
        e = [_mm_nt(jnp.concatenate([sb[q], eye_w], axis=1), e_rhs[q]) for q in range(len(ps))]
        upd = [_mm(e[q], jnp.concatenate([bh[p][rs[i]], kh[p][rs[i]]], axis=0)) * bdmask
               for q, (p, i) in enumerate(ps)]
        ys = [_mm_nt(r2[p][rs[i]], sb[q]) for q, (p, i) in enumerate(ps)]
        for q, (p, i) in enumerate(ps):
            new = state[q] * w_tot[p][i * T:i * T + 1] + upd[q]
            so_ref[gi * ns + i, 2 * p] = new[:hs, :hs]
            so_ref[gi * ns + i, 2 * p + 1] = new[hs:, hs:]
        y = [jnp.concatenate(ys[p * ns:(p + 1) * ns], axis=0) + ry[p][:, LANES:] for p in pr]
        mean = [head_sum(x) * (1.0 / hs) for x in y]
        yc = [y[p] - mean[p] for p in pr]
        var = [head_sum(x * x) * (1.0 / hs) for x in yc]
        bonus = [head_sum(r[p] * k[p] * rk_ref[:, lss[p]]) * v[p] for p in pr]
        for p in pr:
            yn = yc[p] * lax.rsqrt(var[p] + GN_EPS) * lnw_ref[:, lss[p]] + lnb_ref[:, lss[p]]
            z_ref[seqs, :, lss[p]] = ((yn + bonus[p]) * tile(g_ref, lss[p])).reshape(ns, T, LANES)
        return carry

    lax.fori_loop(0, ngroups, group_body, 0)


def _wkv_sample(r, ld, k, v, kn, a, g, s0, lnw, lnb, rk):
    nb, tt, d = r.shape
    nh = d // HEAD_SIZE
    npairs = min(WKV_PAIRS, nh // 2)
    ns = CHUNK // SAMPLE_TOK
    bb = min(nb, 2 * ns)
    lw = npairs * LANES
    kern = functools.partial(_wkv_sample_kernel, ngroups=bb // ns, npairs=npairs)
    seqb = pl.BlockSpec((bb, tt, lw), lambda i, p: (i, 0, p))
    vecb = pl.BlockSpec((1, lw), lambda i, p: (0, p))
    stb = pl.BlockSpec((bb, 2 * npairs, HEAD_SIZE, HEAD_SIZE), lambda i, p: (i, p, 0, 0))
    return pl.pallas_call(
        kern,
        grid=(nb // bb, nh // (2 * npairs)),
        in_specs=[seqb] * 7 + [stb, vecb, vecb, vecb],
        out_specs=[seqb, stb],
        out_shape=[jax.ShapeDtypeStruct((nb, tt, d), F32),
                   jax.ShapeDtypeStruct((nb, nh, HEAD_SIZE, HEAD_SIZE), F32)],
        compiler_params=_params(("arbitrary", "arbitrary")),
        name="wkv_sample",
    )(r, ld, k, v, kn, a, g, s0, lnw, lnb, rk)


def _wo_kernel(x_ref, z_ref, w_ref, o_ref, *, nt):
    if nt:
        z = jnp.concatenate([z_ref[:, t, :] for t in range(nt)], axis=0)
    else:
        z = z_ref[...]
    o_ref[...] = x_ref[...] + _dot(z.astype(BF16), w_ref[...])


def _wo(x, z, w, *, layer, nt=0):
    rows, d = x.shape
    tm = rows if nt else _row_tile(rows, WO_TM_CAP)
    zspec = pl.BlockSpec(z.shape, lambda i: (0, 0, 0)) if nt else pl.BlockSpec((tm, d), lambda i: (i, 0))
    return pl.pallas_call(
        functools.partial(_wo_kernel, nt=nt),
        grid=(rows // tm,),
        in_specs=[pl.BlockSpec((tm, d), lambda i: (i, 0)), zspec,
                  pl.BlockSpec((None, d, d), lambda i: (layer, 0, 0))],
        out_specs=pl.BlockSpec((tm, d), lambda i: (i, 0)),
        out_shape=jax.ShapeDtypeStruct((rows, d), F32),
        compiler_params=_params(("arbitrary",)),
        name="rwkv_wo_sample" if nt else "rwkv_wo",
    )(x, z, w)


def _pad_cols(a, n):
    return jnp.pad(a, [(0, 0)] * (a.ndim - 1) + [(0, n - a.shape[-1])])


def _pad_rows(a, n):
    return jnp.pad(a, [(0, n - a.shape[0])] + [(0, 0)] * (a.ndim - 1))


def kernel(x_prompt, x_sample, state_pool, state_rwkv_shift, state_rwkv_wkv, state_ffn_conv, meta_tokens,
           norm_mix, norm_ffn, norm_out, pool_w, pool_scale, rwkv_mu, rwkv_wr, rwkv_wk, rwkv_wv, rwkv_wo,
           rwkv_w0, rwkv_w1, rwkv_w2, rwkv_a0, rwkv_a1, rwkv_a2, rwkv_v0, rwkv_v1, rwkv_v2, rwkv_g1, rwkv_g2,
           rwkv_kk, rwkv_ka, rwkv_rk, rwkv_lnw, rwkv_lnb, ffn_w_in, ffn_conv_w, ffn_conv_b, ffn_w_out):
    b, seq, d = x_prompt.shape
    bs, nt, _ = x_sample.shape
    depth = norm_mix.shape[0]
    f = ffn_w_out.shape[1]
    fp = _round_up(f, FFN_TF)
    nh = d // HEAD_SIZE
    tp = FRONT_PAD + N_META + seq
    assert tp % CHUNK == 0 and d % (2 * LANES) == 0 and CONV_WIDTH - 1 <= nt <= SAMPLE_TOK

    row = lambda a: a.reshape(1, -1)
    lora_in = lambda a: _pad_cols(a, _round_up(a.shape[-1], LANES)).astype(BF16)
    lora_out = lambda a: _pad_rows(a, _round_up(a.shape[0], LANES)).astype(BF16)
    halves = lambda a: jnp.concatenate([_pad_cols(a[..., :f], fp), _pad_cols(a[..., f:], fp)], axis=-1)
    w_in_all = _cast_halves(ffn_w_in, fp)
    w_out_all = _cast_rows(ffn_w_out, fp)
    ffn = [dict(g=row(norm_ffn[i]), cw=halves(ffn_conv_w[i]), cb=halves(ffn_conv_b[i][None]))
           for i in range(depth)]
    tn = min(PROJ_TN, d)
    bd = jnp.kron(jnp.eye(tn // HEAD_SIZE, dtype=F32), jnp.ones((HEAD_SIZE, HEAD_SIZE), F32)).astype(BF16)
    wr_all, wk_all, wv_all, wo_all = (_cast_rows(w, d) for w in (rwkv_wr, rwkv_wk, rwkv_wv, rwkv_wo))
    rw = []
    for j in range(depth // 2):
        rw.append(dict(
            g=row(norm_mix[2 * j + 1]), mu=rwkv_mu[j], wr=wr_all, wk=wk_all, wv=wv_all,
            w1=lora_in(rwkv_w1[j]), a1=lora_in(rwkv_a1[j]), g1=lora_in(rwkv_g1[j]),
            w2=lora_out(rwkv_w2[j]), a2=lora_out(rwkv_a2[j]), g2=lora_out(rwkv_g2[j]),
            w0=row(rwkv_w0[j]), a0=row(rwkv_a0[j]), kk=row(rwkv_kk[j]), ka=row(rwkv_ka[j]), bd=bd,
            lnw=row(rwkv_lnw[j]), lnb=row(rwkv_lnb[j]), rk=row(rwkv_rk[j])))
    vls = [None] + [dict(v1=lora_in(rwkv_v1[j]), v2=lora_out(rwkv_v2[j]), v0=row(rwkv_v0[j]))
                    for j in range(depth // 2 - 1)]
    pw = [pool_w[j].astype(BF16) for j in range((depth + 1) // 2)]
    g_out = row(norm_out)

    xp = jnp.concatenate([jnp.zeros((b, FRONT_PAD, d), F32),
                          jnp.broadcast_to(meta_tokens[None], (b, N_META, d)), x_prompt], axis=1)
    xs = x_sample.transpose(1, 0, 2)
    zero_wkv = jnp.zeros((b, nh, HEAD_SIZE, HEAD_SIZE), F32)

    pool_p, pool_s, shift_p, shift_s, wkv_p, wkv_s, conv_p, conv_s = [], [], [], [], [], [], [], []
    vfirst_p = vfirst_s = None
    for i in range(depth):
        j = i // 2
        if i % 2 == 0:
            xp, st = _pool_prompt(xp, row(norm_mix[i]), pw[j], row(pool_scale[j]), tp=tp)
            pool_p.append(st[:, POOL_HALO - POOL_STATE:])
            xs, st = _pool_sample(xs, state_pool[j], row(norm_mix[i]), pw[j], row(pool_scale[j]), nt=nt, bs=bs)
            pool_s.append(st)
        else:
            p = rw[j]
            r, ld, k, v, kn, a, g, hl = _proj(xp.reshape(b * tp, d), None, p, vls[j], vfirst_p,
                                              layer=j, tp=tp, sample=False, bs=0)
            if vfirst_p is None:
                vfirst_p = v
            shift_p.append(hl[:, CONV_HALO - 1])
            sq = lambda t: t.reshape(b, tp, d)
            z, s_new = _wkv_prompt(sq(r), sq(ld), sq(k), sq(v), sq(kn), sq(a), sq(g), zero_wkv,
                                   p["lnw"], p["lnb"], p["rk"])
            wkv_p.append(s_new)
            xp = _wo(xp.reshape(b * tp, d), z.reshape(b * tp, d), wo_all, layer=j).reshape(b, tp, d)
            r, ld, k, v, kn, a, g, hl = _proj(xs.reshape(nt * bs, d), state_rwkv_shift[j], p, vls[j], vfirst_s,
                                              layer=j, tp=0, sample=True, bs=bs)
            if vfirst_s is None:
                vfirst_s = v
            shift_s.append(hl)
            z, s_new = _wkv_sample(r, ld, k, v, kn, a, g, state_rwkv_wkv[j], p["lnw"], p["lnb"], p["rk"])
            wkv_s.append(s_new)
            xs = _wo(xs.reshape(nt * bs, d), z, wo_all, layer=j, nt=nt).reshape(nt, bs, d)
        fi = ffn[i]
        last = i == depth - 1
        xo, stv, stg = _ffn_prompt(xp.reshape(b * tp, d), fi["g"], w_in_all, fi["cw"], fi["cb"], w_out_all,
                                   g_out, layer=i, tp=tp, final_norm=last)
        xp = xo.reshape(b, tp, d)
        stv, stg = (t.reshape(b, -1, CONV_HALO, fp)[:, -1] for t in (stv, stg))
        conv_p.append(jnp.concatenate([stv[:, CONV_HALO - 2:, :f], stg[:, CONV_HALO - 2:, :f]], axis=-1))
        xo, sv, sg = _ffn_sample(xs.reshape(nt * bs, d), fi["g"], w_in_all, fi["cw"], fi["cb"], w_out_all, g_out,
                                 state_ffn_conv[i][:, :, :f], state_ffn_conv[i][:, :, f:],
                                 layer=i, bs=bs, nt=nt, final_norm=last)
        xs = xo.reshape(nt, bs, d)
        conv_s.append(jnp.concatenate([sv[:, :, :f], sg[:, :, :f]], axis=-1))

    y_prompt = xp[:, FRONT_PAD + N_META:]
    y_sample = xs.transpose(1, 0, 2)
    return (y_prompt, y_sample, jnp.stack(pool_p), jnp.stack(pool_s), jnp.stack(shift_p), jnp.stack(shift_s),
            jnp.stack(wkv_p), jnp.stack(wkv_s), jnp.stack(conv_p), jnp.stack(conv_s))
```

```python
import functools
import math

import jax
import jax.numpy as jnp
from jax import lax
from jax.experimental import pallas as pl
from jax.experimental.pallas import tpu as pltpu

F32 = jnp.float32
BF16 = jnp.bfloat16

HEAD_SIZE = 64
LANES = 128
SUBLANES = 8
N_META = 16
PAST_LEN = 16384
POOL_WINDOWS = (2, 4, 8, 16)
POOL_STATE = max(POOL_WINDOWS) - 1
POOL_HALO = 16
CONV_WIDTH = 3
CONV_HALO = 8
NORM_EPS = 1e-6
GN_EPS = 64e-5
CHUNK = 64
SAMPLE_CHUNK = 16
SAMPLE_TOK = 8
FRONT_PAD = CHUNK - N_META

FFN_TM_CAP = 704
PROJ_TM_CAP = 352
POOL_TM_CAP = 704
WO_TM_CAP = 704
WKV_ROWS_CAP = 192
WKV_PAIRS = 16
WKV_SAMPLE_PAIRS = 8
CAST_ROWS = 512
FFN_TF = 512
PROJ_TN = 512
PROJ_TN_SAMPLE = 256
VMEM_LIMIT = 56 * 1024 * 1024


def _row_tile(n, cap, mult=16):
    best = None
    for t in range(mult, min(n, cap) + 1, mult):
        if n % t == 0:
            best = t
    assert best is not None, (n, cap)
    return best


def _round_up(n, m):
    return (n + m - 1) // m * m


def _params(sem):
    return pltpu.CompilerParams(dimension_semantics=sem, vmem_limit_bytes=VMEM_LIMIT)


def _rms(x, g):
    return x * lax.rsqrt(jnp.mean(x * x, axis=-1, keepdims=True) + NORM_EPS) * g


def _dot(a, b):
    return jnp.dot(a, b, preferred_element_type=F32)


def _split(x):
    hi = x.astype(BF16)
    lo = (x - hi.astype(F32)).astype(BF16)
    return hi, lo


def _cast_kernel(x_ref, o_ref, *, valid_rows, tr):
    x = x_ref[...]
    if valid_rows is not None:
        row = pl.program_id(1) * tr + lax.broadcasted_iota(jnp.int32, (1, tr, 1), 1)
        x = jnp.where(row < valid_rows, x, 0.0)
    o_ref[...] = x.astype(o_ref.dtype)


def _cast_rows(w, out_rows):
    n, rows, cols = w.shape
    tr = CAST_ROWS
    kern = functools.partial(_cast_kernel, valid_rows=None if rows % tr == 0 else rows, tr=tr)
    return pl.pallas_call(
        kern,
        grid=(n, out_rows // tr),
        in_specs=[pl.BlockSpec((1, tr, cols), lambda l, i: (l, i, 0))],
        out_specs=pl.BlockSpec((1, tr, cols), lambda l, i: (l, i, 0)),
        out_shape=jax.ShapeDtypeStruct((n, out_rows, cols), BF16),
        compiler_params=_params(("arbitrary", "arbitrary")),
        name="cast_rows",
    )(w)


def _cast_halves_kernel(x_ref, o_ref, *, f, fp):
    x = x_ref[0]
    pad = jnp.zeros((x.shape[0], fp - f), o_ref.dtype)
    o_ref[0, :, :f] = x[:, :f].astype(o_ref.dtype)
    o_ref[0, :, fp:fp + f] = x[:, f:].astype(o_ref.dtype)
    if fp > f:
        o_ref[0, :, f:fp] = pad
        o_ref[0, :, fp + f:] = pad


def _cast_halves(w, fp):
    n, rows, f2 = w.shape
    f = f2 // 2
    tr = min(rows, CAST_ROWS // 2)
    return pl.pallas_call(
        functools.partial(_cast_halves_kernel, f=f, fp=fp),
        grid=(n, rows // tr),
        in_specs=[pl.BlockSpec((1, tr, f2), lambda l, i: (l, i, 0))],
        out_specs=pl.BlockSpec((1, tr, 2 * fp), lambda l, i: (l, i, 0)),
        out_shape=jax.ShapeDtypeStruct((n, rows, 2 * fp), BF16),
        compiler_params=_params(("arbitrary", "arbitrary")),
        name="cast_halves",
    )(w)


def _pool_prompt_kernel(x_ref, g_ref, w_ref, sc_ref, o_ref, st_ref, carry_ref, *, tm, padf, cg):
    i = pl.program_id(1)

    @pl.when(i == 0)
    def _():
        carry_ref[...] = jnp.zeros_like(carry_ref)

    x = x_ref[0]
    row = i * tm + lax.broadcasted_iota(jnp.int32, (tm, 1), 0)
    h = jnp.where(row >= padf, _rms(x, g_ref[...]), 0.0)
    ext = jnp.concatenate([carry_ref[...], h], axis=0)
    pos = row - padf
    for g, w in enumerate(POOL_WINDOWS):
        sl = slice(g * cg, (g + 1) * cg)
        a = ext[:, sl]
        k = 1
        while k < w:
            n = a.shape[0]
            a = a[:n - k] + a[k:]
            k *= 2
        win = a[POOL_HALO + 1 - w: POOL_HALO + 1 - w + tm]
        cnt = jnp.clip(pos + 1, 1, w).astype(F32)
        d = win / cnt - h[:, sl]
        y = _dot(d.astype(BF16), w_ref[g])
        o_ref[0, :, sl] = x[:, sl] + y * sc_ref[:, sl]
    carry_ref[...] = h[tm - POOL_HALO:]
    st_ref[0] = h[tm - POOL_HALO:]


def _pool_prompt(x, g, w, sc, *, tp):
    b, _, d = x.shape
    tm = _row_tile(tp, POOL_TM_CAP)
    cg = d // len(POOL_WINDOWS)
    kern = functools.partial(_pool_prompt_kernel, tm=tm, padf=FRONT_PAD, cg=cg)
    return pl.pallas_call(
        kern,
        grid=(b, tp // tm),
        in_specs=[
            pl.BlockSpec((1, tm, d), lambda bi, i: (bi, i, 0)),
            pl.BlockSpec((1, d), lambda bi, i: (0, 0)),
            pl.BlockSpec((len(POOL_WINDOWS), cg, cg), lambda bi, i: (0, 0, 0)),
            pl.BlockSpec((1, d), lambda bi, i: (0, 0)),
        ],
        out_specs=[
            pl.BlockSpec((1, tm, d), lambda bi, i: (bi, i, 0)),
            pl.BlockSpec((1, POOL_HALO, d), lambda bi, i: (bi, 0, 0)),
        ],
        out_shape=[
            jax.ShapeDtypeStruct((b, tp, d), F32),
            jax.ShapeDtypeStruct((b, POOL_HALO, d), F32),
        ],
        scratch_shapes=[pltpu.VMEM((POOL_HALO, d), F32)],
        compiler_params=_params(("arbitrary", "arbitrary")),
        name="pool_prompt",
    )(x, g, w, sc)


def _pool_sample_kernel(x_ref, pre_ref, g_ref, w_ref, sc_ref, o_ref, st_ref, *, tb, d, cg, nt, start):
    hs = [_rms(x_ref[t], g_ref[...]) for t in range(nt)]
    ext = [pre_ref[:, j, :] for j in range(POOL_STATE)] + hs
    for g, w in enumerate(POOL_WINDOWS):
        sl = slice(g * cg, (g + 1) * cg)
        ds = []
        for t in range(nt):
            e = POOL_STATE + t
            acc = ext[e][:, sl]
            for q in range(1, w):
                acc = acc + ext[e - q][:, sl]
            cnt = float(min(w, start + t + 1))
            ds.append(acc / cnt - hs[t][:, sl])
        y = _dot(jnp.concatenate(ds, axis=0).astype(BF16), w_ref[g])
        for t in range(nt):
            o_ref[t, :, sl] = x_ref[t][:, sl] + y[t * tb:(t + 1) * tb] * sc_ref[:, sl]
    for j in range(POOL_STATE):
        st_ref[:, j, :] = ext[nt + j]


def _pool_sample(x, pre, g, w, sc, *, nt, bs):
    d = x.shape[-1]
    tb = min(bs, 32)
    cg = d // len(POOL_WINDOWS)
    kern = functools.partial(_pool_sample_kernel, tb=tb, d=d, cg=cg, nt=nt, start=PAST_LEN)
    return pl.pallas_call(
        kern,
        grid=(bs // tb,),
        in_specs=[
            pl.BlockSpec((nt, tb, d), lambda i: (0, i, 0)),
            pl.BlockSpec((tb, POOL_STATE, d), lambda i: (i, 0, 0)),
            pl.BlockSpec((1, d), lambda i: (0, 0)),
            pl.BlockSpec((len(POOL_WINDOWS), cg, cg), lambda i: (0, 0, 0)),
            pl.BlockSpec((1, d), lambda i: (0, 0)),
        ],
        out_specs=[
            pl.BlockSpec((nt, tb, d), lambda i: (0, i, 0)),
            pl.BlockSpec((tb, POOL_STATE, d), lambda i: (i, 0, 0)),
        ],
        out_shape=[
            jax.ShapeDtypeStruct((nt, bs, d), F32),
            jax.ShapeDtypeStruct((bs, POOL_STATE, d), F32),
        ],
        compiler_params=_params(("arbitrary",)),
        name="pool_sample",
    )(x, pre, g, w, sc)


def _ffn_prompt_kernel(x_ref, g_ref, wv_ref, wg_ref, cwv_ref, cwg_ref, cbv_ref, cbg_ref, wo_ref, go_ref,
                       o_ref, stv_ref, stg_ref, hb_ref, carv_ref, carg_ref, *, tm, tps, padf, nj, final_norm):
    i = pl.program_id(0)
    j = pl.program_id(1)
    ti = i % tps

    @pl.when(j == 0)
    def _():
        x = x_ref[...]
        row = ti * tm + lax.broadcasted_iota(jnp.int32, (tm, 1), 0)
        h = jnp.where(row >= padf, _rms(x, g_ref[...]), 0.0)
        hb_ref[...] = h.astype(BF16)
        o_ref[...] = x

    hb = hb_ref[...]
    keep = ti != 0

    def branch(w_ref, cw_ref, cb_ref, car_ref, st_ref):
        u = _dot(hb, w_ref[...])
        prev = jnp.where(keep, car_ref[j], 0.0)
        ext = jnp.concatenate([prev, u], axis=0)
        cw = cw_ref[...]
        c = (cb_ref[...] + ext[CONV_HALO - 2:CONV_HALO - 2 + tm] * cw[0:1]
             + ext[CONV_HALO - 1:CONV_HALO - 1 + tm] * cw[1:2] + u * cw[2:3])
        car_ref[j] = u[tm - CONV_HALO:]
        st_ref[0] = u[tm - CONV_HALO:]
        return c

    cv = branch(wv_ref, cwv_ref, cbv_ref, carv_ref, stv_ref)
    cgate = branch(wg_ref, cwg_ref, cbg_ref, carg_ref, stg_ref)
    act = cgate * jax.nn.sigmoid(cgate) * cv
    o_ref[...] += _dot(act.astype(BF16), wo_ref[...])

    if final_norm:
        @pl.when(j == nj - 1)
        def _():
            o_ref[...] = _rms(o_ref[...], go_ref[...])


def _ffn_prompt(x, g, w_in, cw, cb, w_out, g_out, *, layer, tp, final_norm):
    rows, d = x.shape
    fp = w_out.shape[1]
    tf = FFN_TF
    nj = fp // tf
    tm = _row_tile(tp, FFN_TM_CAP)
    tps = tp // tm
    kern = functools.partial(_ffn_prompt_kernel, tm=tm, tps=tps, padf=FRONT_PAD, nj=nj, final_norm=final_norm)
    return pl.pallas_call(
        kern,
        grid=(rows // tm, nj),
        in_specs=[
            pl.BlockSpec((tm, d), lambda i, j: (i, 0)),
            pl.BlockSpec((1, d), lambda i, j: (0, 0)),
            pl.BlockSpec((None, d, tf), lambda i, j: (layer, 0, j)),
            pl.BlockSpec((None, d, tf), lambda i, j: (layer, 0, nj + j)),
            pl.BlockSpec((CONV_WIDTH, tf), lambda i, j: (0, j)),
            pl.BlockSpec((CONV_WIDTH, tf), lambda i, j: (0, nj + j)),
            pl.BlockSpec((1, tf), lambda i, j: (0, j)),
            pl.BlockSpec((1, tf), lambda i, j: (0, nj + j)),
            pl.BlockSpec((None, tf, d), lambda i, j: (layer, j, 0)),
            pl.BlockSpec((1, d), lambda i, j: (0, 0)),
        ],
        out_specs=[
            pl.BlockSpec((tm, d), lambda i, j: (i, 0)),
            pl.BlockSpec((1, CONV_HALO, tf), lambda i, j: (i, 0, j)),
            pl.BlockSpec((1, CONV_HALO, tf), lambda i, j: (i, 0, j)),
        ],
        out_shape=[
            jax.ShapeDtypeStruct((rows, d), F32),
            jax.ShapeDtypeStruct((rows // tm, CONV_HALO, fp), F32),
            jax.ShapeDtypeStruct((rows // tm, CONV_HALO, fp), F32),
        ],
        scratch_shapes=[
            pltpu.VMEM((tm, d), BF16),
            pltpu.VMEM((nj, CONV_HALO, tf), F32),
            pltpu.VMEM((nj, CONV_HALO, tf), F32),
        ],
        compiler_params=_params(("arbitrary", "arbitrary")),
        name="ffn_prompt",
    )(x, g, w_in, w_in, cw, cw, cb, cb, w_out, g_out)


def _ffn_sample_kernel(x_ref, g_ref, wv_ref, wg_ref, cwv_ref, cwg_ref, cbv_ref, cbg_ref, wo_ref, go_ref,
                       pv_ref, pg_ref, o_ref, sv_ref, sg_ref, hb_ref, *, bs, nt, nj, tf, f, final_norm):
    j = pl.program_id(0)

    @pl.when(j == 0)
    def _():
        x = x_ref[...]
        hb_ref[...] = _rms(x, g_ref[...]).astype(BF16)
        o_ref[...] = x

    hb = hb_ref[...]
    valid = j * tf + lax.broadcasted_iota(jnp.int32, (1, tf), 1) < f

    def branch(w_ref, cw_ref, cb_ref, p_ref, s_ref):
        u = _dot(hb, w_ref[...])
        prev = [jnp.where(valid, p_ref[:, q, :], 0.0) for q in range(CONV_WIDTH - 1)]
        ext = prev + [u[t * bs:(t + 1) * bs] for t in range(nt)]
        cw = cw_ref[...]
        cs = [cb_ref[...] + ext[t] * cw[0:1] + ext[t + 1] * cw[1:2] + ext[t + 2] * cw[2:3] for t in range(nt)]
        for q in range(CONV_WIDTH - 1):
            s_ref[:, q, :] = ext[nt + q]
        return jnp.concatenate(cs, axis=0)

    cv = branch(wv_ref, cwv_ref, cbv_ref, pv_ref, sv_ref)
    cgate = branch(wg_ref, cwg_ref, cbg_ref, pg_ref, sg_ref)
    act = cgate * jax.nn.sigmoid(cgate) * cv
    o_ref[...] += _dot(act.astype(BF16), wo_ref[...])

    if final_norm:
        @pl.when(j == nj - 1)
        def _():
            o_ref[...] = _rms(o_ref[...], go_ref[...])


def _ffn_sample(x, g, w_in, cw, cb, w_out, g_out, st_val, st_gate, *, layer, bs, nt, final_norm):
    rows, d = x.shape
    fp = w_out.shape[1]
    f = st_val.shape[-1]
    tf = FFN_TF
    nj = fp // tf
    kern = functools.partial(_ffn_sample_kernel, bs=bs, nt=nt, nj=nj, tf=tf, f=f, final_norm=final_norm)
    st_spec = pl.BlockSpec((bs, CONV_WIDTH - 1, tf), lambda j: (0, 0, j))
    return pl.pallas_call(
        kern,
        grid=(nj,),
        in_specs=[
            pl.BlockSpec((rows, d), lambda j: (0, 0)),
            pl.BlockSpec((1, d), lambda j: (0, 0)),
            pl.BlockSpec((None, d, tf), lambda j: (layer, 0, j)),
            pl.BlockSpec((None, d, tf), lambda j: (layer, 0, nj + j)),
            pl.BlockSpec((CONV_WIDTH, tf), lambda j: (0, j)),
            pl.BlockSpec((CONV_WIDTH, tf), lambda j: (0, nj + j)),
            pl.BlockSpec((1, tf), lambda j: (0, j)),
            pl.BlockSpec((1, tf), lambda j: (0, nj + j)),
            pl.BlockSpec((None, tf, d), lambda j: (layer, j, 0)),
            pl.BlockSpec((1, d), lambda j: (0, 0)),
            st_spec, st_spec,
        ],
        out_specs=[pl.BlockSpec((rows, d), lambda j: (0, 0)), st_spec, st_spec],
        out_shape=[jax.ShapeDtypeStruct((rows, d), F32)]
        + [jax.ShapeDtypeStruct((bs, CONV_WIDTH - 1, fp), F32)] * 2,
        scratch_shapes=[pltpu.VMEM((rows, d), BF16)],
        compiler_params=_params(("arbitrary",)),
        name="ffn_sample",
    )(x, g, w_in, w_in, cw, cw, cb, cb, w_out, g_out, st_val, st_gate)


def _proj_kernel(*refs, tm, tps, padf, sample, bs, has_vlora):
    it = iter(refs)
    x_ref = next(it)
    sh_ref = next(it) if sample else None
    g_ref, mu_ref, wr_ref, wk_ref, wv_ref, w1_ref, a1_ref, g1_ref = (next(it) for _ in range(8))
    w2_ref, a2_ref, g2_ref, w0_ref, a0_ref, kk_ref, ka_ref, bd_ref = (next(it) for _ in range(8))
    if has_vlora:
        v1_ref, v2_ref, v0_ref, vf_ref = (next(it) for _ in range(4))
    r_o, ld_o, k_o, v_o, kn_o, a_o, g_o, hl_o = (next(it) for _ in range(8))
    xr_s, xk_s, xv_s, lw_s, la_s, lg_s = (next(it) for _ in range(6))
    lv_s = next(it) if has_vlora else None
    car_s = None if sample else next(it)

    i = pl.program_id(0)
    j = pl.program_id(1)

    @pl.when(j == 0)
    def _():
        x = x_ref[...]
        h = _rms(x, g_ref[...])
        if sample:
            prev = jnp.concatenate([sh_ref[...], h[:tm - bs]], axis=0)
            hl_o[...] = h[tm - bs:]
        else:
            ti = i % tps
            rloc = lax.broadcasted_iota(jnp.int32, (tm, 1), 0)
            h = jnp.where(ti * tm + rloc >= padf, h, 0.0)
            last = jnp.where(ti != 0, car_s[CONV_HALO - 1:CONV_HALO, :], 0.0)
            prev = jnp.where(rloc == 0, last, pltpu.roll(h, 1, 0))
            car_s[...] = h[tm - CONV_HALO:]
            hl_o[0] = h[tm - CONV_HALO:]
        xx = prev - h
        mu = mu_ref[...]
        mix = lambda q: (h + xx * mu[q:q + 1]).astype(BF16)
        xr_s[...] = mix(0)
        xk_s[...] = mix(2)
        xv = mix(3)
        xv_s[...] = xv
        lw_s[...] = jnp.tanh(_dot(mix(1), w1_ref[...])).astype(BF16)
        la_s[...] = _dot(mix(4), a1_ref[...]).astype(BF16)
        lg_s[...] = jax.nn.sigmoid(_dot(mix(5), g1_ref[...])).astype(BF16)
        if has_vlora:
            lv_s[...] = _dot(xv, v1_ref[...]).astype(BF16)

    nt = tm // bs if sample else 0

    def put(o_ref, val):
        if sample:
            for t in range(SAMPLE_TOK):
                o_ref[:, t, :] = val[t * bs:(t + 1) * bs] if t < nt else jnp.zeros((bs, val.shape[1]), F32)
        else:
            o_ref[...] = val

    r = _dot(xr_s[...], wr_ref[...])
    k = _dot(xk_s[...], wk_ref[...])
    v = _dot(xv_s[...], wv_ref[...])
    z = w0_ref[...] + _dot(lw_s[...], w2_ref[...])
    a = jax.nn.sigmoid(a0_ref[...] + _dot(la_s[...], a2_ref[...]))
    put(ld_o, -math.exp(-0.5) * jax.nn.sigmoid(z))
    if has_vlora:
        vf = jnp.concatenate([vf_ref[:, t, :] for t in range(nt)], axis=0) if sample else vf_ref[...]
        v = v + (vf - v) * jax.nn.sigmoid(v0_ref[...] + _dot(lv_s[...], v2_ref[...]))
    put(g_o, _dot(lg_s[...], g2_ref[...]))
    kk = k * kk_ref[...]
    sq_hi, sq_lo = _split(kk * kk)
    ss = _dot(sq_hi, bd_ref[...]) + _dot(sq_lo, bd_ref[...])
    put(kn_o, kk / jnp.maximum(jnp.sqrt(ss), 1e-12))
    put(r_o, r)
    put(k_o, k * (1.0 + (a - 1.0) * ka_ref[...]))
    put(v_o, v)
    put(a_o, a)


def _proj(x, sh, p, vl, vfirst, *, layer, tp, sample, bs):
    rows, d = x.shape
    tn = min(PROJ_TN_SAMPLE if sample else PROJ_TN, d)
    nj = d // tn
    p = dict(p, bd=p["bd"][:tn, :tn])
    if sample:
        tm, tps = rows, 1
    else:
        tm = _row_tile(tp, PROJ_TM_CAP)
        tps = tp // tm
    has_vlora = vl is not None
    kern = functools.partial(_proj_kernel, tm=tm, tps=tps, padf=FRONT_PAD, sample=sample, bs=bs,
                             has_vlora=has_vlora)
    full = lambda a: pl.BlockSpec(a.shape, lambda i, j: (0,) * a.ndim)
    colb = lambda a: pl.BlockSpec((a.shape[0], tn), lambda i, j: (0, j))
    if sample:
        rowb = pl.BlockSpec((bs, SAMPLE_TOK, tn), lambda i, j: (0, 0, j))
        row_shape = jax.ShapeDtypeStruct((bs, SAMPLE_TOK, d), F32)
    else:
        rowb = pl.BlockSpec((tm, tn), lambda i, j: (i, j))
        row_shape = jax.ShapeDtypeStruct((rows, d), F32)
    args, specs = [x], [pl.BlockSpec((tm, d), lambda i, j: (i, 0))]
    if sample:
        args.append(sh)
        specs.append(full(sh))
    for name in ("g", "mu"):
        args.append(p[name]); specs.append(full(p[name]))
    for name in ("wr", "wk", "wv"):
        args.append(p[name]); specs.append(pl.BlockSpec((None, d, tn), lambda i, j: (layer, 0, j)))
    for name in ("w1", "a1", "g1"):
        args.append(p[name]); specs.append(full(p[name]))
    for name in ("w2", "a2", "g2", "w0", "a0", "kk", "ka"):
        args.append(p[name]); specs.append(colb(p[name]))
    args.append(p["bd"]); specs.append(full(p["bd"]))
    if has_vlora:
        args += [vl["v1"], vl["v2"], vl["v0"], vfirst]
        specs += [full(vl["v1"]), colb(vl["v2"]), colb(vl["v0"]), rowb]
    out_shape = [row_shape] * 7
    out_specs = [rowb] * 7
    if sample:
        out_shape.append(jax.ShapeDtypeStruct((bs, d), F32))
        out_specs.append(pl.BlockSpec((bs, d), lambda i, j: (0, 0)))
    else:
        nb = rows // tp
        out_shape.append(jax.ShapeDtypeStruct((nb, CONV_HALO, d), F32))
        out_specs.append(pl.BlockSpec((1, CONV_HALO, d), lambda i, j: (i // tps, 0, 0)))
    lw, la, lg = p["w1"].shape[1], p["a1"].shape[1], p["g1"].shape[1]
    scratch = [pltpu.VMEM((tm, d), BF16)] * 3 + [pltpu.VMEM((tm, lw), BF16), pltpu.VMEM((tm, la), BF16),
                                                 pltpu.VMEM((tm, lg), BF16)]
    if has_vlora:
        scratch.append(pltpu.VMEM((tm, vl["v1"].shape[1]), BF16))
    if not sample:
        scratch.append(pltpu.VMEM((CONV_HALO, d), F32))
    return pl.pallas_call(
        kern,
        grid=(rows // tm, nj),
        in_specs=specs,
        out_specs=out_specs,
        out_shape=out_shape,
        scratch_shapes=scratch,
        compiler_params=_params(("arbitrary", "arbitrary")),
        name="rwkv_proj_sample" if sample else "rwkv_proj_prompt",
    )(*args)


def _mm(a, b):
    return _dot(a.astype(BF16), b.astype(BF16))


def _mm_nt(a, b):
    return lax.dot_general(a.astype(BF16), b.astype(BF16), (((1,), (1,)), ((), ())),
                           preferred_element_type=F32)


def _mm3(a, b):
    ah, al = _split(a)
    bh, bl = _split(b)
    return _dot(ah, bh) + _dot(ah, bl) + _dot(al, bh)


def _wkv_kernel(r_ref, ld_ref, k_ref, v_ref, kn_ref, a_ref, g_ref, s0_ref, lnw_ref, lnb_ref, rk_ref,
                z_ref, so_ref, *, bb, L, nchunks):
    hs = HEAD_SIZE
    lane = lax.broadcasted_iota(jnp.int32, (1, LANES), 1)
    head0 = lane < hs
    m0 = head0.astype(F32)
    m1 = 1.0 - m0
    ri = lax.broadcasted_iota(jnp.int32, (LANES, LANES), 0)
    ci = lax.broadcasted_iota(jnp.int32, (LANES, LANES), 1)
    bdmask = ((ri < hs) == (ci < hs)).astype(F32)
    rl = lax.broadcasted_iota(jnp.int32, (L, L), 0)
    cl = lax.broadcasted_iota(jnp.int32, (L, L), 1)
    tril_incl = (cl <= rl).astype(BF16)
    eye = (cl == rl).astype(F32)
    prow = lax.broadcasted_iota(jnp.int32, (4 * L, 2 * L), 0)
    pcol = lax.broadcasted_iota(jnp.int32, (4 * L, 2 * L), 1) % L
    pmask = jnp.where(prow >= 2 * L, (pcol <= prow % L).astype(F32), (pcol < prow % L).astype(F32))
    kcols = (lax.broadcasted_iota(jnp.int32, (1, 2 * L), 1) >= L).astype(F32)
    lnw, lnb, rk = lnw_ref[...], lnb_ref[...], rk_ref[...]
    nsteps = int(math.log2(L)) - 1

    def sel(zz):
        return jnp.where(head0, zz[:L], zz[L:])

    def head_sum(y):
        s0 = jnp.sum(y * m0, axis=-1, keepdims=True)
        s1 = jnp.sum(y * m1, axis=-1, keepdims=True)
        return jnp.where(head0, s0, s1)

    def inverse(a):
        t = eye + a
        pw = a
        for _ in range(nsteps):
            pw = _mm3(pw, pw)
            t = t + _mm3(t, pw)
        return t

    def seq_body(s, carry):
        s_init = s0_ref[s]
        zed = jnp.zeros((hs, hs), F32)
        state0 = jnp.concatenate([jnp.concatenate([s_init[0], zed], axis=1),
                                  jnp.concatenate([zed, s_init[1]], axis=1)], axis=0)

        def chunk_body(c, state):
            rows = pl.ds(pl.multiple_of(c * L, L), L)
            r, ld, k, v = r_ref[s, rows, :], ld_ref[s, rows, :], k_ref[s, rows, :], v_ref[s, rows, :]
            kn, a, g = kn_ref[s, rows, :], a_ref[s, rows, :], g_ref[s, rows, :]
            ld_hi = ld.astype(BF16)
            ld_r = ld - ld_hi.astype(F32)
            ld_mid = ld_r.astype(BF16)
            ld_lo = (ld_r - ld_mid.astype(F32)).astype(BF16)
            cs = _dot(tril_incl, ld_hi) + _dot(tril_incl, ld_mid) + _dot(tril_incl, ld_lo)
            c_end = cs[L - 1:L, :]
            w_in = jnp.exp(cs)
            w_inv = jnp.exp(-cs)
            w_prev = jnp.exp(cs - ld)
            w_end = jnp.exp(c_end - cs)
            b = kn * a
            rt, kt, bt, at = r * w_in, k * w_inv, b * w_inv, -kn * w_prev
            kh, bh = k * w_end, b * w_end
            lhs = jnp.concatenate([at * m0, at * m1, rt * m0, rt * m1], axis=0)
            pm = _mm_nt(lhs, jnp.concatenate([bt, kt], axis=0)) * pmask
            a_bk = pm[:2 * L]
            r_bk = pm[2 * L:]
            gs = _mm_nt(jnp.concatenate([at, rt], axis=0), state)
            vv = jnp.concatenate([v, v], axis=0)
            x = gs[:L] + sel(_mm(a_bk * kcols, vv))
            a_b = a_bk[:, :L]
            u = jnp.where(head0, _mm3(inverse(a_b[:L]), x), _mm3(inverse(a_b[L:]), x))
            uv = jnp.concatenate([u, v], axis=0)
            y = gs[L:] + sel(_mm(r_bk, uv))
            upd = _mm(uv.T, jnp.concatenate([bh, kh], axis=0))
            state = state * jnp.exp(c_end) + upd * bdmask
            mean = head_sum(y) * (1.0 / hs)
            yc = y - mean
            var = head_sum(yc * yc) * (1.0 / hs)
            yn = yc * lax.rsqrt(var + GN_EPS) * lnw + lnb
            bonus = head_sum(r * k * rk) * v
            z_ref[s, rows, :] = ((yn + bonus) * g).astype(z_ref.dtype)
            return state

        state = lax.fori_loop(0, nchunks, chunk_body, state0)
        so_ref[s, 0] = state[:hs, :hs]
        so_ref[s, 1] = state[hs:, hs:]
        return carry

    lax.fori_loop(0, bb, seq_body, 0)


def _wkv(r, ld, k, v, kn, a, g, s0, lnw, lnb, rk, *, L, bb):
    nb, tt, d = r.shape
    nh = d // HEAD_SIZE
    kern = functools.partial(_wkv_kernel, bb=bb, L=L, nchunks=tt // L)
    seqb = pl.BlockSpec((bb, tt, LANES), lambda i, p: (i, 0, p))
    vecb = pl.BlockSpec((1, LANES), lambda i, p: (0, p))
    stb = pl.BlockSpec((bb, 2, HEAD_SIZE, HEAD_SIZE), lambda i, p: (i, p, 0, 0))
    return pl.pallas_call(
        kern,
        grid=(nb // bb, nh // 2),
        in_specs=[seqb] * 7 + [stb, vecb, vecb, vecb],
        out_specs=[seqb, stb],
        out_shape=[jax.ShapeDtypeStruct((nb, tt, d), BF16),
                   jax.ShapeDtypeStruct((nb, nh, HEAD_SIZE, HEAD_SIZE), F32)],
        compiler_params=_params(("arbitrary", "arbitrary")),
        name=f"wkv_chunk{L}",
    )(r, ld, k, v, kn, a, g, s0, lnw, lnb, rk)


def _wkv_prompt_kernel(r_ref, ld_ref, k_ref, v_ref, kn_ref, a_ref, g_ref, s0_ref, lnw_ref, lnb_ref, rk_ref,
                       z_ref, so_ref, st_ref, *, L, nchunks, npairs):
    hs = HEAD_SIZE
    assert L == hs and npairs % 2 == 0
    ib = pl.program_id(2)
    lane = lax.broadcasted_iota(jnp.int32, (1, LANES), 1)
    head0 = lane < hs
    m0 = head0.astype(F32)
    m1 = 1.0 - m0
    m0w = jnp.concatenate([m0, m0], axis=1)
    m1w = 1.0 - m0w
    ri = lax.broadcasted_iota(jnp.int32, (LANES, LANES), 0)
    ci = lax.broadcasted_iota(jnp.int32, (LANES, LANES), 1)
    bdmask = ((ri < hs) == (ci < hs)).astype(F32)
    rl = lax.broadcasted_iota(jnp.int32, (L, L), 0)
    cl = lax.broadcasted_iota(jnp.int32, (L, L), 1)
    tril_incl = (cl <= rl).astype(BF16)
    prow = lax.broadcasted_iota(jnp.int32, (2 * L, 4 * L), 0)
    pcol = lax.broadcasted_iota(jnp.int32, (2 * L, 4 * L), 1) % L
    pmask = jnp.where(prow >= L, (pcol <= prow - L).astype(F32), (pcol < prow).astype(F32))
    qrow = lax.broadcasted_iota(jnp.int32, (4 * L, 4 * L), 0)
    qcol = lax.broadcasted_iota(jnp.int32, (4 * L, 4 * L), 1)
    bd4 = (qrow // L) == (qcol // L)
    eye4 = (lax.broadcasted_iota(jnp.int32, (L, 4 * L), 1) % L
            == lax.broadcasted_iota(jnp.int32, (L, 4 * L), 0)).astype(F32)
    nsq = int(math.log2(L))
    zeros_l = jnp.zeros((L, LANES), F32)
    zeros_b = jnp.zeros((4 * L, 4 * L), BF16)

    @pl.when(ib == 0)
    def _():
        zed = jnp.zeros((hs, hs), F32)
        for p in range(npairs):
            st_ref[p] = jnp.concatenate([jnp.concatenate([s0_ref[0, 2 * p], zed], axis=1),
                                         jnp.concatenate([zed, s0_ref[0, 2 * p + 1]], axis=1)], axis=0)

    def head_sum(y):
        s0 = jnp.sum(y * m0, axis=-1, keepdims=True)
        s1 = jnp.sum(y * m1, axis=-1, keepdims=True)
        return jnp.where(head0, s0, s1)

    split = lambda x: jnp.concatenate([x * m0, x * m1], axis=0)
    splitw = lambda x: jnp.concatenate([x * m0w, x * m1w], axis=0)

    def stream(rows, pairs, ld_all, cs_all):
        pr = range(len(pairs))
        lss = [slice(p * LANES, (p + 1) * LANES) for p in pairs]
        ld = [ld_all[:, ls] for ls in lss]
        cs = [cs_all[:, ls] for ls in lss]
        c_end = [x[L - 1:L, :] for x in cs]
        r = [r_ref[0, rows, ls] for ls in lss]
        k = [k_ref[0, rows, ls] for ls in lss]
        v = [v_ref[0, rows, ls] for ls in lss]
        kn = [kn_ref[0, rows, ls] for ls in lss]
        b = [kn[p] * a_ref[0, rows, lss[p]] for p in pr]
        w_inv = [jnp.exp(-x) for x in cs]
        rt = [r[p] * jnp.exp(cs[p]) for p in pr]
        kt = [k[p] * w_inv[p] for p in pr]
        bt = [b[p] * w_inv[p] for p in pr]
        at = [-kn[p] * jnp.exp(cs[p] - ld[p]) for p in pr]
        w_end = [jnp.exp(c_end[p] - cs[p]) for p in pr]
        kh = [k[p] * w_end[p] for p in pr]
        bh = [b[p] * w_end[p] for p in pr]
        yield
        pm = [_mm_nt(jnp.concatenate([at[p], rt[p]], axis=0),
                     jnp.concatenate([split(bt[p]), split(kt[p])], axis=0)) * pmask for p in pr]
        yield
        akv = [_mm(pm[p][:L, 2 * L:], split(v[p])) for p in pr]
        gr = range(len(pairs) // 2)
        pw = [jnp.concatenate([pm[2 * g][:L, :2 * L], pm[2 * g + 1][:L, :2 * L]], axis=1) for g in gr]
        tq = [eye4 for _ in gr]
        for i in range(nsq):
            blk = [jnp.where(bd4, jnp.concatenate([x.astype(BF16)] * 4, axis=0), zeros_b) for x in pw]
            if i < nsq - 1:
                both = [_dot(jnp.concatenate([pw[g], tq[g]], axis=0).astype(BF16), blk[g]) for g in gr]
                pw = [x[:L] for x in both]
                tq = [tq[g] + both[g][L:] for g in gr]
            else:
                tq = [tq[g] + _dot(tq[g].astype(BF16), blk[g]) for g in gr]
            yield
        tinv = [t[:, h * 2 * L:(h + 1) * 2 * L] for t in tq for h in range(2)]
        tz = [_mm(tinv[p], splitw(jnp.concatenate([at[p], akv[p]], axis=1))) for p in pr]
        yield
        zv = [jnp.concatenate([zeros_l, v[p]], axis=1) for p in pr]
        ry = [_mm(pm[p][L:], jnp.concatenate([splitw(tz[p]), splitw(zv[p])], axis=0)) for p in pr]
        nc = [_mm(jnp.concatenate([tz[p][:, :LANES], tz[p][:, LANES:], v[p]], axis=0).T,
                  jnp.concatenate([jnp.concatenate([bh[p], zeros_l], axis=1),
                                   jnp.concatenate([zeros_l, bh[p]], axis=1),
                                   jnp.concatenate([zeros_l, kh[p]], axis=1)], axis=0)) for p in pr]
        yield
        r2 = [rt[p] + ry[p][:, :LANES] for p in pr]
        state = [st_ref[q] for q in pairs]
        y = [_mm_nt(r2[p], state[p]) + ry[p][:, LANES:] for p in pr]
        for p, q in enumerate(pairs):
            st_ref[q] = (state[p] * jnp.exp(c_end[p]) + _mm(state[p], nc[p][:, :LANES] * bdmask)
                         + nc[p][:, LANES:] * bdmask)
        yield
        mean = [head_sum(x) * (1.0 / hs) for x in y]
        yc = [y[p] - mean[p] for p in pr]
        var = [head_sum(x * x) * (1.0 / hs) for x in yc]
        bonus = [head_sum(r[p] * k[p] * rk_ref[:, lss[p]]) * v[p] for p in pr]
        for p in pr:
            yn = yc[p] * lax.rsqrt(var[p] + GN_EPS) * lnw_ref[:, lss[p]] + lnb_ref[:, lss[p]]
            z_ref[0, rows, lss[p]] = ((yn + bonus[p]) * g_ref[0, rows, lss[p]]).astype(z_ref.dtype)

    halves = 1
    per = npairs // halves
    live = []
    for c in range(nchunks):
        rows = slice(c * L, (c + 1) * L)
        ld_all = ld_ref[0, rows, :]
        ld_hi = ld_all.astype(BF16)
        ld_r = ld_all - ld_hi.astype(F32)
        ld_mid = ld_r.astype(BF16)
        ld_lo = (ld_r - ld_mid.astype(F32)).astype(BF16)
        cs_all = _dot(tril_incl, ld_hi) + _dot(tril_incl, ld_mid) + _dot(tril_incl, ld_lo)
        live += [stream(rows, list(range(s * per, (s + 1) * per)), ld_all, cs_all) for s in range(halves)]
    lag = 0
    while live:
        for s in list(live[:lag + 1]):
            if next(s, "done") == "done":
                live.remove(s)
        lag += 1

    @pl.when(ib == pl.num_programs(2) - 1)
    def _():
        for p in range(npairs):
            state = st_ref[p]
            so_ref[0, 2 * p] = state[:hs, :hs]
            so_ref[0, 2 * p + 1] = state[hs:, hs:]


def _wkv_prompt(r, ld, k, v, kn, a, g, s0, lnw, lnb, rk):
    nb, tt, d = r.shape
    nh = d // HEAD_SIZE
    npairs = min(WKV_PAIRS, nh // 2)
    tr = _row_tile(tt, WKV_ROWS_CAP, CHUNK)
    lw = npairs * LANES
    kern = functools.partial(_wkv_prompt_kernel, L=CHUNK, nchunks=tr // CHUNK, npairs=npairs)
    seqb = pl.BlockSpec((1, tr, lw), lambda bi, p, i: (bi, i, p))
    vecb = pl.BlockSpec((1, lw), lambda bi, p, i: (0, p))
    stb = pl.BlockSpec((1, 2 * npairs, HEAD_SIZE, HEAD_SIZE), lambda bi, p, i: (bi, p, 0, 0))
    return pl.pallas_call(
        kern,
        grid=(nb, nh // (2 * npairs), tt // tr),
        in_specs=[seqb] * 7 + [stb, vecb, vecb, vecb],
        out_specs=[seqb, stb],
        out_shape=[jax.ShapeDtypeStruct((nb, tt, d), BF16),
                   jax.ShapeDtypeStruct((nb, nh, HEAD_SIZE, HEAD_SIZE), F32)],
        scratch_shapes=[pltpu.VMEM((npairs, LANES, LANES), F32)],
        compiler_params=_params(("arbitrary", "arbitrary", "arbitrary")),
        name="wkv_prompt",
    )(r, ld, k, v, kn, a, g, s0, lnw, lnb, rk)


def _wkv_sample_kernel(r_ref, ld_ref, k_ref, v_ref, kn_ref, a_ref, g_ref, s0_ref, lnw_ref, lnb_ref, rk_ref,
                       z_ref, so_ref, *, ngroups, npairs):
    hs, L, T = HEAD_SIZE, CHUNK, SAMPLE_TOK
    ns = L // T
    lane = lax.broadcasted_iota(jnp.int32, (1, LANES), 1)
    head0 = lane < hs
    head0w = (lax.broadcasted_iota(jnp.int32, (1, 2 * LANES), 1) % LANES) < hs
    m0 = head0.astype(F32)
    m1 = 1.0 - m0
    ri = lax.broadcasted_iota(jnp.int32, (LANES, LANES), 0)
    ci = lax.broadcasted_iota(jnp.int32, (LANES, LANES), 1)
    bdmask = ((ri < hs) == (ci < hs)).astype(F32)
    eye_w = (ri == ci).astype(BF16)
    rl = lax.broadcasted_iota(jnp.int32, (L, L), 0)
    cl = lax.broadcasted_iota(jnp.int32, (L, L), 1)
    same = (rl // T) == (cl // T)
    cum_lhs = jnp.concatenate([jnp.where(same, (cl % T <= rl % T).astype(F32), 0.0),
                               same.astype(F32)], axis=0).astype(BF16)
    eye = (cl == rl).astype(F32)
    prow = lax.broadcasted_iota(jnp.int32, (4 * L, 2 * L), 0)
    pcol = lax.broadcasted_iota(jnp.int32, (4 * L, 2 * L), 1) % L
    psame = ((prow % L) // T) == (pcol // T)
    pcaus = jnp.where(prow >= 2 * L, (pcol % T <= prow % T).astype(F32), (pcol % T < prow % T).astype(F32))
    pmask = jnp.where(psame, pcaus, 0.0)
    kcols = (lax.broadcasted_iota(jnp.int32, (1, 2 * L), 1) >= L).astype(F32)
    nsteps = int(math.log2(T)) - 1
    zeros_l = jnp.zeros((L, LANES), F32)
    zeros_t = jnp.zeros((T, LANES), F32)
    zed = jnp.zeros((hs, hs), F32)

    def head_sum(y):
        s0 = jnp.sum(y * m0, axis=-1, keepdims=True)
        s1 = jnp.sum(y * m1, axis=-1, keepdims=True)
        return jnp.where(head0, s0, s1)

    def group_body(gi, carry):
        seqs = pl.ds(pl.multiple_of(gi * ns, ns), ns)
        pr = range(npairs)
        lss = [slice(p * LANES, (p + 1) * LANES) for p in pr]
        tile = lambda ref, ls: ref[seqs, :, ls].reshape(L, LANES)
        ld = [tile(ld_ref, ls) for ls in lss]
        ld_hi = [x.astype(BF16) for x in ld]
        ld_r = [x - h.astype(F32) for x, h in zip(ld, ld_hi)]
        ld_mid = [x.astype(BF16) for x in ld_r]
        ld_lo = [(x - m.astype(F32)).astype(BF16) for x, m in zip(ld_r, ld_mid)]
        cc = [_dot(cum_lhs, ld_hi[p]) + _dot(cum_lhs, ld_mid[p]) + _dot(cum_lhs, ld_lo[p]) for p in pr]
        cs = [x[:L] for x in cc]
        c_end = [x[L:] for x in cc]
        r = [tile(r_ref, ls) for ls in lss]
        k = [tile(k_ref, ls) for ls in lss]
        v = [tile(v_ref, ls) for ls in lss]
        kn = [tile(kn_ref, ls) for ls in lss]
        b = [kn[p] * tile(a_ref, lss[p]) for p in pr]
        w_inv = [jnp.exp(-x) for x in cs]
        rt = [r[p] * jnp.exp(cs[p]) for p in pr]
        kt = [k[p] * w_inv[p] for p in pr]
        bt = [b[p] * w_inv[p] for p in pr]
        at = [-kn[p] * jnp.exp(cs[p] - ld[p]) for p in pr]
        w_end = [jnp.exp(c_end[p] - cs[p]) for p in pr]
        w_tot = [jnp.exp(x) for x in c_end]
        kh = [k[p] * w_end[p] for p in pr]
        bh = [b[p] * w_end[p] for p in pr]
        pm = [_mm_nt(jnp.concatenate([at[p] * m0, at[p] * m1, rt[p] * m0, rt[p] * m1], axis=0),
                     jnp.concatenate([bt[p], kt[p]], axis=0)) * pmask for p in pr]
        a_bk = [x[:2 * L] for x in pm]
        r_bk = [x[2 * L:] for x in pm]
        akv = [_mm(a_bk[p] * kcols, jnp.concatenate([v[p], v[p]], axis=0)) for p in pr]
        akv = [jnp.where(head0, x[:L], x[L:]) for x in akv]
        pw = [a_bk[p][h * L:(h + 1) * L, :L] for p in pr for h in range(2)]
        tinv = [eye + x for x in pw]
        for _ in range(nsteps):
            pw = [_mm(x, x) for x in pw]
            tinv = [t + _mm(t, x) for t, x in zip(tinv, pw)]
        tz_rhs = [jnp.concatenate([at[p], akv[p]], axis=1) for p in pr]
        tz = [jnp.where(head0w, _mm(tinv[2 * p], tz_rhs[p]), _mm(tinv[2 * p + 1], tz_rhs[p])) for p in pr]
        ry = [_mm(r_bk[p], jnp.concatenate([tz[p], jnp.concatenate([zeros_l, v[p]], axis=1)], axis=0))
              for p in pr]
        ry = [jnp.where(head0w, x[:L], x[L:]) for x in ry]
        r2 = [rt[p] + ry[p][:, :LANES] for p in pr]
        ps = [(p, i) for p in pr for i in range(ns)]
        rs = [slice(i * T, (i + 1) * T) for i in range(ns)]
        state = [jnp.concatenate([jnp.concatenate([s0_ref[gi * ns + i, 2 * p], zed], axis=1),
                                  jnp.concatenate([zed, s0_ref[gi * ns + i, 2 * p + 1]], axis=1)], axis=0)
                 for p, i in ps]
        sb = [x.astype(BF16) for x in state]
        e_rhs = [jnp.concatenate([tz[p][rs[i]], jnp.concatenate([zeros_t, v[p][rs[i]]], axis=1)], axis=0)
                 for p, i in ps]
        e = [_mm_nt(jnp.concatenate([sb[q], eye_w], axis=1), e_rhs[q]) for q in range(len(ps))]
        upd = [_mm(e[q], jnp.concatenate([bh[p][rs[i]], kh[p][rs[i]]], axis=0)) * bdmask
               for q, (p, i) in enumerate(ps)]
        ys = [_mm_nt(r2[p][rs[i]], sb[q]) for q, (p, i) in enumerate(ps)]
        for q, (p, i) in enumerate(ps):
            new = state[q] * w_tot[p][i * T:i * T + 1] + upd[q]
            so_ref[gi * ns + i, 2 * p] = new[:hs, :hs]
            so_ref[gi * ns + i, 2 * p + 1] = new[hs:, hs:]
        y = [jnp.concatenate(ys[p * ns:(p + 1) * ns], axis=0) + ry[p][:, LANES:] for p in pr]
        mean = [head_sum(x) * (1.0 / hs) for x in y]
        yc = [y[p] - mean[p] for p in pr]
        var = [head_sum(x * x) * (1.0 / hs) for x in yc]
        bonus = [head_sum(r[p] * k[p] * rk_ref[:, lss[p]]) * v[p] for p in pr]
        for p in pr:
            yn = yc[p] * lax.rsqrt(var[p] + GN_EPS) * lnw_ref[:, lss[p]] + lnb_ref[:, lss[p]]
            z_ref[seqs, :, lss[p]] = ((yn + bonus[p]) * tile(g_ref, lss[p])).reshape(ns, T, LANES)
        return carry

    lax.fori_loop(0, ngroups, group_body, 0)


def _wkv_sample(r, ld, k, v, kn, a, g, s0, lnw, lnb, rk):
    nb, tt, d = r.shape
    nh = d // HEAD_SIZE
    npairs = min(WKV_SAMPLE_PAIRS, nh // 2)
    ns = CHUNK // SAMPLE_TOK
    bb = min(nb, 2 * ns)
    lw = npairs * LANES
    kern = functools.partial(_wkv_sample_kernel, ngroups=bb // ns, npairs=npairs)
    seqb = pl.BlockSpec((bb, tt, lw), lambda i, p: (i, 0, p))
    vecb = pl.BlockSpec((1, lw), lambda i, p: (0, p))
    stb = pl.BlockSpec((bb, 2 * npairs, HEAD_SIZE, HEAD_SIZE), lambda i, p: (i, p, 0, 0))
    return pl.pallas_call(
        kern,
        grid=(nb // bb, nh // (2 * npairs)),
        in_specs=[seqb] * 7 + [stb, vecb, vecb, vecb],
        out_specs=[seqb, stb],
        out_shape=[jax.ShapeDtypeStruct((nb, tt, d), F32),
                   jax.ShapeDtypeStruct((nb, nh, HEAD_SIZE, HEAD_SIZE), F32)],
        compiler_params=_params(("arbitrary", "arbitrary")),
        name="wkv_sample",
    )(r, ld, k, v, kn, a, g, s0, lnw, lnb, rk)


def _wo_kernel(x_ref, z_ref, w_ref, o_ref, *, nt):
    if nt:
        z = jnp.concatenate([z_ref[:, t, :] for t in range(nt)], axis=0)
    else:
        z = z_ref[...]
    o_ref[...] = x_ref[...] + _dot(z.astype(BF16), w_ref[...])


def _wo(x, z, w, *, layer, nt=0):
    rows, d = x.shape
    tm = rows if nt else _row_tile(rows, WO_TM_CAP)
    zspec = pl.BlockSpec(z.shape, lambda i: (0, 0, 0)) if nt else pl.BlockSpec((tm, d), lambda i: (i, 0))
    return pl.pallas_call(
        functools.partial(_wo_kernel, nt=nt),
        grid=(rows // tm,),
        in_specs=[pl.BlockSpec((tm, d), lambda i: (i, 0)), zspec,
                  pl.BlockSpec((None, d, d), lambda i: (layer, 0, 0))],
        out_specs=pl.BlockSpec((tm, d), lambda i: (i, 0)),
        out_shape=jax.ShapeDtypeStruct((rows, d), F32),
        compiler_params=_params(("arbitrary",)),
        name="rwkv_wo_sample" if nt else "rwkv_wo",
    )(x, z, w)


def _pad_cols(a, n):
    return jnp.pad(a, [(0, 0)] * (a.ndim - 1) + [(0, n - a.shape[-1])])


def _pad_rows(a, n):
    return jnp.pad(a, [(0, n - a.shape[0])] + [(0, 0)] * (a.ndim - 1))


def kernel(x_prompt, x_sample, state_pool, state_rwkv_shift, state_rwkv_wkv, state_ffn_conv, meta_tokens,
           norm_mix, norm_ffn, norm_out, pool_w, pool_scale, rwkv_mu, rwkv_wr, rwkv_wk, rwkv_wv, rwkv_wo,
           rwkv_w0, rwkv_w1, rwkv_w2, rwkv_a0, rwkv_a1, rwkv_a2, rwkv_v0, rwkv_v1, rwkv_v2, rwkv_g1, rwkv_g2,
           rwkv_kk, rwkv_ka, rwkv_rk, rwkv_lnw, rwkv_lnb, ffn_w_in, ffn_conv_w, ffn_conv_b, ffn_w_out):
    b, seq, d = x_prompt.shape
    bs, nt, _ = x_sample.shape
    depth = norm_mix.shape[0]
    f = ffn_w_out.shape[1]
    fp = _round_up(f, FFN_TF)
    nh = d // HEAD_SIZE
    tp = FRONT_PAD + N_META + seq
    assert tp % CHUNK == 0 and d % (2 * LANES) == 0 and CONV_WIDTH - 1 <= nt <= SAMPLE_TOK

    row = lambda a: a.reshape(1, -1)
    lora_in = lambda a: _pad_cols(a, _round_up(a.shape[-1], LANES)).astype(BF16)
    lora_out = lambda a: _pad_rows(a, _round_up(a.shape[0], LANES)).astype(BF16)
    halves = lambda a: jnp.concatenate([_pad_cols(a[..., :f], fp), _pad_cols(a[..., f:], fp)], axis=-1)
    w_in_all = _cast_halves(ffn_w_in, fp)
    w_out_all = _cast_rows(ffn_w_out, fp)
    ffn = [dict(g=row(norm_ffn[i]), cw=halves(ffn_conv_w[i]), cb=halves(ffn_conv_b[i][None]))
           for i in range(depth)]
    tn = min(PROJ_TN, d)
    bd = jnp.kron(jnp.eye(tn // HEAD_SIZE, dtype=F32), jnp.ones((HEAD_SIZE, HEAD_SIZE), F32)).astype(BF16)
    wr_all, wk_all, wv_all, wo_all = (_cast_rows(w, d) for w in (rwkv_wr, rwkv_wk, rwkv_wv, rwkv_wo))
    rw = []
    for j in range(depth // 2):
        rw.append(dict(
            g=row(norm_mix[2 * j + 1]), mu=rwkv_mu[j], wr=wr_all, wk=wk_all, wv=wv_all,
            w1=lora_in(rwkv_w1[j]), a1=lora_in(rwkv_a1[j]), g1=lora_in(rwkv_g1[j]),
            w2=lora_out(rwkv_w2[j]), a2=lora_out(rwkv_a2[j]), g2=lora_out(rwkv_g2[j]),
            w0=row(rwkv_w0[j]), a0=row(rwkv_a0[j]), kk=row(rwkv_kk[j]), ka=row(rwkv_ka[j]), bd=bd,
            lnw=row(rwkv_lnw[j]), lnb=row(rwkv_lnb[j]), rk=row(rwkv_rk[j])))
    vls = [None] + [dict(v1=lora_in(rwkv_v1[j]), v2=lora_out(rwkv_v2[j]), v0=row(rwkv_v0[j]))
                    for j in range(depth // 2 - 1)]
    pw = [pool_w[j].astype(BF16) for j in range((depth + 1) // 2)]
    g_out = row(norm_out)

    xp = jnp.concatenate([jnp.zeros((b, FRONT_PAD, d), F32),
                          jnp.broadcast_to(meta_tokens[None], (b, N_META, d)), x_prompt], axis=1)
    xs = x_sample.transpose(1, 0, 2)
    zero_wkv = jnp.zeros((b, nh, HEAD_SIZE, HEAD_SIZE), F32)

    pool_p, pool_s, shift_p, shift_s, wkv_p, wkv_s, conv_p, conv_s = [], [], [], [], [], [], [], []
    vfirst_p = vfirst_s = None
    for i in range(depth):
        j = i // 2
        if i % 2 == 0:
            xp, st = _pool_prompt(xp, row(norm_mix[i]), pw[j], row(pool_scale[j]), tp=tp)
            pool_p.append(st[:, POOL_HALO - POOL_STATE:])
            xs, st = _pool_sample(xs, state_pool[j], row(norm_mix[i]), pw[j], row(pool_scale[j]), nt=nt, bs=bs)
            pool_s.append(st)
        else:
            p = rw[j]
            r, ld, k, v, kn, a, g, hl = _proj(xp.reshape(b * tp, d), None, p, vls[j], vfirst_p,
                                              layer=j, tp=tp, sample=False, bs=0)
            if vfirst_p is None:
                vfirst_p = v
            shift_p.append(hl[:, CONV_HALO - 1])
            sq = lambda t: t.reshape(b, tp, d)
            z, s_new = _wkv_prompt(sq(r), sq(ld), sq(k), sq(v), sq(kn), sq(a), sq(g), zero_wkv,
                                   p["lnw"], p["lnb"], p["rk"])
            wkv_p.append(s_new)
            xp = _wo(xp.reshape(b * tp, d), z.reshape(b * tp, d), wo_all, layer=j).reshape(b, tp, d)
            r, ld, k, v, kn, a, g, hl = _proj(xs.reshape(nt * bs, d), state_rwkv_shift[j], p, vls[j], vfirst_s,
                                              layer=j, tp=0, sample=True, bs=bs)
            if vfirst_s is None:
                vfirst_s = v
            shift_s.append(hl)
            z, s_new = _wkv_sample(r, ld, k, v, kn, a, g, state_rwkv_wkv[j], p["lnw"], p["lnb"], p["rk"])
            wkv_s.append(s_new)
            xs = _wo(xs.reshape(nt * bs, d), z, wo_all, layer=j, nt=nt).reshape(nt, bs, d)
        fi = ffn[i]
        last = i == depth - 1
        xo, stv, stg = _ffn_prompt(xp.reshape(b * tp, d), fi["g"], w_in_all, fi["cw"], fi["cb"], w_out_all,
                                   g_out, layer=i, tp=tp, final_norm=last)
        xp = xo.reshape(b, tp, d)
        stv, stg = (t.reshape(b, -1, CONV_HALO, fp)[:, -1] for t in (stv, stg))
        conv_p.append(jnp.concatenate([stv[:, CONV_HALO - 2:, :f], stg[:, CONV_HALO - 2:, :f]], axis=-1))
        xo, sv, sg = _ffn_sample(xs.reshape(nt * bs, d), fi["g"], w_in_all, fi["cw"], fi["cb"], w_out_all, g_out,
                                 state_ffn_conv[i][:, :, :f], state_ffn_conv[i][:, :, f:],
                                 layer=i, bs=bs, nt=nt, final_norm=last)
        xs = xo.reshape(nt, bs, d)
        conv_s.append(jnp.concatenate([sv[:, :, :f], sg[:, :, :f]], axis=-1))

    y_prompt = xp[:, FRONT_PAD + N_META:]
    y_sample = xs.transpose(1, 0, 2)
    return (y_prompt, y_sample, jnp.stack(pool_p), jnp.stack(pool_s), jnp.stack(shift_p), jnp.stack(shift_s),
            jnp.stack(wkv_p), jnp.stack(wkv_s), jnp.stack(conv_p), jnp.stack(conv_s))
```

```python
import functools
import math

import jax
import jax.numpy as jnp
from jax import lax
from jax.experimental import pallas as pl
from jax.experimental.pallas import tpu as pltpu

F32 = jnp.float32
BF16 = jnp.bfloat16

HEAD_SIZE = 64
LANES = 128
N_META = 16
PAST_LEN = 16384
POOL_WINDOWS = (2, 4, 8, 16)
POOL_STATE = max(POOL_WINDOWS) - 1
POOL_HALO = 16
CONV_WIDTH = 3
CONV_HALO = 8
NORM_EPS = 1e-6
GN_EPS = 64e-5
CHUNK = 64
SAMPLE_TOK = 8
FRONT_PAD = CHUNK - N_META

FFN_TM_CAP = 704
PROJ_TM_CAP = 352
POOL_TM_CAP = 704
WO_TM_CAP = 704
WKV_ROWS_CAP = 192
WKV_PAIRS = 16
WKV_SAMPLE_PAIRS = 8
CAST_ROWS = 512
FFN_TF = 512
PROJ_TN = 512
PROJ_TN_SAMPLE = 256
VMEM_LIMIT = 56 * 1024 * 1024


def _row_tile(n, cap, mult=16):
    best = None
    for t in range(mult, min(n, cap) + 1, mult):
        if n % t == 0:
            best = t
    assert best is not None, (n, cap)
    return best


def _round_up(n, m):
    return (n + m - 1) // m * m


def _params(sem):
    return pltpu.CompilerParams(dimension_semantics=sem, vmem_limit_bytes=VMEM_LIMIT)


def _rms(x, g):
    return x * lax.rsqrt(jnp.mean(x * x, axis=-1, keepdims=True) + NORM_EPS) * g


def _dot(a, b):
    return jnp.dot(a, b, preferred_element_type=F32)


def _split(x):
    hi = x.astype(BF16)
    lo = (x - hi.astype(F32)).astype(BF16)
    return hi, lo


def _cast_kernel(x_ref, o_ref, *, valid_rows, tr):
    x = x_ref[...]
    if valid_rows is not None:
        row = pl.program_id(1) * tr + lax.broadcasted_iota(jnp.int32, (1, tr, 1), 1)
        x = jnp.where(row < valid_rows, x, 0.0)
    o_ref[...] = x.astype(o_ref.dtype)


def _cast_rows(w, out_rows):
    n, rows, cols = w.shape
    tr = CAST_ROWS
    kern = functools.partial(_cast_kernel, valid_rows=None if rows % tr == 0 else rows, tr=tr)
    return pl.pallas_call(
        kern,
        grid=(n, out_rows // tr),
        in_specs=[pl.BlockSpec((1, tr, cols), lambda l, i: (l, i, 0))],
        out_specs=pl.BlockSpec((1, tr, cols), lambda l, i: (l, i, 0)),
        out_shape=jax.ShapeDtypeStruct((n, out_rows, cols), BF16),
        compiler_params=_params(("arbitrary", "arbitrary")),
        name="cast_rows",
    )(w)


def _cast_halves_kernel(x_ref, o_ref, *, f, fp):
    x = x_ref[0]
    pad = jnp.zeros((x.shape[0], fp - f), o_ref.dtype)
    o_ref[0, :, :f] = x[:, :f].astype(o_ref.dtype)
    o_ref[0, :, fp:fp + f] = x[:, f:].astype(o_ref.dtype)
    if fp > f:
        o_ref[0, :, f:fp] = pad
        o_ref[0, :, fp + f:] = pad


def _cast_halves(w, fp):
    n, rows, f2 = w.shape
    f = f2 // 2
    tr = min(rows, CAST_ROWS // 2)
    return pl.pallas_call(
        functools.partial(_cast_halves_kernel, f=f, fp=fp),
        grid=(n, rows // tr),
        in_specs=[pl.BlockSpec((1, tr, f2), lambda l, i: (l, i, 0))],
        out_specs=pl.BlockSpec((1, tr, 2 * fp), lambda l, i: (l, i, 0)),
        out_shape=jax.ShapeDtypeStruct((n, rows, 2 * fp), BF16),
        compiler_params=_params(("arbitrary", "arbitrary")),
        name="cast_halves",
    )(w)


def _pool_prompt_kernel(*refs, tm, padf, cg, assemble):
    if assemble:
        x_ref, head_ref, g_ref, w_ref, sc_ref, o_ref, st_ref, carry_ref = refs
    else:
        x_ref, g_ref, w_ref, sc_ref, o_ref, st_ref, carry_ref = refs
    i = pl.program_id(1)

    @pl.when(i == 0)
    def _():
        carry_ref[...] = jnp.zeros_like(carry_ref)

    x = jnp.where(i == 0, head_ref[...], x_ref[0]) if assemble else x_ref[0]
    row = i * tm + lax.broadcasted_iota(jnp.int32, (tm, 1), 0)
    h = jnp.where(row >= padf, _rms(x, g_ref[...]), 0.0)
    ext = jnp.concatenate([carry_ref[...], h], axis=0)
    pos = row - padf
    for g, w in enumerate(POOL_WINDOWS):
        sl = slice(g * cg, (g + 1) * cg)
        a = ext[:, sl]
        k = 1
        while k < w:
            n = a.shape[0]
            a = a[:n - k] + a[k:]
            k *= 2
        win = a[POOL_HALO + 1 - w: POOL_HALO + 1 - w + tm]
        cnt = jnp.clip(pos + 1, 1, w).astype(F32)
        d = win / cnt - h[:, sl]
        y = _dot(d.astype(BF16), w_ref[g])
        o_ref[0, :, sl] = x[:, sl] + y * sc_ref[:, sl]
    carry_ref[...] = h[tm - POOL_HALO:]
    st_ref[0] = h[tm - POOL_HALO:]


def _pool_prompt(x, g, w, sc, *, tp, head=None):
    b, _, d = x.shape
    assemble = head is not None
    tm = head.shape[0] if assemble else _row_tile(tp, POOL_TM_CAP)
    cg = d // len(POOL_WINDOWS)
    kern = functools.partial(_pool_prompt_kernel, tm=tm, padf=FRONT_PAD, cg=cg, assemble=assemble)
    if assemble:
        x_specs = [pl.BlockSpec((1, tm, d), lambda bi, i: (bi, jnp.maximum(i - 1, 0), 0)),
                   pl.BlockSpec((tm, d), lambda bi, i: (0, 0))]
        x_args = (x, head)
    else:
        x_specs = [pl.BlockSpec((1, tm, d), lambda bi, i: (bi, i, 0))]
        x_args = (x,)
    return pl.pallas_call(
        kern,
        grid=(b, tp // tm),
        in_specs=x_specs + [
            pl.BlockSpec((1, d), lambda bi, i: (0, 0)),
            pl.BlockSpec((len(POOL_WINDOWS), cg, cg), lambda bi, i: (0, 0, 0)),
            pl.BlockSpec((1, d), lambda bi, i: (0, 0)),
        ],
        out_specs=[
            pl.BlockSpec((1, tm, d), lambda bi, i: (bi, i, 0)),
            pl.BlockSpec((1, POOL_HALO, d), lambda bi, i: (bi, 0, 0)),
        ],
        out_shape=[
            jax.ShapeDtypeStruct((b, tp, d), F32),
            jax.ShapeDtypeStruct((b, POOL_HALO, d), F32),
        ],
        scratch_shapes=[pltpu.VMEM((POOL_HALO, d), F32)],
        compiler_params=_params(("arbitrary", "arbitrary")),
        name="pool_prompt_assemble" if assemble else "pool_prompt",
    )(*x_args, g, w, sc)


def _pool_sample_kernel(x_ref, pre_ref, g_ref, w_ref, sc_ref, *rest, tb, d, cg, nt, start):
    o_ref, st_ref = rest[-2:]
    hs = [_rms(x_ref[t], g_ref[...]) for t in range(nt)]
    ext = [pre_ref[:, j, :] for j in range(POOL_STATE)] + hs
    for g, w in enumerate(POOL_WINDOWS):
        sl = slice(g * cg, (g + 1) * cg)
        ds = []
        for t in range(nt):
            e = POOL_STATE + t
            acc = ext[e][:, sl]
            for q in range(1, w):
                acc = acc + ext[e - q][:, sl]
            cnt = float(min(w, start + t + 1))
            ds.append(acc / cnt - hs[t][:, sl])
        y = _dot(jnp.concatenate(ds, axis=0).astype(BF16), w_ref[g])
        for t in range(nt):
            o_ref[t, :, sl] = x_ref[t][:, sl] + y[t * tb:(t + 1) * tb] * sc_ref[:, sl]
    for j in range(POOL_STATE):
        st_ref[:, j, :] = ext[nt + j]


def _pool_sample(x, pre, g, w, sc, *, nt, bs, layer, n_layers, states):
    d = x.shape[-1]
    tb = min(bs, 32)
    cg = d // len(POOL_WINDOWS)
    kern = functools.partial(_pool_sample_kernel, tb=tb, d=d, cg=cg, nt=nt, start=PAST_LEN)
    extra, extra_specs, aliases = _stacked(states, 5, 1)
    return pl.pallas_call(
        kern,
        grid=(bs // tb,),
        in_specs=[
            pl.BlockSpec((nt, tb, d), lambda i: (0, i, 0)),
            pl.BlockSpec((tb, POOL_STATE, d), lambda i: (i, 0, 0)),
            pl.BlockSpec((1, d), lambda i: (0, 0)),
            pl.BlockSpec((len(POOL_WINDOWS), cg, cg), lambda i: (0, 0, 0)),
            pl.BlockSpec((1, d), lambda i: (0, 0)),
        ] + extra_specs,
        out_specs=[
            pl.BlockSpec((nt, tb, d), lambda i: (0, i, 0)),
            pl.BlockSpec((None, tb, POOL_STATE, d), lambda i: (layer, i, 0, 0)),
        ],
        out_shape=[
            jax.ShapeDtypeStruct((nt, bs, d), F32),
            jax.ShapeDtypeStruct((n_layers, bs, POOL_STATE, d), F32),
        ],
        input_output_aliases=aliases,
        compiler_params=_params(("arbitrary",)),
        name="pool_sample",
    )(x, pre, g, w, sc, *extra)


def _ffn_prompt_kernel(x_ref, g_ref, wv_ref, wg_ref, cwv_ref, cwg_ref, cbv_ref, cbg_ref, wo_ref, go_ref,
                       o_ref, stv_ref, stg_ref, hb_ref, carv_ref, carg_ref, *, tm, tps, padf, nj, final_norm):
    i = pl.program_id(0)
    j = pl.program_id(1)
    ti = i % tps

    @pl.when(j == 0)
    def _():
        x = x_ref[...]
        row = ti * tm + lax.broadcasted_iota(jnp.int32, (tm, 1), 0)
        h = jnp.where(row >= padf, _rms(x, g_ref[...]), 0.0)
        hb_ref[...] = h.astype(BF16)
        o_ref[...] = x

    hb = hb_ref[...]
    keep = ti != 0

    def branch(w_ref, cw_ref, cb_ref, car_ref, st_ref):
        u = _dot(hb, w_ref[...])
        prev = jnp.where(keep, car_ref[j], 0.0)
        ext = jnp.concatenate([prev, u], axis=0)
        cw = cw_ref[...]
        c = (cb_ref[...] + ext[CONV_HALO - 2:CONV_HALO - 2 + tm] * cw[0:1]
             + ext[CONV_HALO - 1:CONV_HALO - 1 + tm] * cw[1:2] + u * cw[2:3])
        car_ref[j] = u[tm - CONV_HALO:]
        st_ref[0] = u[tm - CONV_HALO:]
        return c

    cv = branch(wv_ref, cwv_ref, cbv_ref, carv_ref, stv_ref)
    cgate = branch(wg_ref, cwg_ref, cbg_ref, carg_ref, stg_ref)
    act = cgate * jax.nn.sigmoid(cgate) * cv
    o_ref[...] += _dot(act.astype(BF16), wo_ref[...])

    if final_norm:
        @pl.when(j == nj - 1)
        def _():
            o_ref[...] = _rms(o_ref[...], go_ref[...])


def _ffn_prompt(x, g, w_in, cw, cb, w_out, g_out, *, layer, tp, final_norm):
    rows, d = x.shape
    fp = w_out.shape[1]
    tf = FFN_TF
    nj = fp // tf
    tm = _row_tile(tp, FFN_TM_CAP)
    tps = tp // tm
    kern = functools.partial(_ffn_prompt_kernel, tm=tm, tps=tps, padf=FRONT_PAD, nj=nj, final_norm=final_norm)
    return pl.pallas_call(
        kern,
        grid=(rows // tm, nj),
        in_specs=[
            pl.BlockSpec((tm, d), lambda i, j: (i, 0)),
            pl.BlockSpec((1, d), lambda i, j: (0, 0)),
            pl.BlockSpec((None, d, tf), lambda i, j: (layer, 0, j)),
            pl.BlockSpec((None, d, tf), lambda i, j: (layer, 0, nj + j)),
            pl.BlockSpec((CONV_WIDTH, tf), lambda i, j: (0, j)),
            pl.BlockSpec((CONV_WIDTH, tf), lambda i, j: (0, nj + j)),
            pl.BlockSpec((1, tf), lambda i, j: (0, j)),
            pl.BlockSpec((1, tf), lambda i, j: (0, nj + j)),
            pl.BlockSpec((None, tf, d), lambda i, j: (layer, j, 0)),
            pl.BlockSpec((1, d), lambda i, j: (0, 0)),
        ],
        out_specs=[
            pl.BlockSpec((tm, d), lambda i, j: (i, 0)),
            pl.BlockSpec((1, CONV_HALO, tf), lambda i, j: (i, 0, j)),
            pl.BlockSpec((1, CONV_HALO, tf), lambda i, j: (i, 0, j)),
        ],
        out_shape=[
            jax.ShapeDtypeStruct((rows, d), F32),
            jax.ShapeDtypeStruct((rows // tm, CONV_HALO, fp), F32),
            jax.ShapeDtypeStruct((rows // tm, CONV_HALO, fp), F32),
        ],
        scratch_shapes=[
            pltpu.VMEM((tm, d), BF16),
            pltpu.VMEM((nj, CONV_HALO, tf), F32),
            pltpu.VMEM((nj, CONV_HALO, tf), F32),
        ],
        compiler_params=_params(("arbitrary", "arbitrary")),
        name="ffn_prompt",
    )(x, g, w_in, w_in, cw, cw, cb, cb, w_out, g_out)


def _ffn_sample_kernel(x_ref, g_ref, wv_ref, wg_ref, cwv_ref, cwg_ref, cbv_ref, cbg_ref, wo_ref, go_ref,
                       pv_ref, pg_ref, *rest, bs, nt, nj, tf, f, final_norm):
    o_ref, sv_ref, sg_ref, hb_ref = rest[-4:]
    j = pl.program_id(0)

    @pl.when(j == 0)
    def _():
        x = x_ref[...]
        hb_ref[...] = _rms(x, g_ref[...]).astype(BF16)
        o_ref[...] = x

    hb = hb_ref[...]
    valid = j * tf + lax.broadcasted_iota(jnp.int32, (1, tf), 1) < f

    def branch(w_ref, cw_ref, cb_ref, p_ref, s_ref):
        u = _dot(hb, w_ref[...])
        prev = [jnp.where(valid, p_ref[:, q, :], 0.0) for q in range(CONV_WIDTH - 1)]
        ext = prev + [u[t * bs:(t + 1) * bs] for t in range(nt)]
        cw = cw_ref[...]
        cs = [cb_ref[...] + ext[t] * cw[0:1] + ext[t + 1] * cw[1:2] + ext[t + 2] * cw[2:3] for t in range(nt)]
        for q in range(CONV_WIDTH - 1):
            s_ref[:, q, :] = ext[nt + q]
        return jnp.concatenate(cs, axis=0)

    cv = branch(wv_ref, cwv_ref, cbv_ref, pv_ref, sv_ref)
    cgate = branch(wg_ref, cwg_ref, cbg_ref, pg_ref, sg_ref)
    act = cgate * jax.nn.sigmoid(cgate) * cv
    o_ref[...] += _dot(act.astype(BF16), wo_ref[...])

    if final_norm:
        @pl.when(j == nj - 1)
        def _():
            o_ref[...] = _rms(o_ref[...], go_ref[...])


def _ffn_sample(x, g, w_in, cw, cb, w_out, g_out, st_val, st_gate, *, layer, n_layers, states, bs, nt,
                final_norm):
    rows, d = x.shape
    fp = w_out.shape[1]
    f = st_val.shape[-1]
    tf = FFN_TF
    nj = fp // tf
    kern = functools.partial(_ffn_sample_kernel, bs=bs, nt=nt, nj=nj, tf=tf, f=f, final_norm=final_norm)
    st_spec = pl.BlockSpec((bs, CONV_WIDTH - 1, tf), lambda j: (0, 0, j))
    so_spec = pl.BlockSpec((None, bs, CONV_WIDTH - 1, tf), lambda j: (layer, 0, 0, j))
    extra, extra_specs, aliases = [], [], {}
    for q, prev in enumerate(states or ()):
        e, s, al = _stacked(prev, 12 + q, 1 + q)
        extra, extra_specs, aliases = extra + e, extra_specs + s, {**aliases, **al}
    return pl.pallas_call(
        kern,
        grid=(nj,),
        in_specs=[
            pl.BlockSpec((rows, d), lambda j: (0, 0)),
            pl.BlockSpec((1, d), lambda j: (0, 0)),
            pl.BlockSpec((None, d, tf), lambda j: (layer, 0, j)),
            pl.BlockSpec((None, d, tf), lambda j: (layer, 0, nj + j)),
            pl.BlockSpec((CONV_WIDTH, tf), lambda j: (0, j)),
            pl.BlockSpec((CONV_WIDTH, tf), lambda j: (0, nj + j)),
            pl.BlockSpec((1, tf), lambda j: (0, j)),
            pl.BlockSpec((1, tf), lambda j: (0, nj + j)),
            pl.BlockSpec((None, tf, d), lambda j: (layer, j, 0)),
            pl.BlockSpec((1, d), lambda j: (0, 0)),
            st_spec, st_spec,
        ] + extra_specs,
        out_specs=[pl.BlockSpec((rows, d), lambda j: (0, 0)), so_spec, so_spec],
        out_shape=[jax.ShapeDtypeStruct((rows, d), F32)]
        + [jax.ShapeDtypeStruct((n_layers, bs, CONV_WIDTH - 1, fp), F32)] * 2,
        input_output_aliases=aliases,
        scratch_shapes=[pltpu.VMEM((rows, d), BF16)],
        compiler_params=_params(("arbitrary",)),
        name="ffn_sample",
    )(x, g, w_in, w_in, cw, cw, cb, cb, w_out, g_out, st_val, st_gate, *extra)


def _proj_kernel(*refs, tm, tps, padf, sample, bs, has_vlora):
    it = iter(refs)
    x_ref = next(it)
    sh_ref = next(it) if sample else None
    g_ref, mu_ref, wr_ref, wk_ref, wv_ref, w1_ref, a1_ref, g1_ref = (next(it) for _ in range(8))
    w2_ref, a2_ref, g2_ref, w0_ref, a0_ref, kk_ref, ka_ref, bd_ref = (next(it) for _ in range(8))
    if has_vlora:
        v1_ref, v2_ref, v0_ref, vf_ref = (next(it) for _ in range(4))
    r_o, ld_o, k_o, v_o, kn_o, a_o, g_o, hl_o = (next(it) for _ in range(8))
    xr_s, xk_s, xv_s, lw_s, la_s, lg_s = (next(it) for _ in range(6))
    lv_s = next(it) if has_vlora else None
    car_s = None if sample else next(it)

    i = pl.program_id(0)
    j = pl.program_id(1)

    @pl.when(j == 0)
    def _():
        x = x_ref[...]
        h = _rms(x, g_ref[...])
        if sample:
            prev = jnp.concatenate([sh_ref[...], h[:tm - bs]], axis=0)
            hl_o[...] = h[tm - bs:]
        else:
            ti = i % tps
            rloc = lax.broadcasted_iota(jnp.int32, (tm, 1), 0)
            h = jnp.where(ti * tm + rloc >= padf, h, 0.0)
            last = jnp.where(ti != 0, car_s[CONV_HALO - 1:CONV_HALO, :], 0.0)
            prev = jnp.where(rloc == 0, last, pltpu.roll(h, 1, 0))
            car_s[...] = h[tm - CONV_HALO:]
            hl_o[0] = h[tm - CONV_HALO:]
        xx = prev - h
        mu = mu_ref[...]
        mix = lambda q: (h + xx * mu[q:q + 1]).astype(BF16)
        xr_s[...] = mix(0)
        xk_s[...] = mix(2)
        xv = mix(3)
        xv_s[...] = xv
        lw_s[...] = jnp.tanh(_dot(mix(1), w1_ref[...])).astype(BF16)
        la_s[...] = _dot(mix(4), a1_ref[...]).astype(BF16)
        lg_s[...] = jax.nn.sigmoid(_dot(mix(5), g1_ref[...])).astype(BF16)
        if has_vlora:
            lv_s[...] = _dot(xv, v1_ref[...]).astype(BF16)

    nt = tm // bs if sample else 0

    def put(o_ref, val):
        if sample:
            for t in range(SAMPLE_TOK):
                o_ref[:, t, :] = val[t * bs:(t + 1) * bs] if t < nt else jnp.zeros((bs, val.shape[1]), F32)
        else:
            o_ref[...] = val

    r = _dot(xr_s[...], wr_ref[...])
    k = _dot(xk_s[...], wk_ref[...])
    v = _dot(xv_s[...], wv_ref[...])
    z = w0_ref[...] + _dot(lw_s[...], w2_ref[...])
    a = jax.nn.sigmoid(a0_ref[...] + _dot(la_s[...], a2_ref[...]))
    put(ld_o, -math.exp(-0.5) * jax.nn.sigmoid(z))
    if has_vlora:
        vf = jnp.concatenate([vf_ref[:, t, :] for t in range(nt)], axis=0) if sample else vf_ref[...]
        v = v + (vf - v) * jax.nn.sigmoid(v0_ref[...] + _dot(lv_s[...], v2_ref[...]))
    put(g_o, _dot(lg_s[...], g2_ref[...]))
    kk = k * kk_ref[...]
    sq_hi, sq_lo = _split(kk * kk)
    ss = _dot(sq_hi, bd_ref[...]) + _dot(sq_lo, bd_ref[...])
    put(kn_o, kk / jnp.maximum(jnp.sqrt(ss), 1e-12))
    put(r_o, r)
    put(k_o, k * (1.0 + (a - 1.0) * ka_ref[...]))
    put(v_o, v)
    put(a_o, a)


def _proj(x, sh, p, vl, vfirst, *, layer, tp, sample, bs):
    rows, d = x.shape
    tn = min(PROJ_TN_SAMPLE if sample else PROJ_TN, d)
    nj = d // tn
    p = dict(p, bd=p["bd"][:tn, :tn])
    if sample:
        tm, tps = rows, 1
    else:
        tm = _row_tile(tp, PROJ_TM_CAP)
        tps = tp // tm
    has_vlora = vl is not None
    kern = functools.partial(_proj_kernel, tm=tm, tps=tps, padf=FRONT_PAD, sample=sample, bs=bs,
                             has_vlora=has_vlora)
    full = lambda a: pl.BlockSpec(a.shape, lambda i, j: (0,) * a.ndim)
    colb = lambda a: pl.BlockSpec((a.shape[0], tn), lambda i, j: (0, j))
    if sample:
        rowb = pl.BlockSpec((bs, SAMPLE_TOK, tn), lambda i, j: (0, 0, j))
        row_shape = jax.ShapeDtypeStruct((bs, SAMPLE_TOK, d), F32)
    else:
        rowb = pl.BlockSpec((tm, tn), lambda i, j: (i, j))
        row_shape = jax.ShapeDtypeStruct((rows, d), F32)
    args, specs = [x], [pl.BlockSpec((tm, d), lambda i, j: (i, 0))]
    if sample:
        args.append(sh)
        specs.append(full(sh))
    for name in ("g", "mu"):
        args.append(p[name]); specs.append(full(p[name]))
    for name in ("wr", "wk", "wv"):
        args.append(p[name]); specs.append(pl.BlockSpec((None, d, tn), lambda i, j: (layer, 0, j)))
    for name in ("w1", "a1", "g1"):
        args.append(p[name]); specs.append(full(p[name]))
    for name in ("w2", "a2", "g2", "w0", "a0", "kk", "ka"):
        args.append(p[name]); specs.append(colb(p[name]))
    args.append(p["bd"]); specs.append(full(p["bd"]))
    if has_vlora:
        args += [vl["v1"], vl["v2"], vl["v0"], vfirst]
        specs += [full(vl["v1"]), colb(vl["v2"]), colb(vl["v0"]), rowb]
    out_shape = [row_shape] * 7
    out_specs = [rowb] * 7
    if sample:
        out_shape.append(jax.ShapeDtypeStruct((bs, d), F32))
        out_specs.append(pl.BlockSpec((bs, d), lambda i, j: (0, 0)))
    else:
        nb = rows // tp
        out_shape.append(jax.ShapeDtypeStruct((nb, CONV_HALO, d), F32))
        out_specs.append(pl.BlockSpec((1, CONV_HALO, d), lambda i, j: (i // tps, 0, 0)))
    lw, la, lg = p["w1"].shape[1], p["a1"].shape[1], p["g1"].shape[1]
    scratch = [pltpu.VMEM((tm, d), BF16)] * 3 + [pltpu.VMEM((tm, lw), BF16), pltpu.VMEM((tm, la), BF16),
                                                 pltpu.VMEM((tm, lg), BF16)]
    if has_vlora:
        scratch.append(pltpu.VMEM((tm, vl["v1"].shape[1]), BF16))
    if not sample:
        scratch.append(pltpu.VMEM((CONV_HALO, d), F32))
    return pl.pallas_call(
        kern,
        grid=(rows // tm, nj),
        in_specs=specs,
        out_specs=out_specs,
        out_shape=out_shape,
        scratch_shapes=scratch,
        compiler_params=_params(("arbitrary", "arbitrary")),
        name="rwkv_proj_sample" if sample else "rwkv_proj_prompt",
    )(*args)


def _mm(a, b):
    return _dot(a.astype(BF16), b.astype(BF16))


def _mm_nt(a, b):
    return lax.dot_general(a.astype(BF16), b.astype(BF16), (((1,), (1,)), ((), ())),
                           preferred_element_type=F32)


def _wkv_prompt_kernel(r_ref, ld_ref, k_ref, v_ref, kn_ref, a_ref, g_ref, s0_ref, lnw_ref, lnb_ref, rk_ref,
                       z_ref, so_ref, st_ref, *, L, nchunks, npairs):
    hs = HEAD_SIZE
    assert L == hs and npairs % 2 == 0
    ib = pl.program_id(2)
    lane = lax.broadcasted_iota(jnp.int32, (1, LANES), 1)
    head0 = lane < hs
    m0 = head0.astype(F32)
    m1 = 1.0 - m0
    m0w = jnp.concatenate([m0, m0], axis=1)
    m1w = 1.0 - m0w
    ri = lax.broadcasted_iota(jnp.int32, (LANES, LANES), 0)
    ci = lax.broadcasted_iota(jnp.int32, (LANES, LANES), 1)
    bdmask = ((ri < hs) == (ci < hs)).astype(F32)
    rl = lax.broadcasted_iota(jnp.int32, (L, L), 0)
    cl = lax.broadcasted_iota(jnp.int32, (L, L), 1)
    tril_incl = (cl <= rl).astype(BF16)
    prow = lax.broadcasted_iota(jnp.int32, (2 * L, 4 * L), 0)
    pcol = lax.broadcasted_iota(jnp.int32, (2 * L, 4 * L), 1) % L
    pmask = jnp.where(prow >= L, (pcol <= prow - L).astype(F32), (pcol < prow).astype(F32))
    qrow = lax.broadcasted_iota(jnp.int32, (4 * L, 4 * L), 0)
    qcol = lax.broadcasted_iota(jnp.int32, (4 * L, 4 * L), 1)
    bd4 = (qrow // L) == (qcol // L)
    eye4 = (lax.broadcasted_iota(jnp.int32, (L, 4 * L), 1) % L
            == lax.broadcasted_iota(jnp.int32, (L, 4 * L), 0)).astype(F32)
    nsq = int(math.log2(L))
    zeros_l = jnp.zeros((L, LANES), F32)
    zeros_b = jnp.zeros((4 * L, 4 * L), BF16)

    @pl.when(ib == 0)
    def _():
        zed = jnp.zeros((hs, hs), F32)
        for p in range(npairs):
            st_ref[p] = jnp.concatenate([jnp.concatenate([s0_ref[0, 2 * p], zed], axis=1),
                                         jnp.concatenate([zed, s0_ref[0, 2 * p + 1]], axis=1)], axis=0)

    def head_sum(y):
        s0 = jnp.sum(y * m0, axis=-1, keepdims=True)
        s1 = jnp.sum(y * m1, axis=-1, keepdims=True)
        return jnp.where(head0, s0, s1)

    split = lambda x: jnp.concatenate([x * m0, x * m1], axis=0)
    splitw = lambda x: jnp.concatenate([x * m0w, x * m1w], axis=0)

    def stream(rows, pairs, ld_all, cs_all):
        pr = range(len(pairs))
        lss = [slice(p * LANES, (p + 1) * LANES) for p in pairs]
        ld = [ld_all[:, ls] for ls in lss]
        cs = [cs_all[:, ls] for ls in lss]
        c_end = [x[L - 1:L, :] for x in cs]
        r = [r_ref[0, rows, ls] for ls in lss]
        k = [k_ref[0, rows, ls] for ls in lss]
        v = [v_ref[0, rows, ls] for ls in lss]
        kn = [kn_ref[0, rows, ls] for ls in lss]
        b = [kn[p] * a_ref[0, rows, lss[p]] for p in pr]
        w_inv = [jnp.exp(-x) for x in cs]
        rt = [r[p] * jnp.exp(cs[p]) for p in pr]
        kt = [k[p] * w_inv[p] for p in pr]
        bt = [b[p] * w_inv[p] for p in pr]
        at = [-kn[p] * jnp.exp(cs[p] - ld[p]) for p in pr]
        w_end = [jnp.exp(c_end[p] - cs[p]) for p in pr]
        kh = [k[p] * w_end[p] for p in pr]
        bh = [b[p] * w_end[p] for p in pr]
        yield
        pm = [_mm_nt(jnp.concatenate([at[p], rt[p]], axis=0),
                     jnp.concatenate([split(bt[p]), split(kt[p])], axis=0)) * pmask for p in pr]
        yield
        akv = [_mm(pm[p][:L, 2 * L:], split(v[p])) for p in pr]
        gr = range(len(pairs) // 2)
        pw = [jnp.concatenate([pm[2 * g][:L, :2 * L], pm[2 * g + 1][:L, :2 * L]], axis=1) for g in gr]
        tq = [eye4 for _ in gr]
        for i in range(nsq):
            blk = [jnp.where(bd4, jnp.concatenate([x.astype(BF16)] * 4, axis=0), zeros_b) for x in pw]
            if i < nsq - 1:
                both = [_dot(jnp.concatenate([pw[g], tq[g]], axis=0).astype(BF16), blk[g]) for g in gr]
                pw = [x[:L] for x in both]
                tq = [tq[g] + both[g][L:] for g in gr]
            else:
                tq = [tq[g] + _dot(tq[g].astype(BF16), blk[g]) for g in gr]
            yield
        tinv = [t[:, h * 2 * L:(h + 1) * 2 * L] for t in tq for h in range(2)]
        tz = [_mm(tinv[p], splitw(jnp.concatenate([at[p], akv[p]], axis=1))) for p in pr]
        yield
        zv = [jnp.concatenate([zeros_l, v[p]], axis=1) for p in pr]
        ry = [_mm(pm[p][L:], jnp.concatenate([splitw(tz[p]), splitw(zv[p])], axis=0)) for p in pr]
        nc = [_mm(jnp.concatenate([tz[p][:, :LANES], tz[p][:, LANES:], v[p]], axis=0).T,
                  jnp.concatenate([jnp.concatenate([bh[p], zeros_l], axis=1),
                                   jnp.concatenate([zeros_l, bh[p]], axis=1),
                                   jnp.concatenate([zeros_l, kh[p]], axis=1)], axis=0)) for p in pr]
        yield
        r2 = [rt[p] + ry[p][:, :LANES] for p in pr]
        state = [st_ref[q] for q in pairs]
        y = [_mm_nt(r2[p], state[p]) + ry[p][:, LANES:] for p in pr]
        for p, q in enumerate(pairs):
            st_ref[q] = (state[p] * jnp.exp(c_end[p]) + _mm(state[p], nc[p][:, :LANES] * bdmask)
                         + nc[p][:, LANES:] * bdmask)
        yield
        mean = [head_sum(x) * (1.0 / hs) for x in y]
        yc = [y[p] - mean[p] for p in pr]
        var = [head_sum(x * x) * (1.0 / hs) for x in yc]
        bonus = [head_sum(r[p] * k[p] * rk_ref[:, lss[p]]) * v[p] for p in pr]
        for p in pr:
            yn = yc[p] * lax.rsqrt(var[p] + GN_EPS) * lnw_ref[:, lss[p]] + lnb_ref[:, lss[p]]
            z_ref[0, rows, lss[p]] = ((yn + bonus[p]) * g_ref[0, rows, lss[p]]).astype(z_ref.dtype)

    halves = 1
    per = npairs // halves
    live = []
    for c in range(nchunks):
        rows = slice(c * L, (c + 1) * L)
        ld_all = ld_ref[0, rows, :]
        ld_hi = ld_all.astype(BF16)
        ld_r = ld_all - ld_hi.astype(F32)
        ld_mid = ld_r.astype(BF16)
        ld_lo = (ld_r - ld_mid.astype(F32)).astype(BF16)
        cs_all = _dot(tril_incl, ld_hi) + _dot(tril_incl, ld_mid) + _dot(tril_incl, ld_lo)
        live += [stream(rows, list(range(s * per, (s + 1) * per)), ld_all, cs_all) for s in range(halves)]
    lag = 0
    while live:
        for s in list(live[:lag + 1]):
            if next(s, "done") == "done":
                live.remove(s)
        lag += 1

    @pl.when(ib == pl.num_programs(2) - 1)
    def _():
        for p in range(npairs):
            state = st_ref[p]
            so_ref[0, 2 * p] = state[:hs, :hs]
            so_ref[0, 2 * p + 1] = state[hs:, hs:]


def _wkv_prompt(r, ld, k, v, kn, a, g, s0, lnw, lnb, rk):
    nb, tt, d = r.shape
    nh = d // HEAD_SIZE
    npairs = min(WKV_PAIRS, nh // 2)
    tr = _row_tile(tt, WKV_ROWS_CAP, CHUNK)
    lw = npairs * LANES
    kern = functools.partial(_wkv_prompt_kernel, L=CHUNK, nchunks=tr // CHUNK, npairs=npairs)
    seqb = pl.BlockSpec((1, tr, lw), lambda bi, p, i: (bi, i, p))
    vecb = pl.BlockSpec((1, lw), lambda bi, p, i: (0, p))
    stb = pl.BlockSpec((1, 2 * npairs, HEAD_SIZE, HEAD_SIZE), lambda bi, p, i: (bi, p, 0, 0))
    return pl.pallas_call(
        kern,
        grid=(nb, nh // (2 * npairs), tt // tr),
        in_specs=[seqb] * 7 + [stb, vecb, vecb, vecb],
        out_specs=[seqb, stb],
        out_shape=[jax.ShapeDtypeStruct((nb, tt, d), BF16),
                   jax.ShapeDtypeStruct((nb, nh, HEAD_SIZE, HEAD_SIZE), F32)],
        scratch_shapes=[pltpu.VMEM((npairs, LANES, LANES), F32)],
        compiler_params=_params(("arbitrary", "arbitrary", "arbitrary")),
        name="wkv_prompt",
    )(r, ld, k, v, kn, a, g, s0, lnw, lnb, rk)


def _wkv_sample_kernel(r_ref, ld_ref, k_ref, v_ref, kn_ref, a_ref, g_ref, s0_ref, lnw_ref, lnb_ref, rk_ref,
                       *rest, ngroups, npairs):
    z_ref, so_ref = rest[-2:]
    hs, L, T = HEAD_SIZE, CHUNK, SAMPLE_TOK
    ns = L // T
    lane = lax.broadcasted_iota(jnp.int32, (1, LANES), 1)
    head0 = lane < hs
    head0w = (lax.broadcasted_iota(jnp.int32, (1, 2 * LANES), 1) % LANES) < hs
    m0 = head0.astype(F32)
    m1 = 1.0 - m0
    ri = lax.broadcasted_iota(jnp.int32, (LANES, LANES), 0)
    ci = lax.broadcasted_iota(jnp.int32, (LANES, LANES), 1)
    bdmask = ((ri < hs) == (ci < hs)).astype(F32)
    eye_w = (ri == ci).astype(BF16)
    rl = lax.broadcasted_iota(jnp.int32, (L, L), 0)
    cl = lax.broadcasted_iota(jnp.int32, (L, L), 1)
    same = (rl // T) == (cl // T)
    cum_lhs = jnp.concatenate([jnp.where(same, (cl % T <= rl % T).astype(F32), 0.0),
                               same.astype(F32)], axis=0).astype(BF16)
    eye = (cl == rl).astype(F32)
    prow = lax.broadcasted_iota(jnp.int32, (4 * L, 2 * L), 0)
    pcol = lax.broadcasted_iota(jnp.int32, (4 * L, 2 * L), 1) % L
    psame = ((prow % L) // T) == (pcol // T)
    pcaus = jnp.where(prow >= 2 * L, (pcol % T <= prow % T).astype(F32), (pcol % T < prow % T).astype(F32))
    pmask = jnp.where(psame, pcaus, 0.0)
    kcols = (lax.broadcasted_iota(jnp.int32, (1, 2 * L), 1) >= L).astype(F32)
    nsteps = int(math.log2(T)) - 1
    zeros_l = jnp.zeros((L, LANES), F32)
    zeros_t = jnp.zeros((T, LANES), F32)
    zed = jnp.zeros((hs, hs), F32)

    def head_sum(y):
        s0 = jnp.sum(y * m0, axis=-1, keepdims=True)
        s1 = jnp.sum(y * m1, axis=-1, keepdims=True)
        return jnp.where(head0, s0, s1)

    def group_body(gi, carry):
        seqs = pl.ds(pl.multiple_of(gi * ns, ns), ns)
        pr = range(npairs)
        lss = [slice(p * LANES, (p + 1) * LANES) for p in pr]
        tile = lambda ref, ls: ref[seqs, :, ls].reshape(L, LANES)
        ld = [tile(ld_ref, ls) for ls in lss]
        ld_hi = [x.astype(BF16) for x in ld]
        ld_r = [x - h.astype(F32) for x, h in zip(ld, ld_hi)]
        ld_mid = [x.astype(BF16) for x in ld_r]
        ld_lo = [(x - m.astype(F32)).astype(BF16) for x, m in zip(ld_r, ld_mid)]
        cc = [_dot(cum_lhs, ld_hi[p]) + _dot(cum_lhs, ld_mid[p]) + _dot(cum_lhs, ld_lo[p]) for p in pr]
        cs = [x[:L] for x in cc]
        c_end = [x[L:] for x in cc]
        r = [tile(r_ref, ls) for ls in lss]
        k = [tile(k_ref, ls) for ls in lss]
        v = [tile(v_ref, ls) for ls in lss]
        kn = [tile(kn_ref, ls) for ls in lss]
        b = [kn[p] * tile(a_ref, lss[p]) for p in pr]
        w_inv = [jnp.exp(-x) for x in cs]
        rt = [r[p] * jnp.exp(cs[p]) for p in pr]
        kt = [k[p] * w_inv[p] for p in pr]
        bt = [b[p] * w_inv[p] for p in pr]
        at = [-kn[p] * jnp.exp(cs[p] - ld[p]) for p in pr]
        w_end = [jnp.exp(c_end[p] - cs[p]) for p in pr]
        w_tot = [jnp.exp(x) for x in c_end]
        kh = [k[p] * w_end[p] for p in pr]
        bh = [b[p] * w_end[p] for p in pr]
        pm = [_mm_nt(jnp.concatenate([at[p] * m0, at[p] * m1, rt[p] * m0, rt[p] * m1], axis=0),
                     jnp.concatenate([bt[p], kt[p]], axis=0)) * pmask for p in pr]
        a_bk = [x[:2 * L] for x in pm]
        r_bk = [x[2 * L:] for x in pm]
        akv = [_mm(a_bk[p] * kcols, jnp.concatenate([v[p], v[p]], axis=0)) for p in pr]
        akv = [jnp.where(head0, x[:L], x[L:]) for x in akv]
        pw = [a_bk[p][h * L:(h + 1) * L, :L] for p in pr for h in range(2)]
        tinv = [eye + x for x in pw]
        for _ in range(nsteps):
            pw = [_mm(x, x) for x in pw]
            tinv = [t + _mm(t, x) for t, x in zip(tinv, pw)]
        tz_rhs = [jnp.concatenate([at[p], akv[p]], axis=1) for p in pr]
        tz = [jnp.where(head0w, _mm(tinv[2 * p], tz_rhs[p]), _mm(tinv[2 * p + 1], tz_rhs[p])) for p in pr]
        ry = [_mm(r_bk[p], jnp.concatenate([tz[p], jnp.concatenate([zeros_l, v[p]], axis=1)], axis=0))
              for p in pr]
        ry = [jnp.where(head0w, x[:L], x[L:]) for x in ry]
        r2 = [rt[p] + ry[p][:, :LANES] for p in pr]
        ps = [(p, i) for p in pr for i in range(ns)]
        rs = [slice(i * T, (i + 1) * T) for i in range(ns)]
        state = [jnp.concatenate([jnp.concatenate([s0_ref[gi * ns + i, 2 * p], zed], axis=1),
                                  jnp.concatenate([zed, s0_ref[gi * ns + i, 2 * p + 1]], axis=1)], axis=0)
                 for p, i in ps]
        sb = [x.astype(BF16) for x in state]
        e_rhs = [jnp.concatenate([tz[p][rs[i]], jnp.concatenate([zeros_t, v[p][rs[i]]], axis=1)], axis=0)
                 for p, i in ps]
        e = [_mm_nt(jnp.concatenate([sb[q], eye_w], axis=1), e_rhs[q]) for q in range(len(ps))]
        upd = [_mm(e[q], jnp.concatenate([bh[p][rs[i]], kh[p][rs[i]]], axis=0)) * bdmask
               for q, (p, i) in enumerate(ps)]
        ys = [_mm_nt(r2[p][rs[i]], sb[q]) for q, (p, i) in enumerate(ps)]
        for q, (p, i) in enumerate(ps):
            new = state[q] * w_tot[p][i * T:i * T + 1] + upd[q]
            so_ref[gi * ns + i, 2 * p] = new[:hs, :hs]
            so_ref[gi * ns + i, 2 * p + 1] = new[hs:, hs:]
        y = [jnp.concatenate(ys[p * ns:(p + 1) * ns], axis=0) + ry[p][:, LANES:] for p in pr]
        mean = [head_sum(x) * (1.0 / hs) for x in y]
        yc = [y[p] - mean[p] for p in pr]
        var = [head_sum(x * x) * (1.0 / hs) for x in yc]
        bonus = [head_sum(r[p] * k[p] * rk_ref[:, lss[p]]) * v[p] for p in pr]
        for p in pr:
            yn = yc[p] * lax.rsqrt(var[p] + GN_EPS) * lnw_ref[:, lss[p]] + lnb_ref[:, lss[p]]
            z_ref[seqs, :, lss[p]] = ((yn + bonus[p]) * tile(g_ref, lss[p])).reshape(ns, T, LANES)
        return carry

    lax.fori_loop(0, ngroups, group_body, 0)


def _stacked(prev, n_inputs, out_index):
    if prev is None:
        return [], [], {}
    return [prev], [pl.BlockSpec(memory_space=pl.ANY)], {n_inputs: out_index}


def _wkv_sample(r, ld, k, v, kn, a, g, s0, lnw, lnb, rk, *, layer, n_layers, states):
    nb, tt, d = r.shape
    nh = d // HEAD_SIZE
    npairs = min(WKV_SAMPLE_PAIRS, nh // 2)
    ns = CHUNK // SAMPLE_TOK
    bb = min(nb, 2 * ns)
    lw = npairs * LANES
    kern = functools.partial(_wkv_sample_kernel, ngroups=bb // ns, npairs=npairs)
    seqb = pl.BlockSpec((bb, tt, lw), lambda i, p: (i, 0, p))
    vecb = pl.BlockSpec((1, lw), lambda i, p: (0, p))
    stb = pl.BlockSpec((bb, 2 * npairs, HEAD_SIZE, HEAD_SIZE), lambda i, p: (i, p, 0, 0))
    sto = pl.BlockSpec((None, bb, 2 * npairs, HEAD_SIZE, HEAD_SIZE), lambda i, p: (layer, i, p, 0, 0))
    extra, extra_specs, aliases = _stacked(states, 11, 1)
    return pl.pallas_call(
        kern,
        grid=(nb // bb, nh // (2 * npairs)),
        in_specs=[seqb] * 7 + [stb, vecb, vecb, vecb] + extra_specs,
        out_specs=[seqb, sto],
        out_shape=[jax.ShapeDtypeStruct((nb, tt, d), F32),
                   jax.ShapeDtypeStruct((n_layers, nb, nh, HEAD_SIZE, HEAD_SIZE), F32)],
        input_output_aliases=aliases,
        compiler_params=_params(("arbitrary", "arbitrary")),
        name="wkv_sample",
    )(r, ld, k, v, kn, a, g, s0, lnw, lnb, rk, *extra)


def _wo_kernel(x_ref, z_ref, w_ref, o_ref, *, nt):
    if nt:
        z = jnp.concatenate([z_ref[:, t, :] for t in range(nt)], axis=0)
    else:
        z = z_ref[...]
    o_ref[...] = x_ref[...] + _dot(z.astype(BF16), w_ref[...])


def _wo(x, z, w, *, layer, nt=0):
    rows, d = x.shape
    tm = rows if nt else _row_tile(rows, WO_TM_CAP)
    zspec = pl.BlockSpec(z.shape, lambda i: (0, 0, 0)) if nt else pl.BlockSpec((tm, d), lambda i: (i, 0))
    return pl.pallas_call(
        functools.partial(_wo_kernel, nt=nt),
        grid=(rows // tm,),
        in_specs=[pl.BlockSpec((tm, d), lambda i: (i, 0)), zspec,
                  pl.BlockSpec((None, d, d), lambda i: (layer, 0, 0))],
        out_specs=pl.BlockSpec((tm, d), lambda i: (i, 0)),
        out_shape=jax.ShapeDtypeStruct((rows, d), F32),
        compiler_params=_params(("arbitrary",)),
        name="rwkv_wo_sample" if nt else "rwkv_wo",
    )(x, z, w)


def _pad_cols(a, n):
    return jnp.pad(a, [(0, 0)] * (a.ndim - 1) + [(0, n - a.shape[-1])])


def _pad_rows(a, n):
    return jnp.pad(a, [(0, n - a.shape[0])] + [(0, 0)] * (a.ndim - 1))


def kernel(x_prompt, x_sample, state_pool, state_rwkv_shift, state_rwkv_wkv, state_ffn_conv, meta_tokens,
           norm_mix, norm_ffn, norm_out, pool_w, pool_scale, rwkv_mu, rwkv_wr, rwkv_wk, rwkv_wv, rwkv_wo,
           rwkv_w0, rwkv_w1, rwkv_w2, rwkv_a0, rwkv_a1, rwkv_a2, rwkv_v0, rwkv_v1, rwkv_v2, rwkv_g1, rwkv_g2,
           rwkv_kk, rwkv_ka, rwkv_rk, rwkv_lnw, rwkv_lnb, ffn_w_in, ffn_conv_w, ffn_conv_b, ffn_w_out):
    b, seq, d = x_prompt.shape
    bs, nt, _ = x_sample.shape
    depth = norm_mix.shape[0]
    f = ffn_w_out.shape[1]
    fp = _round_up(f, FFN_TF)
    nh = d // HEAD_SIZE
    tp = FRONT_PAD + N_META + seq
    assert tp % CHUNK == 0 and d % (2 * LANES) == 0 and CONV_WIDTH - 1 <= nt <= SAMPLE_TOK

    row = lambda a: a.reshape(1, -1)
    lora_in = lambda a: _pad_cols(a, _round_up(a.shape[-1], LANES)).astype(BF16)
    lora_out = lambda a: _pad_rows(a, _round_up(a.shape[0], LANES)).astype(BF16)
    halves = lambda a: jnp.concatenate([_pad_cols(a[..., :f], fp), _pad_cols(a[..., f:], fp)], axis=-1)
    w_in_all = _cast_halves(ffn_w_in, fp)
    w_out_all = _cast_rows(ffn_w_out, fp)
    ffn = [dict(g=row(norm_ffn[i]), cw=halves(ffn_conv_w[i]), cb=halves(ffn_conv_b[i][None]))
           for i in range(depth)]
    tn = min(PROJ_TN, d)
    bd = jnp.kron(jnp.eye(tn // HEAD_SIZE, dtype=F32), jnp.ones((HEAD_SIZE, HEAD_SIZE), F32)).astype(BF16)
    wr_all, wk_all, wv_all, wo_all = (_cast_rows(w, d) for w in (rwkv_wr, rwkv_wk, rwkv_wv, rwkv_wo))
    rw = []
    for j in range(depth // 2):
        rw.append(dict(
            g=row(norm_mix[2 * j + 1]), mu=rwkv_mu[j], wr=wr_all, wk=wk_all, wv=wv_all,
            w1=lora_in(rwkv_w1[j]), a1=lora_in(rwkv_a1[j]), g1=lora_in(rwkv_g1[j]),
            w2=lora_out(rwkv_w2[j]), a2=lora_out(rwkv_a2[j]), g2=lora_out(rwkv_g2[j]),
            w0=row(rwkv_w0[j]), a0=row(rwkv_a0[j]), kk=row(rwkv_kk[j]), ka=row(rwkv_ka[j]), bd=bd,
            lnw=row(rwkv_lnw[j]), lnb=row(rwkv_lnb[j]), rk=row(rwkv_rk[j])))
    vls = [None] + [dict(v1=lora_in(rwkv_v1[j]), v2=lora_out(rwkv_v2[j]), v0=row(rwkv_v0[j]))
                    for j in range(depth // 2 - 1)]
    pw = [pool_w[j].astype(BF16) for j in range((depth + 1) // 2)]
    g_out = row(norm_out)

    head = jnp.concatenate([jnp.zeros((FRONT_PAD, d), F32), meta_tokens], axis=0)
    xp = x_prompt
    xs = x_sample.transpose(1, 0, 2)
    zero_wkv = jnp.zeros((b, nh, HEAD_SIZE, HEAD_SIZE), F32)
    n_pool, n_rwkv = (depth + 1) // 2, depth // 2

    pool_p, shift_p, shift_s, wkv_p, conv_p = [], [], [], [], []
    pool_s = wkv_s = conv_s = None
    vfirst_p = vfirst_s = None
    for i in range(depth):
        j = i // 2
        if i % 2 == 0:
            xp, st = _pool_prompt(xp, row(norm_mix[i]), pw[j], row(pool_scale[j]), tp=tp,
                                  head=head if i == 0 else None)
            pool_p.append(st[:, POOL_HALO - POOL_STATE:])
            xs, pool_s = _pool_sample(xs, state_pool[j], row(norm_mix[i]), pw[j], row(pool_scale[j]), nt=nt, bs=bs,
                                      layer=j, n_layers=n_pool, states=pool_s)
        else:
            p = rw[j]
            r, ld, k, v, kn, a, g, hl = _proj(xp.reshape(b * tp, d), None, p, vls[j], vfirst_p,
                                              layer=j, tp=tp, sample=False, bs=0)
            if vfirst_p is None:
                vfirst_p = v
            shift_p.append(hl[:, CONV_HALO - 1])
            sq = lambda t: t.reshape(b, tp, d)
            z, s_new = _wkv_prompt(sq(r), sq(ld), sq(k), sq(v), sq(kn), sq(a), sq(g), zero_wkv,
                                   p["lnw"], p["lnb"], p["rk"])
            wkv_p.append(s_new)
            xp = _wo(xp.reshape(b * tp, d), z.reshape(b * tp, d), wo_all, layer=j).reshape(b, tp, d)
            r, ld, k, v, kn, a, g, hl = _proj(xs.reshape(nt * bs, d), state_rwkv_shift[j], p, vls[j], vfirst_s,
                                              layer=j, tp=0, sample=True, bs=bs)
            if vfirst_s is None:
                vfirst_s = v
            shift_s.append(hl)
            z, wkv_s = _wkv_sample(r, ld, k, v, kn, a, g, state_rwkv_wkv[j], p["lnw"], p["lnb"], p["rk"],
                                   layer=j, n_layers=n_rwkv, states=wkv_s)
            xs = _wo(xs.reshape(nt * bs, d), z, wo_all, layer=j, nt=nt).reshape(nt, bs, d)
        fi = ffn[i]
        last = i == depth - 1
        xo, stv, stg = _ffn_prompt(xp.reshape(b * tp, d), fi["g"], w_in_all, fi["cw"], fi["cb"], w_out_all,
                                   g_out, layer=i, tp=tp, final_norm=last)
        xp = xo.reshape(b, tp, d)
        stv, stg = (t.reshape(b, -1, CONV_HALO, fp)[:, -1] for t in (stv, stg))
        conv_p.append(jnp.concatenate([stv[:, CONV_HALO - 2:, :f], stg[:, CONV_HALO - 2:, :f]], axis=-1))
        xo, *conv_s = _ffn_sample(xs.reshape(nt * bs, d), fi["g"], w_in_all, fi["cw"], fi["cb"], w_out_all, g_out,
                                  state_ffn_conv[i][:, :, :f], state_ffn_conv[i][:, :, f:],
                                  layer=i, n_layers=depth, states=conv_s, bs=bs, nt=nt, final_norm=last)
        xs = xo.reshape(nt, bs, d)

    y_prompt = xp[:, FRONT_PAD + N_META:]
    y_sample = xs.transpose(1, 0, 2)
    conv_s = jnp.concatenate([conv_s[0][..., :f], conv_s[1][..., :f]], axis=-1)
    return (y_prompt, y_sample, jnp.stack(pool_p), pool_s, jnp.stack(shift_p), jnp.stack(shift_s),
            jnp.stack(wkv_p), wkv_s, jnp.stack(conv_p), conv_s)
```

```python
import functools
import math

import jax
import jax.numpy as jnp
from jax import lax
from jax.experimental import pallas as pl
from jax.experimental.pallas import tpu as pltpu

F32 = jnp.float32
BF16 = jnp.bfloat16

HEAD_SIZE = 64
LANES = 128
N_META = 16
PAST_LEN = 16384
POOL_WINDOWS = (2, 4, 8, 16)
POOL_STATE = max(POOL_WINDOWS) - 1
POOL_HALO = 16
CONV_WIDTH = 3
CONV_HALO = 8
NORM_EPS = 1e-6
GN_EPS = 64e-5
CHUNK = 64
SAMPLE_TOK = 8
FRONT_PAD = CHUNK - N_META

FFN_TM_CAP = 704
PROJ_TM_CAP = 352
POOL_TM_CAP = 704
WO_TM_CAP = 704
WKV_ROWS_CAP = 192
WKV_PAIRS = 16
WKV_SAMPLE_PAIRS = 8
CAST_ROWS = 512
FFN_TF = 512
PROJ_TN = 512
PROJ_TN_SAMPLE = 256
VMEM_LIMIT = 56 * 1024 * 1024


def _row_tile(n, cap, mult=16):
    best = None
    for t in range(mult, min(n, cap) + 1, mult):
        if n % t == 0:
            best = t
    assert best is not None, (n, cap)
    return best


def _round_up(n, m):
    return (n + m - 1) // m * m


def _params(sem):
    return pltpu.CompilerParams(dimension_semantics=sem, vmem_limit_bytes=VMEM_LIMIT)


def _rms(x, g):
    return x * lax.rsqrt(jnp.mean(x * x, axis=-1, keepdims=True) + NORM_EPS) * g


def _dot(a, b):
    return jnp.dot(a, b, preferred_element_type=F32)


def _split(x):
    hi = x.astype(BF16)
    lo = (x - hi.astype(F32)).astype(BF16)
    return hi, lo


def _cast_kernel(x_ref, o_ref, *, valid_rows, tr):
    x = x_ref[...]
    if valid_rows is not None:
        row = pl.program_id(1) * tr + lax.broadcasted_iota(jnp.int32, (1, tr, 1), 1)
        x = jnp.where(row < valid_rows, x, 0.0)
    o_ref[...] = x.astype(o_ref.dtype)


def _cast_rows(w, out_rows):
    n, rows, cols = w.shape
    tr = CAST_ROWS
    kern = functools.partial(_cast_kernel, valid_rows=None if rows % tr == 0 else rows, tr=tr)
    return pl.pallas_call(
        kern,
        grid=(n, out_rows // tr),
        in_specs=[pl.BlockSpec((1, tr, cols), lambda l, i: (l, i, 0))],
        out_specs=pl.BlockSpec((1, tr, cols), lambda l, i: (l, i, 0)),
        out_shape=jax.ShapeDtypeStruct((n, out_rows, cols), BF16),
        compiler_params=_params(("arbitrary", "arbitrary")),
        name="cast_rows",
    )(w)


def _cast_halves_kernel(x_ref, o_ref, *, f, fp):
    x = x_ref[0]
    pad = jnp.zeros((x.shape[0], fp - f), o_ref.dtype)
    o_ref[0, :, :f] = x[:, :f].astype(o_ref.dtype)
    o_ref[0, :, fp:fp + f] = x[:, f:].astype(o_ref.dtype)
    if fp > f:
        o_ref[0, :, f:fp] = pad
        o_ref[0, :, fp + f:] = pad


def _cast_halves(w, fp):
    n, rows, f2 = w.shape
    f = f2 // 2
    tr = min(rows, CAST_ROWS // 2)
    return pl.pallas_call(
        functools.partial(_cast_halves_kernel, f=f, fp=fp),
        grid=(n, rows // tr),
        in_specs=[pl.BlockSpec((1, tr, f2), lambda l, i: (l, i, 0))],
        out_specs=pl.BlockSpec((1, tr, 2 * fp), lambda l, i: (l, i, 0)),
        out_shape=jax.ShapeDtypeStruct((n, rows, 2 * fp), BF16),
        compiler_params=_params(("arbitrary", "arbitrary")),
        name="cast_halves",
    )(w)


def _pool_prompt_kernel(*refs, tm, padf, cg, assemble):
    if assemble:
        x_ref, head_ref, g_ref, w_ref, sc_ref, o_ref, st_ref, carry_ref = refs
    else:
        x_ref, g_ref, w_ref, sc_ref, o_ref, st_ref, carry_ref = refs
    i = pl.program_id(1)

    @pl.when(i == 0)
    def _():
        carry_ref[...] = jnp.zeros_like(carry_ref)

    x = jnp.where(i == 0, head_ref[...], x_ref[0]) if assemble else x_ref[0]
    row = i * tm + lax.broadcasted_iota(jnp.int32, (tm, 1), 0)
    h = jnp.where(row >= padf, _rms(x, g_ref[...]), 0.0)
    ext = jnp.concatenate([carry_ref[...], h], axis=0)
    pos = row - padf
    for g, w in enumerate(POOL_WINDOWS):
        sl = slice(g * cg, (g + 1) * cg)
        a = ext[:, sl]
        k = 1
        while k < w:
            n = a.shape[0]
            a = a[:n - k] + a[k:]
            k *= 2
        win = a[POOL_HALO + 1 - w: POOL_HALO + 1 - w + tm]
        cnt = jnp.clip(pos + 1, 1, w).astype(F32)
        d = win / cnt - h[:, sl]
        y = _dot(d.astype(BF16), w_ref[g])
        o_ref[0, :, sl] = x[:, sl] + y * sc_ref[:, sl]
    carry_ref[...] = h[tm - POOL_HALO:]
    st_ref[0] = h[tm - POOL_HALO:]


def _pool_prompt(x, g, w, sc, *, tp, head=None):
    b, _, d = x.shape
    assemble = head is not None
    tm = head.shape[0] if assemble else _row_tile(tp, POOL_TM_CAP)
    cg = d // len(POOL_WINDOWS)
    kern = functools.partial(_pool_prompt_kernel, tm=tm, padf=FRONT_PAD, cg=cg, assemble=assemble)
    if assemble:
        x_specs = [pl.BlockSpec((1, tm, d), lambda bi, i: (bi, jnp.maximum(i - 1, 0), 0)),
                   pl.BlockSpec((tm, d), lambda bi, i: (0, 0))]
        x_args = (x, head)
    else:
        x_specs = [pl.BlockSpec((1, tm, d), lambda bi, i: (bi, i, 0))]
        x_args = (x,)
    return pl.pallas_call(
        kern,
        grid=(b, tp // tm),
        in_specs=x_specs + [
            pl.BlockSpec((1, d), lambda bi, i: (0, 0)),
            pl.BlockSpec((len(POOL_WINDOWS), cg, cg), lambda bi, i: (0, 0, 0)),
            pl.BlockSpec((1, d), lambda bi, i: (0, 0)),
        ],
        out_specs=[
            pl.BlockSpec((1, tm, d), lambda bi, i: (bi, i, 0)),
            pl.BlockSpec((1, POOL_HALO, d), lambda bi, i: (bi, 0, 0)),
        ],
        out_shape=[
            jax.ShapeDtypeStruct((b, tp, d), F32),
            jax.ShapeDtypeStruct((b, POOL_HALO, d), F32),
        ],
        scratch_shapes=[pltpu.VMEM((POOL_HALO, d), F32)],
        compiler_params=_params(("arbitrary", "arbitrary")),
        name="pool_prompt_assemble" if assemble else "pool_prompt",
    )(*x_args, g, w, sc)


def _pool_sample_kernel(x_ref, pre_ref, g_ref, w_ref, sc_ref, *rest, tb, d, cg, nt, start):
    o_ref, st_ref = rest[-2:]
    hs = [_rms(x_ref[t], g_ref[...]) for t in range(nt)]
    ext = [pre_ref[:, j, :] for j in range(POOL_STATE)] + hs
    for g, w in enumerate(POOL_WINDOWS):
        sl = slice(g * cg, (g + 1) * cg)
        ds = []
        for t in range(nt):
            e = POOL_STATE + t
            acc = ext[e][:, sl]
            for q in range(1, w):
                acc = acc + ext[e - q][:, sl]
            cnt = float(min(w, start + t + 1))
            ds.append(acc / cnt - hs[t][:, sl])
        y = _dot(jnp.concatenate(ds, axis=0).astype(BF16), w_ref[g])
        for t in range(nt):
            o_ref[t, :, sl] = x_ref[t][:, sl] + y[t * tb:(t + 1) * tb] * sc_ref[:, sl]
    for j in range(POOL_STATE):
        st_ref[:, j, :] = ext[nt + j]


def _pool_sample(x, pre, g, w, sc, *, nt, bs, layer, n_layers, states):
    d = x.shape[-1]
    tb = min(bs, 32)
    cg = d // len(POOL_WINDOWS)
    kern = functools.partial(_pool_sample_kernel, tb=tb, d=d, cg=cg, nt=nt, start=PAST_LEN)
    st_shape = (n_layers, bs, POOL_STATE, d)
    extra, extra_specs, aliases = _stacked(states, st_shape, 5, 1)
    return pl.pallas_call(
        kern,
        grid=(bs // tb,),
        in_specs=[
            pl.BlockSpec((nt, tb, d), lambda i: (0, i, 0)),
            pl.BlockSpec((tb, POOL_STATE, d), lambda i: (i, 0, 0)),
            pl.BlockSpec((1, d), lambda i: (0, 0)),
            pl.BlockSpec((len(POOL_WINDOWS), cg, cg), lambda i: (0, 0, 0)),
            pl.BlockSpec((1, d), lambda i: (0, 0)),
        ] + extra_specs,
        out_specs=[
            pl.BlockSpec((nt, tb, d), lambda i: (0, i, 0)),
            pl.BlockSpec((None, tb, POOL_STATE, d), lambda i: (layer, i, 0, 0)),
        ],
        out_shape=[
            jax.ShapeDtypeStruct((nt, bs, d), F32),
            jax.ShapeDtypeStruct(st_shape, F32),
        ],
        input_output_aliases=aliases,
        compiler_params=_params(("arbitrary",)),
        name="pool_sample",
    )(x, pre, g, w, sc, *extra)


def _ffn_prompt_kernel(x_ref, g_ref, wv_ref, wg_ref, cwv_ref, cwg_ref, cbv_ref, cbg_ref, wo_ref, go_ref,
                       o_ref, stv_ref, stg_ref, hb_ref, carv_ref, carg_ref, *, tm, tps, padf, nj, final_norm):
    i = pl.program_id(0)
    j = pl.program_id(1)
    ti = i % tps

    @pl.when(j == 0)
    def _():
        x = x_ref[...]
        row = ti * tm + lax.broadcasted_iota(jnp.int32, (tm, 1), 0)
        h = jnp.where(row >= padf, _rms(x, g_ref[...]), 0.0)
        hb_ref[...] = h.astype(BF16)
        o_ref[...] = x

    hb = hb_ref[...]
    keep = ti != 0

    def branch(w_ref, cw_ref, cb_ref, car_ref, st_ref):
        u = _dot(hb, w_ref[...])
        prev = jnp.where(keep, car_ref[j], 0.0)
        ext = jnp.concatenate([prev, u], axis=0)
        cw = cw_ref[...]
        c = (cb_ref[...] + ext[CONV_HALO - 2:CONV_HALO - 2 + tm] * cw[0:1]
             + ext[CONV_HALO - 1:CONV_HALO - 1 + tm] * cw[1:2] + u * cw[2:3])
        car_ref[j] = u[tm - CONV_HALO:]
        st_ref[0] = u[tm - CONV_HALO:]
        return c

    cv = branch(wv_ref, cwv_ref, cbv_ref, carv_ref, stv_ref)
    cgate = branch(wg_ref, cwg_ref, cbg_ref, carg_ref, stg_ref)
    act = cgate * jax.nn.sigmoid(cgate) * cv
    o_ref[...] += _dot(act.astype(BF16), wo_ref[...])

    if final_norm:
        @pl.when(j == nj - 1)
        def _():
            o_ref[...] = _rms(o_ref[...], go_ref[...])


def _ffn_prompt(x, g, w_in, cw, cb, w_out, g_out, *, layer, tp, final_norm):
    rows, d = x.shape
    fp = w_out.shape[1]
    tf = FFN_TF
    nj = fp // tf
    tm = _row_tile(tp, FFN_TM_CAP)
    tps = tp // tm
    kern = functools.partial(_ffn_prompt_kernel, tm=tm, tps=tps, padf=FRONT_PAD, nj=nj, final_norm=final_norm)
    return pl.pallas_call(
        kern,
        grid=(rows // tm, nj),
        in_specs=[
            pl.BlockSpec((tm, d), lambda i, j: (i, 0)),
            pl.BlockSpec((1, d), lambda i, j: (0, 0)),
            pl.BlockSpec((None, d, tf), lambda i, j: (layer, 0, j)),
            pl.BlockSpec((None, d, tf), lambda i, j: (layer, 0, nj + j)),
            pl.BlockSpec((CONV_WIDTH, tf), lambda i, j: (0, j)),
            pl.BlockSpec((CONV_WIDTH, tf), lambda i, j: (0, nj + j)),
            pl.BlockSpec((1, tf), lambda i, j: (0, j)),
            pl.BlockSpec((1, tf), lambda i, j: (0, nj + j)),
            pl.BlockSpec((None, tf, d), lambda i, j: (layer, j, 0)),
            pl.BlockSpec((1, d), lambda i, j: (0, 0)),
        ],
        out_specs=[
            pl.BlockSpec((tm, d), lambda i, j: (i, 0)),
            pl.BlockSpec((1, CONV_HALO, tf), lambda i, j: (i, 0, j)),
            pl.BlockSpec((1, CONV_HALO, tf), lambda i, j: (i, 0, j)),
        ],
        out_shape=[
            jax.ShapeDtypeStruct((rows, d), F32),
            jax.ShapeDtypeStruct((rows // tm, CONV_HALO, fp), F32),
            jax.ShapeDtypeStruct((rows // tm, CONV_HALO, fp), F32),
        ],
        scratch_shapes=[
            pltpu.VMEM((tm, d), BF16),
            pltpu.VMEM((nj, CONV_HALO, tf), F32),
            pltpu.VMEM((nj, CONV_HALO, tf), F32),
        ],
        compiler_params=_params(("arbitrary", "arbitrary")),
        name="ffn_prompt",
    )(x, g, w_in, w_in, cw, cw, cb, cb, w_out, g_out)


def _ffn_sample_kernel(x_ref, g_ref, wv_ref, wg_ref, cwv_ref, cwg_ref, cbv_ref, cbg_ref, wo_ref, go_ref,
                       pv_ref, pg_ref, *rest, bs, nt, nj, tf, f, final_norm):
    o_ref, sv_ref, sg_ref, hb_ref = rest[-4:]
    j = pl.program_id(0)

    @pl.when(j == 0)
    def _():
        x = x_ref[...]
        hb_ref[...] = _rms(x, g_ref[...]).astype(BF16)
        o_ref[...] = x

    hb = hb_ref[...]
    valid = j * tf + lax.broadcasted_iota(jnp.int32, (1, tf), 1) < f

    def branch(w_ref, cw_ref, cb_ref, p_ref, s_ref):
        u = _dot(hb, w_ref[...])
        prev = [jnp.where(valid, p_ref[:, q, :], 0.0) for q in range(CONV_WIDTH - 1)]
        ext = prev + [u[t * bs:(t + 1) * bs] for t in range(nt)]
        cw = cw_ref[...]
        cs = [cb_ref[...] + ext[t] * cw[0:1] + ext[t + 1] * cw[1:2] + ext[t + 2] * cw[2:3] for t in range(nt)]
        for q in range(CONV_WIDTH - 1):
            s_ref[:, q, :] = ext[nt + q]
        return jnp.concatenate(cs, axis=0)

    cv = branch(wv_ref, cwv_ref, cbv_ref, pv_ref, sv_ref)
    cgate = branch(wg_ref, cwg_ref, cbg_ref, pg_ref, sg_ref)
    act = cgate * jax.nn.sigmoid(cgate) * cv
    o_ref[...] += _dot(act.astype(BF16), wo_ref[...])

    if final_norm:
        @pl.when(j == nj - 1)
        def _():
            o_ref[...] = _rms(o_ref[...], go_ref[...])


def _ffn_sample(x, g, w_in, cw, cb, w_out, g_out, st_val, st_gate, *, layer, n_layers, states, bs, nt,
                final_norm):
    rows, d = x.shape
    fp = w_out.shape[1]
    f = st_val.shape[-1]
    tf = FFN_TF
    nj = fp // tf
    kern = functools.partial(_ffn_sample_kernel, bs=bs, nt=nt, nj=nj, tf=tf, f=f, final_norm=final_norm)
    st_spec = pl.BlockSpec((bs, CONV_WIDTH - 1, tf), lambda j: (0, 0, j))
    so_spec = pl.BlockSpec((None, bs, CONV_WIDTH - 1, tf), lambda j: (layer, 0, 0, j))
    st_shape = (n_layers, bs, CONV_WIDTH - 1, fp)
    extra, extra_specs, aliases = [], [], {}
    for q, prev in enumerate(states or (None, None)):
        e, s, al = _stacked(prev, st_shape, 12 + q, 1 + q)
        extra, extra_specs, aliases = extra + e, extra_specs + s, {**aliases, **al}
    return pl.pallas_call(
        kern,
        grid=(nj,),
        in_specs=[
            pl.BlockSpec((rows, d), lambda j: (0, 0)),
            pl.BlockSpec((1, d), lambda j: (0, 0)),
            pl.BlockSpec((None, d, tf), lambda j: (layer, 0, j)),
            pl.BlockSpec((None, d, tf), lambda j: (layer, 0, nj + j)),
            pl.BlockSpec((CONV_WIDTH, tf), lambda j: (0, j)),
            pl.BlockSpec((CONV_WIDTH, tf), lambda j: (0, nj + j)),
            pl.BlockSpec((1, tf), lambda j: (0, j)),
            pl.BlockSpec((1, tf), lambda j: (0, nj + j)),
            pl.BlockSpec((None, tf, d), lambda j: (layer, j, 0)),
            pl.BlockSpec((1, d), lambda j: (0, 0)),
            st_spec, st_spec,
        ] + extra_specs,
        out_specs=[pl.BlockSpec((rows, d), lambda j: (0, 0)), so_spec, so_spec],
        out_shape=[jax.ShapeDtypeStruct((rows, d), F32)] + [jax.ShapeDtypeStruct(st_shape, F32)] * 2,
        input_output_aliases=aliases,
        scratch_shapes=[pltpu.VMEM((rows, d), BF16)],
        compiler_params=_params(("arbitrary",)),
        name="ffn_sample",
    )(x, g, w_in, w_in, cw, cw, cb, cb, w_out, g_out, st_val, st_gate, *extra)


def _proj_kernel(*refs, tm, tps, padf, sample, bs, has_vlora):
    it = iter(refs)
    x_ref = next(it)
    sh_ref = next(it) if sample else None
    g_ref, mu_ref, wr_ref, wk_ref, wv_ref, w1_ref, a1_ref, g1_ref = (next(it) for _ in range(8))
    w2_ref, a2_ref, g2_ref, w0_ref, a0_ref, kk_ref, ka_ref, bd_ref = (next(it) for _ in range(8))
    if has_vlora:
        v1_ref, v2_ref, v0_ref, vf_ref = (next(it) for _ in range(4))
    r_o, ld_o, k_o, v_o, kn_o, a_o, g_o, hl_o = (next(it) for _ in range(8))
    xr_s, xk_s, xv_s, lw_s, la_s, lg_s = (next(it) for _ in range(6))
    lv_s = next(it) if has_vlora else None
    car_s = None if sample else next(it)

    i = pl.program_id(0)
    j = pl.program_id(1)

    @pl.when(j == 0)
    def _():
        x = x_ref[...]
        h = _rms(x, g_ref[...])
        if sample:
            prev = jnp.concatenate([sh_ref[...], h[:tm - bs]], axis=0)
            hl_o[...] = h[tm - bs:]
        else:
            ti = i % tps
            rloc = lax.broadcasted_iota(jnp.int32, (tm, 1), 0)
            h = jnp.where(ti * tm + rloc >= padf, h, 0.0)
            last = jnp.where(ti != 0, car_s[CONV_HALO - 1:CONV_HALO, :], 0.0)
            prev = jnp.where(rloc == 0, last, pltpu.roll(h, 1, 0))
            car_s[...] = h[tm - CONV_HALO:]
            hl_o[0] = h[tm - CONV_HALO:]
        xx = prev - h
        mu = mu_ref[...]
        mix = lambda q: (h + xx * mu[q:q + 1]).astype(BF16)
        xr_s[...] = mix(0)
        xk_s[...] = mix(2)
        xv = mix(3)
        xv_s[...] = xv
        lw_s[...] = jnp.tanh(_dot(mix(1), w1_ref[...])).astype(BF16)
        la_s[...] = _dot(mix(4), a1_ref[...]).astype(BF16)
        lg_s[...] = jax.nn.sigmoid(_dot(mix(5), g1_ref[...])).astype(BF16)
        if has_vlora:
            lv_s[...] = _dot(xv, v1_ref[...]).astype(BF16)

    nt = tm // bs if sample else 0

    def put(o_ref, val):
        if sample:
            for t in range(SAMPLE_TOK):
                o_ref[:, t, :] = val[t * bs:(t + 1) * bs] if t < nt else jnp.zeros((bs, val.shape[1]), F32)
        else:
            o_ref[...] = val

    r = _dot(xr_s[...], wr_ref[...])
    k = _dot(xk_s[...], wk_ref[...])
    v = _dot(xv_s[...], wv_ref[...])
    z = w0_ref[...] + _dot(lw_s[...], w2_ref[...])
    a = jax.nn.sigmoid(a0_ref[...] + _dot(la_s[...], a2_ref[...]))
    put(ld_o, -math.exp(-0.5) * jax.nn.sigmoid(z))
    if has_vlora:
        vf = jnp.concatenate([vf_ref[:, t, :] for t in range(nt)], axis=0) if sample else vf_ref[...]
        v = v + (vf - v) * jax.nn.sigmoid(v0_ref[...] + _dot(lv_s[...], v2_ref[...]))
    put(g_o, _dot(lg_s[...], g2_ref[...]))
    kk = k * kk_ref[...]
    sq_hi, sq_lo = _split(kk * kk)
    ss = _dot(sq_hi, bd_ref[...]) + _dot(sq_lo, bd_ref[...])
    put(kn_o, kk / jnp.maximum(jnp.sqrt(ss), 1e-12))
    put(r_o, r)
    put(k_o, k * (1.0 + (a - 1.0) * ka_ref[...]))
    put(v_o, v)
    put(a_o, a)


def _proj(x, sh, p, vl, vfirst, *, layer, tp, sample, bs):
    rows, d = x.shape
    tn = min(PROJ_TN_SAMPLE if sample else PROJ_TN, d)
    nj = d // tn
    p = dict(p, bd=p["bd"][:tn, :tn])
    if sample:
        tm, tps = rows, 1
    else:
        tm = _row_tile(tp, PROJ_TM_CAP)
        tps = tp // tm
    has_vlora = vl is not None
    kern = functools.partial(_proj_kernel, tm=tm, tps=tps, padf=FRONT_PAD, sample=sample, bs=bs,
                             has_vlora=has_vlora)
    full = lambda a: pl.BlockSpec(a.shape, lambda i, j: (0,) * a.ndim)
    colb = lambda a: pl.BlockSpec((a.shape[0], tn), lambda i, j: (0, j))
    if sample:
        rowb = pl.BlockSpec((bs, SAMPLE_TOK, tn), lambda i, j: (0, 0, j))
        row_shape = jax.ShapeDtypeStruct((bs, SAMPLE_TOK, d), F32)
    else:
        rowb = pl.BlockSpec((tm, tn), lambda i, j: (i, j))
        row_shape = jax.ShapeDtypeStruct((rows, d), F32)
    args, specs = [x], [pl.BlockSpec((tm, d), lambda i, j: (i, 0))]
    if sample:
        args.append(sh)
        specs.append(full(sh))
    for name in ("g", "mu"):
        args.append(p[name]); specs.append(full(p[name]))
    for name in ("wr", "wk", "wv"):
        args.append(p[name]); specs.append(pl.BlockSpec((None, d, tn), lambda i, j: (layer, 0, j)))
    for name in ("w1", "a1", "g1"):
        args.append(p[name]); specs.append(full(p[name]))
    for name in ("w2", "a2", "g2", "w0", "a0", "kk", "ka"):
        args.append(p[name]); specs.append(colb(p[name]))
    args.append(p["bd"]); specs.append(full(p["bd"]))
    if has_vlora:
        args += [vl["v1"], vl["v2"], vl["v0"], vfirst]
        specs += [full(vl["v1"]), colb(vl["v2"]), colb(vl["v0"]), rowb]
    out_shape = [row_shape] * 7
    out_specs = [rowb] * 7
    if sample:
        out_shape.append(jax.ShapeDtypeStruct((bs, d), F32))
        out_specs.append(pl.BlockSpec((bs, d), lambda i, j: (0, 0)))
    else:
        nb = rows // tp
        out_shape.append(jax.ShapeDtypeStruct((nb, CONV_HALO, d), F32))
        out_specs.append(pl.BlockSpec((1, CONV_HALO, d), lambda i, j: (i // tps, 0, 0)))
    lw, la, lg = p["w1"].shape[1], p["a1"].shape[1], p["g1"].shape[1]
    scratch = [pltpu.VMEM((tm, d), BF16)] * 3 + [pltpu.VMEM((tm, lw), BF16), pltpu.VMEM((tm, la), BF16),
                                                 pltpu.VMEM((tm, lg), BF16)]
    if has_vlora:
        scratch.append(pltpu.VMEM((tm, vl["v1"].shape[1]), BF16))
    if not sample:
        scratch.append(pltpu.VMEM((CONV_HALO, d), F32))
    return pl.pallas_call(
        kern,
        grid=(rows // tm, nj),
        in_specs=specs,
        out_specs=out_specs,
        out_shape=out_shape,
        scratch_shapes=scratch,
        compiler_params=_params(("arbitrary", "arbitrary")),
        name="rwkv_proj_sample" if sample else "rwkv_proj_prompt",
    )(*args)


def _mm(a, b):
    return _dot(a.astype(BF16), b.astype(BF16))


def _mm_nt(a, b):
    return lax.dot_general(a.astype(BF16), b.astype(BF16), (((1,), (1,)), ((), ())),
                           preferred_element_type=F32)


def _wkv_prompt_kernel(r_ref, ld_ref, k_ref, v_ref, kn_ref, a_ref, g_ref, s0_ref, lnw_ref, lnb_ref, rk_ref,
                       z_ref, so_ref, st_ref, *, L, nchunks, npairs):
    hs = HEAD_SIZE
    assert L == hs and npairs % 2 == 0
    ib = pl.program_id(2)
    lane = lax.broadcasted_iota(jnp.int32, (1, LANES), 1)
    head0 = lane < hs
    m0 = head0.astype(F32)
    m1 = 1.0 - m0
    m0w = jnp.concatenate([m0, m0], axis=1)
    m1w = 1.0 - m0w
    ri = lax.broadcasted_iota(jnp.int32, (LANES, LANES), 0)
    ci = lax.broadcasted_iota(jnp.int32, (LANES, LANES), 1)
    bdmask = ((ri < hs) == (ci < hs)).astype(F32)
    rl = lax.broadcasted_iota(jnp.int32, (L, L), 0)
    cl = lax.broadcasted_iota(jnp.int32, (L, L), 1)
    tril_incl = (cl <= rl).astype(BF16)
    prow = lax.broadcasted_iota(jnp.int32, (2 * L, 4 * L), 0)
    pcol = lax.broadcasted_iota(jnp.int32, (2 * L, 4 * L), 1) % L
    pmask = jnp.where(prow >= L, (pcol <= prow - L).astype(F32), (pcol < prow).astype(F32))
    qrow = lax.broadcasted_iota(jnp.int32, (4 * L, 4 * L), 0)
    qcol = lax.broadcasted_iota(jnp.int32, (4 * L, 4 * L), 1)
    bd4 = (qrow // L) == (qcol // L)
    eye4 = (lax.broadcasted_iota(jnp.int32, (L, 4 * L), 1) % L
            == lax.broadcasted_iota(jnp.int32, (L, 4 * L), 0)).astype(F32)
    nsq = int(math.log2(L))
    zeros_l = jnp.zeros((L, LANES), F32)
    zeros_b = jnp.zeros((4 * L, 4 * L), BF16)

    @pl.when(ib == 0)
    def _():
        zed = jnp.zeros((hs, hs), F32)
        for p in range(npairs):
            st_ref[p] = jnp.concatenate([jnp.concatenate([s0_ref[0, 2 * p], zed], axis=1),
                                         jnp.concatenate([zed, s0_ref[0, 2 * p + 1]], axis=1)], axis=0)

    def head_sum(y):
        s0 = jnp.sum(y * m0, axis=-1, keepdims=True)
        s1 = jnp.sum(y * m1, axis=-1, keepdims=True)
        return jnp.where(head0, s0, s1)

    split = lambda x: jnp.concatenate([x * m0, x * m1], axis=0)
    splitw = lambda x: jnp.concatenate([x * m0w, x * m1w], axis=0)

    def stream(rows, pairs, ld_all, cs_all):
        pr = range(len(pairs))
        lss = [slice(p * LANES, (p + 1) * LANES) for p in pairs]
        ld = [ld_all[:, ls] for ls in lss]
        cs = [cs_all[:, ls] for ls in lss]
        c_end = [x[L - 1:L, :] for x in cs]
        r = [r_ref[0, rows, ls] for ls in lss]
        k = [k_ref[0, rows, ls] for ls in lss]
        v = [v_ref[0, rows, ls] for ls in lss]
        kn = [kn_ref[0, rows, ls] for ls in lss]
        b = [kn[p] * a_ref[0, rows, lss[p]] for p in pr]
        w_inv = [jnp.exp(-x) for x in cs]
        rt = [r[p] * jnp.exp(cs[p]) for p in pr]
        kt = [k[p] * w_inv[p] for p in pr]
        bt = [b[p] * w_inv[p] for p in pr]
        at = [-kn[p] * jnp.exp(cs[p] - ld[p]) for p in pr]
        w_end = [jnp.exp(c_end[p] - cs[p]) for p in pr]
        kh = [k[p] * w_end[p] for p in pr]
        bh = [b[p] * w_end[p] for p in pr]
        yield
        pm = [_mm_nt(jnp.concatenate([at[p], rt[p]], axis=0),
                     jnp.concatenate([split(bt[p]), split(kt[p])], axis=0)) * pmask for p in pr]
        yield
        akv = [_mm(pm[p][:L, 2 * L:], split(v[p])) for p in pr]
        gr = range(len(pairs) // 2)
        pw = [jnp.concatenate([pm[2 * g][:L, :2 * L], pm[2 * g + 1][:L, :2 * L]], axis=1) for g in gr]
        tq = [eye4 for _ in gr]
        for i in range(nsq):
            blk = [jnp.where(bd4, jnp.concatenate([x.astype(BF16)] * 4, axis=0), zeros_b) for x in pw]
            if i < nsq - 1:
                both = [_dot(jnp.concatenate([pw[g], tq[g]], axis=0).astype(BF16), blk[g]) for g in gr]
                pw = [x[:L] for x in both]
                tq = [tq[g] + both[g][L:] for g in gr]
            else:
                tq = [tq[g] + _dot(tq[g].astype(BF16), blk[g]) for g in gr]
            yield
        tinv = [t[:, h * 2 * L:(h + 1) * 2 * L] for t in tq for h in range(2)]
        tz = [_mm(tinv[p], splitw(jnp.concatenate([at[p], akv[p]], axis=1))) for p in pr]
        yield
        zv = [jnp.concatenate([zeros_l, v[p]], axis=1) for p in pr]
        ry = [_mm(pm[p][L:], jnp.concatenate([splitw(tz[p]), splitw(zv[p])], axis=0)) for p in pr]
        nc = [_mm(jnp.concatenate([tz[p][:, :LANES], tz[p][:, LANES:], v[p]], axis=0).T,
                  jnp.concatenate([jnp.concatenate([bh[p], zeros_l], axis=1),
                                   jnp.concatenate([zeros_l, bh[p]], axis=1),
                                   jnp.concatenate([zeros_l, kh[p]], axis=1)], axis=0)) for p in pr]
        yield
        r2 = [rt[p] + ry[p][:, :LANES] for p in pr]
        state = [st_ref[q] for q in pairs]
        y = [_mm_nt(r2[p], state[p]) + ry[p][:, LANES:] for p in pr]
        for p, q in enumerate(pairs):
            st_ref[q] = (state[p] * jnp.exp(c_end[p]) + _mm(state[p], nc[p][:, :LANES] * bdmask)
                         + nc[p][:, LANES:] * bdmask)
        yield
        mean = [head_sum(x) * (1.0 / hs) for x in y]
        yc = [y[p] - mean[p] for p in pr]
        var = [head_sum(x * x) * (1.0 / hs) for x in yc]
        bonus = [head_sum(r[p] * k[p] * rk_ref[:, lss[p]]) * v[p] for p in pr]
        for p in pr:
            yn = yc[p] * lax.rsqrt(var[p] + GN_EPS) * lnw_ref[:, lss[p]] + lnb_ref[:, lss[p]]
            z_ref[0, rows, lss[p]] = ((yn + bonus[p]) * g_ref[0, rows, lss[p]]).astype(z_ref.dtype)

    halves = 1
    per = npairs // halves
    live = []
    for c in range(nchunks):
        rows = slice(c * L, (c + 1) * L)
        ld_all = ld_ref[0, rows, :]
        ld_hi = ld_all.astype(BF16)
        ld_r = ld_all - ld_hi.astype(F32)
        ld_mid = ld_r.astype(BF16)
        ld_lo = (ld_r - ld_mid.astype(F32)).astype(BF16)
        cs_all = _dot(tril_incl, ld_hi) + _dot(tril_incl, ld_mid) + _dot(tril_incl, ld_lo)
        live += [stream(rows, list(range(s * per, (s + 1) * per)), ld_all, cs_all) for s in range(halves)]
    lag = 0
    while live:
        for s in list(live[:lag + 1]):
            if next(s, "done") == "done":
                live.remove(s)
        lag += 1

    @pl.when(ib == pl.num_programs(2) - 1)
    def _():
        for p in range(npairs):
            state = st_ref[p]
            so_ref[0, 2 * p] = state[:hs, :hs]
            so_ref[0, 2 * p + 1] = state[hs:, hs:]


def _wkv_prompt(r, ld, k, v, kn, a, g, s0, lnw, lnb, rk):
    nb, tt, d = r.shape
    nh = d // HEAD_SIZE
    npairs = min(WKV_PAIRS, nh // 2)
    tr = _row_tile(tt, WKV_ROWS_CAP, CHUNK)
    lw = npairs * LANES
    kern = functools.partial(_wkv_prompt_kernel, L=CHUNK, nchunks=tr // CHUNK, npairs=npairs)
    seqb = pl.BlockSpec((1, tr, lw), lambda bi, p, i: (bi, i, p))
    vecb = pl.BlockSpec((1, lw), lambda bi, p, i: (0, p))
    stb = pl.BlockSpec((1, 2 * npairs, HEAD_SIZE, HEAD_SIZE), lambda bi, p, i: (bi, p, 0, 0))
    return pl.pallas_call(
        kern,
        grid=(nb, nh // (2 * npairs), tt // tr),
        in_specs=[seqb] * 7 + [stb, vecb, vecb, vecb],
        out_specs=[seqb, stb],
        out_shape=[jax.ShapeDtypeStruct((nb, tt, d), BF16),
                   jax.ShapeDtypeStruct((nb, nh, HEAD_SIZE, HEAD_SIZE), F32)],
        scratch_shapes=[pltpu.VMEM((npairs, LANES, LANES), F32)],
        compiler_params=_params(("arbitrary", "arbitrary", "arbitrary")),
        name="wkv_prompt",
    )(r, ld, k, v, kn, a, g, s0, lnw, lnb, rk)


def _wkv_sample_kernel(r_ref, ld_ref, k_ref, v_ref, kn_ref, a_ref, g_ref, s0_ref, lnw_ref, lnb_ref, rk_ref,
                       *rest, ngroups, npairs):
    z_ref, so_ref = rest[-2:]
    hs, L, T = HEAD_SIZE, CHUNK, SAMPLE_TOK
    ns = L // T
    lane = lax.broadcasted_iota(jnp.int32, (1, LANES), 1)
    head0 = lane < hs
    head0w = (lax.broadcasted_iota(jnp.int32, (1, 2 * LANES), 1) % LANES) < hs
    m0 = head0.astype(F32)
    m1 = 1.0 - m0
    ri = lax.broadcasted_iota(jnp.int32, (LANES, LANES), 0)
    ci = lax.broadcasted_iota(jnp.int32, (LANES, LANES), 1)
    bdmask = ((ri < hs) == (ci < hs)).astype(F32)
    eye_w = (ri == ci).astype(BF16)
    rl = lax.broadcasted_iota(jnp.int32, (L, L), 0)
    cl = lax.broadcasted_iota(jnp.int32, (L, L), 1)
    same = (rl // T) == (cl // T)
    cum_lhs = jnp.concatenate([jnp.where(same, (cl % T <= rl % T).astype(F32), 0.0),
                               same.astype(F32)], axis=0).astype(BF16)
    eye = (cl == rl).astype(F32)
    prow = lax.broadcasted_iota(jnp.int32, (4 * L, 2 * L), 0)
    pcol = lax.broadcasted_iota(jnp.int32, (4 * L, 2 * L), 1) % L
    psame = ((prow % L) // T) == (pcol // T)
    pcaus = jnp.where(prow >= 2 * L, (pcol % T <= prow % T).astype(F32), (pcol % T < prow % T).astype(F32))
    pmask = jnp.where(psame, pcaus, 0.0)
    kcols = (lax.broadcasted_iota(jnp.int32, (1, 2 * L), 1) >= L).astype(F32)
    nsteps = int(math.log2(T)) - 1
    zeros_l = jnp.zeros((L, LANES), F32)
    zeros_t = jnp.zeros((T, LANES), F32)
    zed = jnp.zeros((hs, hs), F32)

    def head_sum(y):
        s0 = jnp.sum(y * m0, axis=-1, keepdims=True)
        s1 = jnp.sum(y * m1, axis=-1, keepdims=True)
        return jnp.where(head0, s0, s1)

    def group_body(gi, carry):
        seqs = pl.ds(pl.multiple_of(gi * ns, ns), ns)
        pr = range(npairs)
        lss = [slice(p * LANES, (p + 1) * LANES) for p in pr]
        tile = lambda ref, ls: ref[seqs, :, ls].reshape(L, LANES)
        ld = [tile(ld_ref, ls) for ls in lss]
        ld_hi = [x.astype(BF16) for x in ld]
        ld_r = [x - h.astype(F32) for x, h in zip(ld, ld_hi)]
        ld_mid = [x.astype(BF16) for x in ld_r]
        ld_lo = [(x - m.astype(F32)).astype(BF16) for x, m in zip(ld_r, ld_mid)]
        cc = [_dot(cum_lhs, ld_hi[p]) + _dot(cum_lhs, ld_mid[p]) + _dot(cum_lhs, ld_lo[p]) for p in pr]
        cs = [x[:L] for x in cc]
        c_end = [x[L:] for x in cc]
        r = [tile(r_ref, ls) for ls in lss]
        k = [tile(k_ref, ls) for ls in lss]
        v = [tile(v_ref, ls) for ls in lss]
        kn = [tile(kn_ref, ls) for ls in lss]
        b = [kn[p] * tile(a_ref, lss[p]) for p in pr]
        w_inv = [jnp.exp(-x) for x in cs]
        rt = [r[p] * jnp.exp(cs[p]) for p in pr]
        kt = [k[p] * w_inv[p] for p in pr]
        bt = [b[p] * w_inv[p] for p in pr]
        at = [-kn[p] * jnp.exp(cs[p] - ld[p]) for p in pr]
        w_end = [jnp.exp(c_end[p] - cs[p]) for p in pr]
        w_tot = [jnp.exp(x) for x in c_end]
        kh = [k[p] * w_end[p] for p in pr]
        bh = [b[p] * w_end[p] for p in pr]
        pm = [_mm_nt(jnp.concatenate([at[p] * m0, at[p] * m1, rt[p] * m0, rt[p] * m1], axis=0),
                     jnp.concatenate([bt[p], kt[p]], axis=0)) * pmask for p in pr]
        a_bk = [x[:2 * L] for x in pm]
        r_bk = [x[2 * L:] for x in pm]
        akv = [_mm(a_bk[p] * kcols, jnp.concatenate([v[p], v[p]], axis=0)) for p in pr]
        akv = [jnp.where(head0, x[:L], x[L:]) for x in akv]
        pw = [a_bk[p][h * L:(h + 1) * L, :L] for p in pr for h in range(2)]
        tinv = [eye + x for x in pw]
        for _ in range(nsteps):
            pw = [_mm(x, x) for x in pw]
            tinv = [t + _mm(t, x) for t, x in zip(tinv, pw)]
        tz_rhs = [jnp.concatenate([at[p], akv[p]], axis=1) for p in pr]
        tz = [jnp.where(head0w, _mm(tinv[2 * p], tz_rhs[p]), _mm(tinv[2 * p + 1], tz_rhs[p])) for p in pr]
        ry = [_mm(r_bk[p], jnp.concatenate([tz[p], jnp.concatenate([zeros_l, v[p]], axis=1)], axis=0))
              for p in pr]
        ry = [jnp.where(head0w, x[:L], x[L:]) for x in ry]
        r2 = [rt[p] + ry[p][:, :LANES] for p in pr]
        ps = [(p, i) for p in pr for i in range(ns)]
        rs = [slice(i * T, (i + 1) * T) for i in range(ns)]
        state = [jnp.concatenate([jnp.concatenate([s0_ref[gi * ns + i, 2 * p], zed], axis=1),
                                  jnp.concatenate([zed, s0_ref[gi * ns + i, 2 * p + 1]], axis=1)], axis=0)
                 for p, i in ps]
        sb = [x.astype(BF16) for x in state]
        e_rhs = [jnp.concatenate([tz[p][rs[i]], jnp.concatenate([zeros_t, v[p][rs[i]]], axis=1)], axis=0)
                 for p, i in ps]
        e = [_mm_nt(jnp.concatenate([sb[q], eye_w], axis=1), e_rhs[q]) for q in range(len(ps))]
        upd = [_mm(e[q], jnp.concatenate([bh[p][rs[i]], kh[p][rs[i]]], axis=0)) * bdmask
               for q, (p, i) in enumerate(ps)]
        ys = [_mm_nt(r2[p][rs[i]], sb[q]) for q, (p, i) in enumerate(ps)]
        for q, (p, i) in enumerate(ps):
            new = state[q] * w_tot[p][i * T:i * T + 1] + upd[q]
            so_ref[gi * ns + i, 2 * p] = new[:hs, :hs]
            so_ref[gi * ns + i, 2 * p + 1] = new[hs:, hs:]
        y = [jnp.concatenate(ys[p * ns:(p + 1) * ns], axis=0) + ry[p][:, LANES:] for p in pr]
        mean = [head_sum(x) * (1.0 / hs) for x in y]
        yc = [y[p] - mean[p] for p in pr]
        var = [head_sum(x * x) * (1.0 / hs) for x in yc]
        bonus = [head_sum(r[p] * k[p] * rk_ref[:, lss[p]]) * v[p] for p in pr]
        for p in pr:
            yn = yc[p] * lax.rsqrt(var[p] + GN_EPS) * lnw_ref[:, lss[p]] + lnb_ref[:, lss[p]]
            z_ref[seqs, :, lss[p]] = ((yn + bonus[p]) * tile(g_ref, lss[p])).reshape(ns, T, LANES)
        return carry

    lax.fori_loop(0, ngroups, group_body, 0)


def _stacked(prev, shape, n_inputs, out_index):
    prev = jnp.zeros(shape, F32) if prev is None else prev
    return [prev], [pl.BlockSpec(memory_space=pl.ANY)], {n_inputs: out_index}


def _wkv_sample(r, ld, k, v, kn, a, g, s0, lnw, lnb, rk, *, layer, n_layers, states):
    nb, tt, d = r.shape
    nh = d // HEAD_SIZE
    npairs = min(WKV_SAMPLE_PAIRS, nh // 2)
    ns = CHUNK // SAMPLE_TOK
    bb = min(nb, 2 * ns)
    lw = npairs * LANES
    kern = functools.partial(_wkv_sample_kernel, ngroups=bb // ns, npairs=npairs)
    seqb = pl.BlockSpec((bb, tt, lw), lambda i, p: (i, 0, p))
    vecb = pl.BlockSpec((1, lw), lambda i, p: (0, p))
    stb = pl.BlockSpec((bb, 2 * npairs, HEAD_SIZE, HEAD_SIZE), lambda i, p: (i, p, 0, 0))
    sto = pl.BlockSpec((None, bb, 2 * npairs, HEAD_SIZE, HEAD_SIZE), lambda i, p: (layer, i, p, 0, 0))
    st_shape = (n_layers, nb, nh, HEAD_SIZE, HEAD_SIZE)
    extra, extra_specs, aliases = _stacked(states, st_shape, 11, 1)
    return pl.pallas_call(
        kern,
        grid=(nb // bb, nh // (2 * npairs)),
        in_specs=[seqb] * 7 + [stb, vecb, vecb, vecb] + extra_specs,
        out_specs=[seqb, sto],
        out_shape=[jax.ShapeDtypeStruct((nb, tt, d), F32), jax.ShapeDtypeStruct(st_shape, F32)],
        input_output_aliases=aliases,
        compiler_params=_params(("arbitrary", "arbitrary")),
        name="wkv_sample",
    )(r, ld, k, v, kn, a, g, s0, lnw, lnb, rk, *extra)


def _wo_kernel(x_ref, z_ref, w_ref, o_ref, *, nt):
    if nt:
        z = jnp.concatenate([z_ref[:, t, :] for t in range(nt)], axis=0)
    else:
        z = z_ref[...]
    o_ref[...] = x_ref[...] + _dot(z.astype(BF16), w_ref[...])


def _wo(x, z, w, *, layer, nt=0):
    rows, d = x.shape
    tm = rows if nt else _row_tile(rows, WO_TM_CAP)
    zspec = pl.BlockSpec(z.shape, lambda i: (0, 0, 0)) if nt else pl.BlockSpec((tm, d), lambda i: (i, 0))
    return pl.pallas_call(
        functools.partial(_wo_kernel, nt=nt),
        grid=(rows // tm,),
        in_specs=[pl.BlockSpec((tm, d), lambda i: (i, 0)), zspec,
                  pl.BlockSpec((None, d, d), lambda i: (layer, 0, 0))],
        out_specs=pl.BlockSpec((tm, d), lambda i: (i, 0)),
        out_shape=jax.ShapeDtypeStruct((rows, d), F32),
        compiler_params=_params(("arbitrary",)),
        name="rwkv_wo_sample" if nt else "rwkv_wo",
    )(x, z, w)


def _pad_cols(a, n):
    return jnp.pad(a, [(0, 0)] * (a.ndim - 1) + [(0, n - a.shape[-1])])


def _pad_rows(a, n):
    return jnp.pad(a, [(0, n - a.shape[0])] + [(0, 0)] * (a.ndim - 1))


def kernel(x_prompt, x_sample, state_pool, state_rwkv_shift, state_rwkv_wkv, state_ffn_conv, meta_tokens,
           norm_mix, norm_ffn, norm_out, pool_w, pool_scale, rwkv_mu, rwkv_wr, rwkv_wk, rwkv_wv, rwkv_wo,
           rwkv_w0, rwkv_w1, rwkv_w2, rwkv_a0, rwkv_a1, rwkv_a2, rwkv_v0, rwkv_v1, rwkv_v2, rwkv_g1, rwkv_g2,
           rwkv_kk, rwkv_ka, rwkv_rk, rwkv_lnw, rwkv_lnb, ffn_w_in, ffn_conv_w, ffn_conv_b, ffn_w_out):
    b, seq, d = x_prompt.shape
    bs, nt, _ = x_sample.shape
    depth = norm_mix.shape[0]
    f = ffn_w_out.shape[1]
    fp = _round_up(f, FFN_TF)
    nh = d // HEAD_SIZE
    tp = FRONT_PAD + N_META + seq
    assert tp % CHUNK == 0 and d % (2 * LANES) == 0 and CONV_WIDTH - 1 <= nt <= SAMPLE_TOK

    row = lambda a: a.reshape(1, -1)
    lora_in = lambda a: _pad_cols(a, _round_up(a.shape[-1], LANES)).astype(BF16)
    lora_out = lambda a: _pad_rows(a, _round_up(a.shape[0], LANES)).astype(BF16)
    halves = lambda a: jnp.concatenate([_pad_cols(a[..., :f], fp), _pad_cols(a[..., f:], fp)], axis=-1)
    w_in_all = _cast_halves(ffn_w_in, fp)
    w_out_all = _cast_rows(ffn_w_out, fp)
    ffn = [dict(g=row(norm_ffn[i]), cw=halves(ffn_conv_w[i]), cb=halves(ffn_conv_b[i][None]))
           for i in range(depth)]
    tn = min(PROJ_TN, d)
    bd = jnp.kron(jnp.eye(tn // HEAD_SIZE, dtype=F32), jnp.ones((HEAD_SIZE, HEAD_SIZE), F32)).astype(BF16)
    wr_all, wk_all, wv_all, wo_all = (_cast_rows(w, d) for w in (rwkv_wr, rwkv_wk, rwkv_wv, rwkv_wo))
    rw = []
    for j in range(depth // 2):
        rw.append(dict(
            g=row(norm_mix[2 * j + 1]), mu=rwkv_mu[j], wr=wr_all, wk=wk_all, wv=wv_all,
            w1=lora_in(rwkv_w1[j]), a1=lora_in(rwkv_a1[j]), g1=lora_in(rwkv_g1[j]),
            w2=lora_out(rwkv_w2[j]), a2=lora_out(rwkv_a2[j]), g2=lora_out(rwkv_g2[j]),
            w0=row(rwkv_w0[j]), a0=row(rwkv_a0[j]), kk=row(rwkv_kk[j]), ka=row(rwkv_ka[j]), bd=bd,
            lnw=row(rwkv_lnw[j]), lnb=row(rwkv_lnb[j]), rk=row(rwkv_rk[j])))
    vls = [None] + [dict(v1=lora_in(rwkv_v1[j]), v2=lora_out(rwkv_v2[j]), v0=row(rwkv_v0[j]))
                    for j in range(depth // 2 - 1)]
    pw = [pool_w[j].astype(BF16) for j in range((depth + 1) // 2)]
    g_out = row(norm_out)

    head = jnp.concatenate([jnp.zeros((FRONT_PAD, d), F32), meta_tokens], axis=0)
    xp = x_prompt
    xs = x_sample.transpose(1, 0, 2)
    zero_wkv = jnp.zeros((b, nh, HEAD_SIZE, HEAD_SIZE), F32)
    n_pool, n_rwkv = (depth + 1) // 2, depth // 2

    pool_p, shift_p, shift_s, wkv_p, conv_p = [], [], [], [], []
    pool_s = wkv_s = conv_s = None
    vfirst_p = vfirst_s = None
    for i in range(depth):
        j = i // 2
        if i % 2 == 0:
            xp, st = _pool_prompt(xp, row(norm_mix[i]), pw[j], row(pool_scale[j]), tp=tp,
                                  head=head if i == 0 else None)
            pool_p.append(st[:, POOL_HALO - POOL_STATE:])
            xs, pool_s = _pool_sample(xs, state_pool[j], row(norm_mix[i]), pw[j], row(pool_scale[j]), nt=nt, bs=bs,
                                      layer=j, n_layers=n_pool, states=pool_s)
        else:
            p = rw[j]
            r, ld, k, v, kn, a, g, hl = _proj(xp.reshape(b * tp, d), None, p, vls[j], vfirst_p,
                                              layer=j, tp=tp, sample=False, bs=0)
            if vfirst_p is None:
                vfirst_p = v
            shift_p.append(hl[:, CONV_HALO - 1])
            sq = lambda t: t.reshape(b, tp, d)
            z, s_new = _wkv_prompt(sq(r), sq(ld), sq(k), sq(v), sq(kn), sq(a), sq(g), zero_wkv,
                                   p["lnw"], p["lnb"], p["rk"])
            wkv_p.append(s_new)
            xp = _wo(xp.reshape(b * tp, d), z.reshape(b * tp, d), wo_all, layer=j).reshape(b, tp, d)
            r, ld, k, v, kn, a, g, hl = _proj(xs.reshape(nt * bs, d), state_rwkv_shift[j], p, vls[j], vfirst_s,
                                              layer=j, tp=0, sample=True, bs=bs)
            if vfirst_s is None:
                vfirst_s = v
            shift_s.append(hl)
            z, wkv_s = _wkv_sample(r, ld, k, v, kn, a, g, state_rwkv_wkv[j], p["lnw"], p["lnb"], p["rk"],
                                   layer=j, n_layers=n_rwkv, states=wkv_s)
            xs = _wo(xs.reshape(nt * bs, d), z, wo_all, layer=j, nt=nt).reshape(nt, bs, d)
        fi = ffn[i]
        last = i == depth - 1
        xo, stv, stg = _ffn_prompt(xp.reshape(b * tp, d), fi["g"], w_in_all, fi["cw"], fi["cb"], w_out_all,
                                   g_out, layer=i, tp=tp, final_norm=last)
        xp = xo.reshape(b, tp, d)
        stv, stg = (t.reshape(b, -1, CONV_HALO, fp)[:, -1] for t in (stv, stg))
        conv_p.append(jnp.concatenate([stv[:, CONV_HALO - 2:, :f], stg[:, CONV_HALO - 2:, :f]], axis=-1))
        xo, *conv_s = _ffn_sample(xs.reshape(nt * bs, d), fi["g"], w_in_all, fi["cw"], fi["cb"], w_out_all, g_out,
                                  state_ffn_conv[i][:, :, :f], state_ffn_conv[i][:, :, f:],
                                  layer=i, n_layers=depth, states=conv_s, bs=bs, nt=nt, final_norm=last)
        xs = xo.reshape(nt, bs, d)

    y_prompt = xp[:, FRONT_PAD + N_META:]
    y_sample = xs.transpose(1, 0, 2)
    conv_s = jnp.concatenate([conv_s[0][..., :f], conv_s[1][..., :f]], axis=-1)
    return (y_prompt, y_sample, jnp.stack(pool_p), pool_s, jnp.stack(shift_p), jnp.stack(shift_s),
            jnp.stack(wkv_p), wkv_s, jnp.stack(conv_p), conv_s)
```

```python
import functools
import math

import jax
import jax.numpy as jnp
from jax import lax
from jax.experimental import pallas as pl
from jax.experimental.pallas import tpu as pltpu

F32 = jnp.float32
BF16 = jnp.bfloat16

HEAD_SIZE = 64
LANES = 128
N_META = 16
PAST_LEN = 16384
POOL_WINDOWS = (2, 4, 8, 16)
POOL_STATE = max(POOL_WINDOWS) - 1
POOL_HALO = 16
CONV_WIDTH = 3
CONV_HALO = 8
NORM_EPS = 1e-6
GN_EPS = 64e-5
CHUNK = 64
SAMPLE_TOK = 8
FRONT_PAD = CHUNK - N_META

FFN_TM_CAP = 704
PROJ_TM_CAP = 352
POOL_TM_CAP = 704
POOL_ASSEMBLE = 3
WO_TM_CAP = 704
WKV_ROWS_CAP = 192
WKV_PAIRS = 16
WKV_SAMPLE_PAIRS = 8
CAST_ROWS = 512
FFN_TF = 512
PROJ_TN = 512
PROJ_TN_SAMPLE = 256
VMEM_LIMIT = 56 * 1024 * 1024


def _row_tile(n, cap, mult=16):
    best = None
    for t in range(mult, min(n, cap) + 1, mult):
        if n % t == 0:
            best = t
    assert best is not None, (n, cap)
    return best


def _round_up(n, m):
    return (n + m - 1) // m * m


def _params(sem):
    return pltpu.CompilerParams(dimension_semantics=sem, vmem_limit_bytes=VMEM_LIMIT)


def _rms(x, g):
    return x * lax.rsqrt(jnp.mean(x * x, axis=-1, keepdims=True) + NORM_EPS) * g


def _dot(a, b):
    return jnp.dot(a, b, preferred_element_type=F32)


def _split(x):
    hi = x.astype(BF16)
    lo = (x - hi.astype(F32)).astype(BF16)
    return hi, lo


def _cast_kernel(x_ref, o_ref, *, valid_rows, tr):
    x = x_ref[...]
    if valid_rows is not None:
        row = pl.program_id(1) * tr + lax.broadcasted_iota(jnp.int32, (1, tr, 1), 1)
        x = jnp.where(row < valid_rows, x, 0.0)
    o_ref[...] = x.astype(o_ref.dtype)


def _cast_rows(w, out_rows):
    n, rows, cols = w.shape
    tr = CAST_ROWS
    kern = functools.partial(_cast_kernel, valid_rows=None if rows % tr == 0 else rows, tr=tr)
    return pl.pallas_call(
        kern,
        grid=(n, out_rows // tr),
        in_specs=[pl.BlockSpec((1, tr, cols), lambda l, i: (l, i, 0))],
        out_specs=pl.BlockSpec((1, tr, cols), lambda l, i: (l, i, 0)),
        out_shape=jax.ShapeDtypeStruct((n, out_rows, cols), BF16),
        compiler_params=_params(("arbitrary", "arbitrary")),
        name="cast_rows",
    )(w)


def _cast_halves_kernel(x_ref, o_ref, *, f, fp):
    x = x_ref[0]
    pad = jnp.zeros((x.shape[0], fp - f), o_ref.dtype)
    o_ref[0, :, :f] = x[:, :f].astype(o_ref.dtype)
    o_ref[0, :, fp:fp + f] = x[:, f:].astype(o_ref.dtype)
    if fp > f:
        o_ref[0, :, f:fp] = pad
        o_ref[0, :, fp + f:] = pad


def _cast_halves(w, fp):
    n, rows, f2 = w.shape
    f = f2 // 2
    tr = min(rows, CAST_ROWS // 2)
    return pl.pallas_call(
        functools.partial(_cast_halves_kernel, f=f, fp=fp),
        grid=(n, rows // tr),
        in_specs=[pl.BlockSpec((1, tr, f2), lambda l, i: (l, i, 0))],
        out_specs=pl.BlockSpec((1, tr, 2 * fp), lambda l, i: (l, i, 0)),
        out_shape=jax.ShapeDtypeStruct((n, rows, 2 * fp), BF16),
        compiler_params=_params(("arbitrary", "arbitrary")),
        name="cast_halves",
    )(w)


def _pool_prompt_kernel(*refs, tm, padf, cg, assemble):
    nx = assemble or 1
    x_refs, refs = refs[:nx], refs[nx:]
    if assemble:
        head_ref, refs = refs[0], refs[1:]
    g_ref, w_ref, sc_ref, o_ref, st_ref, carry_ref = refs
    i = pl.program_id(1)

    @pl.when(i == 0)
    def _():
        carry_ref[...] = jnp.zeros_like(carry_ref)

    if assemble:
        x = jnp.concatenate([jnp.where(i == 0, head_ref[...], x_refs[0][0])] + [r[0] for r in x_refs[1:]], axis=0)
    else:
        x = x_refs[0][0]
    row = i * tm + lax.broadcasted_iota(jnp.int32, (tm, 1), 0)
    h = jnp.where(row >= padf, _rms(x, g_ref[...]), 0.0)
    ext = jnp.concatenate([carry_ref[...], h], axis=0)
    pos = row - padf
    for g, w in enumerate(POOL_WINDOWS):
        sl = slice(g * cg, (g + 1) * cg)
        a = ext[:, sl]
        k = 1
        while k < w:
            n = a.shape[0]
            a = a[:n - k] + a[k:]
            k *= 2
        win = a[POOL_HALO + 1 - w: POOL_HALO + 1 - w + tm]
        cnt = jnp.clip(pos + 1, 1, w).astype(F32)
        d = win / cnt - h[:, sl]
        y = _dot(d.astype(BF16), w_ref[g])
        o_ref[0, :, sl] = x[:, sl] + y * sc_ref[:, sl]
    carry_ref[...] = h[tm - POOL_HALO:]
    st_ref[0] = h[tm - POOL_HALO:]


def _pool_prompt(x, g, w, sc, *, tp, head=None):
    b, _, d = x.shape
    cg = d // len(POOL_WINDOWS)
    if head is not None:
        hb = head.shape[0]
        assemble = POOL_ASSEMBLE if (tp // hb) % POOL_ASSEMBLE == 0 else 1
        tm = assemble * hb
        x_specs = [pl.BlockSpec((1, hb, d), lambda bi, i, k=k: (bi, jnp.maximum(assemble * i + k - 1, 0), 0))
                   for k in range(assemble)] + [pl.BlockSpec((hb, d), lambda bi, i: (0, 0))]
        x_args = (x,) * assemble + (head,)
    else:
        assemble = 0
        tm = _row_tile(tp, POOL_TM_CAP)
        x_specs = [pl.BlockSpec((1, tm, d), lambda bi, i: (bi, i, 0))]
        x_args = (x,)
    kern = functools.partial(_pool_prompt_kernel, tm=tm, padf=FRONT_PAD, cg=cg, assemble=assemble)
    return pl.pallas_call(
        kern,
        grid=(b, tp // tm),
        in_specs=x_specs + [
            pl.BlockSpec((1, d), lambda bi, i: (0, 0)),
            pl.BlockSpec((len(POOL_WINDOWS), cg, cg), lambda bi, i: (0, 0, 0)),
            pl.BlockSpec((1, d), lambda bi, i: (0, 0)),
        ],
        out_specs=[
            pl.BlockSpec((1, tm, d), lambda bi, i: (bi, i, 0)),
            pl.BlockSpec((1, POOL_HALO, d), lambda bi, i: (bi, 0, 0)),
        ],
        out_shape=[
            jax.ShapeDtypeStruct((b, tp, d), F32),
            jax.ShapeDtypeStruct((b, POOL_HALO, d), F32),
        ],
        scratch_shapes=[pltpu.VMEM((POOL_HALO, d), F32)],
        compiler_params=_params(("arbitrary", "arbitrary")),
        name="pool_prompt_assemble" if assemble else "pool_prompt",
    )(*x_args, g, w, sc)


def _pool_sample_kernel(x_ref, pre_ref, g_ref, w_ref, sc_ref, *rest, tb, d, cg, nt, start):
    o_ref, st_ref = rest[-2:]
    hs = [_rms(x_ref[t], g_ref[...]) for t in range(nt)]
    ext = [pre_ref[:, j, :] for j in range(POOL_STATE)] + hs
    for g, w in enumerate(POOL_WINDOWS):
        sl = slice(g * cg, (g + 1) * cg)
        ds = []
        for t in range(nt):
            e = POOL_STATE + t
            acc = ext[e][:, sl]
            for q in range(1, w):
                acc = acc + ext[e - q][:, sl]
            cnt = float(min(w, start + t + 1))
            ds.append(acc / cnt - hs[t][:, sl])
        y = _dot(jnp.concatenate(ds, axis=0).astype(BF16), w_ref[g])
        for t in range(nt):
            o_ref[t, :, sl] = x_ref[t][:, sl] + y[t * tb:(t + 1) * tb] * sc_ref[:, sl]
    for j in range(POOL_STATE):
        st_ref[:, j, :] = ext[nt + j]


def _pool_sample(x, pre, g, w, sc, *, nt, bs, layer, n_layers, states):
    d = x.shape[-1]
    tb = min(bs, 32)
    cg = d // len(POOL_WINDOWS)
    kern = functools.partial(_pool_sample_kernel, tb=tb, d=d, cg=cg, nt=nt, start=PAST_LEN)
    st_shape = (n_layers, bs, POOL_STATE, d)
    extra, extra_specs, aliases = _stacked(states, st_shape, 5, 1)
    return pl.pallas_call(
        kern,
        grid=(bs // tb,),
        in_specs=[
            pl.BlockSpec((nt, tb, d), lambda i: (0, i, 0)),
            pl.BlockSpec((None, tb, POOL_STATE, d), lambda i: (layer, i, 0, 0)),
            pl.BlockSpec((1, d), lambda i: (0, 0)),
            pl.BlockSpec((len(POOL_WINDOWS), cg, cg), lambda i: (0, 0, 0)),
            pl.BlockSpec((1, d), lambda i: (0, 0)),
        ] + extra_specs,
        out_specs=[
            pl.BlockSpec((nt, tb, d), lambda i: (0, i, 0)),
            pl.BlockSpec((None, tb, POOL_STATE, d), lambda i: (layer, i, 0, 0)),
        ],
        out_shape=[
            jax.ShapeDtypeStruct((nt, bs, d), F32),
            jax.ShapeDtypeStruct(st_shape, F32),
        ],
        input_output_aliases=aliases,
        compiler_params=_params(("arbitrary",)),
        name="pool_sample",
    )(x, pre, g, w, sc, *extra)


def _ffn_prompt_kernel(x_ref, g_ref, wv_ref, wg_ref, cwv_ref, cwg_ref, cbv_ref, cbg_ref, wo_ref, go_ref,
                       o_ref, stv_ref, stg_ref, hb_ref, carv_ref, carg_ref, *, tm, tps, padf, nj, final_norm):
    i = pl.program_id(0)
    j = pl.program_id(1)
    ti = i % tps

    @pl.when(j == 0)
    def _():
        x = x_ref[...]
        row = ti * tm + lax.broadcasted_iota(jnp.int32, (tm, 1), 0)
        h = jnp.where(row >= padf, _rms(x, g_ref[...]), 0.0)
        hb_ref[...] = h.astype(BF16)
        o_ref[...] = x

    hb = hb_ref[...]
    keep = ti != 0

    def branch(w_ref, cw_ref, cb_ref, car_ref, st_ref):
        u = _dot(hb, w_ref[...])
        prev = jnp.where(keep, car_ref[j], 0.0)
        ext = jnp.concatenate([prev, u], axis=0)
        cw = cw_ref[...]
        c = (cb_ref[...] + ext[CONV_HALO - 2:CONV_HALO - 2 + tm] * cw[0:1]
             + ext[CONV_HALO - 1:CONV_HALO - 1 + tm] * cw[1:2] + u * cw[2:3])
        car_ref[j] = u[tm - CONV_HALO:]
        st_ref[0] = u[tm - CONV_HALO:]
        return c

    cv = branch(wv_ref, cwv_ref, cbv_ref, carv_ref, stv_ref)
    cgate = branch(wg_ref, cwg_ref, cbg_ref, carg_ref, stg_ref)
    act = cgate * jax.nn.sigmoid(cgate) * cv
    o_ref[...] += _dot(act.astype(BF16), wo_ref[...])

    if final_norm:
        @pl.when(j == nj - 1)
        def _():
            o_ref[...] = _rms(o_ref[...], go_ref[...])


def _ffn_prompt(x, g, w_in, cw, cb, w_out, g_out, *, layer, tp, final_norm):
    rows, d = x.shape
    fp = w_out.shape[1]
    tf = FFN_TF
    nj = fp // tf
    tm = _row_tile(tp, FFN_TM_CAP)
    tps = tp // tm
    kern = functools.partial(_ffn_prompt_kernel, tm=tm, tps=tps, padf=FRONT_PAD, nj=nj, final_norm=final_norm)
    return pl.pallas_call(
        kern,
        grid=(rows // tm, nj),
        in_specs=[
            pl.BlockSpec((tm, d), lambda i, j: (i, 0)),
            pl.BlockSpec((1, d), lambda i, j: (0, 0)),
            pl.BlockSpec((None, d, tf), lambda i, j: (layer, 0, j)),
            pl.BlockSpec((None, d, tf), lambda i, j: (layer, 0, nj + j)),
            pl.BlockSpec((CONV_WIDTH, tf), lambda i, j: (0, j)),
            pl.BlockSpec((CONV_WIDTH, tf), lambda i, j: (0, nj + j)),
            pl.BlockSpec((1, tf), lambda i, j: (0, j)),
            pl.BlockSpec((1, tf), lambda i, j: (0, nj + j)),
            pl.BlockSpec((None, tf, d), lambda i, j: (layer, j, 0)),
            pl.BlockSpec((1, d), lambda i, j: (0, 0)),
        ],
        out_specs=[
            pl.BlockSpec((tm, d), lambda i, j: (i, 0)),
            pl.BlockSpec((1, CONV_HALO, tf), lambda i, j: (i, 0, j)),
            pl.BlockSpec((1, CONV_HALO, tf), lambda i, j: (i, 0, j)),
        ],
        out_shape=[
            jax.ShapeDtypeStruct((rows, d), F32),
            jax.ShapeDtypeStruct((rows // tm, CONV_HALO, fp), F32),
            jax.ShapeDtypeStruct((rows // tm, CONV_HALO, fp), F32),
        ],
        scratch_shapes=[
            pltpu.VMEM((tm, d), BF16),
            pltpu.VMEM((nj, CONV_HALO, tf), F32),
            pltpu.VMEM((nj, CONV_HALO, tf), F32),
        ],
        compiler_params=_params(("arbitrary", "arbitrary")),
        name="ffn_prompt",
    )(x, g, w_in, w_in, cw, cw, cb, cb, w_out, g_out)


def _ffn_sample_kernel(x_ref, g_ref, wv_ref, wg_ref, cwv_ref, cwg_ref, cbv_ref, cbg_ref, wo_ref, go_ref,
                       pv_ref, pg_ref, *rest, bs, nt, nj, tf, f, final_norm):
    o_ref, sv_ref, sg_ref, hb_ref = rest[-4:]
    j = pl.program_id(0)

    @pl.when(j == 0)
    def _():
        x = x_ref[...]
        hb_ref[...] = _rms(x, g_ref[...]).astype(BF16)
        o_ref[...] = x

    hb = hb_ref[...]
    valid = j * tf + lax.broadcasted_iota(jnp.int32, (1, tf), 1) < f

    def branch(w_ref, cw_ref, cb_ref, p_ref, s_ref):
        u = _dot(hb, w_ref[...])
        prev = [jnp.where(valid, p_ref[:, q, :], 0.0) for q in range(CONV_WIDTH - 1)]
        ext = prev + [u[t * bs:(t + 1) * bs] for t in range(nt)]
        cw = cw_ref[...]
        cs = [cb_ref[...] + ext[t] * cw[0:1] + ext[t + 1] * cw[1:2] + ext[t + 2] * cw[2:3] for t in range(nt)]
        for q in range(CONV_WIDTH - 1):
            s_ref[:, q, :] = ext[nt + q]
        return jnp.concatenate(cs, axis=0)

    cv = branch(wv_ref, cwv_ref, cbv_ref, pv_ref, sv_ref)
    cgate = branch(wg_ref, cwg_ref, cbg_ref, pg_ref, sg_ref)
    act = cgate * jax.nn.sigmoid(cgate) * cv
    o_ref[...] += _dot(act.astype(BF16), wo_ref[...])

    if final_norm:
        @pl.when(j == nj - 1)
        def _():
            o_ref[...] = _rms(o_ref[...], go_ref[...])


def _ffn_sample(x, g, w_in, cw, cb, w_out, g_out, st_val, st_gate, *, layer, n_layers, states, bs, nt,
                final_norm):
    rows, d = x.shape
    fp = w_out.shape[1]
    f = st_gate.shape[-1]
    tf = FFN_TF
    nj = fp // tf
    kern = functools.partial(_ffn_sample_kernel, bs=bs, nt=nt, nj=nj, tf=tf, f=f, final_norm=final_norm)
    st_spec = pl.BlockSpec((bs, CONV_WIDTH - 1, tf), lambda j: (0, 0, j))
    sv_spec = pl.BlockSpec((None, bs, CONV_WIDTH - 1, tf), lambda j: (layer, 0, 0, j))
    so_spec = pl.BlockSpec((None, bs, CONV_WIDTH - 1, tf), lambda j: (layer, 0, 0, j))
    st_shape = (n_layers, bs, CONV_WIDTH - 1, fp)
    extra, extra_specs, aliases = [], [], {}
    for q, prev in enumerate(states or (None, None)):
        e, s, al = _stacked(prev, st_shape, 12 + q, 1 + q)
        extra, extra_specs, aliases = extra + e, extra_specs + s, {**aliases, **al}
    return pl.pallas_call(
        kern,
        grid=(nj,),
        in_specs=[
            pl.BlockSpec((rows, d), lambda j: (0, 0)),
            pl.BlockSpec((1, d), lambda j: (0, 0)),
            pl.BlockSpec((None, d, tf), lambda j: (layer, 0, j)),
            pl.BlockSpec((None, d, tf), lambda j: (layer, 0, nj + j)),
            pl.BlockSpec((CONV_WIDTH, tf), lambda j: (0, j)),
            pl.BlockSpec((CONV_WIDTH, tf), lambda j: (0, nj + j)),
            pl.BlockSpec((1, tf), lambda j: (0, j)),
            pl.BlockSpec((1, tf), lambda j: (0, nj + j)),
            pl.BlockSpec((None, tf, d), lambda j: (layer, j, 0)),
            pl.BlockSpec((1, d), lambda j: (0, 0)),
            sv_spec, st_spec,
        ] + extra_specs,
        out_specs=[pl.BlockSpec((rows, d), lambda j: (0, 0)), so_spec, so_spec],
        out_shape=[jax.ShapeDtypeStruct((rows, d), F32)] + [jax.ShapeDtypeStruct(st_shape, F32)] * 2,
        input_output_aliases=aliases,
        scratch_shapes=[pltpu.VMEM((rows, d), BF16)],
        compiler_params=_params(("arbitrary",)),
        name="ffn_sample",
    )(x, g, w_in, w_in, cw, cw, cb, cb, w_out, g_out, st_val, st_gate, *extra)


def _proj_kernel(*refs, tm, tps, padf, sample, bs, has_vlora):
    it = iter(refs)
    x_ref = next(it)
    sh_ref = next(it) if sample else None
    g_ref, mu_ref, wr_ref, wk_ref, wv_ref, w1_ref, a1_ref, g1_ref = (next(it) for _ in range(8))
    w2_ref, a2_ref, g2_ref, w0_ref, a0_ref, kk_ref, ka_ref, bd_ref = (next(it) for _ in range(8))
    if has_vlora:
        v1_ref, v2_ref, v0_ref, vf_ref = (next(it) for _ in range(4))
    r_o, ld_o, k_o, v_o, kn_o, a_o, g_o, hl_o = (next(it) for _ in range(8))
    xr_s, xk_s, xv_s, lw_s, la_s, lg_s = (next(it) for _ in range(6))
    lv_s = next(it) if has_vlora else None
    car_s = None if sample else next(it)

    i = pl.program_id(0)
    j = pl.program_id(1)

    @pl.when(j == 0)
    def _():
        x = x_ref[...]
        h = _rms(x, g_ref[...])
        if sample:
            prev = jnp.concatenate([sh_ref[...], h[:tm - bs]], axis=0)
            hl_o[...] = h[tm - bs:]
        else:
            ti = i % tps
            rloc = lax.broadcasted_iota(jnp.int32, (tm, 1), 0)
            h = jnp.where(ti * tm + rloc >= padf, h, 0.0)
            last = jnp.where(ti != 0, car_s[CONV_HALO - 1:CONV_HALO, :], 0.0)
            prev = jnp.where(rloc == 0, last, pltpu.roll(h, 1, 0))
            car_s[...] = h[tm - CONV_HALO:]
            hl_o[0] = h[tm - CONV_HALO:]
        xx = prev - h
        mu = mu_ref[...]
        mix = lambda q: (h + xx * mu[q:q + 1]).astype(BF16)
        xr_s[...] = mix(0)
        xk_s[...] = mix(2)
        xv = mix(3)
        xv_s[...] = xv
        lw_s[...] = jnp.tanh(_dot(mix(1), w1_ref[...])).astype(BF16)
        la_s[...] = _dot(mix(4), a1_ref[...]).astype(BF16)
        lg_s[...] = jax.nn.sigmoid(_dot(mix(5), g1_ref[...])).astype(BF16)
        if has_vlora:
            lv_s[...] = _dot(xv, v1_ref[...]).astype(BF16)

    nt = tm // bs if sample else 0

    def put(o_ref, val):
        if sample:
            for t in range(SAMPLE_TOK):
                o_ref[:, t, :] = val[t * bs:(t + 1) * bs] if t < nt else jnp.zeros((bs, val.shape[1]), F32)
        else:
            o_ref[...] = val

    r = _dot(xr_s[...], wr_ref[...])
    k = _dot(xk_s[...], wk_ref[...])
    v = _dot(xv_s[...], wv_ref[...])
    z = w0_ref[...] + _dot(lw_s[...], w2_ref[...])
    a = jax.nn.sigmoid(a0_ref[...] + _dot(la_s[...], a2_ref[...]))
    put(ld_o, -math.exp(-0.5) * jax.nn.sigmoid(z))
    if has_vlora:
        vf = jnp.concatenate([vf_ref[:, t, :] for t in range(nt)], axis=0) if sample else vf_ref[...]
        v = v + (vf - v) * jax.nn.sigmoid(v0_ref[...] + _dot(lv_s[...], v2_ref[...]))
    put(g_o, _dot(lg_s[...], g2_ref[...]))
    kk = k * kk_ref[...]
    sq_hi, sq_lo = _split(kk * kk)
    ss = _dot(sq_hi, bd_ref[...]) + _dot(sq_lo, bd_ref[...])
    put(kn_o, kk / jnp.maximum(jnp.sqrt(ss), 1e-12))
    put(r_o, r)
    put(k_o, k * (1.0 + (a - 1.0) * ka_ref[...]))
    put(v_o, v)
    put(a_o, a)


def _proj(x, sh, p, vl, vfirst, *, layer, tp, sample, bs):
    rows, d = x.shape
    tn = min(PROJ_TN_SAMPLE if sample else PROJ_TN, d)
    nj = d // tn
    p = dict(p, bd=p["bd"][:tn, :tn])
    if sample:
        tm, tps = rows, 1
    else:
        tm = _row_tile(tp, PROJ_TM_CAP)
        tps = tp // tm
    has_vlora = vl is not None
    kern = functools.partial(_proj_kernel, tm=tm, tps=tps, padf=FRONT_PAD, sample=sample, bs=bs,
                             has_vlora=has_vlora)
    full = lambda a: pl.BlockSpec(a.shape, lambda i, j: (0,) * a.ndim)
    colb = lambda a: pl.BlockSpec((a.shape[0], tn), lambda i, j: (0, j))
    if sample:
        rowb = pl.BlockSpec((bs, SAMPLE_TOK, tn), lambda i, j: (0, 0, j))
        row_shape = jax.ShapeDtypeStruct((bs, SAMPLE_TOK, d), F32)
    else:
        rowb = pl.BlockSpec((tm, tn), lambda i, j: (i, j))
        row_shape = jax.ShapeDtypeStruct((rows, d), F32)
    args, specs = [x], [pl.BlockSpec((tm, d), lambda i, j: (i, 0))]
    if sample:
        args.append(sh)
        specs.append(full(sh))
    for name in ("g", "mu"):
        args.append(p[name]); specs.append(full(p[name]))
    for name in ("wr", "wk", "wv"):
        args.append(p[name]); specs.append(pl.BlockSpec((None, d, tn), lambda i, j: (layer, 0, j)))
    for name in ("w1", "a1", "g1"):
        args.append(p[name]); specs.append(full(p[name]))
    for name in ("w2", "a2", "g2", "w0", "a0", "kk", "ka"):
        args.append(p[name]); specs.append(colb(p[name]))
    args.append(p["bd"]); specs.append(full(p["bd"]))
    if has_vlora:
        args += [vl["v1"], vl["v2"], vl["v0"], vfirst]
        specs += [full(vl["v1"]), colb(vl["v2"]), colb(vl["v0"]), rowb]
    out_shape = [row_shape] * 7
    out_specs = [rowb] * 7
    if sample:
        out_shape.append(jax.ShapeDtypeStruct((bs, d), F32))
        out_specs.append(pl.BlockSpec((bs, d), lambda i, j: (0, 0)))
    else:
        nb = rows // tp
        out_shape.append(jax.ShapeDtypeStruct((nb, CONV_HALO, d), F32))
        out_specs.append(pl.BlockSpec((1, CONV_HALO, d), lambda i, j: (i // tps, 0, 0)))
    lw, la, lg = p["w1"].shape[1], p["a1"].shape[1], p["g1"].shape[1]
    scratch = [pltpu.VMEM((tm, d), BF16)] * 3 + [pltpu.VMEM((tm, lw), BF16), pltpu.VMEM((tm, la), BF16),
                                                 pltpu.VMEM((tm, lg), BF16)]
    if has_vlora:
        scratch.append(pltpu.VMEM((tm, vl["v1"].shape[1]), BF16))
    if not sample:
        scratch.append(pltpu.VMEM((CONV_HALO, d), F32))
    return pl.pallas_call(
        kern,
        grid=(rows // tm, nj),
        in_specs=specs,
        out_specs=out_specs,
        out_shape=out_shape,
        scratch_shapes=scratch,
        compiler_params=_params(("arbitrary", "arbitrary")),
        name="rwkv_proj_sample" if sample else "rwkv_proj_prompt",
    )(*args)


def _mm(a, b):
    return _dot(a.astype(BF16), b.astype(BF16))


def _mm_nt(a, b):
    return lax.dot_general(a.astype(BF16), b.astype(BF16), (((1,), (1,)), ((), ())),
                           preferred_element_type=F32)


def _wkv_prompt_kernel(r_ref, ld_ref, k_ref, v_ref, kn_ref, a_ref, g_ref, s0_ref, lnw_ref, lnb_ref, rk_ref,
                       z_ref, so_ref, st_ref, *, L, nchunks, npairs):
    hs = HEAD_SIZE
    assert L == hs and npairs % 2 == 0
    ib = pl.program_id(2)
    lane = lax.broadcasted_iota(jnp.int32, (1, LANES), 1)
    head0 = lane < hs
    m0 = head0.astype(F32)
    m1 = 1.0 - m0
    m0w = jnp.concatenate([m0, m0], axis=1)
    m1w = 1.0 - m0w
    ri = lax.broadcasted_iota(jnp.int32, (LANES, LANES), 0)
    ci = lax.broadcasted_iota(jnp.int32, (LANES, LANES), 1)
    bdmask = ((ri < hs) == (ci < hs)).astype(F32)
    rl = lax.broadcasted_iota(jnp.int32, (L, L), 0)
    cl = lax.broadcasted_iota(jnp.int32, (L, L), 1)
    tril_incl = (cl <= rl).astype(BF16)
    prow = lax.broadcasted_iota(jnp.int32, (2 * L, 4 * L), 0)
    pcol = lax.broadcasted_iota(jnp.int32, (2 * L, 4 * L), 1) % L
    pmask = jnp.where(prow >= L, (pcol <= prow - L).astype(F32), (pcol < prow).astype(F32))
    qrow = lax.broadcasted_iota(jnp.int32, (4 * L, 4 * L), 0)
    qcol = lax.broadcasted_iota(jnp.int32, (4 * L, 4 * L), 1)
    bd4 = (qrow // L) == (qcol // L)
    eye4 = (lax.broadcasted_iota(jnp.int32, (L, 4 * L), 1) % L
            == lax.broadcasted_iota(jnp.int32, (L, 4 * L), 0)).astype(F32)
    nsq = int(math.log2(L))
    zeros_l = jnp.zeros((L, LANES), F32)
    zeros_b = jnp.zeros((4 * L, 4 * L), BF16)

    @pl.when(ib == 0)
    def _():
        zed = jnp.zeros((hs, hs), F32)
        for p in range(npairs):
            st_ref[p] = jnp.concatenate([jnp.concatenate([s0_ref[0, 2 * p], zed], axis=1),
                                         jnp.concatenate([zed, s0_ref[0, 2 * p + 1]], axis=1)], axis=0)

    def head_sum(y):
        s0 = jnp.sum(y * m0, axis=-1, keepdims=True)
        s1 = jnp.sum(y * m1, axis=-1, keepdims=True)
        return jnp.where(head0, s0, s1)

    split = lambda x: jnp.concatenate([x * m0, x * m1], axis=0)
    splitw = lambda x: jnp.concatenate([x * m0w, x * m1w], axis=0)

    def stream(rows, pairs, ld_all, cs_all):
        pr = range(len(pairs))
        lss = [slice(p * LANES, (p + 1) * LANES) for p in pairs]
        ld = [ld_all[:, ls] for ls in lss]
        cs = [cs_all[:, ls] for ls in lss]
        c_end = [x[L - 1:L, :] for x in cs]
        r = [r_ref[0, rows, ls] for ls in lss]
        k = [k_ref[0, rows, ls] for ls in lss]
        v = [v_ref[0, rows, ls] for ls in lss]
        kn = [kn_ref[0, rows, ls] for ls in lss]
        b = [kn[p] * a_ref[0, rows, lss[p]] for p in pr]
        w_inv = [jnp.exp(-x) for x in cs]
        rt = [r[p] * jnp.exp(cs[p]) for p in pr]
        kt = [k[p] * w_inv[p] for p in pr]
        bt = [b[p] * w_inv[p] for p in pr]
        at = [-kn[p] * jnp.exp(cs[p] - ld[p]) for p in pr]
        w_end = [jnp.exp(c_end[p] - cs[p]) for p in pr]
        kh = [k[p] * w_end[p] for p in pr]
        bh = [b[p] * w_end[p] for p in pr]
        yield
        pm = [_mm_nt(jnp.concatenate([at[p], rt[p]], axis=0),
                     jnp.concatenate([split(bt[p]), split(kt[p])], axis=0)) * pmask for p in pr]
        yield
        akv = [_mm(pm[p][:L, 2 * L:], split(v[p])) for p in pr]
        gr = range(len(pairs) // 2)
        pw = [jnp.concatenate([pm[2 * g][:L, :2 * L], pm[2 * g + 1][:L, :2 * L]], axis=1) for g in gr]
        tq = [eye4 for _ in gr]
        for i in range(nsq):
            blk = [jnp.where(bd4, jnp.concatenate([x.astype(BF16)] * 4, axis=0), zeros_b) for x in pw]
            if i < nsq - 1:
                both = [_dot(jnp.concatenate([pw[g], tq[g]], axis=0).astype(BF16), blk[g]) for g in gr]
                pw = [x[:L] for x in both]
                tq = [tq[g] + both[g][L:] for g in gr]
            else:
                tq = [tq[g] + _dot(tq[g].astype(BF16), blk[g]) for g in gr]
            yield
        tinv = [t[:, h * 2 * L:(h + 1) * 2 * L] for t in tq for h in range(2)]
        tz = [_mm(tinv[p], splitw(jnp.concatenate([at[p], akv[p]], axis=1))) for p in pr]
        yield
        zv = [jnp.concatenate([zeros_l, v[p]], axis=1) for p in pr]
        ry = [_mm(pm[p][L:], jnp.concatenate([splitw(tz[p]), splitw(zv[p])], axis=0)) for p in pr]
        nc = [_mm(jnp.concatenate([tz[p][:, :LANES], tz[p][:, LANES:], v[p]], axis=0).T,
                  jnp.concatenate([jnp.concatenate([bh[p], zeros_l], axis=1),
                                   jnp.concatenate([zeros_l, bh[p]], axis=1),
                                   jnp.concatenate([zeros_l, kh[p]], axis=1)], axis=0)) for p in pr]
        yield
        r2 = [rt[p] + ry[p][:, :LANES] for p in pr]
        state = [st_ref[q] for q in pairs]
        y = [_mm_nt(r2[p], state[p]) + ry[p][:, LANES:] for p in pr]
        for p, q in enumerate(pairs):
            st_ref[q] = (state[p] * jnp.exp(c_end[p]) + _mm(state[p], nc[p][:, :LANES] * bdmask)
                         + nc[p][:, LANES:] * bdmask)
        yield
        mean = [head_sum(x) * (1.0 / hs) for x in y]
        yc = [y[p] - mean[p] for p in pr]
        var = [head_sum(x * x) * (1.0 / hs) for x in yc]
        bonus = [head_sum(r[p] * k[p] * rk_ref[:, lss[p]]) * v[p] for p in pr]
        for p in pr:
            yn = yc[p] * lax.rsqrt(var[p] + GN_EPS) * lnw_ref[:, lss[p]] + lnb_ref[:, lss[p]]
            z_ref[0, rows, lss[p]] = ((yn + bonus[p]) * g_ref[0, rows, lss[p]]).astype(z_ref.dtype)

    halves = 1
    per = npairs // halves
    live = []
    for c in range(nchunks):
        rows = slice(c * L, (c + 1) * L)
        ld_all = ld_ref[0, rows, :]
        ld_hi = ld_all.astype(BF16)
        ld_r = ld_all - ld_hi.astype(F32)
        ld_mid = ld_r.astype(BF16)
        ld_lo = (ld_r - ld_mid.astype(F32)).astype(BF16)
        cs_all = _dot(tril_incl, ld_hi) + _dot(tril_incl, ld_mid) + _dot(tril_incl, ld_lo)
        live += [stream(rows, list(range(s * per, (s + 1) * per)), ld_all, cs_all) for s in range(halves)]
    lag = 0
    while live:
        for s in list(live[:lag + 1]):
            if next(s, "done") == "done":
                live.remove(s)
        lag += 1

    @pl.when(ib == pl.num_programs(2) - 1)
    def _():
        for p in range(npairs):
            state = st_ref[p]
            so_ref[0, 2 * p] = state[:hs, :hs]
            so_ref[0, 2 * p + 1] = state[hs:, hs:]


def _wkv_prompt(r, ld, k, v, kn, a, g, s0, lnw, lnb, rk):
    nb, tt, d = r.shape
    nh = d // HEAD_SIZE
    npairs = min(WKV_PAIRS, nh // 2)
    tr = _row_tile(tt, WKV_ROWS_CAP, CHUNK)
    lw = npairs * LANES
    kern = functools.partial(_wkv_prompt_kernel, L=CHUNK, nchunks=tr // CHUNK, npairs=npairs)
    seqb = pl.BlockSpec((1, tr, lw), lambda bi, p, i: (bi, i, p))
    vecb = pl.BlockSpec((1, lw), lambda bi, p, i: (0, p))
    stb = pl.BlockSpec((1, 2 * npairs, HEAD_SIZE, HEAD_SIZE), lambda bi, p, i: (bi, p, 0, 0))
    return pl.pallas_call(
        kern,
        grid=(nb, nh // (2 * npairs), tt // tr),
        in_specs=[seqb] * 7 + [stb, vecb, vecb, vecb],
        out_specs=[seqb, stb],
        out_shape=[jax.ShapeDtypeStruct((nb, tt, d), BF16),
                   jax.ShapeDtypeStruct((nb, nh, HEAD_SIZE, HEAD_SIZE), F32)],
        scratch_shapes=[pltpu.VMEM((npairs, LANES, LANES), F32)],
        compiler_params=_params(("arbitrary", "arbitrary", "arbitrary")),
        name="wkv_prompt",
    )(r, ld, k, v, kn, a, g, s0, lnw, lnb, rk)


def _wkv_sample_kernel(r_ref, ld_ref, k_ref, v_ref, kn_ref, a_ref, g_ref, s0_ref, lnw_ref, lnb_ref, rk_ref,
                       *rest, ngroups, npairs):
    z_ref, so_ref = rest[-2:]
    hs, L, T = HEAD_SIZE, CHUNK, SAMPLE_TOK
    ns = L // T
    lane = lax.broadcasted_iota(jnp.int32, (1, LANES), 1)
    head0 = lane < hs
    head0w = (lax.broadcasted_iota(jnp.int32, (1, 2 * LANES), 1) % LANES) < hs
    m0 = head0.astype(F32)
    m1 = 1.0 - m0
    ri = lax.broadcasted_iota(jnp.int32, (LANES, LANES), 0)
    ci = lax.broadcasted_iota(jnp.int32, (LANES, LANES), 1)
    bdmask = ((ri < hs) == (ci < hs)).astype(F32)
    eye_w = (ri == ci).astype(BF16)
    rl = lax.broadcasted_iota(jnp.int32, (L, L), 0)
    cl = lax.broadcasted_iota(jnp.int32, (L, L), 1)
    same = (rl // T) == (cl // T)
    cum_lhs = jnp.concatenate([jnp.where(same, (cl % T <= rl % T).astype(F32), 0.0),
                               same.astype(F32)], axis=0).astype(BF16)
    eye = (cl == rl).astype(F32)
    prow = lax.broadcasted_iota(jnp.int32, (4 * L, 2 * L), 0)
    pcol = lax.broadcasted_iota(jnp.int32, (4 * L, 2 * L), 1) % L
    psame = ((prow % L) // T) == (pcol // T)
    pcaus = jnp.where(prow >= 2 * L, (pcol % T <= prow % T).astype(F32), (pcol % T < prow % T).astype(F32))
    pmask = jnp.where(psame, pcaus, 0.0)
    kcols = (lax.broadcasted_iota(jnp.int32, (1, 2 * L), 1) >= L).astype(F32)
    nsteps = int(math.log2(T)) - 1
    zeros_l = jnp.zeros((L, LANES), F32)
    zeros_t = jnp.zeros((T, LANES), F32)
    zed = jnp.zeros((hs, hs), F32)

    def head_sum(y):
        s0 = jnp.sum(y * m0, axis=-1, keepdims=True)
        s1 = jnp.sum(y * m1, axis=-1, keepdims=True)
        return jnp.where(head0, s0, s1)

    def group_body(gi, carry):
        seqs = pl.ds(pl.multiple_of(gi * ns, ns), ns)
        pr = range(npairs)
        lss = [slice(p * LANES, (p + 1) * LANES) for p in pr]
        tile = lambda ref, ls: ref[seqs, :, ls].reshape(L, LANES)
        ld = [tile(ld_ref, ls) for ls in lss]
        ld_hi = [x.astype(BF16) for x in ld]
        ld_r = [x - h.astype(F32) for x, h in zip(ld, ld_hi)]
        ld_mid = [x.astype(BF16) for x in ld_r]
        ld_lo = [(x - m.astype(F32)).astype(BF16) for x, m in zip(ld_r, ld_mid)]
        cc = [_dot(cum_lhs, ld_hi[p]) + _dot(cum_lhs, ld_mid[p]) + _dot(cum_lhs, ld_lo[p]) for p in pr]
        cs = [x[:L] for x in cc]
        c_end = [x[L:] for x in cc]
        r = [tile(r_ref, ls) for ls in lss]
        k = [tile(k_ref, ls) for ls in lss]
        v = [tile(v_ref, ls) for ls in lss]
        kn = [tile(kn_ref, ls) for ls in lss]
        b = [kn[p] * tile(a_ref, lss[p]) for p in pr]
        w_inv = [jnp.exp(-x) for x in cs]
        rt = [r[p] * jnp.exp(cs[p]) for p in pr]
        kt = [k[p] * w_inv[p] for p in pr]
        bt = [b[p] * w_inv[p] for p in pr]
        at = [-kn[p] * jnp.exp(cs[p] - ld[p]) for p in pr]
        w_end = [jnp.exp(c_end[p] - cs[p]) for p in pr]
        w_tot = [jnp.exp(x) for x in c_end]
        kh = [k[p] * w_end[p] for p in pr]
        bh = [b[p] * w_end[p] for p in pr]
        pm = [_mm_nt(jnp.concatenate([at[p] * m0, at[p] * m1, rt[p] * m0, rt[p] * m1], axis=0),
                     jnp.concatenate([bt[p], kt[p]], axis=0)) * pmask for p in pr]
        a_bk = [x[:2 * L] for x in pm]
        r_bk = [x[2 * L:] for x in pm]
        akv = [_mm(a_bk[p] * kcols, jnp.concatenate([v[p], v[p]], axis=0)) for p in pr]
        akv = [jnp.where(head0, x[:L], x[L:]) for x in akv]
        pw = [a_bk[p][h * L:(h + 1) * L, :L] for p in pr for h in range(2)]
        tinv = [eye + x for x in pw]
        for _ in range(nsteps):
            pw = [_mm(x, x) for x in pw]
            tinv = [t + _mm(t, x) for t, x in zip(tinv, pw)]
        tz_rhs = [jnp.concatenate([at[p], akv[p]], axis=1) for p in pr]
        tz = [jnp.where(head0w, _mm(tinv[2 * p], tz_rhs[p]), _mm(tinv[2 * p + 1], tz_rhs[p])) for p in pr]
        ry = [_mm(r_bk[p], jnp.concatenate([tz[p], jnp.concatenate([zeros_l, v[p]], axis=1)], axis=0))
              for p in pr]
        ry = [jnp.where(head0w, x[:L], x[L:]) for x in ry]
        r2 = [rt[p] + ry[p][:, :LANES] for p in pr]
        ps = [(p, i) for p in pr for i in range(ns)]
        rs = [slice(i * T, (i + 1) * T) for i in range(ns)]
        state = [jnp.concatenate([jnp.concatenate([s0_ref[gi * ns + i, 2 * p], zed], axis=1),
                                  jnp.concatenate([zed, s0_ref[gi * ns + i, 2 * p + 1]], axis=1)], axis=0)
                 for p, i in ps]
        sb = [x.astype(BF16) for x in state]
        e_rhs = [jnp.concatenate([tz[p][rs[i]], jnp.concatenate([zeros_t, v[p][rs[i]]], axis=1)], axis=0)
                 for p, i in ps]
        e = [_mm_nt(jnp.concatenate([sb[q], eye_w], axis=1), e_rhs[q]) for q in range(len(ps))]
        upd = [_mm(e[q], jnp.concatenate([bh[p][rs[i]], kh[p][rs[i]]], axis=0)) * bdmask
               for q, (p, i) in enumerate(ps)]
        ys = [_mm_nt(r2[p][rs[i]], sb[q]) for q, (p, i) in enumerate(ps)]
        for q, (p, i) in enumerate(ps):
            new = state[q] * w_tot[p][i * T:i * T + 1] + upd[q]
            so_ref[gi * ns + i, 2 * p] = new[:hs, :hs]
            so_ref[gi * ns + i, 2 * p + 1] = new[hs:, hs:]
        y = [jnp.concatenate(ys[p * ns:(p + 1) * ns], axis=0) + ry[p][:, LANES:] for p in pr]
        mean = [head_sum(x) * (1.0 / hs) for x in y]
        yc = [y[p] - mean[p] for p in pr]
        var = [head_sum(x * x) * (1.0 / hs) for x in yc]
        bonus = [head_sum(r[p] * k[p] * rk_ref[:, lss[p]]) * v[p] for p in pr]
        for p in pr:
            yn = yc[p] * lax.rsqrt(var[p] + GN_EPS) * lnw_ref[:, lss[p]] + lnb_ref[:, lss[p]]
            z_ref[seqs, :, lss[p]] = ((yn + bonus[p]) * tile(g_ref, lss[p])).reshape(ns, T, LANES)
        return carry

    lax.fori_loop(0, ngroups, group_body, 0)


def _stacked(prev, shape, n_inputs, out_index):
    prev = jnp.zeros(shape, F32) if prev is None else prev
    return [prev], [pl.BlockSpec(memory_space=pl.ANY)], {n_inputs: out_index}


def _wkv_sample(r, ld, k, v, kn, a, g, s0, lnw, lnb, rk, *, layer, n_layers, states):
    nb, tt, d = r.shape
    nh = d // HEAD_SIZE
    npairs = min(WKV_SAMPLE_PAIRS, nh // 2)
    ns = CHUNK // SAMPLE_TOK
    bb = min(nb, 2 * ns)
    lw = npairs * LANES
    kern = functools.partial(_wkv_sample_kernel, ngroups=bb // ns, npairs=npairs)
    seqb = pl.BlockSpec((bb, tt, lw), lambda i, p: (i, 0, p))
    vecb = pl.BlockSpec((1, lw), lambda i, p: (0, p))
    sto = pl.BlockSpec((None, bb, 2 * npairs, HEAD_SIZE, HEAD_SIZE), lambda i, p: (layer, i, p, 0, 0))
    st_shape = (n_layers, nb, nh, HEAD_SIZE, HEAD_SIZE)
    assert s0.shape == st_shape
    extra, extra_specs, aliases = _stacked(states, st_shape, 11, 1)
    return pl.pallas_call(
        kern,
        grid=(nb // bb, nh // (2 * npairs)),
        in_specs=[seqb] * 7 + [sto, vecb, vecb, vecb] + extra_specs,
        out_specs=[seqb, sto],
        out_shape=[jax.ShapeDtypeStruct((nb, tt, d), F32), jax.ShapeDtypeStruct(st_shape, F32)],
        input_output_aliases=aliases,
        compiler_params=_params(("arbitrary", "arbitrary")),
        name="wkv_sample",
    )(r, ld, k, v, kn, a, g, s0, lnw, lnb, rk, *extra)


def _wo_kernel(x_ref, z_ref, w_ref, o_ref, *, nt):
    if nt:
        z = jnp.concatenate([z_ref[:, t, :] for t in range(nt)], axis=0)
    else:
        z = z_ref[...]
    o_ref[...] = x_ref[...] + _dot(z.astype(BF16), w_ref[...])


def _wo(x, z, w, *, layer, nt=0):
    rows, d = x.shape
    tm = rows if nt else _row_tile(rows, WO_TM_CAP)
    zspec = pl.BlockSpec(z.shape, lambda i: (0, 0, 0)) if nt else pl.BlockSpec((tm, d), lambda i: (i, 0))
    return pl.pallas_call(
        functools.partial(_wo_kernel, nt=nt),
        grid=(rows // tm,),
        in_specs=[pl.BlockSpec((tm, d), lambda i: (i, 0)), zspec,
                  pl.BlockSpec((None, d, d), lambda i: (layer, 0, 0))],
        out_specs=pl.BlockSpec((tm, d), lambda i: (i, 0)),
        out_shape=jax.ShapeDtypeStruct((rows, d), F32),
        compiler_params=_params(("arbitrary",)),
        name="rwkv_wo_sample" if nt else "rwkv_wo",
    )(x, z, w)


def _pad_cols(a, n):
    return jnp.pad(a, [(0, 0)] * (a.ndim - 1) + [(0, n - a.shape[-1])])


def _pad_rows(a, n):
    return jnp.pad(a, [(0, n - a.shape[0])] + [(0, 0)] * (a.ndim - 1))


def kernel(x_prompt, x_sample, state_pool, state_rwkv_shift, state_rwkv_wkv, state_ffn_conv, meta_tokens,
           norm_mix, norm_ffn, norm_out, pool_w, pool_scale, rwkv_mu, rwkv_wr, rwkv_wk, rwkv_wv, rwkv_wo,
           rwkv_w0, rwkv_w1, rwkv_w2, rwkv_a0, rwkv_a1, rwkv_a2, rwkv_v0, rwkv_v1, rwkv_v2, rwkv_g1, rwkv_g2,
           rwkv_kk, rwkv_ka, rwkv_rk, rwkv_lnw, rwkv_lnb, ffn_w_in, ffn_conv_w, ffn_conv_b, ffn_w_out):
    b, seq, d = x_prompt.shape
    bs, nt, _ = x_sample.shape
    depth = norm_mix.shape[0]
    f = ffn_w_out.shape[1]
    fp = _round_up(f, FFN_TF)
    nh = d // HEAD_SIZE
    tp = FRONT_PAD + N_META + seq
    assert tp % CHUNK == 0 and d % (2 * LANES) == 0 and CONV_WIDTH - 1 <= nt <= SAMPLE_TOK

    row = lambda a: a.reshape(1, -1)
    lora_in = lambda a: _pad_cols(a, _round_up(a.shape[-1], LANES)).astype(BF16)
    lora_out = lambda a: _pad_rows(a, _round_up(a.shape[0], LANES)).astype(BF16)
    halves = lambda a: jnp.concatenate([_pad_cols(a[..., :f], fp), _pad_cols(a[..., f:], fp)], axis=-1)
    w_in_all = _cast_halves(ffn_w_in, fp)
    w_out_all = _cast_rows(ffn_w_out, fp)
    ffn = [dict(g=row(norm_ffn[i]), cw=halves(ffn_conv_w[i]), cb=halves(ffn_conv_b[i][None]))
           for i in range(depth)]
    tn = min(PROJ_TN, d)
    bd = jnp.kron(jnp.eye(tn // HEAD_SIZE, dtype=F32), jnp.ones((HEAD_SIZE, HEAD_SIZE), F32)).astype(BF16)
    wr_all, wk_all, wv_all, wo_all = (_cast_rows(w, d) for w in (rwkv_wr, rwkv_wk, rwkv_wv, rwkv_wo))
    rw = []
    for j in range(depth // 2):
        rw.append(dict(
            g=row(norm_mix[2 * j + 1]), mu=rwkv_mu[j], wr=wr_all, wk=wk_all, wv=wv_all,
            w1=lora_in(rwkv_w1[j]), a1=lora_in(rwkv_a1[j]), g1=lora_in(rwkv_g1[j]),
            w2=lora_out(rwkv_w2[j]), a2=lora_out(rwkv_a2[j]), g2=lora_out(rwkv_g2[j]),
            w0=row(rwkv_w0[j]), a0=row(rwkv_a0[j]), kk=row(rwkv_kk[j]), ka=row(rwkv_ka[j]), bd=bd,
            lnw=row(rwkv_lnw[j]), lnb=row(rwkv_lnb[j]), rk=row(rwkv_rk[j])))
    vls = [None] + [dict(v1=lora_in(rwkv_v1[j]), v2=lora_out(rwkv_v2[j]), v0=row(rwkv_v0[j]))
                    for j in range(depth // 2 - 1)]
    pw = [pool_w[j].astype(BF16) for j in range((depth + 1) // 2)]
    g_out = row(norm_out)

    head = jnp.concatenate([jnp.zeros((FRONT_PAD, d), F32), meta_tokens], axis=0)
    xp = x_prompt
    xs = x_sample.transpose(1, 0, 2)
    zero_wkv = jnp.zeros((b, nh, HEAD_SIZE, HEAD_SIZE), F32)
    n_pool, n_rwkv = (depth + 1) // 2, depth // 2

    pool_p, shift_p, shift_s, wkv_p, conv_p = [], [], [], [], []
    pool_s = wkv_s = conv_s = None
    vfirst_p = vfirst_s = None
    for i in range(depth):
        j = i // 2
        if i % 2 == 0:
            xp, st = _pool_prompt(xp, row(norm_mix[i]), pw[j], row(pool_scale[j]), tp=tp,
                                  head=head if i == 0 else None)
            pool_p.append(st[:, POOL_HALO - POOL_STATE:])
            xs, pool_s = _pool_sample(xs, state_pool, row(norm_mix[i]), pw[j], row(pool_scale[j]), nt=nt, bs=bs,
                                      layer=j, n_layers=n_pool, states=pool_s)
        else:
            p = rw[j]
            r, ld, k, v, kn, a, g, hl = _proj(xp.reshape(b * tp, d), None, p, vls[j], vfirst_p,
                                              layer=j, tp=tp, sample=False, bs=0)
            if vfirst_p is None:
                vfirst_p = v
            shift_p.append(hl[:, CONV_HALO - 1])
            sq = lambda t: t.reshape(b, tp, d)
            z, s_new = _wkv_prompt(sq(r), sq(ld), sq(k), sq(v), sq(kn), sq(a), sq(g), zero_wkv,
                                   p["lnw"], p["lnb"], p["rk"])
            wkv_p.append(s_new)
            xp = _wo(xp.reshape(b * tp, d), z.reshape(b * tp, d), wo_all, layer=j).reshape(b, tp, d)
            r, ld, k, v, kn, a, g, hl = _proj(xs.reshape(nt * bs, d), state_rwkv_shift[j], p, vls[j], vfirst_s,
                                              layer=j, tp=0, sample=True, bs=bs)
            if vfirst_s is None:
                vfirst_s = v
            shift_s.append(hl)
            z, wkv_s = _wkv_sample(r, ld, k, v, kn, a, g, state_rwkv_wkv, p["lnw"], p["lnb"], p["rk"],
                                   layer=j, n_layers=n_rwkv, states=wkv_s)
            xs = _wo(xs.reshape(nt * bs, d), z, wo_all, layer=j, nt=nt).reshape(nt, bs, d)
        fi = ffn[i]
        last = i == depth - 1
        xo, stv, stg = _ffn_prompt(xp.reshape(b * tp, d), fi["g"], w_in_all, fi["cw"], fi["cb"], w_out_all,
                                   g_out, layer=i, tp=tp, final_norm=last)
        xp = xo.reshape(b, tp, d)
        stv, stg = (t.reshape(b, -1, CONV_HALO, fp)[:, -1] for t in (stv, stg))
        conv_p.append(jnp.concatenate([stv[:, CONV_HALO - 2:, :f], stg[:, CONV_HALO - 2:, :f]], axis=-1))
        xo, *conv_s = _ffn_sample(xs.reshape(nt * bs, d), fi["g"], w_in_all, fi["cw"], fi["cb"], w_out_all, g_out,
                                  state_ffn_conv, state_ffn_conv[i][:, :, f:],
                                  layer=i, n_layers=depth, states=conv_s, bs=bs, nt=nt, final_norm=last)
        xs = xo.reshape(nt, bs, d)

    y_prompt = xp[:, FRONT_PAD + N_META:]
    y_sample = xs.transpose(1, 0, 2)
    conv_s = jnp.concatenate([conv_s[0][..., :f], conv_s[1][..., :f]], axis=-1)
    return (y_prompt, y_sample, jnp.stack(pool_p), pool_s, jnp.stack(shift_p), jnp.stack(shift_s),
            jnp.stack(wkv_p), wkv_s, jnp.stack(conv_p), conv_s)
```

```python
import functools
import math

import jax
import jax.numpy as jnp
from jax import lax
from jax.experimental import pallas as pl
from jax.experimental.pallas import tpu as pltpu

F32 = jnp.float32
BF16 = jnp.bfloat16

HEAD_SIZE = 64
LANES = 128
N_META = 16
PAST_LEN = 16384
POOL_WINDOWS = (2, 4, 8, 16)
POOL_STATE = max(POOL_WINDOWS) - 1
POOL_HALO = 16
CONV_WIDTH = 3
CONV_HALO = 8
NORM_EPS = 1e-6
GN_EPS = 64e-5
CHUNK = 64
SAMPLE_TOK = 8
FRONT_PAD = CHUNK - N_META

FFN_TM_CAP = 704
PROJ_TM_CAP = 352
POOL_TM_CAP = 704
POOL_ASSEMBLE = 3
WO_TM_CAP = 704
WKV_ROWS_CAP = 192
WKV_PAIRS = 16
WKV_SAMPLE_PAIRS = 8
CAST_ROWS = 512
FFN_TF = 512
PROJ_TN = 512
PROJ_TN_SAMPLE = 256
VMEM_LIMIT = 56 * 1024 * 1024


def _row_tile(n, cap, mult=16):
    best = None
    for t in range(mult, min(n, cap) + 1, mult):
        if n % t == 0:
            best = t
    assert best is not None, (n, cap)
    return best


def _round_up(n, m):
    return (n + m - 1) // m * m


def _params(sem):
    return pltpu.CompilerParams(dimension_semantics=sem, vmem_limit_bytes=VMEM_LIMIT)


def _rms(x, g):
    return x * lax.rsqrt(jnp.mean(x * x, axis=-1, keepdims=True) + NORM_EPS) * g


def _dot(a, b):
    return jnp.dot(a, b, preferred_element_type=F32)


def _split(x):
    hi = x.astype(BF16)
    lo = (x - hi.astype(F32)).astype(BF16)
    return hi, lo


def _cast_kernel(x_ref, o_ref, *, valid_rows, tr):
    x = x_ref[...]
    if valid_rows is not None:
        row = pl.program_id(1) * tr + lax.broadcasted_iota(jnp.int32, (1, tr, 1), 1)
        x = jnp.where(row < valid_rows, x, 0.0)
    o_ref[...] = x.astype(o_ref.dtype)


def _cast_rows(w, out_rows):
    n, rows, cols = w.shape
    tr = CAST_ROWS
    kern = functools.partial(_cast_kernel, valid_rows=None if rows % tr == 0 else rows, tr=tr)
    return pl.pallas_call(
        kern,
        grid=(n, out_rows // tr),
        in_specs=[pl.BlockSpec((1, tr, cols), lambda l, i: (l, i, 0))],
        out_specs=pl.BlockSpec((1, tr, cols), lambda l, i: (l, i, 0)),
        out_shape=jax.ShapeDtypeStruct((n, out_rows, cols), BF16),
        compiler_params=_params(("arbitrary", "arbitrary")),
        name="cast_rows",
    )(w)


def _cast_halves_kernel(x_ref, o_ref, *, f, fp):
    x = x_ref[0]
    pad = jnp.zeros((x.shape[0], fp - f), o_ref.dtype)
    o_ref[0, :, :f] = x[:, :f].astype(o_ref.dtype)
    o_ref[0, :, fp:fp + f] = x[:, f:].astype(o_ref.dtype)
    if fp > f:
        o_ref[0, :, f:fp] = pad
        o_ref[0, :, fp + f:] = pad


def _cast_halves(w, fp):
    n, rows, f2 = w.shape
    f = f2 // 2
    tr = min(rows, CAST_ROWS // 2)
    return pl.pallas_call(
        functools.partial(_cast_halves_kernel, f=f, fp=fp),
        grid=(n, rows // tr),
        in_specs=[pl.BlockSpec((1, tr, f2), lambda l, i: (l, i, 0))],
        out_specs=pl.BlockSpec((1, tr, 2 * fp), lambda l, i: (l, i, 0)),
        out_shape=jax.ShapeDtypeStruct((n, rows, 2 * fp), BF16),
        compiler_params=_params(("arbitrary", "arbitrary")),
        name="cast_halves",
    )(w)


def _pool_prompt_kernel(*refs, tm, padf, cg, assemble):
    nx = assemble or 1
    x_refs, refs = refs[:nx], refs[nx:]
    if assemble:
        head_ref, refs = refs[0], refs[1:]
    g_ref, w_ref, sc_ref, o_ref, st_ref, carry_ref = refs
    i = pl.program_id(1)

    @pl.when(i == 0)
    def _():
        carry_ref[...] = jnp.zeros_like(carry_ref)

    if assemble:
        x = jnp.concatenate([jnp.where(i == 0, head_ref[...], x_refs[0][0])] + [r[0] for r in x_refs[1:]], axis=0)
    else:
        x = x_refs[0][0]
    row = i * tm + lax.broadcasted_iota(jnp.int32, (tm, 1), 0)
    h = jnp.where(row >= padf, _rms(x, g_ref[...]), 0.0)
    ext = jnp.concatenate([carry_ref[...], h], axis=0)
    pos = row - padf
    for g, w in enumerate(POOL_WINDOWS):
        sl = slice(g * cg, (g + 1) * cg)
        a = ext[:, sl]
        k = 1
        while k < w:
            n = a.shape[0]
            a = a[:n - k] + a[k:]
            k *= 2
        win = a[POOL_HALO + 1 - w: POOL_HALO + 1 - w + tm]
        cnt = jnp.clip(pos + 1, 1, w).astype(F32)
        d = win / cnt - h[:, sl]
        y = _dot(d.astype(BF16), w_ref[g])
        o_ref[0, :, sl] = x[:, sl] + y * sc_ref[:, sl]
    carry_ref[...] = h[tm - POOL_HALO:]
    st_ref[0] = h[tm - POOL_HALO:]


def _pool_prompt(x, g, w, sc, *, tp, head=None):
    b, _, d = x.shape
    cg = d // len(POOL_WINDOWS)
    if head is not None:
        hb = head.shape[0]
        assemble = POOL_ASSEMBLE if (tp // hb) % POOL_ASSEMBLE == 0 else 1
        tm = assemble * hb
        x_specs = [pl.BlockSpec((1, hb, d), lambda bi, i, k=k: (bi, jnp.maximum(assemble * i + k - 1, 0), 0))
                   for k in range(assemble)] + [pl.BlockSpec((hb, d), lambda bi, i: (0, 0))]
        x_args = (x,) * assemble + (head,)
    else:
        assemble = 0
        tm = _row_tile(tp, POOL_TM_CAP)
        x_specs = [pl.BlockSpec((1, tm, d), lambda bi, i: (bi, i, 0))]
        x_args = (x,)
    kern = functools.partial(_pool_prompt_kernel, tm=tm, padf=FRONT_PAD, cg=cg, assemble=assemble)
    return pl.pallas_call(
        kern,
        grid=(b, tp // tm),
        in_specs=x_specs + [
            pl.BlockSpec((1, d), lambda bi, i: (0, 0)),
            pl.BlockSpec((len(POOL_WINDOWS), cg, cg), lambda bi, i: (0, 0, 0)),
            pl.BlockSpec((1, d), lambda bi, i: (0, 0)),
        ],
        out_specs=[
            pl.BlockSpec((1, tm, d), lambda bi, i: (bi, i, 0)),
            pl.BlockSpec((1, POOL_HALO, d), lambda bi, i: (bi, 0, 0)),
        ],
        out_shape=[
            jax.ShapeDtypeStruct((b, tp, d), F32),
            jax.ShapeDtypeStruct((b, POOL_HALO, d), F32),
        ],
        scratch_shapes=[pltpu.VMEM((POOL_HALO, d), F32)],
        compiler_params=_params(("arbitrary", "arbitrary")),
        name="pool_prompt_assemble" if assemble else "pool_prompt",
    )(*x_args, g, w, sc)


def _pool_sample_kernel(x_ref, pre_ref, g_ref, w_ref, sc_ref, *rest, tb, d, cg, nt, start):
    o_ref, st_ref = rest[-2:]
    hs = [_rms(x_ref[t], g_ref[...]) for t in range(nt)]
    ext = [pre_ref[:, j, :] for j in range(POOL_STATE)] + hs
    for g, w in enumerate(POOL_WINDOWS):
        sl = slice(g * cg, (g + 1) * cg)
        ds = []
        for t in range(nt):
            e = POOL_STATE + t
            acc = ext[e][:, sl]
            for q in range(1, w):
                acc = acc + ext[e - q][:, sl]
            cnt = float(min(w, start + t + 1))
            ds.append(acc / cnt - hs[t][:, sl])
        y = _dot(jnp.concatenate(ds, axis=0).astype(BF16), w_ref[g])
        for t in range(nt):
            o_ref[t, :, sl] = x_ref[t][:, sl] + y[t * tb:(t + 1) * tb] * sc_ref[:, sl]
    for j in range(POOL_STATE):
        st_ref[:, j, :] = ext[nt + j]


def _pool_sample(x, pre, g, w, sc, *, nt, bs, layer, n_layers, states):
    d = x.shape[-1]
    tb = min(bs, 32)
    cg = d // len(POOL_WINDOWS)
    kern = functools.partial(_pool_sample_kernel, tb=tb, d=d, cg=cg, nt=nt, start=PAST_LEN)
    st_shape = (n_layers, bs, POOL_STATE, d)
    extra, extra_specs, aliases = _stacked(states, st_shape, 5, 1)
    return pl.pallas_call(
        kern,
        grid=(bs // tb,),
        in_specs=[
            pl.BlockSpec((nt, tb, d), lambda i: (0, i, 0)),
            pl.BlockSpec((None, tb, POOL_STATE, d), lambda i: (layer, i, 0, 0)),
            pl.BlockSpec((1, d), lambda i: (0, 0)),
            pl.BlockSpec((len(POOL_WINDOWS), cg, cg), lambda i: (0, 0, 0)),
            pl.BlockSpec((1, d), lambda i: (0, 0)),
        ] + extra_specs,
        out_specs=[
            pl.BlockSpec((nt, tb, d), lambda i: (0, i, 0)),
            pl.BlockSpec((None, tb, POOL_STATE, d), lambda i: (layer, i, 0, 0)),
        ],
        out_shape=[
            jax.ShapeDtypeStruct((nt, bs, d), F32),
            jax.ShapeDtypeStruct(st_shape, F32),
        ],
        input_output_aliases=aliases,
        compiler_params=_params(("arbitrary",)),
        name="pool_sample",
    )(x, pre, g, w, sc, *extra)


def _ffn_prompt_kernel(x_ref, g_ref, wv_ref, wg_ref, cwv_ref, cwg_ref, cbv_ref, cbg_ref, wo_ref, go_ref,
                       o_ref, stv_ref, stg_ref, hb_ref, carv_ref, carg_ref, *, tm, tps, padf, nj, final_norm):
    i = pl.program_id(0)
    j = pl.program_id(1)
    ti = i % tps

    @pl.when(j == 0)
    def _():
        x = x_ref[...]
        row = ti * tm + lax.broadcasted_iota(jnp.int32, (tm, 1), 0)
        h = jnp.where(row >= padf, _rms(x, g_ref[...]), 0.0)
        hb_ref[...] = h.astype(BF16)
        o_ref[...] = x

    hb = hb_ref[...]
    keep = ti != 0

    def branch(w_ref, cw_ref, cb_ref, car_ref, st_ref):
        u = _dot(hb, w_ref[...])
        prev = jnp.where(keep, car_ref[j], 0.0)
        ext = jnp.concatenate([prev, u], axis=0)
        cw = cw_ref[...]
        c = (cb_ref[...] + ext[CONV_HALO - 2:CONV_HALO - 2 + tm] * cw[0:1]
             + ext[CONV_HALO - 1:CONV_HALO - 1 + tm] * cw[1:2] + u * cw[2:3])
        car_ref[j] = u[tm - CONV_HALO:]
        st_ref[0] = u[tm - CONV_HALO:]
        return c

    cv = branch(wv_ref, cwv_ref, cbv_ref, carv_ref, stv_ref)
    cgate = branch(wg_ref, cwg_ref, cbg_ref, carg_ref, stg_ref)
    act = cgate * jax.nn.sigmoid(cgate) * cv
    o_ref[...] += _dot(act.astype(BF16), wo_ref[...])

    if final_norm:
        @pl.when(j == nj - 1)
        def _():
            o_ref[...] = _rms(o_ref[...], go_ref[...])


def _ffn_prompt(x, g, w_in, cw, cb, w_out, g_out, *, layer, tp, final_norm):
    rows, d = x.shape
    fp = w_out.shape[1]
    tf = FFN_TF
    nj = fp // tf
    tm = _row_tile(tp, FFN_TM_CAP)
    tps = tp // tm
    kern = functools.partial(_ffn_prompt_kernel, tm=tm, tps=tps, padf=FRONT_PAD, nj=nj, final_norm=final_norm)
    return pl.pallas_call(
        kern,
        grid=(rows // tm, nj),
        in_specs=[
            pl.BlockSpec((tm, d), lambda i, j: (i, 0)),
            pl.BlockSpec((1, d), lambda i, j: (0, 0)),
            pl.BlockSpec((None, d, tf), lambda i, j: (layer, 0, j)),
            pl.BlockSpec((None, d, tf), lambda i, j: (layer, 0, nj + j)),
            pl.BlockSpec((CONV_WIDTH, tf), lambda i, j: (0, j)),
            pl.BlockSpec((CONV_WIDTH, tf), lambda i, j: (0, nj + j)),
            pl.BlockSpec((1, tf), lambda i, j: (0, j)),
            pl.BlockSpec((1, tf), lambda i, j: (0, nj + j)),
            pl.BlockSpec((None, tf, d), lambda i, j: (layer, j, 0)),
            pl.BlockSpec((1, d), lambda i, j: (0, 0)),
        ],
        out_specs=[
            pl.BlockSpec((tm, d), lambda i, j: (i, 0)),
            pl.BlockSpec((1, CONV_HALO, tf), lambda i, j: (i, 0, j)),
            pl.BlockSpec((1, CONV_HALO, tf), lambda i, j: (i, 0, j)),
        ],
        out_shape=[
            jax.ShapeDtypeStruct((rows, d), F32),
            jax.ShapeDtypeStruct((rows // tm, CONV_HALO, fp), F32),
            jax.ShapeDtypeStruct((rows // tm, CONV_HALO, fp), F32),
        ],
        scratch_shapes=[
            pltpu.VMEM((tm, d), BF16),
            pltpu.VMEM((nj, CONV_HALO, tf), F32),
            pltpu.VMEM((nj, CONV_HALO, tf), F32),
        ],
        compiler_params=_params(("arbitrary", "arbitrary")),
        name="ffn_prompt",
    )(x, g, w_in, w_in, cw, cw, cb, cb, w_out, g_out)


def _ffn_sample_kernel(x_ref, g_ref, wv_ref, wg_ref, cwv_ref, cwg_ref, cbv_ref, cbg_ref, wo_ref, go_ref,
                       pv_ref, pg_ref, *rest, bs, nt, nj, tf, f, final_norm):
    o_ref, sv_ref, sg_ref, hb_ref = rest[-4:]
    j = pl.program_id(0)

    @pl.when(j == 0)
    def _():
        x = x_ref[...]
        hb_ref[...] = _rms(x, g_ref[...]).astype(BF16)
        o_ref[...] = x

    hb = hb_ref[...]
    valid = j * tf + lax.broadcasted_iota(jnp.int32, (1, tf), 1) < f

    def branch(w_ref, cw_ref, cb_ref, p_ref, s_ref):
        u = _dot(hb, w_ref[...])
        prev = [jnp.where(valid, p_ref[:, q, :], 0.0) for q in range(CONV_WIDTH - 1)]
        ext = prev + [u[t * bs:(t + 1) * bs] for t in range(nt)]
        cw = cw_ref[...]
        cs = [cb_ref[...] + ext[t] * cw[0:1] + ext[t + 1] * cw[1:2] + ext[t + 2] * cw[2:3] for t in range(nt)]
        for q in range(CONV_WIDTH - 1):
            s_ref[:, q, :] = ext[nt + q]
        return jnp.concatenate(cs, axis=0)

    cv = branch(wv_ref, cwv_ref, cbv_ref, pv_ref, sv_ref)
    cgate = branch(wg_ref, cwg_ref, cbg_ref, pg_ref, sg_ref)
    act = cgate * jax.nn.sigmoid(cgate) * cv
    o_ref[...] += _dot(act.astype(BF16), wo_ref[...])

    if final_norm:
        @pl.when(j == nj - 1)
        def _():
            o_ref[...] = _rms(o_ref[...], go_ref[...])


def _ffn_sample(x, g, w_in, cw, cb, w_out, g_out, st_val, st_gate, *, layer, n_layers, states, bs, nt,
                final_norm):
    rows, d = x.shape
    fp = w_out.shape[1]
    f = st_val.shape[-1]
    tf = FFN_TF
    nj = fp // tf
    kern = functools.partial(_ffn_sample_kernel, bs=bs, nt=nt, nj=nj, tf=tf, f=f, final_norm=final_norm)
    st_spec = pl.BlockSpec((bs, CONV_WIDTH - 1, tf), lambda j: (0, 0, j))
    so_spec = pl.BlockSpec((None, bs, CONV_WIDTH - 1, tf), lambda j: (layer, 0, 0, j))
    st_shape = (n_layers, bs, CONV_WIDTH - 1, fp)
    extra, extra_specs, aliases = [], [], {}
    for q, prev in enumerate(states or (None, None)):
        e, s, al = _stacked(prev, st_shape, 12 + q, 1 + q)
        extra, extra_specs, aliases = extra + e, extra_specs + s, {**aliases, **al}
    return pl.pallas_call(
        kern,
        grid=(nj,),
        in_specs=[
            pl.BlockSpec((rows, d), lambda j: (0, 0)),
            pl.BlockSpec((1, d), lambda j: (0, 0)),
            pl.BlockSpec((None, d, tf), lambda j: (layer, 0, j)),
            pl.BlockSpec((None, d, tf), lambda j: (layer, 0, nj + j)),
            pl.BlockSpec((CONV_WIDTH, tf), lambda j: (0, j)),
            pl.BlockSpec((CONV_WIDTH, tf), lambda j: (0, nj + j)),
            pl.BlockSpec((1, tf), lambda j: (0, j)),
            pl.BlockSpec((1, tf), lambda j: (0, nj + j)),
            pl.BlockSpec((None, tf, d), lambda j: (layer, j, 0)),
            pl.BlockSpec((1, d), lambda j: (0, 0)),
            st_spec, st_spec,
        ] + extra_specs,
        out_specs=[pl.BlockSpec((rows, d), lambda j: (0, 0)), so_spec, so_spec],
        out_shape=[jax.ShapeDtypeStruct((rows, d), F32)] + [jax.ShapeDtypeStruct(st_shape, F32)] * 2,
        input_output_aliases=aliases,
        scratch_shapes=[pltpu.VMEM((rows, d), BF16)],
        compiler_params=_params(("arbitrary",)),
        name="ffn_sample",
    )(x, g, w_in, w_in, cw, cw, cb, cb, w_out, g_out, st_val, st_gate, *extra)


def _proj_kernel(*refs, tm, tps, padf, sample, bs, has_vlora):
    it = iter(refs)
    x_ref = next(it)
    sh_ref = next(it) if sample else None
    g_ref, mu_ref, wr_ref, wk_ref, wv_ref, w1_ref, a1_ref, g1_ref = (next(it) for _ in range(8))
    w2_ref, a2_ref, g2_ref, w0_ref, a0_ref, kk_ref, ka_ref, bd_ref = (next(it) for _ in range(8))
    if has_vlora:
        v1_ref, v2_ref, v0_ref, vf_ref = (next(it) for _ in range(4))
    r_o, ld_o, k_o, v_o, kn_o, a_o, g_o, hl_o = (next(it) for _ in range(8))
    xr_s, xk_s, xv_s, lw_s, la_s, lg_s = (next(it) for _ in range(6))
    lv_s = next(it) if has_vlora else None
    car_s = None if sample else next(it)

    i = pl.program_id(0)
    j = pl.program_id(1)

    @pl.when(j == 0)
    def _():
        x = x_ref[...]
        h = _rms(x, g_ref[...])
        if sample:
            prev = jnp.concatenate([sh_ref[...], h[:tm - bs]], axis=0)
            hl_o[...] = h[tm - bs:]
        else:
            ti = i % tps
            rloc = lax.broadcasted_iota(jnp.int32, (tm, 1), 0)
            h = jnp.where(ti * tm + rloc >= padf, h, 0.0)
            last = jnp.where(ti != 0, car_s[CONV_HALO - 1:CONV_HALO, :], 0.0)
            prev = jnp.where(rloc == 0, last, pltpu.roll(h, 1, 0))
            car_s[...] = h[tm - CONV_HALO:]
            hl_o[0] = h[tm - CONV_HALO:]
        xx = prev - h
        mu = mu_ref[...]
        mix = lambda q: (h + xx * mu[q:q + 1]).astype(BF16)
        xr_s[...] = mix(0)
        xk_s[...] = mix(2)
        xv = mix(3)
        xv_s[...] = xv
        lw_s[...] = jnp.tanh(_dot(mix(1), w1_ref[...])).astype(BF16)
        la_s[...] = _dot(mix(4), a1_ref[...]).astype(BF16)
        lg_s[...] = jax.nn.sigmoid(_dot(mix(5), g1_ref[...])).astype(BF16)
        if has_vlora:
            lv_s[...] = _dot(xv, v1_ref[...]).astype(BF16)

    nt = tm // bs if sample else 0

    def put(o_ref, val):
        if sample:
            for t in range(SAMPLE_TOK):
                o_ref[:, t, :] = val[t * bs:(t + 1) * bs] if t < nt else jnp.zeros((bs, val.shape[1]), F32)
        else:
            o_ref[...] = val

    r = _dot(xr_s[...], wr_ref[...])
    k = _dot(xk_s[...], wk_ref[...])
    v = _dot(xv_s[...], wv_ref[...])
    z = w0_ref[...] + _dot(lw_s[...], w2_ref[...])
    a = jax.nn.sigmoid(a0_ref[...] + _dot(la_s[...], a2_ref[...]))
    put(ld_o, -math.exp(-0.5) * jax.nn.sigmoid(z))
    if has_vlora:
        vf = jnp.concatenate([vf_ref[:, t, :] for t in range(nt)], axis=0) if sample else vf_ref[...]
        v = v + (vf - v) * jax.nn.sigmoid(v0_ref[...] + _dot(lv_s[...], v2_ref[...]))
    put(g_o, _dot(lg_s[...], g2_ref[...]))
    kk = k * kk_ref[...]
    sq_hi, sq_lo = _split(kk * kk)
    ss = _dot(sq_hi, bd_ref[...]) + _dot(sq_lo, bd_ref[...])
    put(kn_o, kk / jnp.maximum(jnp.sqrt(ss), 1e-12))
    put(r_o, r)
    put(k_o, k * (1.0 + (a - 1.0) * ka_ref[...]))
    put(v_o, v)
    put(a_o, a)


def _proj(x, sh, p, vl, vfirst, *, layer, tp, sample, bs):
    rows, d = x.shape
    tn = min(PROJ_TN_SAMPLE if sample else PROJ_TN, d)
    nj = d // tn
    p = dict(p, bd=p["bd"][:tn, :tn])
    if sample:
        tm, tps = rows, 1
    else:
        tm = _row_tile(tp, PROJ_TM_CAP)
        tps = tp // tm
    has_vlora = vl is not None
    kern = functools.partial(_proj_kernel, tm=tm, tps=tps, padf=FRONT_PAD, sample=sample, bs=bs,
                             has_vlora=has_vlora)
    full = lambda a: pl.BlockSpec(a.shape, lambda i, j: (0,) * a.ndim)
    colb = lambda a: pl.BlockSpec((a.shape[0], tn), lambda i, j: (0, j))
    if sample:
        rowb = pl.BlockSpec((bs, SAMPLE_TOK, tn), lambda i, j: (0, 0, j))
        row_shape = jax.ShapeDtypeStruct((bs, SAMPLE_TOK, d), F32)
    else:
        rowb = pl.BlockSpec((tm, tn), lambda i, j: (i, j))
        row_shape = jax.ShapeDtypeStruct((rows, d), F32)
    args, specs = [x], [pl.BlockSpec((tm, d), lambda i, j: (i, 0))]
    if sample:
        args.append(sh)
        specs.append(full(sh))
    for name in ("g", "mu"):
        args.append(p[name]); specs.append(full(p[name]))
    for name in ("wr", "wk", "wv"):
        args.append(p[name]); specs.append(pl.BlockSpec((None, d, tn), lambda i, j: (layer, 0, j)))
    for name in ("w1", "a1", "g1"):
        args.append(p[name]); specs.append(full(p[name]))
    for name in ("w2", "a2", "g2", "w0", "a0", "kk", "ka"):
        args.append(p[name]); specs.append(colb(p[name]))
    args.append(p["bd"]); specs.append(full(p["bd"]))
    if has_vlora:
        args += [vl["v1"], vl["v2"], vl["v0"], vfirst]
        specs += [full(vl["v1"]), colb(vl["v2"]), colb(vl["v0"]), rowb]
    out_shape = [row_shape] * 7
    out_specs = [rowb] * 7
    if sample:
        out_shape.append(jax.ShapeDtypeStruct((bs, d), F32))
        out_specs.append(pl.BlockSpec((bs, d), lambda i, j: (0, 0)))
    else:
        nb = rows // tp
        out_shape.append(jax.ShapeDtypeStruct((nb, CONV_HALO, d), F32))
        out_specs.append(pl.BlockSpec((1, CONV_HALO, d), lambda i, j: (i // tps, 0, 0)))
    lw, la, lg = p["w1"].shape[1], p["a1"].shape[1], p["g1"].shape[1]
    scratch = [pltpu.VMEM((tm, d), BF16)] * 3 + [pltpu.VMEM((tm, lw), BF16), pltpu.VMEM((tm, la), BF16),
                                                 pltpu.VMEM((tm, lg), BF16)]
    if has_vlora:
        scratch.append(pltpu.VMEM((tm, vl["v1"].shape[1]), BF16))
    if not sample:
        scratch.append(pltpu.VMEM((CONV_HALO, d), F32))
    return pl.pallas_call(
        kern,
        grid=(rows // tm, nj),
        in_specs=specs,
        out_specs=out_specs,
        out_shape=out_shape,
        scratch_shapes=scratch,
        compiler_params=_params(("arbitrary", "arbitrary")),
        name="rwkv_proj_sample" if sample else "rwkv_proj_prompt",
    )(*args)


def _mm(a, b):
    return _dot(a.astype(BF16), b.astype(BF16))


def _mm_nt(a, b):
    return lax.dot_general(a.astype(BF16), b.astype(BF16), (((1,), (1,)), ((), ())),
                           preferred_element_type=F32)


def _wkv_prompt_kernel(r_ref, ld_ref, k_ref, v_ref, kn_ref, a_ref, g_ref, s0_ref, lnw_ref, lnb_ref, rk_ref,
                       z_ref, so_ref, st_ref, *, L, nchunks, npairs):
    hs = HEAD_SIZE
    assert L == hs and npairs % 2 == 0
    ib = pl.program_id(2)
    lane = lax.broadcasted_iota(jnp.int32, (1, LANES), 1)
    head0 = lane < hs
    m0 = head0.astype(F32)
    m1 = 1.0 - m0
    m0w = jnp.concatenate([m0, m0], axis=1)
    m1w = 1.0 - m0w
    ri = lax.broadcasted_iota(jnp.int32, (LANES, LANES), 0)
    ci = lax.broadcasted_iota(jnp.int32, (LANES, LANES), 1)
    bdmask = ((ri < hs) == (ci < hs)).astype(F32)
    rl = lax.broadcasted_iota(jnp.int32, (L, L), 0)
    cl = lax.broadcasted_iota(jnp.int32, (L, L), 1)
    tril_incl = (cl <= rl).astype(BF16)
    prow = lax.broadcasted_iota(jnp.int32, (2 * L, 4 * L), 0)
    pcol = lax.broadcasted_iota(jnp.int32, (2 * L, 4 * L), 1) % L
    pmask = jnp.where(prow >= L, (pcol <= prow - L).astype(F32), (pcol < prow).astype(F32))
    qrow = lax.broadcasted_iota(jnp.int32, (4 * L, 4 * L), 0)
    qcol = lax.broadcasted_iota(jnp.int32, (4 * L, 4 * L), 1)
    bd4 = (qrow // L) == (qcol // L)
    eye4 = (lax.broadcasted_iota(jnp.int32, (L, 4 * L), 1) % L
            == lax.broadcasted_iota(jnp.int32, (L, 4 * L), 0)).astype(F32)
    nsq = int(math.log2(L))
    zeros_l = jnp.zeros((L, LANES), F32)
    zeros_b = jnp.zeros((4 * L, 4 * L), BF16)

    @pl.when(ib == 0)
    def _():
        zed = jnp.zeros((hs, hs), F32)
        for p in range(npairs):
            st_ref[p] = jnp.concatenate([jnp.concatenate([s0_ref[0, 2 * p], zed], axis=1),
                                         jnp.concatenate([zed, s0_ref[0, 2 * p + 1]], axis=1)], axis=0)

    def head_sum(y):
        s0 = jnp.sum(y * m0, axis=-1, keepdims=True)
        s1 = jnp.sum(y * m1, axis=-1, keepdims=True)
        return jnp.where(head0, s0, s1)

    split = lambda x: jnp.concatenate([x * m0, x * m1], axis=0)
    splitw = lambda x: jnp.concatenate([x * m0w, x * m1w], axis=0)

    def stream(rows, pairs, ld_all, cs_all):
        pr = range(len(pairs))
        lss = [slice(p * LANES, (p + 1) * LANES) for p in pairs]
        ld = [ld_all[:, ls] for ls in lss]
        cs = [cs_all[:, ls] for ls in lss]
        c_end = [x[L - 1:L, :] for x in cs]
        r = [r_ref[0, rows, ls] for ls in lss]
        k = [k_ref[0, rows, ls] for ls in lss]
        v = [v_ref[0, rows, ls] for ls in lss]
        kn = [kn_ref[0, rows, ls] for ls in lss]
        b = [kn[p] * a_ref[0, rows, lss[p]] for p in pr]
        w_inv = [jnp.exp(-x) for x in cs]
        rt = [r[p] * jnp.exp(cs[p]) for p in pr]
        kt = [k[p] * w_inv[p] for p in pr]
        bt = [b[p] * w_inv[p] for p in pr]
        at = [-kn[p] * jnp.exp(cs[p] - ld[p]) for p in pr]
        w_end = [jnp.exp(c_end[p] - cs[p]) for p in pr]
        kh = [k[p] * w_end[p] for p in pr]
        bh = [b[p] * w_end[p] for p in pr]
        yield
        pm = [_mm_nt(jnp.concatenate([at[p], rt[p]], axis=0),
                     jnp.concatenate([split(bt[p]), split(kt[p])], axis=0)) * pmask for p in pr]
        yield
        akv = [_mm(pm[p][:L, 2 * L:], split(v[p])) for p in pr]
        gr = range(len(pairs) // 2)
        pw = [jnp.concatenate([pm[2 * g][:L, :2 * L], pm[2 * g + 1][:L, :2 * L]], axis=1) for g in gr]
        tq = [eye4 for _ in gr]
        for i in range(nsq):
            blk = [jnp.where(bd4, jnp.concatenate([x.astype(BF16)] * 4, axis=0), zeros_b) for x in pw]
            if i < nsq - 1:
                both = [_dot(jnp.concatenate([pw[g], tq[g]], axis=0).astype(BF16), blk[g]) for g in gr]
                pw = [x[:L] for x in both]
                tq = [tq[g] + both[g][L:] for g in gr]
            else:
                tq = [tq[g] + _dot(tq[g].astype(BF16), blk[g]) for g in gr]
            yield
        tinv = [t[:, h * 2 * L:(h + 1) * 2 * L] for t in tq for h in range(2)]
        tz = [_mm(tinv[p], splitw(jnp.concatenate([at[p], akv[p]], axis=1))) for p in pr]
        yield
        zv = [jnp.concatenate([zeros_l, v[p]], axis=1) for p in pr]
        ry = [_mm(pm[p][L:], jnp.concatenate([splitw(tz[p]), splitw(zv[p])], axis=0)) for p in pr]
        nc = [_mm(jnp.concatenate([tz[p][:, :LANES], tz[p][:, LANES:], v[p]], axis=0).T,
                  jnp.concatenate([jnp.concatenate([bh[p], zeros_l], axis=1),
                                   jnp.concatenate([zeros_l, bh[p]], axis=1),
                                   jnp.concatenate([zeros_l, kh[p]], axis=1)], axis=0)) for p in pr]
        yield
        r2 = [rt[p] + ry[p][:, :LANES] for p in pr]
        state = [st_ref[q] for q in pairs]
        y = [_mm_nt(r2[p], state[p]) + ry[p][:, LANES:] for p in pr]
        for p, q in enumerate(pairs):
            st_ref[q] = (state[p] * jnp.exp(c_end[p]) + _mm(state[p], nc[p][:, :LANES] * bdmask)
                         + nc[p][:, LANES:] * bdmask)
        yield
        mean = [head_sum(x) * (1.0 / hs) for x in y]
        yc = [y[p] - mean[p] for p in pr]
        var = [head_sum(x * x) * (1.0 / hs) for x in yc]
        bonus = [head_sum(r[p] * k[p] * rk_ref[:, lss[p]]) * v[p] for p in pr]
        for p in pr:
            yn = yc[p] * lax.rsqrt(var[p] + GN_EPS) * lnw_ref[:, lss[p]] + lnb_ref[:, lss[p]]
            z_ref[0, rows, lss[p]] = ((yn + bonus[p]) * g_ref[0, rows, lss[p]]).astype(z_ref.dtype)

    halves = 1
    per = npairs // halves
    live = []
    for c in range(nchunks):
        rows = slice(c * L, (c + 1) * L)
        ld_all = ld_ref[0, rows, :]
        ld_hi = ld_all.astype(BF16)
        ld_r = ld_all - ld_hi.astype(F32)
        ld_mid = ld_r.astype(BF16)
        ld_lo = (ld_r - ld_mid.astype(F32)).astype(BF16)
        cs_all = _dot(tril_incl, ld_hi) + _dot(tril_incl, ld_mid) + _dot(tril_incl, ld_lo)
        live += [stream(rows, list(range(s * per, (s + 1) * per)), ld_all, cs_all) for s in range(halves)]
    lag = 0
    while live:
        for s in list(live[:lag + 1]):
            if next(s, "done") == "done":
                live.remove(s)
        lag += 1

    @pl.when(ib == pl.num_programs(2) - 1)
    def _():
        for p in range(npairs):
            state = st_ref[p]
            so_ref[0, 2 * p] = state[:hs, :hs]
            so_ref[0, 2 * p + 1] = state[hs:, hs:]


def _wkv_prompt(r, ld, k, v, kn, a, g, s0, lnw, lnb, rk):
    nb, tt, d = r.shape
    nh = d // HEAD_SIZE
    npairs = min(WKV_PAIRS, nh // 2)
    tr = _row_tile(tt, WKV_ROWS_CAP, CHUNK)
    lw = npairs * LANES
    kern = functools.partial(_wkv_prompt_kernel, L=CHUNK, nchunks=tr // CHUNK, npairs=npairs)
    seqb = pl.BlockSpec((1, tr, lw), lambda bi, p, i: (bi, i, p))
    vecb = pl.BlockSpec((1, lw), lambda bi, p, i: (0, p))
    stb = pl.BlockSpec((1, 2 * npairs, HEAD_SIZE, HEAD_SIZE), lambda bi, p, i: (bi, p, 0, 0))
    return pl.pallas_call(
        kern,
        grid=(nb, nh // (2 * npairs), tt // tr),
        in_specs=[seqb] * 7 + [stb, vecb, vecb, vecb],
        out_specs=[seqb, stb],
        out_shape=[jax.ShapeDtypeStruct((nb, tt, d), BF16),
                   jax.ShapeDtypeStruct((nb, nh, HEAD_SIZE, HEAD_SIZE), F32)],
        scratch_shapes=[pltpu.VMEM((npairs, LANES, LANES), F32)],
        compiler_params=_params(("arbitrary", "arbitrary", "arbitrary")),
        name="wkv_prompt",
    )(r, ld, k, v, kn, a, g, s0, lnw, lnb, rk)


def _wkv_sample_kernel(r_ref, ld_ref, k_ref, v_ref, kn_ref, a_ref, g_ref, s0_ref, lnw_ref, lnb_ref, rk_ref,
                       *rest, ngroups, npairs):
    z_ref, so_ref = rest[-2:]
    hs, L, T = HEAD_SIZE, CHUNK, SAMPLE_TOK
    ns = L // T
    lane = lax.broadcasted_iota(jnp.int32, (1, LANES), 1)
    head0 = lane < hs
    head0w = (lax.broadcasted_iota(jnp.int32, (1, 2 * LANES), 1) % LANES) < hs
    m0 = head0.astype(F32)
    m1 = 1.0 - m0
    ri = lax.broadcasted_iota(jnp.int32, (LANES, LANES), 0)
    ci = lax.broadcasted_iota(jnp.int32, (LANES, LANES), 1)
    bdmask = ((ri < hs) == (ci < hs)).astype(F32)
    eye_w = (ri == ci).astype(BF16)
    rl = lax.broadcasted_iota(jnp.int32, (L, L), 0)
    cl = lax.broadcasted_iota(jnp.int32, (L, L), 1)
    same = (rl // T) == (cl // T)
    cum_lhs = jnp.concatenate([jnp.where(same, (cl % T <= rl % T).astype(F32), 0.0),
                               same.astype(F32)], axis=0).astype(BF16)
    eye = (cl == rl).astype(F32)
    prow = lax.broadcasted_iota(jnp.int32, (4 * L, 2 * L), 0)
    pcol = lax.broadcasted_iota(jnp.int32, (4 * L, 2 * L), 1) % L
    psame = ((prow % L) // T) == (pcol // T)
    pcaus = jnp.where(prow >= 2 * L, (pcol % T <= prow % T).astype(F32), (pcol % T < prow % T).astype(F32))
    pmask = jnp.where(psame, pcaus, 0.0)
    kcols = (lax.broadcasted_iota(jnp.int32, (1, 2 * L), 1) >= L).astype(F32)
    nsteps = int(math.log2(T)) - 1
    zeros_l = jnp.zeros((L, LANES), F32)
    zeros_t = jnp.zeros((T, LANES), F32)
    zed = jnp.zeros((hs, hs), F32)

    def head_sum(y):
        s0 = jnp.sum(y * m0, axis=-1, keepdims=True)
        s1 = jnp.sum(y * m1, axis=-1, keepdims=True)
        return jnp.where(head0, s0, s1)

    def group_body(gi, carry):
        seqs = pl.ds(pl.multiple_of(gi * ns, ns), ns)
        pr = range(npairs)
        lss = [slice(p * LANES, (p + 1) * LANES) for p in pr]
        tile = lambda ref, ls: ref[seqs, :, ls].reshape(L, LANES)
        ld = [tile(ld_ref, ls) for ls in lss]
        ld_hi = [x.astype(BF16) for x in ld]
        ld_r = [x - h.astype(F32) for x, h in zip(ld, ld_hi)]
        ld_mid = [x.astype(BF16) for x in ld_r]
        ld_lo = [(x - m.astype(F32)).astype(BF16) for x, m in zip(ld_r, ld_mid)]
        cc = [_dot(cum_lhs, ld_hi[p]) + _dot(cum_lhs, ld_mid[p]) + _dot(cum_lhs, ld_lo[p]) for p in pr]
        cs = [x[:L] for x in cc]
        c_end = [x[L:] for x in cc]
        r = [tile(r_ref, ls) for ls in lss]
        k = [tile(k_ref, ls) for ls in lss]
        v = [tile(v_ref, ls) for ls in lss]
        kn = [tile(kn_ref, ls) for ls in lss]
        b = [kn[p] * tile(a_ref, lss[p]) for p in pr]
        w_inv = [jnp.exp(-x) for x in cs]
        rt = [r[p] * jnp.exp(cs[p]) for p in pr]
        kt = [k[p] * w_inv[p] for p in pr]
        bt = [b[p] * w_inv[p] for p in pr]
        at = [-kn[p] * jnp.exp(cs[p] - ld[p]) for p in pr]
        w_end = [jnp.exp(c_end[p] - cs[p]) for p in pr]
        w_tot = [jnp.exp(x) for x in c_end]
        kh = [k[p] * w_end[p] for p in pr]
        bh = [b[p] * w_end[p] for p in pr]
        pm = [_mm_nt(jnp.concatenate([at[p] * m0, at[p] * m1, rt[p] * m0, rt[p] * m1], axis=0),
                     jnp.concatenate([bt[p], kt[p]], axis=0)) * pmask for p in pr]
        a_bk = [x[:2 * L] for x in pm]
        r_bk = [x[2 * L:] for x in pm]
        akv = [_mm(a_bk[p] * kcols, jnp.concatenate([v[p], v[p]], axis=0)) for p in pr]
        akv = [jnp.where(head0, x[:L], x[L:]) for x in akv]
        pw = [a_bk[p][h * L:(h + 1) * L, :L] for p in pr for h in range(2)]
        tinv = [eye + x for x in pw]
        for _ in range(nsteps):
            pw = [_mm(x, x) for x in pw]
            tinv = [t + _mm(t, x) for t, x in zip(tinv, pw)]
        tz_rhs = [jnp.concatenate([at[p], akv[p]], axis=1) for p in pr]
        tz = [jnp.where(head0w, _mm(tinv[2 * p], tz_rhs[p]), _mm(tinv[2 * p + 1], tz_rhs[p])) for p in pr]
        ry = [_mm(r_bk[p], jnp.concatenate([tz[p], jnp.concatenate([zeros_l, v[p]], axis=1)], axis=0))
              for p in pr]
        ry = [jnp.where(head0w, x[:L], x[L:]) for x in ry]
        r2 = [rt[p] + ry[p][:, :LANES] for p in pr]
        ps = [(p, i) for p in pr for i in range(ns)]
        rs = [slice(i * T, (i + 1) * T) for i in range(ns)]
        state = [jnp.concatenate([jnp.concatenate([s0_ref[gi * ns + i, 2 * p], zed], axis=1),
                                  jnp.concatenate([zed, s0_ref[gi * ns + i, 2 * p + 1]], axis=1)], axis=0)
                 for p, i in ps]
        sb = [x.astype(BF16) for x in state]
        e_rhs = [jnp.concatenate([tz[p][rs[i]], jnp.concatenate([zeros_t, v[p][rs[i]]], axis=1)], axis=0)
                 for p, i in ps]
        e = [_mm_nt(jnp.concatenate([sb[q], eye_w], axis=1), e_rhs[q]) for q in range(len(ps))]
        upd = [_mm(e[q], jnp.concatenate([bh[p][rs[i]], kh[p][rs[i]]], axis=0)) * bdmask
               for q, (p, i) in enumerate(ps)]
        ys = [_mm_nt(r2[p][rs[i]], sb[q]) for q, (p, i) in enumerate(ps)]
        for q, (p, i) in enumerate(ps):
            new = state[q] * w_tot[p][i * T:i * T + 1] + upd[q]
            so_ref[gi * ns + i, 2 * p] = new[:hs, :hs]
            so_ref[gi * ns + i, 2 * p + 1] = new[hs:, hs:]
        y = [jnp.concatenate(ys[p * ns:(p + 1) * ns], axis=0) + ry[p][:, LANES:] for p in pr]
        mean = [head_sum(x) * (1.0 / hs) for x in y]
        yc = [y[p] - mean[p] for p in pr]
        var = [head_sum(x * x) * (1.0 / hs) for x in yc]
        bonus = [head_sum(r[p] * k[p] * rk_ref[:, lss[p]]) * v[p] for p in pr]
        for p in pr:
            yn = yc[p] * lax.rsqrt(var[p] + GN_EPS) * lnw_ref[:, lss[p]] + lnb_ref[:, lss[p]]
            z_ref[seqs, :, lss[p]] = ((yn + bonus[p]) * tile(g_ref, lss[p])).reshape(ns, T, LANES)
        return carry

    lax.fori_loop(0, ngroups, group_body, 0)


def _stacked(prev, shape, n_inputs, out_index):
    prev = jnp.zeros(shape, F32) if prev is None else prev
    return [prev], [pl.BlockSpec(memory_space=pl.ANY)], {n_inputs: out_index}


def _wkv_sample(r, ld, k, v, kn, a, g, s0, lnw, lnb, rk, *, layer, n_layers, states):
    nb, tt, d = r.shape
    nh = d // HEAD_SIZE
    npairs = min(WKV_SAMPLE_PAIRS, nh // 2)
    ns = CHUNK // SAMPLE_TOK
    bb = min(nb, 2 * ns)
    lw = npairs * LANES
    kern = functools.partial(_wkv_sample_kernel, ngroups=bb // ns, npairs=npairs)
    seqb = pl.BlockSpec((bb, tt, lw), lambda i, p: (i, 0, p))
    vecb = pl.BlockSpec((1, lw), lambda i, p: (0, p))
    sto = pl.BlockSpec((None, bb, 2 * npairs, HEAD_SIZE, HEAD_SIZE), lambda i, p: (layer, i, p, 0, 0))
    st_shape = (n_layers, nb, nh, HEAD_SIZE, HEAD_SIZE)
    assert s0.shape == st_shape
    extra, extra_specs, aliases = _stacked(states, st_shape, 11, 1)
    return pl.pallas_call(
        kern,
        grid=(nb // bb, nh // (2 * npairs)),
        in_specs=[seqb] * 7 + [sto, vecb, vecb, vecb] + extra_specs,
        out_specs=[seqb, sto],
        out_shape=[jax.ShapeDtypeStruct((nb, tt, d), F32), jax.ShapeDtypeStruct(st_shape, F32)],
        input_output_aliases=aliases,
        compiler_params=_params(("arbitrary", "arbitrary")),
        name="wkv_sample",
    )(r, ld, k, v, kn, a, g, s0, lnw, lnb, rk, *extra)


def _wo_kernel(x_ref, z_ref, w_ref, o_ref, *, nt):
    if nt:
        z = jnp.concatenate([z_ref[:, t, :] for t in range(nt)], axis=0)
    else:
        z = z_ref[...]
    o_ref[...] = x_ref[...] + _dot(z.astype(BF16), w_ref[...])


def _wo(x, z, w, *, layer, nt=0):
    rows, d = x.shape
    tm = rows if nt else _row_tile(rows, WO_TM_CAP)
    zspec = pl.BlockSpec(z.shape, lambda i: (0, 0, 0)) if nt else pl.BlockSpec((tm, d), lambda i: (i, 0))
    return pl.pallas_call(
        functools.partial(_wo_kernel, nt=nt),
        grid=(rows // tm,),
        in_specs=[pl.BlockSpec((tm, d), lambda i: (i, 0)), zspec,
                  pl.BlockSpec((None, d, d), lambda i: (layer, 0, 0))],
        out_specs=pl.BlockSpec((tm, d), lambda i: (i, 0)),
        out_shape=jax.ShapeDtypeStruct((rows, d), F32),
        compiler_params=_params(("arbitrary",)),
        name="rwkv_wo_sample" if nt else "rwkv_wo",
    )(x, z, w)


def _pad_cols(a, n):
    return jnp.pad(a, [(0, 0)] * (a.ndim - 1) + [(0, n - a.shape[-1])])


def _pad_rows(a, n):
    return jnp.pad(a, [(0, n - a.shape[0])] + [(0, 0)] * (a.ndim - 1))


def kernel(x_prompt, x_sample, state_pool, state_rwkv_shift, state_rwkv_wkv, state_ffn_conv, meta_tokens,
           norm_mix, norm_ffn, norm_out, pool_w, pool_scale, rwkv_mu, rwkv_wr, rwkv_wk, rwkv_wv, rwkv_wo,
           rwkv_w0, rwkv_w1, rwkv_w2, rwkv_a0, rwkv_a1, rwkv_a2, rwkv_v0, rwkv_v1, rwkv_v2, rwkv_g1, rwkv_g2,
           rwkv_kk, rwkv_ka, rwkv_rk, rwkv_lnw, rwkv_lnb, ffn_w_in, ffn_conv_w, ffn_conv_b, ffn_w_out):
    b, seq, d = x_prompt.shape
    bs, nt, _ = x_sample.shape
    depth = norm_mix.shape[0]
    f = ffn_w_out.shape[1]
    fp = _round_up(f, FFN_TF)
    nh = d // HEAD_SIZE
    tp = FRONT_PAD + N_META + seq
    assert tp % CHUNK == 0 and d % (2 * LANES) == 0 and CONV_WIDTH - 1 <= nt <= SAMPLE_TOK

    row = lambda a: a.reshape(1, -1)
    lora_in = lambda a: _pad_cols(a, _round_up(a.shape[-1], LANES)).astype(BF16)
    lora_out = lambda a: _pad_rows(a, _round_up(a.shape[0], LANES)).astype(BF16)
    halves = lambda a: jnp.concatenate([_pad_cols(a[..., :f], fp), _pad_cols(a[..., f:], fp)], axis=-1)
    w_in_all = _cast_halves(ffn_w_in, fp)
    w_out_all = _cast_rows(ffn_w_out, fp)
    ffn = [dict(g=row(norm_ffn[i]), cw=halves(ffn_conv_w[i]), cb=halves(ffn_conv_b[i][None]))
           for i in range(depth)]
    tn = min(PROJ_TN, d)
    bd = jnp.kron(jnp.eye(tn // HEAD_SIZE, dtype=F32), jnp.ones((HEAD_SIZE, HEAD_SIZE), F32)).astype(BF16)
    wr_all, wk_all, wv_all, wo_all = (_cast_rows(w, d) for w in (rwkv_wr, rwkv_wk, rwkv_wv, rwkv_wo))
    rw = []
    for j in range(depth // 2):
        rw.append(dict(
            g=row(norm_mix[2 * j + 1]), mu=rwkv_mu[j], wr=wr_all, wk=wk_all, wv=wv_all,
            w1=lora_in(rwkv_w1[j]), a1=lora_in(rwkv_a1[j]), g1=lora_in(rwkv_g1[j]),
            w2=lora_out(rwkv_w2[j]), a2=lora_out(rwkv_a2[j]), g2=lora_out(rwkv_g2[j]),
            w0=row(rwkv_w0[j]), a0=row(rwkv_a0[j]), kk=row(rwkv_kk[j]), ka=row(rwkv_ka[j]), bd=bd,
            lnw=row(rwkv_lnw[j]), lnb=row(rwkv_lnb[j]), rk=row(rwkv_rk[j])))
    vls = [None] + [dict(v1=lora_in(rwkv_v1[j]), v2=lora_out(rwkv_v2[j]), v0=row(rwkv_v0[j]))
                    for j in range(depth // 2 - 1)]
    pw = [pool_w[j].astype(BF16) for j in range((depth + 1) // 2)]
    g_out = row(norm_out)

    head = jnp.concatenate([jnp.zeros((FRONT_PAD, d), F32), meta_tokens], axis=0)
    xp = x_prompt
    xs = x_sample.transpose(1, 0, 2)
    zero_wkv = jnp.zeros((b, nh, HEAD_SIZE, HEAD_SIZE), F32)
    n_pool, n_rwkv = (depth + 1) // 2, depth // 2

    pool_p, shift_p, shift_s, wkv_p, conv_p = [], [], [], [], []
    pool_s = wkv_s = conv_s = None
    vfirst_p = vfirst_s = None
    for i in range(depth):
        j = i // 2
        if i % 2 == 0:
            xp, st = _pool_prompt(xp, row(norm_mix[i]), pw[j], row(pool_scale[j]), tp=tp,
                                  head=head if i == 0 else None)
            pool_p.append(st[:, POOL_HALO - POOL_STATE:])
            xs, pool_s = _pool_sample(xs, state_pool, row(norm_mix[i]), pw[j], row(pool_scale[j]), nt=nt, bs=bs,
                                      layer=j, n_layers=n_pool, states=pool_s)
        else:
            p = rw[j]
            r, ld, k, v, kn, a, g, hl = _proj(xp.reshape(b * tp, d), None, p, vls[j], vfirst_p,
                                              layer=j, tp=tp, sample=False, bs=0)
            if vfirst_p is None:
                vfirst_p = v
            shift_p.append(hl[:, CONV_HALO - 1])
            sq = lambda t: t.reshape(b, tp, d)
            z, s_new = _wkv_prompt(sq(r), sq(ld), sq(k), sq(v), sq(kn), sq(a), sq(g), zero_wkv,
                                   p["lnw"], p["lnb"], p["rk"])
            wkv_p.append(s_new)
            xp = _wo(xp.reshape(b * tp, d), z.reshape(b * tp, d), wo_all, layer=j).reshape(b, tp, d)
            r, ld, k, v, kn, a, g, hl = _proj(xs.reshape(nt * bs, d), state_rwkv_shift[j], p, vls[j], vfirst_s,
                                              layer=j, tp=0, sample=True, bs=bs)
            if vfirst_s is None:
                vfirst_s = v
            shift_s.append(hl)
            z, wkv_s = _wkv_sample(r, ld, k, v, kn, a, g, state_rwkv_wkv, p["lnw"], p["lnb"], p["rk"],
                                   layer=j, n_layers=n_rwkv, states=wkv_s)
            xs = _wo(xs.reshape(nt * bs, d), z, wo_all, layer=j, nt=nt).reshape(nt, bs, d)
        fi = ffn[i]
        last = i == depth - 1
        xo, stv, stg = _ffn_prompt(xp.reshape(b * tp, d), fi["g"], w_in_all, fi["cw"], fi["cb"], w_out_all,
                                   g_out, layer=i, tp=tp, final_norm=last)
        xp = xo.reshape(b, tp, d)
        stv, stg = (t.reshape(b, -1, CONV_HALO, fp)[:, -1] for t in (stv, stg))
        conv_p.append(jnp.concatenate([stv[:, CONV_HALO - 2:, :f], stg[:, CONV_HALO - 2:, :f]], axis=-1))
        xo, *conv_s = _ffn_sample(xs.reshape(nt * bs, d), fi["g"], w_in_all, fi["cw"], fi["cb"], w_out_all, g_out,
                                  state_ffn_conv[i][:, :, :f], state_ffn_conv[i][:, :, f:],
                                  layer=i, n_layers=depth, states=conv_s, bs=bs, nt=nt, final_norm=last)
        xs = xo.reshape(nt, bs, d)

    y_prompt = xp[:, FRONT_PAD + N_META:]
    y_sample = xs.transpose(1, 0, 2)
    conv_s = jnp.concatenate([conv_s[0][..., :f], conv_s[1][..., :f]], axis=-1)
    return (y_prompt, y_sample, jnp.stack(pool_p), pool_s, jnp.stack(shift_p), jnp.stack(shift_s),
            jnp.stack(wkv_p), wkv_s, jnp.stack(conv_p), conv_s)
```

```python
import functools
import math

import jax
import jax.numpy as jnp
from jax import lax
from jax.experimental import pallas as pl
from jax.experimental.pallas import tpu as pltpu

F32 = jnp.float32
BF16 = jnp.bfloat16

HEAD_SIZE = 64
LANES = 128
N_META = 16
PAST_LEN = 16384
POOL_WINDOWS = (2, 4, 8, 16)
POOL_STATE = max(POOL_WINDOWS) - 1
POOL_HALO = 16
CONV_WIDTH = 3
CONV_HALO = 8
NORM_EPS = 1e-6
GN_EPS = 64e-5
CHUNK = 64
SAMPLE_TOK = 8
FRONT_PAD = CHUNK - N_META

FFN_TM_CAP = 704
PROJ_TM_CAP = 352
POOL_TM_CAP = 704
POOL_ASSEMBLE = 3
WO_TM_CAP = 704
WKV_ROWS_CAP = 192
WKV_PAIRS = 16
WKV_SAMPLE_PAIRS = 8
CAST_ROWS = 512
FFN_TF = 512
PROJ_TN = 512
PROJ_TN_SAMPLE = 256
VMEM_LIMIT = 56 * 1024 * 1024


def _row_tile(n, cap, mult=16):
    best = None
    for t in range(mult, min(n, cap) + 1, mult):
        if n % t == 0:
            best = t
    assert best is not None, (n, cap)
    return best


def _round_up(n, m):
    return (n + m - 1) // m * m


def _params(sem):
    return pltpu.CompilerParams(dimension_semantics=sem, vmem_limit_bytes=VMEM_LIMIT)


def _rms(x, g):
    return x * lax.rsqrt(jnp.mean(x * x, axis=-1, keepdims=True) + NORM_EPS) * g


def _dot(a, b):
    return jnp.dot(a, b, preferred_element_type=F32)


def _split(x):
    hi = x.astype(BF16)
    lo = (x - hi.astype(F32)).astype(BF16)
    return hi, lo


def _cast_kernel(x_ref, o_ref, *, valid_rows, tr):
    x = x_ref[...]
    if valid_rows is not None:
        row = pl.program_id(1) * tr + lax.broadcasted_iota(jnp.int32, (1, tr, 1), 1)
        x = jnp.where(row < valid_rows, x, 0.0)
    o_ref[...] = x.astype(o_ref.dtype)


def _cast_rows(w, out_rows):
    n, rows, cols = w.shape
    tr = CAST_ROWS
    kern = functools.partial(_cast_kernel, valid_rows=None if rows % tr == 0 else rows, tr=tr)
    return pl.pallas_call(
        kern,
        grid=(n, out_rows // tr),
        in_specs=[pl.BlockSpec((1, tr, cols), lambda l, i: (l, i, 0))],
        out_specs=pl.BlockSpec((1, tr, cols), lambda l, i: (l, i, 0)),
        out_shape=jax.ShapeDtypeStruct((n, out_rows, cols), BF16),
        compiler_params=_params(("arbitrary", "arbitrary")),
        name="cast_rows",
    )(w)


def _cast_halves_kernel(x_ref, o_ref, *, f, fp):
    x = x_ref[0]
    pad = jnp.zeros((x.shape[0], fp - f), o_ref.dtype)
    o_ref[0, :, :f] = x[:, :f].astype(o_ref.dtype)
    o_ref[0, :, fp:fp + f] = x[:, f:].astype(o_ref.dtype)
    if fp > f:
        o_ref[0, :, f:fp] = pad
        o_ref[0, :, fp + f:] = pad


def _cast_halves(w, fp):
    n, rows, f2 = w.shape
    f = f2 // 2
    tr = min(rows, CAST_ROWS // 2)
    return pl.pallas_call(
        functools.partial(_cast_halves_kernel, f=f, fp=fp),
        grid=(n, rows // tr),
        in_specs=[pl.BlockSpec((1, tr, f2), lambda l, i: (l, i, 0))],
        out_specs=pl.BlockSpec((1, tr, 2 * fp), lambda l, i: (l, i, 0)),
        out_shape=jax.ShapeDtypeStruct((n, rows, 2 * fp), BF16),
        compiler_params=_params(("arbitrary", "arbitrary")),
        name="cast_halves",
    )(w)


def _pool_prompt_kernel(*refs, tm, padf, cg, assemble):
    nx = assemble or 1
    x_refs, refs = refs[:nx], refs[nx:]
    if assemble:
        head_ref, refs = refs[0], refs[1:]
    g_ref, w_ref, sc_ref, o_ref, st_ref, carry_ref = refs
    i = pl.program_id(1)

    @pl.when(i == 0)
    def _():
        carry_ref[...] = jnp.zeros_like(carry_ref)

    if assemble:
        x = jnp.concatenate([jnp.where(i == 0, head_ref[...], x_refs[0][0])] + [r[0] for r in x_refs[1:]], axis=0)
    else:
        x = x_refs[0][0]
    row = i * tm + lax.broadcasted_iota(jnp.int32, (tm, 1), 0)
    h = jnp.where(row >= padf, _rms(x, g_ref[...]), 0.0)
    ext = jnp.concatenate([carry_ref[...], h], axis=0)
    pos = row - padf
    for g, w in enumerate(POOL_WINDOWS):
        sl = slice(g * cg, (g + 1) * cg)
        a = ext[:, sl]
        k = 1
        while k < w:
            n = a.shape[0]
            a = a[:n - k] + a[k:]
            k *= 2
        win = a[POOL_HALO + 1 - w: POOL_HALO + 1 - w + tm]
        cnt = jnp.clip(pos + 1, 1, w).astype(F32)
        d = win / cnt - h[:, sl]
        y = _dot(d.astype(BF16), w_ref[g])
        o_ref[0, :, sl] = x[:, sl] + y * sc_ref[:, sl]
    carry_ref[...] = h[tm - POOL_HALO:]
    st_ref[0] = h[tm - POOL_HALO:]


def _pool_prompt(x, g, w, sc, *, tp, head=None):
    b, _, d = x.shape
    cg = d // len(POOL_WINDOWS)
    if head is not None:
        hb = head.shape[0]
        assemble = POOL_ASSEMBLE if (tp // hb) % POOL_ASSEMBLE == 0 else 1
        tm = assemble * hb
        x_specs = [pl.BlockSpec((1, hb, d), lambda bi, i, k=k: (bi, jnp.maximum(assemble * i + k - 1, 0), 0))
                   for k in range(assemble)] + [pl.BlockSpec((hb, d), lambda bi, i: (0, 0))]
        x_args = (x,) * assemble + (head,)
    else:
        assemble = 0
        tm = _row_tile(tp, POOL_TM_CAP)
        x_specs = [pl.BlockSpec((1, tm, d), lambda bi, i: (bi, i, 0))]
        x_args = (x,)
    kern = functools.partial(_pool_prompt_kernel, tm=tm, padf=FRONT_PAD, cg=cg, assemble=assemble)
    return pl.pallas_call(
        kern,
        grid=(b, tp // tm),
        in_specs=x_specs + [
            pl.BlockSpec((1, d), lambda bi, i: (0, 0)),
            pl.BlockSpec((len(POOL_WINDOWS), cg, cg), lambda bi, i: (0, 0, 0)),
            pl.BlockSpec((1, d), lambda bi, i: (0, 0)),
        ],
        out_specs=[
            pl.BlockSpec((1, tm, d), lambda bi, i: (bi, i, 0)),
            pl.BlockSpec((1, POOL_HALO, d), lambda bi, i: (bi, 0, 0)),
        ],
        out_shape=[
            jax.ShapeDtypeStruct((b, tp, d), F32),
            jax.ShapeDtypeStruct((b, POOL_HALO, d), F32),
        ],
        scratch_shapes=[pltpu.VMEM((POOL_HALO, d), F32)],
        compiler_params=_params(("arbitrary", "arbitrary")),
        name="pool_prompt_assemble" if assemble else "pool_prompt",
    )(*x_args, g, w, sc)


def _pool_sample_kernel(x_ref, pre_ref, g_ref, w_ref, sc_ref, *rest, tb, d, cg, nt, start):
    o_ref, st_ref = rest[-2:]
    hs = [_rms(x_ref[t], g_ref[...]) for t in range(nt)]
    ext = [pre_ref[:, j, :] for j in range(POOL_STATE)] + hs
    for g, w in enumerate(POOL_WINDOWS):
        sl = slice(g * cg, (g + 1) * cg)
        ds = []
        for t in range(nt):
            e = POOL_STATE + t
            acc = ext[e][:, sl]
            for q in range(1, w):
                acc = acc + ext[e - q][:, sl]
            cnt = float(min(w, start + t + 1))
            ds.append(acc / cnt - hs[t][:, sl])
        y = _dot(jnp.concatenate(ds, axis=0).astype(BF16), w_ref[g])
        for t in range(nt):
            o_ref[t, :, sl] = x_ref[t][:, sl] + y[t * tb:(t + 1) * tb] * sc_ref[:, sl]
    for j in range(POOL_STATE):
        st_ref[:, j, :] = ext[nt + j]


def _pool_sample(x, pre, g, w, sc, *, nt, bs, layer, n_layers, states):
    d = x.shape[-1]
    tb = min(bs, 32)
    cg = d // len(POOL_WINDOWS)
    kern = functools.partial(_pool_sample_kernel, tb=tb, d=d, cg=cg, nt=nt, start=PAST_LEN)
    st_shape = (n_layers, bs, POOL_STATE, d)
    extra, extra_specs, aliases = _stacked(states, st_shape, 5, 1)
    return pl.pallas_call(
        kern,
        grid=(bs // tb,),
        in_specs=[
            pl.BlockSpec((nt, tb, d), lambda i: (0, i, 0)),
            pl.BlockSpec((tb, POOL_STATE, d), lambda i: (i, 0, 0)),
            pl.BlockSpec((1, d), lambda i: (0, 0)),
            pl.BlockSpec((len(POOL_WINDOWS), cg, cg), lambda i: (0, 0, 0)),
            pl.BlockSpec((1, d), lambda i: (0, 0)),
        ] + extra_specs,
        out_specs=[
            pl.BlockSpec((nt, tb, d), lambda i: (0, i, 0)),
            pl.BlockSpec((None, tb, POOL_STATE, d), lambda i: (layer, i, 0, 0)),
        ],
        out_shape=[
            jax.ShapeDtypeStruct((nt, bs, d), F32),
            jax.ShapeDtypeStruct(st_shape, F32),
        ],
        input_output_aliases=aliases,
        compiler_params=_params(("arbitrary",)),
        name="pool_sample",
    )(x, pre, g, w, sc, *extra)


def _ffn_prompt_kernel(x_ref, g_ref, wv_ref, wg_ref, cwv_ref, cwg_ref, cbv_ref, cbg_ref, wo_ref, go_ref,
                       o_ref, stv_ref, stg_ref, hb_ref, carv_ref, carg_ref, *, tm, tps, padf, nj, final_norm):
    i = pl.program_id(0)
    j = pl.program_id(1)
    ti = i % tps

    @pl.when(j == 0)
    def _():
        x = x_ref[...]
        row = ti * tm + lax.broadcasted_iota(jnp.int32, (tm, 1), 0)
        h = jnp.where(row >= padf, _rms(x, g_ref[...]), 0.0)
        hb_ref[...] = h.astype(BF16)
        o_ref[...] = x

    hb = hb_ref[...]
    keep = ti != 0

    def branch(w_ref, cw_ref, cb_ref, car_ref, st_ref):
        u = _dot(hb, w_ref[...])
        prev = jnp.where(keep, car_ref[j], 0.0)
        ext = jnp.concatenate([prev, u], axis=0)
        cw = cw_ref[...]
        c = (cb_ref[...] + ext[CONV_HALO - 2:CONV_HALO - 2 + tm] * cw[0:1]
             + ext[CONV_HALO - 1:CONV_HALO - 1 + tm] * cw[1:2] + u * cw[2:3])
        car_ref[j] = u[tm - CONV_HALO:]
        st_ref[0] = u[tm - CONV_HALO:]
        return c

    cv = branch(wv_ref, cwv_ref, cbv_ref, carv_ref, stv_ref)
    cgate = branch(wg_ref, cwg_ref, cbg_ref, carg_ref, stg_ref)
    act = cgate * jax.nn.sigmoid(cgate) * cv
    o_ref[...] += _dot(act.astype(BF16), wo_ref[...])

    if final_norm:
        @pl.when(j == nj - 1)
        def _():
            o_ref[...] = _rms(o_ref[...], go_ref[...])


def _ffn_prompt(x, g, w_in, cw, cb, w_out, g_out, *, layer, tp, final_norm):
    rows, d = x.shape
    fp = w_out.shape[1]
    tf = FFN_TF
    nj = fp // tf
    tm = _row_tile(tp, FFN_TM_CAP)
    tps = tp // tm
    kern = functools.partial(_ffn_prompt_kernel, tm=tm, tps=tps, padf=FRONT_PAD, nj=nj, final_norm=final_norm)
    return pl.pallas_call(
        kern,
        grid=(rows // tm, nj),
        in_specs=[
            pl.BlockSpec((tm, d), lambda i, j: (i, 0)),
            pl.BlockSpec((1, d), lambda i, j: (0, 0)),
            pl.BlockSpec((None, d, tf), lambda i, j: (layer, 0, j)),
            pl.BlockSpec((None, d, tf), lambda i, j: (layer, 0, nj + j)),
            pl.BlockSpec((CONV_WIDTH, tf), lambda i, j: (0, j)),
            pl.BlockSpec((CONV_WIDTH, tf), lambda i, j: (0, nj + j)),
            pl.BlockSpec((1, tf), lambda i, j: (0, j)),
            pl.BlockSpec((1, tf), lambda i, j: (0, nj + j)),
            pl.BlockSpec((None, tf, d), lambda i, j: (layer, j, 0)),
            pl.BlockSpec((1, d), lambda i, j: (0, 0)),
        ],
        out_specs=[
            pl.BlockSpec((tm, d), lambda i, j: (i, 0)),
            pl.BlockSpec((1, CONV_HALO, tf), lambda i, j: (i, 0, j)),
            pl.BlockSpec((1, CONV_HALO, tf), lambda i, j: (i, 0, j)),
        ],
        out_shape=[
            jax.ShapeDtypeStruct((rows, d), F32),
            jax.ShapeDtypeStruct((rows // tm, CONV_HALO, fp), F32),
            jax.ShapeDtypeStruct((rows // tm, CONV_HALO, fp), F32),
        ],
        scratch_shapes=[
            pltpu.VMEM((tm, d), BF16),
            pltpu.VMEM((nj, CONV_HALO, tf), F32),
            pltpu.VMEM((nj, CONV_HALO, tf), F32),
        ],
        compiler_params=_params(("arbitrary", "arbitrary")),
        name="ffn_prompt",
    )(x, g, w_in, w_in, cw, cw, cb, cb, w_out, g_out)


def _ffn_sample_kernel(x_ref, g_ref, wv_ref, wg_ref, cwv_ref, cwg_ref, cbv_ref, cbg_ref, wo_ref, go_ref,
                       pv_ref, pg_ref, *rest, bs, nt, nj, tf, f, final_norm):
    o_ref, sv_ref, sg_ref, hb_ref = rest[-4:]
    j = pl.program_id(0)

    @pl.when(j == 0)
    def _():
        x = x_ref[...]
        hb_ref[...] = _rms(x, g_ref[...]).astype(BF16)
        o_ref[...] = x

    hb = hb_ref[...]
    valid = j * tf + lax.broadcasted_iota(jnp.int32, (1, tf), 1) < f

    def branch(w_ref, cw_ref, cb_ref, p_ref, s_ref):
        u = _dot(hb, w_ref[...])
        prev = [jnp.where(valid, p_ref[:, q, :], 0.0) for q in range(CONV_WIDTH - 1)]
        ext = prev + [u[t * bs:(t + 1) * bs] for t in range(nt)]
        cw = cw_ref[...]
        cs = [cb_ref[...] + ext[t] * cw[0:1] + ext[t + 1] * cw[1:2] + ext[t + 2] * cw[2:3] for t in range(nt)]
        for q in range(CONV_WIDTH - 1):
            s_ref[:, q, :] = ext[nt + q]
        return jnp.concatenate(cs, axis=0)

    cv = branch(wv_ref, cwv_ref, cbv_ref, pv_ref, sv_ref)
    cgate = branch(wg_ref, cwg_ref, cbg_ref, pg_ref, sg_ref)
    act = cgate * jax.nn.sigmoid(cgate) * cv
    o_ref[...] += _dot(act.astype(BF16), wo_ref[...])

    if final_norm:
        @pl.when(j == nj - 1)
        def _():
            o_ref[...] = _rms(o_ref[...], go_ref[...])


def _ffn_sample(x, g, w_in, cw, cb, w_out, g_out, st_val, st_gate, *, layer, n_layers, states, bs, nt,
                final_norm):
    rows, d = x.shape
    fp = w_out.shape[1]
    f = st_val.shape[-1]
    tf = FFN_TF
    nj = fp // tf
    kern = functools.partial(_ffn_sample_kernel, bs=bs, nt=nt, nj=nj, tf=tf, f=f, final_norm=final_norm)
    st_spec = pl.BlockSpec((bs, CONV_WIDTH - 1, tf), lambda j: (0, 0, j))
    so_spec = pl.BlockSpec((None, bs, CONV_WIDTH - 1, tf), lambda j: (layer, 0, 0, j))
    st_shape = (n_layers, bs, CONV_WIDTH - 1, fp)
    extra, extra_specs, aliases = [], [], {}
    for q, prev in enumerate(states or (None, None)):
        e, s, al = _stacked(prev, st_shape, 12 + q, 1 + q)
        extra, extra_specs, aliases = extra + e, extra_specs + s, {**aliases, **al}
    return pl.pallas_call(
        kern,
        grid=(nj,),
        in_specs=[
            pl.BlockSpec((rows, d), lambda j: (0, 0)),
            pl.BlockSpec((1, d), lambda j: (0, 0)),
            pl.BlockSpec((None, d, tf), lambda j: (layer, 0, j)),
            pl.BlockSpec((None, d, tf), lambda j: (layer, 0, nj + j)),
            pl.BlockSpec((CONV_WIDTH, tf), lambda j: (0, j)),
            pl.BlockSpec((CONV_WIDTH, tf), lambda j: (0, nj + j)),
            pl.BlockSpec((1, tf), lambda j: (0, j)),
            pl.BlockSpec((1, tf), lambda j: (0, nj + j)),
            pl.BlockSpec((None, tf, d), lambda j: (layer, j, 0)),
            pl.BlockSpec((1, d), lambda j: (0, 0)),
            st_spec, st_spec,
        ] + extra_specs,
        out_specs=[pl.BlockSpec((rows, d), lambda j: (0, 0)), so_spec, so_spec],
        out_shape=[jax.ShapeDtypeStruct((rows, d), F32)] + [jax.ShapeDtypeStruct(st_shape, F32)] * 2,
        input_output_aliases=aliases,
        scratch_shapes=[pltpu.VMEM((rows, d), BF16)],
        compiler_params=_params(("arbitrary",)),
        name="ffn_sample",
    )(x, g, w_in, w_in, cw, cw, cb, cb, w_out, g_out, st_val, st_gate, *extra)


def _proj_kernel(*refs, tm, tps, padf, sample, bs, has_vlora):
    it = iter(refs)
    x_ref = next(it)
    sh_ref = next(it) if sample else None
    g_ref, mu_ref, wr_ref, wk_ref, wv_ref, w1_ref, a1_ref, g1_ref = (next(it) for _ in range(8))
    w2_ref, a2_ref, g2_ref, w0_ref, a0_ref, kk_ref, ka_ref, bd_ref = (next(it) for _ in range(8))
    if has_vlora:
        v1_ref, v2_ref, v0_ref, vf_ref = (next(it) for _ in range(4))
    r_o, ld_o, k_o, v_o, kn_o, a_o, g_o, hl_o = (next(it) for _ in range(8))
    xr_s, xk_s, xv_s, lw_s, la_s, lg_s = (next(it) for _ in range(6))
    lv_s = next(it) if has_vlora else None
    car_s = None if sample else next(it)

    i = pl.program_id(0)
    j = pl.program_id(1)

    @pl.when(j == 0)
    def _():
        x = x_ref[...]
        h = _rms(x, g_ref[...])
        if sample:
            prev = jnp.concatenate([sh_ref[...], h[:tm - bs]], axis=0)
            hl_o[...] = h[tm - bs:]
        else:
            ti = i % tps
            rloc = lax.broadcasted_iota(jnp.int32, (tm, 1), 0)
            h = jnp.where(ti * tm + rloc >= padf, h, 0.0)
            last = jnp.where(ti != 0, car_s[CONV_HALO - 1:CONV_HALO, :], 0.0)
            prev = jnp.where(rloc == 0, last, pltpu.roll(h, 1, 0))
            car_s[...] = h[tm - CONV_HALO:]
            hl_o[0] = h[tm - CONV_HALO:]
        xx = prev - h
        mu = mu_ref[...]
        mix = lambda q: (h + xx * mu[q:q + 1]).astype(BF16)
        xr_s[...] = mix(0)
        xk_s[...] = mix(2)
        xv = mix(3)
        xv_s[...] = xv
        lw_s[...] = jnp.tanh(_dot(mix(1), w1_ref[...])).astype(BF16)
        la_s[...] = _dot(mix(4), a1_ref[...]).astype(BF16)
        lg_s[...] = jax.nn.sigmoid(_dot(mix(5), g1_ref[...])).astype(BF16)
        if has_vlora:
            lv_s[...] = _dot(xv, v1_ref[...]).astype(BF16)

    nt = tm // bs if sample else 0

    def put(o_ref, val):
        if sample:
            for t in range(SAMPLE_TOK):
                o_ref[:, t, :] = val[t * bs:(t + 1) * bs] if t < nt else jnp.zeros((bs, val.shape[1]), F32)
        else:
            o_ref[...] = val

    r = _dot(xr_s[...], wr_ref[...])
    k = _dot(xk_s[...], wk_ref[...])
    v = _dot(xv_s[...], wv_ref[...])
    z = w0_ref[...] + _dot(lw_s[...], w2_ref[...])
    a = jax.nn.sigmoid(a0_ref[...] + _dot(la_s[...], a2_ref[...]))
    put(ld_o, -math.exp(-0.5) * jax.nn.sigmoid(z))
    if has_vlora:
        vf = jnp.concatenate([vf_ref[:, t, :] for t in range(nt)], axis=0) if sample else vf_ref[...]
        v = v + (vf - v) * jax.nn.sigmoid(v0_ref[...] + _dot(lv_s[...], v2_ref[...]))
    put(g_o, _dot(lg_s[...], g2_ref[...]))
    kk = k * kk_ref[...]
    sq_hi, sq_lo = _split(kk * kk)
    ss = _dot(sq_hi, bd_ref[...]) + _dot(sq_lo, bd_ref[...])
    put(kn_o, kk / jnp.maximum(jnp.sqrt(ss), 1e-12))
    put(r_o, r)
    put(k_o, k * (1.0 + (a - 1.0) * ka_ref[...]))
    put(v_o, v)
    put(a_o, a)


def _proj(x, sh, p, vl, vfirst, *, layer, tp, sample, bs):
    rows, d = x.shape
    tn = min(PROJ_TN_SAMPLE if sample else PROJ_TN, d)
    nj = d // tn
    p = dict(p, bd=p["bd"][:tn, :tn])
    if sample:
        tm, tps = rows, 1
    else:
        tm = _row_tile(tp, PROJ_TM_CAP)
        tps = tp // tm
    has_vlora = vl is not None
    kern = functools.partial(_proj_kernel, tm=tm, tps=tps, padf=FRONT_PAD, sample=sample, bs=bs,
                             has_vlora=has_vlora)
    full = lambda a: pl.BlockSpec(a.shape, lambda i, j: (0,) * a.ndim)
    colb = lambda a: pl.BlockSpec((a.shape[0], tn), lambda i, j: (0, j))
    if sample:
        rowb = pl.BlockSpec((bs, SAMPLE_TOK, tn), lambda i, j: (0, 0, j))
        row_shape = jax.ShapeDtypeStruct((bs, SAMPLE_TOK, d), F32)
    else:
        rowb = pl.BlockSpec((tm, tn), lambda i, j: (i, j))
        row_shape = jax.ShapeDtypeStruct((rows, d), F32)
    args, specs = [x], [pl.BlockSpec((tm, d), lambda i, j: (i, 0))]
    if sample:
        args.append(sh)
        specs.append(full(sh))
    for name in ("g", "mu"):
        args.append(p[name]); specs.append(full(p[name]))
    for name in ("wr", "wk", "wv"):
        args.append(p[name]); specs.append(pl.BlockSpec((None, d, tn), lambda i, j: (layer, 0, j)))
    for name in ("w1", "a1", "g1"):
        args.append(p[name]); specs.append(full(p[name]))
    for name in ("w2", "a2", "g2", "w0", "a0", "kk", "ka"):
        args.append(p[name]); specs.append(colb(p[name]))
    args.append(p["bd"]); specs.append(full(p["bd"]))
    if has_vlora:
        args += [vl["v1"], vl["v2"], vl["v0"], vfirst]
        specs += [full(vl["v1"]), colb(vl["v2"]), colb(vl["v0"]), rowb]
    out_shape = [row_shape] * 7
    out_specs = [rowb] * 7
    if sample:
        out_shape.append(jax.ShapeDtypeStruct((bs, d), F32))
        out_specs.append(pl.BlockSpec((bs, d), lambda i, j: (0, 0)))
    else:
        nb = rows // tp
        out_shape.append(jax.ShapeDtypeStruct((nb, CONV_HALO, d), F32))
        out_specs.append(pl.BlockSpec((1, CONV_HALO, d), lambda i, j: (i // tps, 0, 0)))
    lw, la, lg = p["w1"].shape[1], p["a1"].shape[1], p["g1"].shape[1]
    scratch = [pltpu.VMEM((tm, d), BF16)] * 3 + [pltpu.VMEM((tm, lw), BF16), pltpu.VMEM((tm, la), BF16),
                                                 pltpu.VMEM((tm, lg), BF16)]
    if has_vlora:
        scratch.append(pltpu.VMEM((tm, vl["v1"].shape[1]), BF16))
    if not sample:
        scratch.append(pltpu.VMEM((CONV_HALO, d), F32))
    return pl.pallas_call(
        kern,
        grid=(rows // tm, nj),
        in_specs=specs,
        out_specs=out_specs,
        out_shape=out_shape,
        scratch_shapes=scratch,
        compiler_params=_params(("arbitrary", "arbitrary")),
        name="rwkv_proj_sample" if sample else "rwkv_proj_prompt",
    )(*args)


def _mm(a, b):
    return _dot(a.astype(BF16), b.astype(BF16))


def _mm_nt(a, b):
    return lax.dot_general(a.astype(BF16), b.astype(BF16), (((1,), (1,)), ((), ())),
                           preferred_element_type=F32)


def _wkv_prompt_kernel(r_ref, ld_ref, k_ref, v_ref, kn_ref, a_ref, g_ref, s0_ref, lnw_ref, lnb_ref, rk_ref,
                       z_ref, so_ref, st_ref, *, L, nchunks, npairs):
    hs = HEAD_SIZE
    assert L == hs and npairs % 2 == 0
    ib = pl.program_id(2)
    lane = lax.broadcasted_iota(jnp.int32, (1, LANES), 1)
    head0 = lane < hs
    m0 = head0.astype(F32)
    m1 = 1.0 - m0
    m0w = jnp.concatenate([m0, m0], axis=1)
    m1w = 1.0 - m0w
    ri = lax.broadcasted_iota(jnp.int32, (LANES, LANES), 0)
    ci = lax.broadcasted_iota(jnp.int32, (LANES, LANES), 1)
    bdmask = ((ri < hs) == (ci < hs)).astype(F32)
    rl = lax.broadcasted_iota(jnp.int32, (L, L), 0)
    cl = lax.broadcasted_iota(jnp.int32, (L, L), 1)
    tril_incl = (cl <= rl).astype(BF16)
    prow = lax.broadcasted_iota(jnp.int32, (2 * L, 4 * L), 0)
    pcol = lax.broadcasted_iota(jnp.int32, (2 * L, 4 * L), 1) % L
    pmask = jnp.where(prow >= L, (pcol <= prow - L).astype(F32), (pcol < prow).astype(F32))
    qrow = lax.broadcasted_iota(jnp.int32, (4 * L, 4 * L), 0)
    qcol = lax.broadcasted_iota(jnp.int32, (4 * L, 4 * L), 1)
    bd4 = (qrow // L) == (qcol // L)
    eye4 = (lax.broadcasted_iota(jnp.int32, (L, 4 * L), 1) % L
            == lax.broadcasted_iota(jnp.int32, (L, 4 * L), 0)).astype(F32)
    nsq = int(math.log2(L))
    zeros_l = jnp.zeros((L, LANES), F32)
    zeros_b = jnp.zeros((4 * L, 4 * L), BF16)

    @pl.when(ib == 0)
    def _():
        zed = jnp.zeros((hs, hs), F32)
        for p in range(npairs):
            st_ref[p] = jnp.concatenate([jnp.concatenate([s0_ref[0, 2 * p], zed], axis=1),
                                         jnp.concatenate([zed, s0_ref[0, 2 * p + 1]], axis=1)], axis=0)

    def head_sum(y):
        s0 = jnp.sum(y * m0, axis=-1, keepdims=True)
        s1 = jnp.sum(y * m1, axis=-1, keepdims=True)
        return jnp.where(head0, s0, s1)

    split = lambda x: jnp.concatenate([x * m0, x * m1], axis=0)
    splitw = lambda x: jnp.concatenate([x * m0w, x * m1w], axis=0)

    def stream(rows, pairs, ld_all, cs_all):
        pr = range(len(pairs))
        lss = [slice(p * LANES, (p + 1) * LANES) for p in pairs]
        ld = [ld_all[:, ls] for ls in lss]
        cs = [cs_all[:, ls] for ls in lss]
        c_end = [x[L - 1:L, :] for x in cs]
        r = [r_ref[0, rows, ls] for ls in lss]
        k = [k_ref[0, rows, ls] for ls in lss]
        v = [v_ref[0, rows, ls] for ls in lss]
        kn = [kn_ref[0, rows, ls] for ls in lss]
        b = [kn[p] * a_ref[0, rows, lss[p]] for p in pr]
        w_inv = [jnp.exp(-x) for x in cs]
        rt = [r[p] * jnp.exp(cs[p]) for p in pr]
        kt = [k[p] * w_inv[p] for p in pr]
        bt = [b[p] * w_inv[p] for p in pr]
        at = [-kn[p] * jnp.exp(cs[p] - ld[p]) for p in pr]
        w_end = [jnp.exp(c_end[p] - cs[p]) for p in pr]
        kh = [k[p] * w_end[p] for p in pr]
        bh = [b[p] * w_end[p] for p in pr]
        yield
        pm = [_mm_nt(jnp.concatenate([at[p], rt[p]], axis=0),
                     jnp.concatenate([split(bt[p]), split(kt[p])], axis=0)) * pmask for p in pr]
        yield
        akv = [_mm(pm[p][:L, 2 * L:], split(v[p])) for p in pr]
        gr = range(len(pairs) // 2)
        pw = [jnp.concatenate([pm[2 * g][:L, :2 * L], pm[2 * g + 1][:L, :2 * L]], axis=1) for g in gr]
        tq = [eye4 for _ in gr]
        for i in range(nsq):
            blk = [jnp.where(bd4, jnp.concatenate([x.astype(BF16)] * 4, axis=0), zeros_b) for x in pw]
            if i < nsq - 1:
                both = [_dot(jnp.concatenate([pw[g], tq[g]], axis=0).astype(BF16), blk[g]) for g in gr]
                pw = [x[:L] for x in both]
                tq = [tq[g] + both[g][L:] for g in gr]
            else:
                tq = [tq[g] + _dot(tq[g].astype(BF16), blk[g]) for g in gr]
            yield
        tinv = [t[:, h * 2 * L:(h + 1) * 2 * L] for t in tq for h in range(2)]
        tz = [_mm(tinv[p], splitw(jnp.concatenate([at[p], akv[p]], axis=1))) for p in pr]
        yield
        zv = [jnp.concatenate([zeros_l, v[p]], axis=1) for p in pr]
        ry = [_mm(pm[p][L:], jnp.concatenate([splitw(tz[p]), splitw(zv[p])], axis=0)) for p in pr]
        nc = [_mm(jnp.concatenate([tz[p][:, :LANES], tz[p][:, LANES:], v[p]], axis=0).T,
                  jnp.concatenate([jnp.concatenate([bh[p], zeros_l], axis=1),
                                   jnp.concatenate([zeros_l, bh[p]], axis=1),
                                   jnp.concatenate([zeros_l, kh[p]], axis=1)], axis=0)) for p in pr]
        yield
        r2 = [rt[p] + ry[p][:, :LANES] for p in pr]
        state = [st_ref[q] for q in pairs]
        y = [_mm_nt(r2[p], state[p]) + ry[p][:, LANES:] for p in pr]
        for p, q in enumerate(pairs):
            st_ref[q] = (state[p] * jnp.exp(c_end[p]) + _mm(state[p], nc[p][:, :LANES] * bdmask)
                         + nc[p][:, LANES:] * bdmask)
        yield
        mean = [head_sum(x) * (1.0 / hs) for x in y]
        yc = [y[p] - mean[p] for p in pr]
        var = [head_sum(x * x) * (1.0 / hs) for x in yc]
        bonus = [head_sum(r[p] * k[p] * rk_ref[:, lss[p]]) * v[p] for p in pr]
        for p in pr:
            yn = yc[p] * lax.rsqrt(var[p] + GN_EPS) * lnw_ref[:, lss[p]] + lnb_ref[:, lss[p]]
            z_ref[0, rows, lss[p]] = ((yn + bonus[p]) * g_ref[0, rows, lss[p]]).astype(z_ref.dtype)

    halves = 1
    per = npairs // halves
    live = []
    for c in range(nchunks):
        rows = slice(c * L, (c + 1) * L)
        ld_all = ld_ref[0, rows, :]
        ld_hi = ld_all.astype(BF16)
        ld_r = ld_all - ld_hi.astype(F32)
        ld_mid = ld_r.astype(BF16)
        ld_lo = (ld_r - ld_mid.astype(F32)).astype(BF16)
        cs_all = _dot(tril_incl, ld_hi) + _dot(tril_incl, ld_mid) + _dot(tril_incl, ld_lo)
        live += [stream(rows, list(range(s * per, (s + 1) * per)), ld_all, cs_all) for s in range(halves)]
    lag = 0
    while live:
        for s in list(live[:lag + 1]):
            if next(s, "done") == "done":
                live.remove(s)
        lag += 1

    @pl.when(ib == pl.num_programs(2) - 1)
    def _():
        for p in range(npairs):
            state = st_ref[p]
            so_ref[0, 2 * p] = state[:hs, :hs]
            so_ref[0, 2 * p + 1] = state[hs:, hs:]


def _wkv_prompt(r, ld, k, v, kn, a, g, s0, lnw, lnb, rk):
    nb, tt, d = r.shape
    nh = d // HEAD_SIZE
    npairs = min(WKV_PAIRS, nh // 2)
    tr = _row_tile(tt, WKV_ROWS_CAP, CHUNK)
    lw = npairs * LANES
    kern = functools.partial(_wkv_prompt_kernel, L=CHUNK, nchunks=tr // CHUNK, npairs=npairs)
    seqb = pl.BlockSpec((1, tr, lw), lambda bi, p, i: (bi, i, p))
    vecb = pl.BlockSpec((1, lw), lambda bi, p, i: (0, p))
    stb = pl.BlockSpec((1, 2 * npairs, HEAD_SIZE, HEAD_SIZE), lambda bi, p, i: (bi, p, 0, 0))
    return pl.pallas_call(
        kern,
        grid=(nb, nh // (2 * npairs), tt // tr),
        in_specs=[seqb] * 7 + [stb, vecb, vecb, vecb],
        out_specs=[seqb, stb],
        out_shape=[jax.ShapeDtypeStruct((nb, tt, d), BF16),
                   jax.ShapeDtypeStruct((nb, nh, HEAD_SIZE, HEAD_SIZE), F32)],
        scratch_shapes=[pltpu.VMEM((npairs, LANES, LANES), F32)],
        compiler_params=_params(("arbitrary", "arbitrary", "arbitrary")),
        name="wkv_prompt",
    )(r, ld, k, v, kn, a, g, s0, lnw, lnb, rk)


def _wkv_sample_kernel(r_ref, ld_ref, k_ref, v_ref, kn_ref, a_ref, g_ref, s0_ref, lnw_ref, lnb_ref, rk_ref,
                       *rest, ngroups, npairs):
    z_ref, so_ref = rest[-2:]
    hs, L, T = HEAD_SIZE, CHUNK, SAMPLE_TOK
    ns = L // T
    lane = lax.broadcasted_iota(jnp.int32, (1, LANES), 1)
    head0 = lane < hs
    head0w = (lax.broadcasted_iota(jnp.int32, (1, 2 * LANES), 1) % LANES) < hs
    m0 = head0.astype(F32)
    m1 = 1.0 - m0
    ri = lax.broadcasted_iota(jnp.int32, (LANES, LANES), 0)
    ci = lax.broadcasted_iota(jnp.int32, (LANES, LANES), 1)
    bdmask = ((ri < hs) == (ci < hs)).astype(F32)
    eye_w = (ri == ci).astype(BF16)
    rl = lax.broadcasted_iota(jnp.int32, (L, L), 0)
    cl = lax.broadcasted_iota(jnp.int32, (L, L), 1)
    same = (rl // T) == (cl // T)
    cum_lhs = jnp.concatenate([jnp.where(same, (cl % T <= rl % T).astype(F32), 0.0),
                               same.astype(F32)], axis=0).astype(BF16)
    eye = (cl == rl).astype(F32)
    prow = lax.broadcasted_iota(jnp.int32, (4 * L, 2 * L), 0)
    pcol = lax.broadcasted_iota(jnp.int32, (4 * L, 2 * L), 1) % L
    psame = ((prow % L) // T) == (pcol // T)
    pcaus = jnp.where(prow >= 2 * L, (pcol % T <= prow % T).astype(F32), (pcol % T < prow % T).astype(F32))
    pmask = jnp.where(psame, pcaus, 0.0)
    kcols = (lax.broadcasted_iota(jnp.int32, (1, 2 * L), 1) >= L).astype(F32)
    nsteps = int(math.log2(T)) - 1
    zeros_l = jnp.zeros((L, LANES), F32)
    zeros_t = jnp.zeros((T, LANES), F32)
    zed = jnp.zeros((hs, hs), F32)

    def head_sum(y):
        s0 = jnp.sum(y * m0, axis=-1, keepdims=True)
        s1 = jnp.sum(y * m1, axis=-1, keepdims=True)
        return jnp.where(head0, s0, s1)

    def group_body(gi, carry):
        seqs = pl.ds(pl.multiple_of(gi * ns, ns), ns)
        pr = range(npairs)
        lss = [slice(p * LANES, (p + 1) * LANES) for p in pr]
        tile = lambda ref, ls: ref[seqs, :, ls].reshape(L, LANES)
        ld = [tile(ld_ref, ls) for ls in lss]
        ld_hi = [x.astype(BF16) for x in ld]
        ld_r = [x - h.astype(F32) for x, h in zip(ld, ld_hi)]
        ld_mid = [x.astype(BF16) for x in ld_r]
        ld_lo = [(x - m.astype(F32)).astype(BF16) for x, m in zip(ld_r, ld_mid)]
        cc = [_dot(cum_lhs, ld_hi[p]) + _dot(cum_lhs, ld_mid[p]) + _dot(cum_lhs, ld_lo[p]) for p in pr]
        cs = [x[:L] for x in cc]
        c_end = [x[L:] for x in cc]
        r = [tile(r_ref, ls) for ls in lss]
        k = [tile(k_ref, ls) for ls in lss]
        v = [tile(v_ref, ls) for ls in lss]
        kn = [tile(kn_ref, ls) for ls in lss]
        b = [kn[p] * tile(a_ref, lss[p]) for p in pr]
        w_inv = [jnp.exp(-x) for x in cs]
        rt = [r[p] * jnp.exp(cs[p]) for p in pr]
        kt = [k[p] * w_inv[p] for p in pr]
        bt = [b[p] * w_inv[p] for p in pr]
        at = [-kn[p] * jnp.exp(cs[p] - ld[p]) for p in pr]
        w_end = [jnp.exp(c_end[p] - cs[p]) for p in pr]
        w_tot = [jnp.exp(x) for x in c_end]
        kh = [k[p] * w_end[p] for p in pr]
        bh = [b[p] * w_end[p] for p in pr]
        pm = [_mm_nt(jnp.concatenate([at[p] * m0, at[p] * m1, rt[p] * m0, rt[p] * m1], axis=0),
                     jnp.concatenate([bt[p], kt[p]], axis=0)) * pmask for p in pr]
        a_bk = [x[:2 * L] for x in pm]
        r_bk = [x[2 * L:] for x in pm]
        akv = [_mm(a_bk[p] * kcols, jnp.concatenate([v[p], v[p]], axis=0)) for p in pr]
        akv = [jnp.where(head0, x[:L], x[L:]) for x in akv]
        pw = [a_bk[p][h * L:(h + 1) * L, :L] for p in pr for h in range(2)]
        tinv = [eye + x for x in pw]
        for _ in range(nsteps):
            pw = [_mm(x, x) for x in pw]
            tinv = [t + _mm(t, x) for t, x in zip(tinv, pw)]
        tz_rhs = [jnp.concatenate([at[p], akv[p]], axis=1) for p in pr]
        tz = [jnp.where(head0w, _mm(tinv[2 * p], tz_rhs[p]), _mm(tinv[2 * p + 1], tz_rhs[p])) for p in pr]
        ry = [_mm(r_bk[p], jnp.concatenate([tz[p], jnp.concatenate([zeros_l, v[p]], axis=1)], axis=0))
              for p in pr]
        ry = [jnp.where(head0w, x[:L], x[L:]) for x in ry]
        r2 = [rt[p] + ry[p][:, :LANES] for p in pr]
        ps = [(p, i) for p in pr for i in range(ns)]
        rs = [slice(i * T, (i + 1) * T) for i in range(ns)]
        state = [jnp.concatenate([jnp.concatenate([s0_ref[gi * ns + i, 2 * p], zed], axis=1),
                                  jnp.concatenate([zed, s0_ref[gi * ns + i, 2 * p + 1]], axis=1)], axis=0)
                 for p, i in ps]
        sb = [x.astype(BF16) for x in state]
        e_rhs = [jnp.concatenate([tz[p][rs[i]], jnp.concatenate([zeros_t, v[p][rs[i]]], axis=1)], axis=0)
                 for p, i in ps]
        e = [_mm_nt(jnp.concatenate([sb[q], eye_w], axis=1), e_rhs[q]) for q in range(len(ps))]
        upd = [_mm(e[q], jnp.concatenate([bh[p][rs[i]], kh[p][rs[i]]], axis=0)) * bdmask
               for q, (p, i) in enumerate(ps)]
        ys = [_mm_nt(r2[p][rs[i]], sb[q]) for q, (p, i) in enumerate(ps)]
        for q, (p, i) in enumerate(ps):
            new = state[q] * w_tot[p][i * T:i * T + 1] + upd[q]
            so_ref[gi * ns + i, 2 * p] = new[:hs, :hs]
            so_ref[gi * ns + i, 2 * p + 1] = new[hs:, hs:]
        y = [jnp.concatenate(ys[p * ns:(p + 1) * ns], axis=0) + ry[p][:, LANES:] for p in pr]
        mean = [head_sum(x) * (1.0 / hs) for x in y]
        yc = [y[p] - mean[p] for p in pr]
        var = [head_sum(x * x) * (1.0 / hs) for x in yc]
        bonus = [head_sum(r[p] * k[p] * rk_ref[:, lss[p]]) * v[p] for p in pr]
        for p in pr:
            yn = yc[p] * lax.rsqrt(var[p] + GN_EPS) * lnw_ref[:, lss[p]] + lnb_ref[:, lss[p]]
            z_ref[seqs, :, lss[p]] = ((yn + bonus[p]) * tile(g_ref, lss[p])).reshape(ns, T, LANES)
        return carry

    lax.fori_loop(0, ngroups, group_body, 0)


def _stacked(prev, shape, n_inputs, out_index):
    prev = jnp.zeros(shape, F32) if prev is None else prev
    return [prev], [pl.BlockSpec(memory_space=pl.ANY)], {n_inputs: out_index}


def _wkv_sample(r, ld, k, v, kn, a, g, s0, lnw, lnb, rk, *, layer, n_layers, states):
    nb, tt, d = r.shape
    nh = d // HEAD_SIZE
    npairs = min(WKV_SAMPLE_PAIRS, nh // 2)
    ns = CHUNK // SAMPLE_TOK
    bb = min(nb, 2 * ns)
    lw = npairs * LANES
    kern = functools.partial(_wkv_sample_kernel, ngroups=bb // ns, npairs=npairs)
    seqb = pl.BlockSpec((bb, tt, lw), lambda i, p: (i, 0, p))
    vecb = pl.BlockSpec((1, lw), lambda i, p: (0, p))
    stb = pl.BlockSpec((bb, 2 * npairs, HEAD_SIZE, HEAD_SIZE), lambda i, p: (i, p, 0, 0))
    sto = pl.BlockSpec((None, bb, 2 * npairs, HEAD_SIZE, HEAD_SIZE), lambda i, p: (layer, i, p, 0, 0))
    st_shape = (n_layers, nb, nh, HEAD_SIZE, HEAD_SIZE)
    extra, extra_specs, aliases = _stacked(states, st_shape, 11, 1)
    return pl.pallas_call(
        kern,
        grid=(nb // bb, nh // (2 * npairs)),
        in_specs=[seqb] * 7 + [stb, vecb, vecb, vecb] + extra_specs,
        out_specs=[seqb, sto],
        out_shape=[jax.ShapeDtypeStruct((nb, tt, d), F32), jax.ShapeDtypeStruct(st_shape, F32)],
        input_output_aliases=aliases,
        compiler_params=_params(("arbitrary", "arbitrary")),
        name="wkv_sample",
    )(r, ld, k, v, kn, a, g, s0, lnw, lnb, rk, *extra)


def _wo_kernel(x_ref, z_ref, w_ref, o_ref, *, nt):
    if nt:
        z = jnp.concatenate([z_ref[:, t, :] for t in range(nt)], axis=0)
    else:
        z = z_ref[...]
    o_ref[...] = x_ref[...] + _dot(z.astype(BF16), w_ref[...])


def _wo(x, z, w, *, layer, nt=0):
    rows, d = x.shape
    tm = rows if nt else _row_tile(rows, WO_TM_CAP)
    zspec = pl.BlockSpec(z.shape, lambda i: (0, 0, 0)) if nt else pl.BlockSpec((tm, d), lambda i: (i, 0))
    return pl.pallas_call(
        functools.partial(_wo_kernel, nt=nt),
        grid=(rows // tm,),
        in_specs=[pl.BlockSpec((tm, d), lambda i: (i, 0)), zspec,
                  pl.BlockSpec((None, d, d), lambda i: (layer, 0, 0))],
        out_specs=pl.BlockSpec((tm, d), lambda i: (i, 0)),
        out_shape=jax.ShapeDtypeStruct((rows, d), F32),
        compiler_params=_params(("arbitrary",)),
        name="rwkv_wo_sample" if nt else "rwkv_wo",
    )(x, z, w)


def _pad_cols(a, n):
    return jnp.pad(a, [(0, 0)] * (a.ndim - 1) + [(0, n - a.shape[-1])])


def _pad_rows(a, n):
    return jnp.pad(a, [(0, n - a.shape[0])] + [(0, 0)] * (a.ndim - 1))


def kernel(x_prompt, x_sample, state_pool, state_rwkv_shift, state_rwkv_wkv, state_ffn_conv, meta_tokens,
           norm_mix, norm_ffn, norm_out, pool_w, pool_scale, rwkv_mu, rwkv_wr, rwkv_wk, rwkv_wv, rwkv_wo,
           rwkv_w0, rwkv_w1, rwkv_w2, rwkv_a0, rwkv_a1, rwkv_a2, rwkv_v0, rwkv_v1, rwkv_v2, rwkv_g1, rwkv_g2,
           rwkv_kk, rwkv_ka, rwkv_rk, rwkv_lnw, rwkv_lnb, ffn_w_in, ffn_conv_w, ffn_conv_b, ffn_w_out):
    b, seq, d = x_prompt.shape
    bs, nt, _ = x_sample.shape
    depth = norm_mix.shape[0]
    f = ffn_w_out.shape[1]
    fp = _round_up(f, FFN_TF)
    nh = d // HEAD_SIZE
    tp = FRONT_PAD + N_META + seq
    assert tp % CHUNK == 0 and d % (2 * LANES) == 0 and CONV_WIDTH - 1 <= nt <= SAMPLE_TOK

    row = lambda a: a.reshape(1, -1)
    lora_in = lambda a: _pad_cols(a, _round_up(a.shape[-1], LANES)).astype(BF16)
    lora_out = lambda a: _pad_rows(a, _round_up(a.shape[0], LANES)).astype(BF16)
    halves = lambda a: jnp.concatenate([_pad_cols(a[..., :f], fp), _pad_cols(a[..., f:], fp)], axis=-1)
    w_in_all = _cast_halves(ffn_w_in, fp)
    w_out_all = _cast_rows(ffn_w_out, fp)
    ffn = [dict(g=row(norm_ffn[i]), cw=halves(ffn_conv_w[i]), cb=halves(ffn_conv_b[i][None]))
           for i in range(depth)]
    tn = min(PROJ_TN, d)
    bd = jnp.kron(jnp.eye(tn // HEAD_SIZE, dtype=F32), jnp.ones((HEAD_SIZE, HEAD_SIZE), F32)).astype(BF16)
    wr_all, wk_all, wv_all, wo_all = (_cast_rows(w, d) for w in (rwkv_wr, rwkv_wk, rwkv_wv, rwkv_wo))
    rw = []
    for j in range(depth // 2):
        rw.append(dict(
            g=row(norm_mix[2 * j + 1]), mu=rwkv_mu[j], wr=wr_all, wk=wk_all, wv=wv_all,
            w1=lora_in(rwkv_w1[j]), a1=lora_in(rwkv_a1[j]), g1=lora_in(rwkv_g1[j]),
            w2=lora_out(rwkv_w2[j]), a2=lora_out(rwkv_a2[j]), g2=lora_out(rwkv_g2[j]),
            w0=row(rwkv_w0[j]), a0=row(rwkv_a0[j]), kk=row(rwkv_kk[j]), ka=row(rwkv_ka[j]), bd=bd,
            lnw=row(rwkv_lnw[j]), lnb=row(rwkv_lnb[j]), rk=row(rwkv_rk[j])))
    vls = [None] + [dict(v1=lora_in(rwkv_v1[j]), v2=lora_out(rwkv_v2[j]), v0=row(rwkv_v0[j]))
                    for j in range(depth // 2 - 1)]
    pw = [pool_w[j].astype(BF16) for j in range((depth + 1) // 2)]
    g_out = row(norm_out)

    head = jnp.concatenate([jnp.zeros((FRONT_PAD, d), F32), meta_tokens], axis=0)
    xp = x_prompt
    xs = x_sample.transpose(1, 0, 2)
    zero_wkv = jnp.zeros((b, nh, HEAD_SIZE, HEAD_SIZE), F32)
    n_pool, n_rwkv = (depth + 1) // 2, depth // 2

    pool_p, shift_p, shift_s, wkv_p, conv_p = [], [], [], [], []
    pool_s = wkv_s = conv_s = None
    vfirst_p = vfirst_s = None
    for i in range(depth):
        j = i // 2
        if i % 2 == 0:
            xp, st = _pool_prompt(xp, row(norm_mix[i]), pw[j], row(pool_scale[j]), tp=tp,
                                  head=head if i == 0 else None)
            pool_p.append(st[:, POOL_HALO - POOL_STATE:])
            xs, pool_s = _pool_sample(xs, state_pool[j], row(norm_mix[i]), pw[j], row(pool_scale[j]), nt=nt, bs=bs,
                                      layer=j, n_layers=n_pool, states=pool_s)
        else:
            p = rw[j]
            r, ld, k, v, kn, a, g, hl = _proj(xp.reshape(b * tp, d), None, p, vls[j], vfirst_p,
                                              layer=j, tp=tp, sample=False, bs=0)
            if vfirst_p is None:
                vfirst_p = v
            shift_p.append(hl[:, CONV_HALO - 1])
            sq = lambda t: t.reshape(b, tp, d)
            z, s_new = _wkv_prompt(sq(r), sq(ld), sq(k), sq(v), sq(kn), sq(a), sq(g), zero_wkv,
                                   p["lnw"], p["lnb"], p["rk"])
            wkv_p.append(s_new)
            xp = _wo(xp.reshape(b * tp, d), z.reshape(b * tp, d), wo_all, layer=j).reshape(b, tp, d)
            r, ld, k, v, kn, a, g, hl = _proj(xs.reshape(nt * bs, d), state_rwkv_shift[j], p, vls[j], vfirst_s,
                                              layer=j, tp=0, sample=True, bs=bs)
            if vfirst_s is None:
                vfirst_s = v
            shift_s.append(hl)
            z, wkv_s = _wkv_sample(r, ld, k, v, kn, a, g, state_rwkv_wkv[j], p["lnw"], p["lnb"], p["rk"],
                                   layer=j, n_layers=n_rwkv, states=wkv_s)
            xs = _wo(xs.reshape(nt * bs, d), z, wo_all, layer=j, nt=nt).reshape(nt, bs, d)
        fi = ffn[i]
        last = i == depth - 1
        xo, stv, stg = _ffn_prompt(xp.reshape(b * tp, d), fi["g"], w_in_all, fi["cw"], fi["cb"], w_out_all,
                                   g_out, layer=i, tp=tp, final_norm=last)
        xp = xo.reshape(b, tp, d)
        stv, stg = (t.reshape(b, -1, CONV_HALO, fp)[:, -1] for t in (stv, stg))
        conv_p.append(jnp.concatenate([stv[:, CONV_HALO - 2:, :f], stg[:, CONV_HALO - 2:, :f]], axis=-1))
        xo, *conv_s = _ffn_sample(xs.reshape(nt * bs, d), fi["g"], w_in_all, fi["cw"], fi["cb"], w_out_all, g_out,
                                  state_ffn_conv[i][:, :, :f], state_ffn_conv[i][:, :, f:],
                                  layer=i, n_layers=depth, states=conv_s, bs=bs, nt=nt, final_norm=last)
        xs = xo.reshape(nt, bs, d)

    y_prompt = xp[:, FRONT_PAD + N_META:]
    y_sample = xs.transpose(1, 0, 2)
    conv_s = jnp.concatenate([conv_s[0][..., :f], conv_s[1][..., :f]], axis=-1)
    return (y_prompt, y_sample, jnp.stack(pool_p), pool_s, jnp.stack(shift_p), jnp.stack(shift_s),
            jnp.stack(wkv_p), wkv_s, jnp.stack(conv_p), conv_s)
```

```python
import functools
import math

import jax
import jax.numpy as jnp
from jax import lax
from jax.experimental import pallas as pl
from jax.experimental.pallas import tpu as pltpu

F32 = jnp.float32
BF16 = jnp.bfloat16

HEAD_SIZE = 64
LANES = 128
N_META = 16
PAST_LEN = 16384
POOL_WINDOWS = (2, 4, 8, 16)
POOL_STATE = max(POOL_WINDOWS) - 1
POOL_HALO = 16
CONV_WIDTH = 3
CONV_HALO = 8
NORM_EPS = 1e-6
GN_EPS = 64e-5
CHUNK = 64
SAMPLE_TOK = 8
FRONT_PAD = CHUNK - N_META

FFN_TM_CAP = 704
PROJ_TM_CAP = 352
POOL_TM_CAP = 704
POOL_ASSEMBLE = 11
WO_TM_CAP = 704
WKV_ROWS_CAP = 192
WKV_PAIRS = 16
WKV_SAMPLE_PAIRS = 8
CAST_ROWS = 512
FFN_TF = 512
PROJ_TN = 512
PROJ_TN_SAMPLE = 256
VMEM_LIMIT = 56 * 1024 * 1024


def _row_tile(n, cap, mult=16):
    best = None
    for t in range(mult, min(n, cap) + 1, mult):
        if n % t == 0:
            best = t
    assert best is not None, (n, cap)
    return best


def _round_up(n, m):
    return (n + m - 1) // m * m


def _params(sem):
    return pltpu.CompilerParams(dimension_semantics=sem, vmem_limit_bytes=VMEM_LIMIT)


def _rms(x, g):
    return x * lax.rsqrt(jnp.mean(x * x, axis=-1, keepdims=True) + NORM_EPS) * g


def _dot(a, b):
    return jnp.dot(a, b, preferred_element_type=F32)


def _split(x):
    hi = x.astype(BF16)
    lo = (x - hi.astype(F32)).astype(BF16)
    return hi, lo


def _cast_kernel(x_ref, o_ref, *, valid_rows, tr):
    x = x_ref[...]
    if valid_rows is not None:
        row = pl.program_id(1) * tr + lax.broadcasted_iota(jnp.int32, (1, tr, 1), 1)
        x = jnp.where(row < valid_rows, x, 0.0)
    o_ref[...] = x.astype(o_ref.dtype)


def _cast_rows(w, out_rows):
    n, rows, cols = w.shape
    tr = CAST_ROWS
    kern = functools.partial(_cast_kernel, valid_rows=None if rows % tr == 0 else rows, tr=tr)
    return pl.pallas_call(
        kern,
        grid=(n, out_rows // tr),
        in_specs=[pl.BlockSpec((1, tr, cols), lambda l, i: (l, i, 0))],
        out_specs=pl.BlockSpec((1, tr, cols), lambda l, i: (l, i, 0)),
        out_shape=jax.ShapeDtypeStruct((n, out_rows, cols), BF16),
        compiler_params=_params(("arbitrary", "arbitrary")),
        name="cast_rows",
    )(w)


def _cast_halves_kernel(x_ref, o_ref, *, f, fp):
    x = x_ref[0]
    pad = jnp.zeros((x.shape[0], fp - f), o_ref.dtype)
    o_ref[0, :, :f] = x[:, :f].astype(o_ref.dtype)
    o_ref[0, :, fp:fp + f] = x[:, f:].astype(o_ref.dtype)
    if fp > f:
        o_ref[0, :, f:fp] = pad
        o_ref[0, :, fp + f:] = pad


def _cast_halves(w, fp):
    n, rows, f2 = w.shape
    f = f2 // 2
    tr = min(rows, CAST_ROWS // 2)
    return pl.pallas_call(
        functools.partial(_cast_halves_kernel, f=f, fp=fp),
        grid=(n, rows // tr),
        in_specs=[pl.BlockSpec((1, tr, f2), lambda l, i: (l, i, 0))],
        out_specs=pl.BlockSpec((1, tr, 2 * fp), lambda l, i: (l, i, 0)),
        out_shape=jax.ShapeDtypeStruct((n, rows, 2 * fp), BF16),
        compiler_params=_params(("arbitrary", "arbitrary")),
        name="cast_halves",
    )(w)


def _pool_prompt_kernel(*refs, tm, padf, cg, assemble):
    nx = assemble or 1
    x_refs, refs = refs[:nx], refs[nx:]
    if assemble:
        head_ref, refs = refs[0], refs[1:]
    g_ref, w_ref, sc_ref, o_ref, st_ref, carry_ref = refs
    i = pl.program_id(1)

    @pl.when(i == 0)
    def _():
        carry_ref[...] = jnp.zeros_like(carry_ref)

    if assemble:
        x = jnp.concatenate([jnp.where(i == 0, head_ref[...], x_refs[0][0])] + [r[0] for r in x_refs[1:]], axis=0)
    else:
        x = x_refs[0][0]
    row = i * tm + lax.broadcasted_iota(jnp.int32, (tm, 1), 0)
    h = jnp.where(row >= padf, _rms(x, g_ref[...]), 0.0)
    ext = jnp.concatenate([carry_ref[...], h], axis=0)
    pos = row - padf
    for g, w in enumerate(POOL_WINDOWS):
        sl = slice(g * cg, (g + 1) * cg)
        a = ext[:, sl]
        k = 1
        while k < w:
            n = a.shape[0]
            a = a[:n - k] + a[k:]
            k *= 2
        win = a[POOL_HALO + 1 - w: POOL_HALO + 1 - w + tm]
        cnt = jnp.clip(pos + 1, 1, w).astype(F32)
        d = win / cnt - h[:, sl]
        y = _dot(d.astype(BF16), w_ref[g])
        o_ref[0, :, sl] = x[:, sl] + y * sc_ref[:, sl]
    carry_ref[...] = h[tm - POOL_HALO:]
    st_ref[0] = h[tm - POOL_HALO:]


def _pool_prompt(x, g, w, sc, *, tp, head=None):
    b, _, d = x.shape
    cg = d // len(POOL_WINDOWS)
    if head is not None:
        hb = head.shape[0]
        assemble = POOL_ASSEMBLE if (tp // hb) % POOL_ASSEMBLE == 0 else 1
        tm = assemble * hb
        x_specs = [pl.BlockSpec((1, hb, d), lambda bi, i, k=k: (bi, jnp.maximum(assemble * i + k - 1, 0), 0))
                   for k in range(assemble)] + [pl.BlockSpec((hb, d), lambda bi, i: (0, 0))]
        x_args = (x,) * assemble + (head,)
    else:
        assemble = 0
        tm = _row_tile(tp, POOL_TM_CAP)
        x_specs = [pl.BlockSpec((1, tm, d), lambda bi, i: (bi, i, 0))]
        x_args = (x,)
    kern = functools.partial(_pool_prompt_kernel, tm=tm, padf=FRONT_PAD, cg=cg, assemble=assemble)
    return pl.pallas_call(
        kern,
        grid=(b, tp // tm),
        in_specs=x_specs + [
            pl.BlockSpec((1, d), lambda bi, i: (0, 0)),
            pl.BlockSpec((len(POOL_WINDOWS), cg, cg), lambda bi, i: (0, 0, 0)),
            pl.BlockSpec((1, d), lambda bi, i: (0, 0)),
        ],
        out_specs=[
            pl.BlockSpec((1, tm, d), lambda bi, i: (bi, i, 0)),
            pl.BlockSpec((1, POOL_HALO, d), lambda bi, i: (bi, 0, 0)),
        ],
        out_shape=[
            jax.ShapeDtypeStruct((b, tp, d), F32),
            jax.ShapeDtypeStruct((b, POOL_HALO, d), F32),
        ],
        scratch_shapes=[pltpu.VMEM((POOL_HALO, d), F32)],
        compiler_params=_params(("arbitrary", "arbitrary")),
        name="pool_prompt_assemble" if assemble else "pool_prompt",
    )(*x_args, g, w, sc)


def _pool_sample_kernel(x_ref, pre_ref, g_ref, w_ref, sc_ref, *rest, tb, d, cg, nt, start):
    o_ref, st_ref = rest[-2:]
    hs = [_rms(x_ref[t], g_ref[...]) for t in range(nt)]
    ext = [pre_ref[:, j, :] for j in range(POOL_STATE)] + hs
    for g, w in enumerate(POOL_WINDOWS):
        sl = slice(g * cg, (g + 1) * cg)
        ds = []
        for t in range(nt):
            e = POOL_STATE + t
            acc = ext[e][:, sl]
            for q in range(1, w):
                acc = acc + ext[e - q][:, sl]
            cnt = float(min(w, start + t + 1))
            ds.append(acc / cnt - hs[t][:, sl])
        y = _dot(jnp.concatenate(ds, axis=0).astype(BF16), w_ref[g])
        for t in range(nt):
            o_ref[t, :, sl] = x_ref[t][:, sl] + y[t * tb:(t + 1) * tb] * sc_ref[:, sl]
    for j in range(POOL_STATE):
        st_ref[:, j, :] = ext[nt + j]


def _pool_sample(x, pre, g, w, sc, *, nt, bs, layer, n_layers, states):
    d = x.shape[-1]
    tb = min(bs, 32)
    cg = d // len(POOL_WINDOWS)
    kern = functools.partial(_pool_sample_kernel, tb=tb, d=d, cg=cg, nt=nt, start=PAST_LEN)
    st_shape = (n_layers, bs, POOL_STATE, d)
    extra, extra_specs, aliases = _stacked(states, st_shape, 5, 1)
    return pl.pallas_call(
        kern,
        grid=(bs // tb,),
        in_specs=[
            pl.BlockSpec((nt, tb, d), lambda i: (0, i, 0)),
            pl.BlockSpec((tb, POOL_STATE, d), lambda i: (i, 0, 0)),
            pl.BlockSpec((1, d), lambda i: (0, 0)),
            pl.BlockSpec((len(POOL_WINDOWS), cg, cg), lambda i: (0, 0, 0)),
            pl.BlockSpec((1, d), lambda i: (0, 0)),
        ] + extra_specs,
        out_specs=[
            pl.BlockSpec((nt, tb, d), lambda i: (0, i, 0)),
            pl.BlockSpec((None, tb, POOL_STATE, d), lambda i: (layer, i, 0, 0)),
        ],
        out_shape=[
            jax.ShapeDtypeStruct((nt, bs, d), F32),
            jax.ShapeDtypeStruct(st_shape, F32),
        ],
        input_output_aliases=aliases,
        compiler_params=_params(("arbitrary",)),
        name="pool_sample",
    )(x, pre, g, w, sc, *extra)


def _ffn_prompt_kernel(x_ref, g_ref, wv_ref, wg_ref, cwv_ref, cwg_ref, cbv_ref, cbg_ref, wo_ref, go_ref,
                       o_ref, stv_ref, stg_ref, hb_ref, carv_ref, carg_ref, *, tm, tps, padf, nj, final_norm):
    i = pl.program_id(0)
    j = pl.program_id(1)
    ti = i % tps

    @pl.when(j == 0)
    def _():
        x = x_ref[...]
        row = ti * tm + lax.broadcasted_iota(jnp.int32, (tm, 1), 0)
        h = jnp.where(row >= padf, _rms(x, g_ref[...]), 0.0)
        hb_ref[...] = h.astype(BF16)
        o_ref[...] = x

    hb = hb_ref[...]
    keep = ti != 0

    def branch(w_ref, cw_ref, cb_ref, car_ref, st_ref):
        u = _dot(hb, w_ref[...])
        prev = jnp.where(keep, car_ref[j], 0.0)
        ext = jnp.concatenate([prev, u], axis=0)
        cw = cw_ref[...]
        c = (cb_ref[...] + ext[CONV_HALO - 2:CONV_HALO - 2 + tm] * cw[0:1]
             + ext[CONV_HALO - 1:CONV_HALO - 1 + tm] * cw[1:2] + u * cw[2:3])
        car_ref[j] = u[tm - CONV_HALO:]
        st_ref[0] = u[tm - CONV_HALO:]
        return c

    cv = branch(wv_ref, cwv_ref, cbv_ref, carv_ref, stv_ref)
    cgate = branch(wg_ref, cwg_ref, cbg_ref, carg_ref, stg_ref)
    act = cgate * jax.nn.sigmoid(cgate) * cv
    o_ref[...] += _dot(act.astype(BF16), wo_ref[...])

    if final_norm:
        @pl.when(j == nj - 1)
        def _():
            o_ref[...] = _rms(o_ref[...], go_ref[...])


def _ffn_prompt(x, g, w_in, cw, cb, w_out, g_out, *, layer, tp, final_norm):
    rows, d = x.shape
    fp = w_out.shape[1]
    tf = FFN_TF
    nj = fp // tf
    tm = _row_tile(tp, FFN_TM_CAP)
    tps = tp // tm
    kern = functools.partial(_ffn_prompt_kernel, tm=tm, tps=tps, padf=FRONT_PAD, nj=nj, final_norm=final_norm)
    return pl.pallas_call(
        kern,
        grid=(rows // tm, nj),
        in_specs=[
            pl.BlockSpec((tm, d), lambda i, j: (i, 0)),
            pl.BlockSpec((1, d), lambda i, j: (0, 0)),
            pl.BlockSpec((None, d, tf), lambda i, j: (layer, 0, j)),
            pl.BlockSpec((None, d, tf), lambda i, j: (layer, 0, nj + j)),
            pl.BlockSpec((CONV_WIDTH, tf), lambda i, j: (0, j)),
            pl.BlockSpec((CONV_WIDTH, tf), lambda i, j: (0, nj + j)),
            pl.BlockSpec((1, tf), lambda i, j: (0, j)),
            pl.BlockSpec((1, tf), lambda i, j: (0, nj + j)),
            pl.BlockSpec((None, tf, d), lambda i, j: (layer, j, 0)),
            pl.BlockSpec((1, d), lambda i, j: (0, 0)),
        ],
        out_specs=[
            pl.BlockSpec((tm, d), lambda i, j: (i, 0)),
            pl.BlockSpec((1, CONV_HALO, tf), lambda i, j: (i, 0, j)),
            pl.BlockSpec((1, CONV_HALO, tf), lambda i, j: (i, 0, j)),
        ],
        out_shape=[
            jax.ShapeDtypeStruct((rows, d), F32),
            jax.ShapeDtypeStruct((rows // tm, CONV_HALO, fp), F32),
            jax.ShapeDtypeStruct((rows // tm, CONV_HALO, fp), F32),
        ],
        scratch_shapes=[
            pltpu.VMEM((tm, d), BF16),
            pltpu.VMEM((nj, CONV_HALO, tf), F32),
            pltpu.VMEM((nj, CONV_HALO, tf), F32),
        ],
        compiler_params=_params(("arbitrary", "arbitrary")),
        name="ffn_prompt",
    )(x, g, w_in, w_in, cw, cw, cb, cb, w_out, g_out)


def _ffn_sample_kernel(x_ref, g_ref, wv_ref, wg_ref, cwv_ref, cwg_ref, cbv_ref, cbg_ref, wo_ref, go_ref,
                       pv_ref, pg_ref, *rest, bs, nt, nj, tf, f, final_norm):
    o_ref, sv_ref, sg_ref, hb_ref = rest[-4:]
    j = pl.program_id(0)

    @pl.when(j == 0)
    def _():
        x = x_ref[...]
        hb_ref[...] = _rms(x, g_ref[...]).astype(BF16)
        o_ref[...] = x

    hb = hb_ref[...]
    valid = j * tf + lax.broadcasted_iota(jnp.int32, (1, tf), 1) < f

    def branch(w_ref, cw_ref, cb_ref, p_ref, s_ref):
        u = _dot(hb, w_ref[...])
        prev = [jnp.where(valid, p_ref[:, q, :], 0.0) for q in range(CONV_WIDTH - 1)]
        ext = prev + [u[t * bs:(t + 1) * bs] for t in range(nt)]
        cw = cw_ref[...]
        cs = [cb_ref[...] + ext[t] * cw[0:1] + ext[t + 1] * cw[1:2] + ext[t + 2] * cw[2:3] for t in range(nt)]
        for q in range(CONV_WIDTH - 1):
            s_ref[:, q, :] = ext[nt + q]
        return jnp.concatenate(cs, axis=0)

    cv = branch(wv_ref, cwv_ref, cbv_ref, pv_ref, sv_ref)
    cgate = branch(wg_ref, cwg_ref, cbg_ref, pg_ref, sg_ref)
    act = cgate * jax.nn.sigmoid(cgate) * cv
    o_ref[...] += _dot(act.astype(BF16), wo_ref[...])

    if final_norm:
        @pl.when(j == nj - 1)
        def _():
            o_ref[...] = _rms(o_ref[...], go_ref[...])


def _ffn_sample(x, g, w_in, cw, cb, w_out, g_out, st_val, st_gate, *, layer, n_layers, states, bs, nt,
                final_norm):
    rows, d = x.shape
    fp = w_out.shape[1]
    f = st_val.shape[-1]
    tf = FFN_TF
    nj = fp // tf
    kern = functools.partial(_ffn_sample_kernel, bs=bs, nt=nt, nj=nj, tf=tf, f=f, final_norm=final_norm)
    st_spec = pl.BlockSpec((bs, CONV_WIDTH - 1, tf), lambda j: (0, 0, j))
    so_spec = pl.BlockSpec((None, bs, CONV_WIDTH - 1, tf), lambda j: (layer, 0, 0, j))
    st_shape = (n_layers, bs, CONV_WIDTH - 1, fp)
    extra, extra_specs, aliases = [], [], {}
    for q, prev in enumerate(states or (None, None)):
        e, s, al = _stacked(prev, st_shape, 12 + q, 1 + q)
        extra, extra_specs, aliases = extra + e, extra_specs + s, {**aliases, **al}
    return pl.pallas_call(
        kern,
        grid=(nj,),
        in_specs=[
            pl.BlockSpec((rows, d), lambda j: (0, 0)),
            pl.BlockSpec((1, d), lambda j: (0, 0)),
            pl.BlockSpec((None, d, tf), lambda j: (layer, 0, j)),
            pl.BlockSpec((None, d, tf), lambda j: (layer, 0, nj + j)),
            pl.BlockSpec((CONV_WIDTH, tf), lambda j: (0, j)),
            pl.BlockSpec((CONV_WIDTH, tf), lambda j: (0, nj + j)),
            pl.BlockSpec((1, tf), lambda j: (0, j)),
            pl.BlockSpec((1, tf), lambda j: (0, nj + j)),
            pl.BlockSpec((None, tf, d), lambda j: (layer, j, 0)),
            pl.BlockSpec((1, d), lambda j: (0, 0)),
            st_spec, st_spec,
        ] + extra_specs,
        out_specs=[pl.BlockSpec((rows, d), lambda j: (0, 0)), so_spec, so_spec],
        out_shape=[jax.ShapeDtypeStruct((rows, d), F32)] + [jax.ShapeDtypeStruct(st_shape, F32)] * 2,
        input_output_aliases=aliases,
        scratch_shapes=[pltpu.VMEM((rows, d), BF16)],
        compiler_params=_params(("arbitrary",)),
        name="ffn_sample",
    )(x, g, w_in, w_in, cw, cw, cb, cb, w_out, g_out, st_val, st_gate, *extra)


def _proj_kernel(*refs, tm, tps, padf, sample, bs, has_vlora):
    it = iter(refs)
    x_ref = next(it)
    sh_ref = next(it) if sample else None
    g_ref, mu_ref, wr_ref, wk_ref, wv_ref, w1_ref, a1_ref, g1_ref = (next(it) for _ in range(8))
    w2_ref, a2_ref, g2_ref, w0_ref, a0_ref, kk_ref, ka_ref, bd_ref = (next(it) for _ in range(8))
    if has_vlora:
        v1_ref, v2_ref, v0_ref, vf_ref = (next(it) for _ in range(4))
    r_o, ld_o, k_o, v_o, kn_o, a_o, g_o, hl_o = (next(it) for _ in range(8))
    xr_s, xk_s, xv_s, lw_s, la_s, lg_s = (next(it) for _ in range(6))
    lv_s = next(it) if has_vlora else None
    car_s = None if sample else next(it)

    i = pl.program_id(0)
    j = pl.program_id(1)

    @pl.when(j == 0)
    def _():
        x = x_ref[...]
        h = _rms(x, g_ref[...])
        if sample:
            prev = jnp.concatenate([sh_ref[...], h[:tm - bs]], axis=0)
            hl_o[...] = h[tm - bs:]
        else:
            ti = i % tps
            rloc = lax.broadcasted_iota(jnp.int32, (tm, 1), 0)
            h = jnp.where(ti * tm + rloc >= padf, h, 0.0)
            last = jnp.where(ti != 0, car_s[CONV_HALO - 1:CONV_HALO, :], 0.0)
            prev = jnp.where(rloc == 0, last, pltpu.roll(h, 1, 0))
            car_s[...] = h[tm - CONV_HALO:]
            hl_o[0] = h[tm - CONV_HALO:]
        xx = prev - h
        mu = mu_ref[...]
        mix = lambda q: (h + xx * mu[q:q + 1]).astype(BF16)
        xr_s[...] = mix(0)
        xk_s[...] = mix(2)
        xv = mix(3)
        xv_s[...] = xv
        lw_s[...] = jnp.tanh(_dot(mix(1), w1_ref[...])).astype(BF16)
        la_s[...] = _dot(mix(4), a1_ref[...]).astype(BF16)
        lg_s[...] = jax.nn.sigmoid(_dot(mix(5), g1_ref[...])).astype(BF16)
        if has_vlora:
            lv_s[...] = _dot(xv, v1_ref[...]).astype(BF16)

    nt = tm // bs if sample else 0

    def put(o_ref, val):
        if sample:
            for t in range(SAMPLE_TOK):
                o_ref[:, t, :] = val[t * bs:(t + 1) * bs] if t < nt else jnp.zeros((bs, val.shape[1]), F32)
        else:
            o_ref[...] = val

    r = _dot(xr_s[...], wr_ref[...])
    k = _dot(xk_s[...], wk_ref[...])
    v = _dot(xv_s[...], wv_ref[...])
    z = w0_ref[...] + _dot(lw_s[...], w2_ref[...])
    a = jax.nn.sigmoid(a0_ref[...] + _dot(la_s[...], a2_ref[...]))
    put(ld_o, -math.exp(-0.5) * jax.nn.sigmoid(z))
    if has_vlora:
        vf = jnp.concatenate([vf_ref[:, t, :] for t in range(nt)], axis=0) if sample else vf_ref[...]
        v = v + (vf - v) * jax.nn.sigmoid(v0_ref[...] + _dot(lv_s[...], v2_ref[...]))
    put(g_o, _dot(lg_s[...], g2_ref[...]))
    kk = k * kk_ref[...]
    sq_hi, sq_lo = _split(kk * kk)
    ss = _dot(sq_hi, bd_ref[...]) + _dot(sq_lo, bd_ref[...])
    put(kn_o, kk / jnp.maximum(jnp.sqrt(ss), 1e-12))
    put(r_o, r)
    put(k_o, k * (1.0 + (a - 1.0) * ka_ref[...]))
    put(v_o, v)
    put(a_o, a)


def _proj(x, sh, p, vl, vfirst, *, layer, tp, sample, bs):
    rows, d = x.shape
    tn = min(PROJ_TN_SAMPLE if sample else PROJ_TN, d)
    nj = d // tn
    p = dict(p, bd=p["bd"][:tn, :tn])
    if sample:
        tm, tps = rows, 1
    else:
        tm = _row_tile(tp, PROJ_TM_CAP)
        tps = tp // tm
    has_vlora = vl is not None
    kern = functools.partial(_proj_kernel, tm=tm, tps=tps, padf=FRONT_PAD, sample=sample, bs=bs,
                             has_vlora=has_vlora)
    full = lambda a: pl.BlockSpec(a.shape, lambda i, j: (0,) * a.ndim)
    colb = lambda a: pl.BlockSpec((a.shape[0], tn), lambda i, j: (0, j))
    if sample:
        rowb = pl.BlockSpec((bs, SAMPLE_TOK, tn), lambda i, j: (0, 0, j))
        row_shape = jax.ShapeDtypeStruct((bs, SAMPLE_TOK, d), F32)
    else:
        rowb = pl.BlockSpec((tm, tn), lambda i, j: (i, j))
        row_shape = jax.ShapeDtypeStruct((rows, d), F32)
    args, specs = [x], [pl.BlockSpec((tm, d), lambda i, j: (i, 0))]
    if sample:
        args.append(sh)
        specs.append(full(sh))
    for name in ("g", "mu"):
        args.append(p[name]); specs.append(full(p[name]))
    for name in ("wr", "wk", "wv"):
        args.append(p[name]); specs.append(pl.BlockSpec((None, d, tn), lambda i, j: (layer, 0, j)))
    for name in ("w1", "a1", "g1"):
        args.append(p[name]); specs.append(full(p[name]))
    for name in ("w2", "a2", "g2", "w0", "a0", "kk", "ka"):
        args.append(p[name]); specs.append(colb(p[name]))
    args.append(p["bd"]); specs.append(full(p["bd"]))
    if has_vlora:
        args += [vl["v1"], vl["v2"], vl["v0"], vfirst]
        specs += [full(vl["v1"]), colb(vl["v2"]), colb(vl["v0"]), rowb]
    out_shape = [row_shape] * 7
    out_specs = [rowb] * 7
    if sample:
        out_shape.append(jax.ShapeDtypeStruct((bs, d), F32))
        out_specs.append(pl.BlockSpec((bs, d), lambda i, j: (0, 0)))
    else:
        nb = rows // tp
        out_shape.append(jax.ShapeDtypeStruct((nb, CONV_HALO, d), F32))
        out_specs.append(pl.BlockSpec((1, CONV_HALO, d), lambda i, j: (i // tps, 0, 0)))
    lw, la, lg = p["w1"].shape[1], p["a1"].shape[1], p["g1"].shape[1]
    scratch = [pltpu.VMEM((tm, d), BF16)] * 3 + [pltpu.VMEM((tm, lw), BF16), pltpu.VMEM((tm, la), BF16),
                                                 pltpu.VMEM((tm, lg), BF16)]
    if has_vlora:
        scratch.append(pltpu.VMEM((tm, vl["v1"].shape[1]), BF16))
    if not sample:
        scratch.append(pltpu.VMEM((CONV_HALO, d), F32))
    return pl.pallas_call(
        kern,
        grid=(rows // tm, nj),
        in_specs=specs,
        out_specs=out_specs,
        out_shape=out_shape,
        scratch_shapes=scratch,
        compiler_params=_params(("arbitrary", "arbitrary")),
        name="rwkv_proj_sample" if sample else "rwkv_proj_prompt",
    )(*args)


def _mm(a, b):
    return _dot(a.astype(BF16), b.astype(BF16))


def _mm_nt(a, b):
    return lax.dot_general(a.astype(BF16), b.astype(BF16), (((1,), (1,)), ((), ())),
                           preferred_element_type=F32)


def _wkv_prompt_kernel(r_ref, ld_ref, k_ref, v_ref, kn_ref, a_ref, g_ref, s0_ref, lnw_ref, lnb_ref, rk_ref,
                       z_ref, so_ref, st_ref, *, L, nchunks, npairs):
    hs = HEAD_SIZE
    assert L == hs and npairs % 2 == 0
    ib = pl.program_id(2)
    lane = lax.broadcasted_iota(jnp.int32, (1, LANES), 1)
    head0 = lane < hs
    m0 = head0.astype(F32)
    m1 = 1.0 - m0
    m0w = jnp.concatenate([m0, m0], axis=1)
    m1w = 1.0 - m0w
    ri = lax.broadcasted_iota(jnp.int32, (LANES, LANES), 0)
    ci = lax.broadcasted_iota(jnp.int32, (LANES, LANES), 1)
    bdmask = ((ri < hs) == (ci < hs)).astype(F32)
    rl = lax.broadcasted_iota(jnp.int32, (L, L), 0)
    cl = lax.broadcasted_iota(jnp.int32, (L, L), 1)
    tril_incl = (cl <= rl).astype(BF16)
    prow = lax.broadcasted_iota(jnp.int32, (2 * L, 4 * L), 0)
    pcol = lax.broadcasted_iota(jnp.int32, (2 * L, 4 * L), 1) % L
    pmask = jnp.where(prow >= L, (pcol <= prow - L).astype(F32), (pcol < prow).astype(F32))
    qrow = lax.broadcasted_iota(jnp.int32, (4 * L, 4 * L), 0)
    qcol = lax.broadcasted_iota(jnp.int32, (4 * L, 4 * L), 1)
    bd4 = (qrow // L) == (qcol // L)
    eye4 = (lax.broadcasted_iota(jnp.int32, (L, 4 * L), 1) % L
            == lax.broadcasted_iota(jnp.int32, (L, 4 * L), 0)).astype(F32)
    nsq = int(math.log2(L))
    zeros_l = jnp.zeros((L, LANES), F32)
    zeros_b = jnp.zeros((4 * L, 4 * L), BF16)

    @pl.when(ib == 0)
    def _():
        zed = jnp.zeros((hs, hs), F32)
        for p in range(npairs):
            st_ref[p] = jnp.concatenate([jnp.concatenate([s0_ref[0, 2 * p], zed], axis=1),
                                         jnp.concatenate([zed, s0_ref[0, 2 * p + 1]], axis=1)], axis=0)

    def head_sum(y):
        s0 = jnp.sum(y * m0, axis=-1, keepdims=True)
        s1 = jnp.sum(y * m1, axis=-1, keepdims=True)
        return jnp.where(head0, s0, s1)

    split = lambda x: jnp.concatenate([x * m0, x * m1], axis=0)
    splitw = lambda x: jnp.concatenate([x * m0w, x * m1w], axis=0)

    def stream(rows, pairs, ld_all, cs_all):
        pr = range(len(pairs))
        lss = [slice(p * LANES, (p + 1) * LANES) for p in pairs]
        ld = [ld_all[:, ls] for ls in lss]
        cs = [cs_all[:, ls] for ls in lss]
        c_end = [x[L - 1:L, :] for x in cs]
        r = [r_ref[0, rows, ls] for ls in lss]
        k = [k_ref[0, rows, ls] for ls in lss]
        v = [v_ref[0, rows, ls] for ls in lss]
        kn = [kn_ref[0, rows, ls] for ls in lss]
        b = [kn[p] * a_ref[0, rows, lss[p]] for p in pr]
        w_inv = [jnp.exp(-x) for x in cs]
        rt = [r[p] * jnp.exp(cs[p]) for p in pr]
        kt = [k[p] * w_inv[p] for p in pr]
        bt = [b[p] * w_inv[p] for p in pr]
        at = [-kn[p] * jnp.exp(cs[p] - ld[p]) for p in pr]
        w_end = [jnp.exp(c_end[p] - cs[p]) for p in pr]
        kh = [k[p] * w_end[p] for p in pr]
        bh = [b[p] * w_end[p] for p in pr]
        yield
        pm = [_mm_nt(jnp.concatenate([at[p], rt[p]], axis=0),
                     jnp.concatenate([split(bt[p]), split(kt[p])], axis=0)) * pmask for p in pr]
        yield
        akv = [_mm(pm[p][:L, 2 * L:], split(v[p])) for p in pr]
        gr = range(len(pairs) // 2)
        pw = [jnp.concatenate([pm[2 * g][:L, :2 * L], pm[2 * g + 1][:L, :2 * L]], axis=1) for g in gr]
        tq = [eye4 for _ in gr]
        for i in range(nsq):
            blk = [jnp.where(bd4, jnp.concatenate([x.astype(BF16)] * 4, axis=0), zeros_b) for x in pw]
            if i < nsq - 1:
                both = [_dot(jnp.concatenate([pw[g], tq[g]], axis=0).astype(BF16), blk[g]) for g in gr]
                pw = [x[:L] for x in both]
                tq = [tq[g] + both[g][L:] for g in gr]
            else:
                tq = [tq[g] + _dot(tq[g].astype(BF16), blk[g]) for g in gr]
            yield
        tinv = [t[:, h * 2 * L:(h + 1) * 2 * L] for t in tq for h in range(2)]
        tz = [_mm(tinv[p], splitw(jnp.concatenate([at[p], akv[p]], axis=1))) for p in pr]
        yield
        zv = [jnp.concatenate([zeros_l, v[p]], axis=1) for p in pr]
        ry = [_mm(pm[p][L:], jnp.concatenate([splitw(tz[p]), splitw(zv[p])], axis=0)) for p in pr]
        nc = [_mm(jnp.concatenate([tz[p][:, :LANES], tz[p][:, LANES:], v[p]], axis=0).T,
                  jnp.concatenate([jnp.concatenate([bh[p], zeros_l], axis=1),
                                   jnp.concatenate([zeros_l, bh[p]], axis=1),
                                   jnp.concatenate([zeros_l, kh[p]], axis=1)], axis=0)) for p in pr]
        yield
        r2 = [rt[p] + ry[p][:, :LANES] for p in pr]
        state = [st_ref[q] for q in pairs]
        y = [_mm_nt(r2[p], state[p]) + ry[p][:, LANES:] for p in pr]
        for p, q in enumerate(pairs):
            st_ref[q] = (state[p] * jnp.exp(c_end[p]) + _mm(state[p], nc[p][:, :LANES] * bdmask)
                         + nc[p][:, LANES:] * bdmask)
        yield
        mean = [head_sum(x) * (1.0 / hs) for x in y]
        yc = [y[p] - mean[p] for p in pr]
        var = [head_sum(x * x) * (1.0 / hs) for x in yc]
        bonus = [head_sum(r[p] * k[p] * rk_ref[:, lss[p]]) * v[p] for p in pr]
        for p in pr:
            yn = yc[p] * lax.rsqrt(var[p] + GN_EPS) * lnw_ref[:, lss[p]] + lnb_ref[:, lss[p]]
            z_ref[0, rows, lss[p]] = ((yn + bonus[p]) * g_ref[0, rows, lss[p]]).astype(z_ref.dtype)

    halves = 1
    per = npairs // halves
    live = []
    for c in range(nchunks):
        rows = slice(c * L, (c + 1) * L)
        ld_all = ld_ref[0, rows, :]
        ld_hi = ld_all.astype(BF16)
        ld_r = ld_all - ld_hi.astype(F32)
        ld_mid = ld_r.astype(BF16)
        ld_lo = (ld_r - ld_mid.astype(F32)).astype(BF16)
        cs_all = _dot(tril_incl, ld_hi) + _dot(tril_incl, ld_mid) + _dot(tril_incl, ld_lo)
        live += [stream(rows, list(range(s * per, (s + 1) * per)), ld_all, cs_all) for s in range(halves)]
    lag = 0
    while live:
        for s in list(live[:lag + 1]):
            if next(s, "done") == "done":
                live.remove(s)
        lag += 1

    @pl.when(ib == pl.num_programs(2) - 1)
    def _():
        for p in range(npairs):
            state = st_ref[p]
            so_ref[0, 2 * p] = state[:hs, :hs]
            so_ref[0, 2 * p + 1] = state[hs:, hs:]


def _wkv_prompt(r, ld, k, v, kn, a, g, s0, lnw, lnb, rk):
    nb, tt, d = r.shape
    nh = d // HEAD_SIZE
    npairs = min(WKV_PAIRS, nh // 2)
    tr = _row_tile(tt, WKV_ROWS_CAP, CHUNK)
    lw = npairs * LANES
    kern = functools.partial(_wkv_prompt_kernel, L=CHUNK, nchunks=tr // CHUNK, npairs=npairs)
    seqb = pl.BlockSpec((1, tr, lw), lambda bi, p, i: (bi, i, p))
    vecb = pl.BlockSpec((1, lw), lambda bi, p, i: (0, p))
    stb = pl.BlockSpec((1, 2 * npairs, HEAD_SIZE, HEAD_SIZE), lambda bi, p, i: (bi, p, 0, 0))
    return pl.pallas_call(
        kern,
        grid=(nb, nh // (2 * npairs), tt // tr),
        in_specs=[seqb] * 7 + [stb, vecb, vecb, vecb],
        out_specs=[seqb, stb],
        out_shape=[jax.ShapeDtypeStruct((nb, tt, d), BF16),
                   jax.ShapeDtypeStruct((nb, nh, HEAD_SIZE, HEAD_SIZE), F32)],
        scratch_shapes=[pltpu.VMEM((npairs, LANES, LANES), F32)],
        compiler_params=_params(("arbitrary", "arbitrary", "arbitrary")),
        name="wkv_prompt",
    )(r, ld, k, v, kn, a, g, s0, lnw, lnb, rk)


def _wkv_sample_kernel(r_ref, ld_ref, k_ref, v_ref, kn_ref, a_ref, g_ref, s0_ref, lnw_ref, lnb_ref, rk_ref,
                       *rest, ngroups, npairs):
    z_ref, so_ref = rest[-2:]
    hs, L, T = HEAD_SIZE, CHUNK, SAMPLE_TOK
    ns = L // T
    lane = lax.broadcasted_iota(jnp.int32, (1, LANES), 1)
    head0 = lane < hs
    head0w = (lax.broadcasted_iota(jnp.int32, (1, 2 * LANES), 1) % LANES) < hs
    m0 = head0.astype(F32)
    m1 = 1.0 - m0
    ri = lax.broadcasted_iota(jnp.int32, (LANES, LANES), 0)
    ci = lax.broadcasted_iota(jnp.int32, (LANES, LANES), 1)
    bdmask = ((ri < hs) == (ci < hs)).astype(F32)
    eye_w = (ri == ci).astype(BF16)
    rl = lax.broadcasted_iota(jnp.int32, (L, L), 0)
    cl = lax.broadcasted_iota(jnp.int32, (L, L), 1)
    same = (rl // T) == (cl // T)
    cum_lhs = jnp.concatenate([jnp.where(same, (cl % T <= rl % T).astype(F32), 0.0),
                               same.astype(F32)], axis=0).astype(BF16)
    eye = (cl == rl).astype(F32)
    prow = lax.broadcasted_iota(jnp.int32, (4 * L, 2 * L), 0)
    pcol = lax.broadcasted_iota(jnp.int32, (4 * L, 2 * L), 1) % L
    psame = ((prow % L) // T) == (pcol // T)
    pcaus = jnp.where(prow >= 2 * L, (pcol % T <= prow % T).astype(F32), (pcol % T < prow % T).astype(F32))
    pmask = jnp.where(psame, pcaus, 0.0)
    kcols = (lax.broadcasted_iota(jnp.int32, (1, 2 * L), 1) >= L).astype(F32)
    nsteps = int(math.log2(T)) - 1
    zeros_l = jnp.zeros((L, LANES), F32)
    zeros_t = jnp.zeros((T, LANES), F32)
    zed = jnp.zeros((hs, hs), F32)

    def head_sum(y):
        s0 = jnp.sum(y * m0, axis=-1, keepdims=True)
        s1 = jnp.sum(y * m1, axis=-1, keepdims=True)
        return jnp.where(head0, s0, s1)

    def group_body(gi, carry):
        seqs = pl.ds(pl.multiple_of(gi * ns, ns), ns)
        pr = range(npairs)
        lss = [slice(p * LANES, (p + 1) * LANES) for p in pr]
        tile = lambda ref, ls: ref[seqs, :, ls].reshape(L, LANES)
        ld = [tile(ld_ref, ls) for ls in lss]
        ld_hi = [x.astype(BF16) for x in ld]
        ld_r = [x - h.astype(F32) for x, h in zip(ld, ld_hi)]
        ld_mid = [x.astype(BF16) for x in ld_r]
        ld_lo = [(x - m.astype(F32)).astype(BF16) for x, m in zip(ld_r, ld_mid)]
        cc = [_dot(cum_lhs, ld_hi[p]) + _dot(cum_lhs, ld_mid[p]) + _dot(cum_lhs, ld_lo[p]) for p in pr]
        cs = [x[:L] for x in cc]
        c_end = [x[L:] for x in cc]
        r = [tile(r_ref, ls) for ls in lss]
        k = [tile(k_ref, ls) for ls in lss]
        v = [tile(v_ref, ls) for ls in lss]
        kn = [tile(kn_ref, ls) for ls in lss]
        b = [kn[p] * tile(a_ref, lss[p]) for p in pr]
        w_inv = [jnp.exp(-x) for x in cs]
        rt = [r[p] * jnp.exp(cs[p]) for p in pr]
        kt = [k[p] * w_inv[p] for p in pr]
        bt = [b[p] * w_inv[p] for p in pr]
        at = [-kn[p] * jnp.exp(cs[p] - ld[p]) for p in pr]
        w_end = [jnp.exp(c_end[p] - cs[p]) for p in pr]
        w_tot = [jnp.exp(x) for x in c_end]
        kh = [k[p] * w_end[p] for p in pr]
        bh = [b[p] * w_end[p] for p in pr]
        pm = [_mm_nt(jnp.concatenate([at[p] * m0, at[p] * m1, rt[p] * m0, rt[p] * m1], axis=0),
                     jnp.concatenate([bt[p], kt[p]], axis=0)) * pmask for p in pr]
        a_bk = [x[:2 * L] for x in pm]
        r_bk = [x[2 * L:] for x in pm]
        akv = [_mm(a_bk[p] * kcols, jnp.concatenate([v[p], v[p]], axis=0)) for p in pr]
        akv = [jnp.where(head0, x[:L], x[L:]) for x in akv]
        pw = [a_bk[p][h * L:(h + 1) * L, :L] for p in pr for h in range(2)]
        tinv = [eye + x for x in pw]
        for _ in range(nsteps):
            pw = [_mm(x, x) for x in pw]
            tinv = [t + _mm(t, x) for t, x in zip(tinv, pw)]
        tz_rhs = [jnp.concatenate([at[p], akv[p]], axis=1) for p in pr]
        tz = [jnp.where(head0w, _mm(tinv[2 * p], tz_rhs[p]), _mm(tinv[2 * p + 1], tz_rhs[p])) for p in pr]
        ry = [_mm(r_bk[p], jnp.concatenate([tz[p], jnp.concatenate([zeros_l, v[p]], axis=1)], axis=0))
              for p in pr]
        ry = [jnp.where(head0w, x[:L], x[L:]) for x in ry]
        r2 = [rt[p] + ry[p][:, :LANES] for p in pr]
        ps = [(p, i) for p in pr for i in range(ns)]
        rs = [slice(i * T, (i + 1) * T) for i in range(ns)]
        state = [jnp.concatenate([jnp.concatenate([s0_ref[gi * ns + i, 2 * p], zed], axis=1),
                                  jnp.concatenate([zed, s0_ref[gi * ns + i, 2 * p + 1]], axis=1)], axis=0)
                 for p, i in ps]
        sb = [x.astype(BF16) for x in state]
        e_rhs = [jnp.concatenate([tz[p][rs[i]], jnp.concatenate([zeros_t, v[p][rs[i]]], axis=1)], axis=0)
                 for p, i in ps]
        e = [_mm_nt(jnp.concatenate([sb[q], eye_w], axis=1), e_rhs[q]) for q in range(len(ps))]
        upd = [_mm(e[q], jnp.concatenate([bh[p][rs[i]], kh[p][rs[i]]], axis=0)) * bdmask
               for q, (p, i) in enumerate(ps)]
        ys = [_mm_nt(r2[p][rs[i]], sb[q]) for q, (p, i) in enumerate(ps)]
        for q, (p, i) in enumerate(ps):
            new = state[q] * w_tot[p][i * T:i * T + 1] + upd[q]
            so_ref[gi * ns + i, 2 * p] = new[:hs, :hs]
            so_ref[gi * ns + i, 2 * p + 1] = new[hs:, hs:]
        y = [jnp.concatenate(ys[p * ns:(p + 1) * ns], axis=0) + ry[p][:, LANES:] for p in pr]
        mean = [head_sum(x) * (1.0 / hs) for x in y]
        yc = [y[p] - mean[p] for p in pr]
        var = [head_sum(x * x) * (1.0 / hs) for x in yc]
        bonus = [head_sum(r[p] * k[p] * rk_ref[:, lss[p]]) * v[p] for p in pr]
        for p in pr:
            yn = yc[p] * lax.rsqrt(var[p] + GN_EPS) * lnw_ref[:, lss[p]] + lnb_ref[:, lss[p]]
            z_ref[seqs, :, lss[p]] = ((yn + bonus[p]) * tile(g_ref, lss[p])).reshape(ns, T, LANES)
        return carry

    lax.fori_loop(0, ngroups, group_body, 0)


def _stacked(prev, shape, n_inputs, out_index):
    prev = jnp.zeros(shape, F32) if prev is None else prev
    return [prev], [pl.BlockSpec(memory_space=pl.ANY)], {n_inputs: out_index}


def _wkv_sample(r, ld, k, v, kn, a, g, s0, lnw, lnb, rk, *, layer, n_layers, states):
    nb, tt, d = r.shape
    nh = d // HEAD_SIZE
    npairs = min(WKV_SAMPLE_PAIRS, nh // 2)
    ns = CHUNK // SAMPLE_TOK
    bb = min(nb, 2 * ns)
    lw = npairs * LANES
    kern = functools.partial(_wkv_sample_kernel, ngroups=bb // ns, npairs=npairs)
    seqb = pl.BlockSpec((bb, tt, lw), lambda i, p: (i, 0, p))
    vecb = pl.BlockSpec((1, lw), lambda i, p: (0, p))
    stb = pl.BlockSpec((bb, 2 * npairs, HEAD_SIZE, HEAD_SIZE), lambda i, p: (i, p, 0, 0))
    sto = pl.BlockSpec((None, bb, 2 * npairs, HEAD_SIZE, HEAD_SIZE), lambda i, p: (layer, i, p, 0, 0))
    st_shape = (n_layers, nb, nh, HEAD_SIZE, HEAD_SIZE)
    extra, extra_specs, aliases = _stacked(states, st_shape, 11, 1)
    return pl.pallas_call(
        kern,
        grid=(nb // bb, nh // (2 * npairs)),
        in_specs=[seqb] * 7 + [stb, vecb, vecb, vecb] + extra_specs,
        out_specs=[seqb, sto],
        out_shape=[jax.ShapeDtypeStruct((nb, tt, d), F32), jax.ShapeDtypeStruct(st_shape, F32)],
        input_output_aliases=aliases,
        compiler_params=_params(("arbitrary", "arbitrary")),
        name="wkv_sample",
    )(r, ld, k, v, kn, a, g, s0, lnw, lnb, rk, *extra)


def _wo_kernel(x_ref, z_ref, w_ref, o_ref, *, nt):
    if nt:
        z = jnp.concatenate([z_ref[:, t, :] for t in range(nt)], axis=0)
    else:
        z = z_ref[...]
    o_ref[...] = x_ref[...] + _dot(z.astype(BF16), w_ref[...])


def _wo(x, z, w, *, layer, nt=0):
    rows, d = x.shape
    tm = rows if nt else _row_tile(rows, WO_TM_CAP)
    zspec = pl.BlockSpec(z.shape, lambda i: (0, 0, 0)) if nt else pl.BlockSpec((tm, d), lambda i: (i, 0))
    return pl.pallas_call(
        functools.partial(_wo_kernel, nt=nt),
        grid=(rows // tm,),
        in_specs=[pl.BlockSpec((tm, d), lambda i: (i, 0)), zspec,
                  pl.BlockSpec((None, d, d), lambda i: (layer, 0, 0))],
        out_specs=pl.BlockSpec((tm, d), lambda i: (i, 0)),
        out_shape=jax.ShapeDtypeStruct((rows, d), F32),
        compiler_params=_params(("arbitrary",)),
        name="rwkv_wo_sample" if nt else "rwkv_wo",
    )(x, z, w)


def _pad_cols(a, n):
    return jnp.pad(a, [(0, 0)] * (a.ndim - 1) + [(0, n - a.shape[-1])])


def _pad_rows(a, n):
    return jnp.pad(a, [(0, n - a.shape[0])] + [(0, 0)] * (a.ndim - 1))


def kernel(x_prompt, x_sample, state_pool, state_rwkv_shift, state_rwkv_wkv, state_ffn_conv, meta_tokens,
           norm_mix, norm_ffn, norm_out, pool_w, pool_scale, rwkv_mu, rwkv_wr, rwkv_wk, rwkv_wv, rwkv_wo,
           rwkv_w0, rwkv_w1, rwkv_w2, rwkv_a0, rwkv_a1, rwkv_a2, rwkv_v0, rwkv_v1, rwkv_v2, rwkv_g1, rwkv_g2,
           rwkv_kk, rwkv_ka, rwkv_rk, rwkv_lnw, rwkv_lnb, ffn_w_in, ffn_conv_w, ffn_conv_b, ffn_w_out):
    b, seq, d = x_prompt.shape
    bs, nt, _ = x_sample.shape
    depth = norm_mix.shape[0]
    f = ffn_w_out.shape[1]
    fp = _round_up(f, FFN_TF)
    nh = d // HEAD_SIZE
    tp = FRONT_PAD + N_META + seq
    assert tp % CHUNK == 0 and d % (2 * LANES) == 0 and CONV_WIDTH - 1 <= nt <= SAMPLE_TOK

    row = lambda a: a.reshape(1, -1)
    lora_in = lambda a: _pad_cols(a, _round_up(a.shape[-1], LANES)).astype(BF16)
    lora_out = lambda a: _pad_rows(a, _round_up(a.shape[0], LANES)).astype(BF16)
    halves = lambda a: jnp.concatenate([_pad_cols(a[..., :f], fp), _pad_cols(a[..., f:], fp)], axis=-1)
    w_in_all = _cast_halves(ffn_w_in, fp)
    w_out_all = _cast_rows(ffn_w_out, fp)
    ffn = [dict(g=row(norm_ffn[i]), cw=halves(ffn_conv_w[i]), cb=halves(ffn_conv_b[i][None]))
           for i in range(depth)]
    tn = min(PROJ_TN, d)
    bd = jnp.kron(jnp.eye(tn // HEAD_SIZE, dtype=F32), jnp.ones((HEAD_SIZE, HEAD_SIZE), F32)).astype(BF16)
    wr_all, wk_all, wv_all, wo_all = (_cast_rows(w, d) for w in (rwkv_wr, rwkv_wk, rwkv_wv, rwkv_wo))
    rw = []
    for j in range(depth // 2):
        rw.append(dict(
            g=row(norm_mix[2 * j + 1]), mu=rwkv_mu[j], wr=wr_all, wk=wk_all, wv=wv_all,
            w1=lora_in(rwkv_w1[j]), a1=lora_in(rwkv_a1[j]), g1=lora_in(rwkv_g1[j]),
            w2=lora_out(rwkv_w2[j]), a2=lora_out(rwkv_a2[j]), g2=lora_out(rwkv_g2[j]),
            w0=row(rwkv_w0[j]), a0=row(rwkv_a0[j]), kk=row(rwkv_kk[j]), ka=row(rwkv_ka[j]), bd=bd,
            lnw=row(rwkv_lnw[j]), lnb=row(rwkv_lnb[j]), rk=row(rwkv_rk[j])))
    vls = [None] + [dict(v1=lora_in(rwkv_v1[j]), v2=lora_out(rwkv_v2[j]), v0=row(rwkv_v0[j]))
                    for j in range(depth // 2 - 1)]
    pw = [pool_w[j].astype(BF16) for j in range((depth + 1) // 2)]
    g_out = row(norm_out)

    head = jnp.concatenate([jnp.zeros((FRONT_PAD, d), F32), meta_tokens], axis=0)
    xp = x_prompt
    xs = x_sample.transpose(1, 0, 2)
    zero_wkv = jnp.zeros((b, nh, HEAD_SIZE, HEAD_SIZE), F32)
    n_pool, n_rwkv = (depth + 1) // 2, depth // 2

    pool_p, shift_p, shift_s, wkv_p, conv_p = [], [], [], [], []
    pool_s = wkv_s = conv_s = None
    vfirst_p = vfirst_s = None
    for i in range(depth):
        j = i // 2
        if i % 2 == 0:
            xp, st = _pool_prompt(xp, row(norm_mix[i]), pw[j], row(pool_scale[j]), tp=tp,
                                  head=head if i == 0 else None)
            pool_p.append(st[:, POOL_HALO - POOL_STATE:])
            xs, pool_s = _pool_sample(xs, state_pool[j], row(norm_mix[i]), pw[j], row(pool_scale[j]), nt=nt, bs=bs,
                                      layer=j, n_layers=n_pool, states=pool_s)
        else:
            p = rw[j]
            r, ld, k, v, kn, a, g, hl = _proj(xp.reshape(b * tp, d), None, p, vls[j], vfirst_p,
                                              layer=j, tp=tp, sample=False, bs=0)
            if vfirst_p is None:
                vfirst_p = v
            shift_p.append(hl[:, CONV_HALO - 1])
            sq = lambda t: t.reshape(b, tp, d)
            z, s_new = _wkv_prompt(sq(r), sq(ld), sq(k), sq(v), sq(kn), sq(a), sq(g), zero_wkv,
                                   p["lnw"], p["lnb"], p["rk"])
            wkv_p.append(s_new)
            xp = _wo(xp.reshape(b * tp, d), z.reshape(b * tp, d), wo_all, layer=j).reshape(b, tp, d)
            r, ld, k, v, kn, a, g, hl = _proj(xs.reshape(nt * bs, d), state_rwkv_shift[j], p, vls[j], vfirst_s,
                                              layer=j, tp=0, sample=True, bs=bs)
            if vfirst_s is None:
                vfirst_s = v
            shift_s.append(hl)
            z, wkv_s = _wkv_sample(r, ld, k, v, kn, a, g, state_rwkv_wkv[j], p["lnw"], p["lnb"], p["rk"],
                                   layer=j, n_layers=n_rwkv, states=wkv_s)
            xs = _wo(xs.reshape(nt * bs, d), z, wo_all, layer=j, nt=nt).reshape(nt, bs, d)
        fi = ffn[i]
        last = i == depth - 1
        xo, stv, stg = _ffn_prompt(xp.reshape(b * tp, d), fi["g"], w_in_all, fi["cw"], fi["cb"], w_out_all,
                                   g_out, layer=i, tp=tp, final_norm=last)
        xp = xo.reshape(b, tp, d)
        stv, stg = (t.reshape(b, -1, CONV_HALO, fp)[:, -1] for t in (stv, stg))
        conv_p.append(jnp.concatenate([stv[:, CONV_HALO - 2:, :f], stg[:, CONV_HALO - 2:, :f]], axis=-1))
        xo, *conv_s = _ffn_sample(xs.reshape(nt * bs, d), fi["g"], w_in_all, fi["cw"], fi["cb"], w_out_all, g_out,
                                  state_ffn_conv[i][:, :, :f], state_ffn_conv[i][:, :, f:],
                                  layer=i, n_layers=depth, states=conv_s, bs=bs, nt=nt, final_norm=last)
        xs = xo.reshape(nt, bs, d)

    y_prompt = xp[:, FRONT_PAD + N_META:]
    y_sample = xs.transpose(1, 0, 2)
    conv_s = jnp.concatenate([conv_s[0][..., :f], conv_s[1][..., :f]], axis=-1)
    return (y_prompt, y_sample, jnp.stack(pool_p), pool_s, jnp.stack(shift_p), jnp.stack(shift_s),
            jnp.stack(wkv_p), wkv_s, jnp.stack(conv_p), conv_s)
```
